```python
import math
import jax, jax.numpy as jnp
from jax import lax
import numpy as np

D_MODEL = 1024
BATCH = 16
SEQ = 256
DEPTH = 1
DEC_BATCH = 4
DEC_SEQ = 1024
PAST_LEN = 512

GRID_W = 64
N_HEADS = 8
HEAD_DIM = 64
V_HEAD_DIM = 2 * HEAD_DIM
Q_WIDTH = N_HEADS * 2 * HEAD_DIM
ATTN_WIDTH = N_HEADS * V_HEAD_DIM
ROPE_AXIS_DIM = HEAD_DIM // 2
ROPE_BASE = 10000.0
Q_BLOCK = 128
CHUNK = 128
GMLP_WIDTH = D_MODEL
N_GMLP_GROUPS = 8
GMLP_GROUP_DIM = GMLP_WIDTH // N_GMLP_GROUPS
N_EXPERTS = 16
EC_CAPACITY_FACTOR = 2
D_EXPERT = 2048
IN_WIDTH = 2 * Q_WIDTH + ATTN_WIDTH + 2 * GMLP_WIDTH + 2 * D_MODEL
EPS = 1e-6

kernel_name = "hybrid_diffattn_gmlp_ecmoe_prefix_step"


def rms_norm(x, g):
    xf = x.astype(jnp.float32)
    y = xf * lax.rsqrt(jnp.mean(xf * xf, axis=-1, keepdims=True) + EPS)
    return (y * g.astype(jnp.float32)).astype(x.dtype)


def layer_norm(x, g, b):
    xf = x.astype(jnp.float32)
    mu = jnp.mean(xf, axis=-1, keepdims=True)
    var = jnp.mean(jnp.square(xf - mu), axis=-1, keepdims=True)
    y = (xf - mu) * lax.rsqrt(var + EPS)
    return (y * g.astype(jnp.float32) + b.astype(jnp.float32)).astype(x.dtype)


def lambda_init_for(layer):
    return 0.8 - 0.6 * math.exp(-0.3 * layer)


def rope_2d(n_tokens, dtype):
    rows = n_tokens // GRID_W
    row = jnp.broadcast_to(jnp.arange(rows, dtype=jnp.float32)[:, None], (rows, GRID_W)).reshape(-1)
    col = jnp.broadcast_to(jnp.arange(GRID_W, dtype=jnp.float32)[None, :], (rows, GRID_W)).reshape(-1)
    half = ROPE_AXIS_DIM // 2
    inv_freq = ROPE_BASE ** (-jnp.arange(half, dtype=jnp.float32) / half)
    ar = row[:, None] * inv_freq
    ac = col[:, None] * inv_freq
    ang = jnp.concatenate([ar, ar, ac, ac], axis=-1)
    cos = jnp.cos(ang)[:, None, None, :].astype(dtype)
    sin = jnp.sin(ang)[:, None, None, :].astype(dtype)
    return cos, sin


def apply_rope(x, cos, sin):
    x1, x2, x3, x4 = jnp.split(x, 4, axis=-1)
    rot = jnp.concatenate([-x2, x1, -x4, x3], axis=-1)
    return x * cos + rot * sin


def diff_attention(q, k, v, lam):
    B, N = q.shape[0], q.shape[1]
    nb = N // Q_BLOCK
    scale = HEAD_DIM ** -0.5
    kf = k.astype(jnp.float32)
    vf = v.astype(jnp.float32)
    qb = q.reshape(B, nb, Q_BLOCK, N_HEADS, 2, HEAD_DIM).transpose(1, 0, 2, 3, 4, 5)

    def block(qblk):
        s = jnp.einsum('bqhsd,bkhsd->bshqk', qblk.astype(jnp.float32), kf) * scale
        p = jax.nn.softmax(s, axis=-1)
        a = p[:, 0] - lam * p[:, 1]
        return jnp.einsum('bhqk,bkhv->bqhv', a, vf)

    o = lax.map(block, qb)
    return o.transpose(1, 0, 2, 3, 4).reshape(B, N, N_HEADS, V_HEAD_DIM).astype(v.dtype)


def spatial_gating(zu, zv, ln_g, ln_b, w_sp, b_sp):
    B, N, _ = zu.shape
    zu = jax.nn.gelu(zu)
    zv = layer_norm(jax.nn.gelu(zv), ln_g, ln_b)
    zc = zv.reshape(B, N // CHUNK, CHUNK, N_GMLP_GROUPS, GMLP_GROUP_DIM)
    sp = jnp.einsum('gpq,bnqgc->bnpgc', w_sp, zc) + b_sp.T[:, :, None]
    return zu * sp.reshape(B, N, GMLP_WIDTH)


def token_mixer(h, lp, lambda_init, rope, ctx_kv):
    B, N, _ = h.shape
    proj = h @ lp['w_in']
    offs = np.cumsum([Q_WIDTH, Q_WIDTH, ATTN_WIDTH, GMLP_WIDTH, GMLP_WIDTH, D_MODEL]).tolist()
    q, k, v, zu, zv, g_a, g_b = jnp.split(proj, offs, axis=-1)
    q = rms_norm(q.reshape(B, N, N_HEADS, 2, HEAD_DIM), lp['q_norm_g'])
    k = rms_norm(k.reshape(B, N, N_HEADS, 2, HEAD_DIM), lp['k_norm_g'])
    v = v.reshape(B, N, N_HEADS, V_HEAD_DIM)
    if rope is not None:
        cos, sin = rope
        q = apply_rope(q, cos, sin)
        k = apply_rope(k, cos, sin)
    if ctx_kv is not None:
        k_all = jnp.concatenate([ctx_kv[0].astype(k.dtype), k], axis=1)
        v_all = jnp.concatenate([ctx_kv[1].astype(v.dtype), v], axis=1)
    else:
        k_all, v_all = k, v
    f32 = jnp.float32
    lam = (jnp.exp(jnp.sum(lp['lambda_q1'].astype(f32) * lp['lambda_k1'].astype(f32)))
           - jnp.exp(jnp.sum(lp['lambda_q2'].astype(f32) * lp['lambda_k2'].astype(f32)))
           + lambda_init)
    o_a = diff_attention(q, k_all, v_all, lam)
    o_a = (rms_norm(o_a, lp['subln_g']) * (1.0 - lambda_init)).reshape(B, N, ATTN_WIDTH)
    o_b = spatial_gating(zu, zv, lp['gmlp_ln_g'], lp['gmlp_ln_b'], lp['w_spatial'], lp['b_spatial'])
    merged = jax.nn.sigmoid(g_a) * o_a + jax.nn.sigmoid(g_b) * o_b
    return merged @ lp['w_out'], k, v


def expert_choice_moe(h, w_router, w_gate, w_up, w_down):
    B, N, D = h.shape
    cap = max(1, EC_CAPACITY_FACTOR * N // N_EXPERTS)
    aff = jax.nn.softmax((h @ w_router).astype(jnp.float32), axis=-1)
    gates, idx = lax.top_k(aff.transpose(0, 2, 1), cap)

    def one(hb, gb, ib):
        xe = hb[ib]
        hid = jax.nn.silu(jnp.einsum('ecd,edf->ecf', xe, w_gate)) * jnp.einsum('ecd,edf->ecf', xe, w_up)
        ye = jnp.einsum('ecf,efd->ecd', hid, w_down) * gb[..., None].astype(hb.dtype)
        return jnp.zeros_like(hb).at[ib.reshape(-1)].add(ye.reshape(-1, D))

    return jax.vmap(one)(h, gates, idx)


def trunk_layer(x, cond, lp, lambda_init, rope, ctx_kv):
    m = jax.nn.silu(cond) @ lp['w_ada'] + lp['b_ada']
    shift1, scale1, gate1, shift2, scale2, gate2 = jnp.split(m[:, None, :], 6, axis=-1)
    h = rms_norm(x, lp['norm1_g']) * (1.0 + scale1) + shift1
    mix, k, v = token_mixer(h, lp, lambda_init, rope, ctx_kv)
    x = x + gate1 * mix
    h2 = rms_norm(x, lp['norm2_g']) * (1.0 + scale2) + shift2
    x = x + gate2 * expert_choice_moe(h2, lp['w_router'], lp['w_gate_e'], lp['w_up_e'], lp['w_down_e'])
    return x, k, v


def setup_inputs(seed: int = 0) -> dict:
    key = jax.random.key(seed)
    ks = jax.random.split(key, 32)
    f32 = jnp.float32
    nrm = lambda k, shape, s: jax.random.normal(k, shape, f32) * s
    D = D_MODEL
    return {
        'x_prompt': nrm(ks[0], (BATCH, SEQ, D), 1.0),
        'x_sample': nrm(ks[1], (DEC_BATCH, DEC_SEQ, D), 1.0),
        'cache_k': nrm(ks[2], (DEC_BATCH, DEPTH, PAST_LEN, N_HEADS, 2, HEAD_DIM), 1.0),
        'cache_v': nrm(ks[3], (DEC_BATCH, DEPTH, PAST_LEN, N_HEADS, V_HEAD_DIM), 1.0),
        'c': nrm(ks[4], (DEC_BATCH, D), 1.0),
        'c_ctx': nrm(ks[5], (D,), 1.0),
        'w_ada': nrm(ks[6], (DEPTH, D, 6 * D), D ** -0.5),
        'b_ada': nrm(ks[7], (DEPTH, 6 * D), 0.02),
        'norm1_g': 1.0 + nrm(ks[8], (DEPTH, D), 0.02),
        'norm2_g': 1.0 + nrm(ks[9], (DEPTH, D), 0.02),
        'w_in': nrm(ks[10], (DEPTH, D, IN_WIDTH), D ** -0.5),
        'q_norm_g': 1.0 + nrm(ks[11], (DEPTH, HEAD_DIM), 0.02),
        'k_norm_g': 1.0 + nrm(ks[12], (DEPTH, HEAD_DIM), 0.02),
        'lambda_q1': nrm(ks[13], (DEPTH, HEAD_DIM), 0.1),
        'lambda_k1': nrm(ks[14], (DEPTH, HEAD_DIM), 0.1),
        'lambda_q2': nrm(ks[15], (DEPTH, HEAD_DIM), 0.1),
        'lambda_k2': nrm(ks[16], (DEPTH, HEAD_DIM), 0.1),
        'subln_g': 1.0 + nrm(ks[17], (DEPTH, V_HEAD_DIM), 0.02),
        'gmlp_ln_g': 1.0 + nrm(ks[18], (DEPTH, GMLP_WIDTH), 0.02),
        'gmlp_ln_b': nrm(ks[19], (DEPTH, GMLP_WIDTH), 0.02),
        'w_spatial': nrm(ks[20], (DEPTH, N_GMLP_GROUPS, CHUNK, CHUNK), CHUNK ** -0.5),
        'b_spatial': 1.0 + nrm(ks[21], (DEPTH, N_GMLP_GROUPS, CHUNK), 0.02),
        'w_out': nrm(ks[22], (DEPTH, D, D), D ** -0.5),
        'w_router': nrm(ks[23], (DEPTH, D, N_EXPERTS), D ** -0.5),
        'w_gate_e': nrm(ks[24], (DEPTH, N_EXPERTS, D, D_EXPERT), D ** -0.5),
        'w_up_e': nrm(ks[25], (DEPTH, N_EXPERTS, D, D_EXPERT), D ** -0.5),
        'w_down_e': nrm(ks[26], (DEPTH, N_EXPERTS, D_EXPERT, D), D_EXPERT ** -0.5),
    }


def reference(x_prompt, x_sample, cache_k, cache_v, c, c_ctx, w_ada, b_ada, norm1_g, norm2_g, w_in,
              q_norm_g, k_norm_g, lambda_q1, lambda_k1, lambda_q2, lambda_k2, subln_g, gmlp_ln_g,
              gmlp_ln_b, w_spatial, b_spatial, w_out, w_router, w_gate_e, w_up_e, w_down_e):
    rope_lat = rope_2d(x_sample.shape[1], x_sample.dtype)
    cond_ctx = c_ctx[None, :]
    xp, xs = x_prompt, x_sample
    new_k, new_v = [], []
    for l in range(DEPTH):
        lp = {
            'w_ada': w_ada[l], 'b_ada': b_ada[l], 'norm1_g': norm1_g[l], 'norm2_g': norm2_g[l],
            'w_in': w_in[l], 'q_norm_g': q_norm_g[l], 'k_norm_g': k_norm_g[l],
            'lambda_q1': lambda_q1[l], 'lambda_k1': lambda_k1[l],
            'lambda_q2': lambda_q2[l], 'lambda_k2': lambda_k2[l], 'subln_g': subln_g[l],
            'gmlp_ln_g': gmlp_ln_g[l], 'gmlp_ln_b': gmlp_ln_b[l],
            'w_spatial': w_spatial[l], 'b_spatial': b_spatial[l], 'w_out': w_out[l],
            'w_router': w_router[l], 'w_gate_e': w_gate_e[l], 'w_up_e': w_up_e[l], 'w_down_e': w_down_e[l],
        }
        lam_init = lambda_init_for(l)
        xp, kp, vp = trunk_layer(xp, cond_ctx, lp, lam_init, None, None)
        new_k.append(kp)
        new_v.append(vp)
        xs, _, _ = trunk_layer(xs, c, lp, lam_init, rope_lat, (cache_k[:, l], cache_v[:, l]))
    new_cache_k = jnp.stack(new_k, axis=1)
    new_cache_v = jnp.stack(new_v, axis=1)
    return (xp, xs, new_cache_k, new_cache_v)
```

```python
import functools
import math

import jax
import jax.numpy as jnp
from jax import lax
from jax.experimental import pallas as pl
from jax.experimental.pallas import tpu as pltpu

D = 1024
N_HEADS = 8
HEAD_DIM = 64
V_HEAD_DIM = 128
GRID_W = 64
ROPE_BASE = 10000.0
CHUNK = 128
N_GROUPS = 8
N_EXPERTS = 16
CAPACITY_FACTOR = 2
D_EXPERT = 2048
N_SEG = 7
EPS = 1e-6
LAMBDA_INIT = 0.8 - 0.6 * math.exp(-0.3 * 0)

LANES = 128
MOD_ROWS = 8
VMEM_LIMIT = 56 * 1024 * 1024

BF = jnp.bfloat16
F32 = jnp.float32


def _dot(a, b):
    return jnp.dot(a, b, preferred_element_type=F32)


def _dot_nt(a, b):
    return lax.dot_general(a, b, (((1,), (1,)), ((), ())), preferred_element_type=F32)


def _split_bf16(x):
    hi = x.astype(BF)
    lo = (x - hi.astype(F32)).astype(BF)
    return hi, lo


def _params(n_grid_dims):
    return pltpu.CompilerParams(
        dimension_semantics=("arbitrary",) * n_grid_dims, vmem_limit_bytes=VMEM_LIMIT)


def _const_spec(shape):
    nd = len(shape)
    return pl.BlockSpec(shape, lambda *_: (0,) * nd)


def _ada_kernel(cond_ref, w_ref, b_ref, lq1_ref, lk1_ref, lq2_ref, lk2_ref, m_ref, lam_ref):
    c = cond_ref[...]
    a = c * jax.nn.sigmoid(c)
    a_hi, a_lo = _split_bf16(a)
    w_hi, w_lo = _split_bf16(w_ref[...])
    m_ref[...] = _dot(a_hi, w_hi) + _dot(a_lo, w_hi) + _dot(a_hi, w_lo) + b_ref[...]
    s1 = jnp.sum(lq1_ref[...] * lk1_ref[...], axis=-1, keepdims=True)
    s2 = jnp.sum(lq2_ref[...] * lk2_ref[...], axis=-1, keepdims=True)
    lam = jnp.exp(s1) - jnp.exp(s2) + LAMBDA_INIT
    lam_ref[...] = jnp.broadcast_to(lam, lam_ref.shape)


def _ada_call(cond, w_ada, b_ada, lq1, lk1, lq2, lk2):
    tn = D
    vec = pl.BlockSpec((1, HEAD_DIM), lambda j: (0, 0))
    return pl.pallas_call(
        _ada_kernel,
        grid=(6 * D // tn,),
        in_specs=[
            pl.BlockSpec((MOD_ROWS, D), lambda j: (0, 0)),
            pl.BlockSpec((D, tn), lambda j: (0, j)),
            pl.BlockSpec((1, tn), lambda j: (0, j)),
            vec, vec, vec, vec,
        ],
        out_specs=[
            pl.BlockSpec((MOD_ROWS, tn), lambda j: (0, j)),
            pl.BlockSpec((MOD_ROWS, LANES), lambda j: (0, 0)),
        ],
        out_shape=[
            jax.ShapeDtypeStruct((MOD_ROWS, 6 * D), F32),
            jax.ShapeDtypeStruct((MOD_ROWS, LANES), F32),
        ],
        compiler_params=_params(1),
        name="ada",
    )(cond, w_ada, b_ada, lq1, lk1, lq2, lk2)


def _rms(x, g):
    return x * lax.rsqrt(jnp.mean(x * x, axis=-1, keepdims=True) + EPS) * g


def _head_norm(t, g, gsum, gbcast):
    ss = _dot((t * t).astype(BF), gsum)
    inv = lax.rsqrt(ss * (1.0 / HEAD_DIM) + EPS)
    hi, lo = _split_bf16(inv)
    bc = _dot(jnp.concatenate([hi, lo], axis=-1), gbcast)
    return t * bc * g


def _rope(t, cos, sin_signed, first_half):
    q = HEAD_DIM // 4
    fwd = pltpu.roll(t, D - q, axis=1)
    bwd = pltpu.roll(t, q, axis=1)
    return t * cos + jnp.where(first_half, fwd, bwd) * sin_signed


def _inproj_kernel(rope, x_ref, m_ref, n1g_ref, w_ref, qg_ref, kg_ref, gsum_ref, gbcast_ref,
                   lng_ref, lnb_ref, wsp_ref, bsp_ref, *rest):
    if rope:
        cos_ref, sin_ref, q_out, k_out, v_out, sga_out, tb_out = rest
    else:
        q_out, k_out, v_out, sga_out, tb_out = rest
    tm = x_ref.shape[0]
    m = m_ref[0]
    shift1, scale1 = m[:, 0:D], m[:, D:2 * D]
    h = (_rms(x_ref[...], n1g_ref[...]) * (1.0 + scale1) + shift1).astype(BF)

    def seg(j):
        return _dot(h, w_ref[:, j * D:(j + 1) * D])

    if rope:
        lane = lax.broadcasted_iota(jnp.int32, (1, D), 1)
        first_half = (lane % (HEAD_DIM // 2)) < (HEAD_DIM // 4)

    def qk(j, g_ref):
        t = _head_norm(seg(j), g_ref[...], gsum_ref[...], gbcast_ref[...])
        if rope:
            t = _rope(t, cos_ref[...], sin_ref[...], first_half)
        return t

    q_out[...] = qk(0, qg_ref).astype(q_out.dtype)
    k_out[...] = qk(1, kg_ref).astype(k_out.dtype)
    v_out[...] = seg(2).astype(v_out.dtype)
    sga_out[...] = jax.nn.sigmoid(seg(5)).astype(sga_out.dtype)

    zv = jax.nn.gelu(seg(4))
    mu = jnp.mean(zv, axis=-1, keepdims=True)
    zc = zv - mu
    var = jnp.mean(zc * zc, axis=-1, keepdims=True)
    zvn = (zc * lax.rsqrt(var + EPS) * lng_ref[...] + lnb_ref[...]).astype(BF)
    pre = jax.nn.gelu(seg(3)) * jax.nn.sigmoid(seg(6))
    for c in range(tm // CHUNK):
        rows = slice(c * CHUNK, (c + 1) * CHUNK)
        for g in range(N_GROUPS):
            cols = slice(g * LANES, (g + 1) * LANES)
            sp = _dot(wsp_ref[g], zvn[rows, cols]) + bsp_ref[:, cols]
            tb_out[rows, cols] = (pre[rows, cols] * sp).astype(tb_out.dtype)


def _inproj_call(x, m3, mod_row, rope_tabs, kv_dtype, consts):
    t = x.shape[0]
    tm = 256
    rope = rope_tabs is not None
    tok = pl.BlockSpec((tm, D), lambda i: (i, 0))
    in_specs = [
        tok,
        pl.BlockSpec((1, 1, 6 * D), lambda i: (mod_row(i, tm), 0, 0)),
        _const_spec((1, D)),
        pl.BlockSpec((D, N_SEG * D), lambda i: (0, 0), pipeline_mode=pl.Buffered(1)),
        _const_spec((1, D)), _const_spec((1, D)),
        _const_spec((D, LANES)), _const_spec((2 * LANES, D)),
        _const_spec((1, D)), _const_spec((1, D)),
        _const_spec((N_GROUPS, CHUNK, CHUNK)), _const_spec((CHUNK, D)),
    ]
    args = [x, m3, consts["n1g"], consts["w_in"], consts["qg"], consts["kg"], consts["gsum"],
            consts["gbcast"], consts["lng"], consts["lnb"], consts["wsp"], consts["bsp"]]
    if rope:
        n_pos = rope_tabs[0].shape[0]
        tab = pl.BlockSpec((tm, D), lambda i: (i % (n_pos // tm), 0))
        in_specs += [tab, tab]
        args += list(rope_tabs)
    out_dtypes = [BF, kv_dtype, kv_dtype, BF, BF]
    return pl.pallas_call(
        functools.partial(_inproj_kernel, rope),
        grid=(t // tm,),
        in_specs=in_specs,
        out_specs=[tok] * 5,
        out_shape=[jax.ShapeDtypeStruct((t, D), dt) for dt in out_dtypes],
        compiler_params=_params(1),
        name="inproj_rope" if rope else "inproj",
    )(*args)


def _attn_kernel(has_ctx, q_ref, k_ref, v_ref, *rest):
    if has_ctx:
        kc_ref, vc_ref, lam_ref, sg_ref, o_ref = rest
    else:
        lam_ref, sg_ref, o_ref = rest
    lam = lam_ref[0:1, 0:1]
    lane = lax.broadcasted_iota(jnp.int32, (1, LANES), 1)
    first = lane < HEAD_DIM
    zero = jnp.zeros((), BF)
    for h in range(N_HEADS):
        cols = slice(h * LANES, (h + 1) * LANES)
        q = q_ref[:, cols] * (HEAD_DIM ** -0.5)
        qs = (jnp.where(first, q, zero), jnp.where(first, zero, q))
        keys = [k_ref[:, cols].astype(BF)]
        vals = [v_ref[:, cols].astype(BF)]
        if has_ctx:
            keys.append(kc_ref[:, cols].astype(BF))
            vals.append(vc_ref[:, cols].astype(BF))
        exps, coefs = [], []
        for s in range(2):
            sc = [_dot_nt(qs[s], kk) for kk in keys]
            mx = functools.reduce(jnp.maximum, [jnp.max(x, axis=-1, keepdims=True) for x in sc])
            ex = [jnp.exp(x - mx) for x in sc]
            den = functools.reduce(jnp.add, [jnp.sum(x, axis=-1, keepdims=True) for x in ex])
            exps.append(ex)
            coefs.append(1.0 / den)
        c0, c1 = coefs[0], coefs[1] * lam
        o = None
        for p in range(len(keys)):
            a = (exps[0][p] * c0 - exps[1][p] * c1).astype(BF)
            part = _dot(a, vals[p])
            o = part if o is None else o + part
        o = _rms(o, sg_ref[...]) * (1.0 - LAMBDA_INIT)
        o_ref[:, cols] = o.astype(o_ref.dtype)


def _attn_call(q, k, v, ctx, lamv, subln_g, n_req, n_tok):
    tq = 256
    nqb = n_tok // tq
    has_ctx = ctx is not None
    in_specs = [
        pl.BlockSpec((tq, D), lambda b, i: (b * nqb + i, 0)),
        pl.BlockSpec((n_tok, D), lambda b, i: (b, 0)),
        pl.BlockSpec((n_tok, D), lambda b, i: (b, 0)),
    ]
    args = [q, k, v]
    if has_ctx:
        n_ctx = ctx[0].shape[0] // n_req
        in_specs += [pl.BlockSpec((n_ctx, D), lambda b, i: (b, 0))] * 2
        args += list(ctx)
    in_specs += [_const_spec((MOD_ROWS, LANES)), _const_spec((1, V_HEAD_DIM))]
    args += [lamv, subln_g]
    return pl.pallas_call(
        functools.partial(_attn_kernel, has_ctx),
        grid=(n_req, nqb),
        in_specs=in_specs,
        out_specs=pl.BlockSpec((tq, D), lambda b, i: (b * nqb + i, 0)),
        out_shape=jax.ShapeDtypeStruct((n_req * n_tok, D), BF),
        compiler_params=_params(2),
        name="attn_ctx" if has_ctx else "attn",
    )(*args)


def _outproj_kernel(oa_ref, sga_ref, tb_ref, x_ref, m_ref, wo_ref, n2g_ref, wr_hi_ref, wr_lo_ref,
                    x1_out, h2_out, lg_out):
    m = m_ref[0]
    gate1, shift2, scale2 = m[:, 2 * D:3 * D], m[:, 3 * D:4 * D], m[:, 4 * D:5 * D]
    merged = (sga_ref[...].astype(F32) * oa_ref[...].astype(F32) + tb_ref[...].astype(F32)).astype(BF)
    x1 = x_ref[...] + gate1 * _dot(merged, wo_ref[...])
    x1_out[...] = x1
    h2 = _rms(x1, n2g_ref[...]) * (1.0 + scale2) + shift2
    h2_out[...] = h2.astype(BF)
    hi, lo = _split_bf16(h2)
    wr_hi = wr_hi_ref[...]
    lg_out[...] = _dot(hi, wr_hi) + _dot(lo, wr_hi) + _dot(hi, wr_lo_ref[...])


def _outproj_call(oa, sga, tb, x, m3, mod_row, consts):
    t = x.shape[0]
    tm = 256
    tok = pl.BlockSpec((tm, D), lambda i: (i, 0))
    return pl.pallas_call(
        _outproj_kernel,
        grid=(t // tm,),
        in_specs=[
            tok, tok, tok, tok,
            pl.BlockSpec((1, 1, 6 * D), lambda i: (mod_row(i, tm), 0, 0)),
            _const_spec((D, D)), _const_spec((1, D)),
            _const_spec((D, LANES)), _const_spec((D, LANES)),
        ],
        out_specs=[tok, tok, pl.BlockSpec((tm, LANES), lambda i: (i, 0))],
        out_shape=[
            jax.ShapeDtypeStruct((t, D), F32),
            jax.ShapeDtypeStruct((t, D), BF),
            jax.ShapeDtypeStruct((t, LANES), F32),
        ],
        compiler_params=_params(1),
        name="outproj",
    )(oa, sga, tb, x, m3, consts["w_out"], consts["n2g"], consts["wr_hi"], consts["wr_lo"])


RANK_TILE = 256
GATHER_ROWS = 512


def _route_kernel(cap, lg_ref, h2_ref, xe_ref, gate_ref, rank_t_ref, aff_s, aff_t_s, rank_s, p_s):
    n = lg_ref.shape[0]
    lane = lax.broadcasted_iota(jnp.int32, (1, LANES), 1)
    valid = lane < N_EXPERTS
    lg = jnp.where(valid, lg_ref[...], -1e30)
    ex = jnp.where(valid, jnp.exp(lg - jnp.max(lg, axis=-1, keepdims=True)), 0.0)
    aff = ex / jnp.sum(ex, axis=-1, keepdims=True)
    aff_s[...] = aff
    aff_t_s[...] = aff.T
    rank_s[...] = jnp.zeros(rank_s.shape, F32)
    earlier = (lax.broadcasted_iota(jnp.int32, (RANK_TILE, 1), 0)
               < lax.broadcasted_iota(jnp.int32, (1, RANK_TILE), 1)).astype(F32)
    slot = lax.broadcasted_iota(jnp.int32, (cap, 1), 0).astype(F32)

    def per_expert(e, carry):
        a_row = aff_t_s[pl.ds(e, 1), :]
        rank = jnp.zeros((1, n), F32)
        for t in range(n // RANK_TILE):
            lo, hi = t * RANK_TILE, (t + 1) * RANK_TILE
            a_col = jnp.sum(jnp.where(lane == e, aff_s[lo:hi, :], 0.0), axis=-1, keepdims=True)
            parts = []
            if lo > 0:
                parts.append(jnp.where(a_col > a_row[:, :lo], 1.0, 0.0))
            a_diag = a_row[:, lo:hi]
            parts.append(jnp.where(a_col > a_diag, 1.0, jnp.where(a_col == a_diag, earlier, 0.0)))
            if hi < n:
                parts.append(jnp.where(a_col >= a_row[:, hi:], 1.0, 0.0))
            counts = [jnp.sum(p, axis=0, keepdims=True) for p in parts]
            rank = rank + (counts[0] if len(counts) == 1 else jnp.concatenate(counts, axis=1))
        rank_s[pl.ds(e, 1), :] = rank
        chosen = rank == slot
        p_s[pl.ds(pl.multiple_of(e * cap, cap), cap), :] = jnp.where(chosen, 1.0, 0.0).astype(BF)
        gate_ref[e] = jnp.sum(jnp.where(chosen, a_row, 0.0), axis=-1, keepdims=True)
        return carry

    lax.fori_loop(0, N_EXPERTS, per_expert, 0)
    rank_t_ref[...] = rank_s[...].T
    h2 = h2_ref[...]
    rows = min(GATHER_ROWS, N_EXPERTS * cap)
    e_per = rows // cap
    for t in range(N_EXPERTS * cap // rows):
        xe = _dot(p_s[t * rows:(t + 1) * rows, :], h2).astype(BF)
        xe_ref[t * e_per:(t + 1) * e_per] = xe.reshape(e_per, cap, D)


def _route_call(lg, h2, n_req, n_tok):
    cap = max(1, CAPACITY_FACTOR * n_tok // N_EXPERTS)
    return pl.pallas_call(
        functools.partial(_route_kernel, cap),
        grid=(n_req,),
        in_specs=[
            pl.BlockSpec((n_tok, LANES), lambda b: (b, 0)),
            pl.BlockSpec((n_tok, D), lambda b: (b, 0)),
        ],
        out_specs=[
            pl.BlockSpec((N_EXPERTS, cap, D), lambda b: (0, b, 0)),
            pl.BlockSpec((N_EXPERTS, cap, 1), lambda b: (0, b, 0)),
            pl.BlockSpec((n_tok, LANES), lambda b: (b, 0)),
        ],
        out_shape=[
            jax.ShapeDtypeStruct((N_EXPERTS, n_req * cap, D), BF),
            jax.ShapeDtypeStruct((N_EXPERTS, n_req * cap, 1), F32),
            jax.ShapeDtypeStruct((n_req * n_tok, LANES), F32),
        ],
        scratch_shapes=[
            pltpu.VMEM((n_tok, LANES), F32),
            pltpu.VMEM((LANES, n_tok), F32),
            pltpu.VMEM((LANES, n_tok), F32),
            pltpu.VMEM((N_EXPERTS * cap, n_tok), BF),
        ],
        compiler_params=_params(1),
        name="route",
    )(lg, h2)


def _experts_kernel(xa_ref, xb_ref, ga_ref, gb_ref, wg_ref, wu_ref, wd_ref, ya_ref, yb_ref, x_s, acc_s):
    f = pl.program_id(1)
    ra = xa_ref.shape[1]

    @pl.when(f == 0)
    def _():
        x_s[0:ra, :] = xa_ref[0]
        x_s[ra:, :] = xb_ref[0]
        acc_s[...] = jnp.zeros(acc_s.shape, F32)

    x = x_s[...]
    gate = _dot(x, wg_ref[0].astype(BF))
    up = _dot(x, wu_ref[0].astype(BF))
    hid = (gate * jax.nn.sigmoid(gate) * up).astype(BF)
    acc_s[...] += _dot(hid, wd_ref[0].astype(BF))

    @pl.when(f == pl.num_programs(1) - 1)
    def _():
        ya_ref[0] = (acc_s[0:ra, :] * ga_ref[0]).astype(ya_ref.dtype)
        yb_ref[0] = (acc_s[ra:, :] * gb_ref[0]).astype(yb_ref.dtype)


def _experts_call(xa, xb, ga, gb, wg, wu, wd):
    ra, rb = xa.shape[1], xb.shape[1]
    tf = 512
    xa_spec = pl.BlockSpec((1, ra, D), lambda e, f: (e, 0, 0))
    xb_spec = pl.BlockSpec((1, rb, D), lambda e, f: (e, 0, 0))
    return pl.pallas_call(
        _experts_kernel,
        grid=(N_EXPERTS, D_EXPERT // tf),
        in_specs=[
            xa_spec, xb_spec,
            pl.BlockSpec((1, ra, 1), lambda e, f: (e, 0, 0)),
            pl.BlockSpec((1, rb, 1), lambda e, f: (e, 0, 0)),
            pl.BlockSpec((1, D, tf), lambda e, f: (e, 0, f)),
            pl.BlockSpec((1, D, tf), lambda e, f: (e, 0, f)),
            pl.BlockSpec((1, tf, D), lambda e, f: (e, f, 0)),
        ],
        out_specs=[xa_spec, xb_spec],
        out_shape=[
            jax.ShapeDtypeStruct((N_EXPERTS, ra, D), BF),
            jax.ShapeDtypeStruct((N_EXPERTS, rb, D), BF),
        ],
        scratch_shapes=[pltpu.VMEM((ra + rb, D), BF), pltpu.VMEM((ra + rb, D), F32)],
        compiler_params=_params(2),
        name="experts",
    )(xa, xb, ga, gb, wg, wu, wd)


def _scatter_kernel(cap, rank_t_ref, y_ref, x1_ref, m_ref, expand_ref, out_ref):
    gate2 = m_ref[0][:, 5 * D:6 * D]
    rank = jnp.minimum(rank_t_ref[...], float(cap)).astype(BF)
    rank_wide = _dot(rank, expand_ref[...])
    slot = (lax.broadcasted_iota(jnp.int32, (1, N_EXPERTS * cap), 1) % cap).astype(F32)
    onehot = jnp.where(rank_wide == slot, 1.0, 0.0).astype(BF)
    y = y_ref[...].reshape(N_EXPERTS * cap, D)
    out_ref[...] = x1_ref[...] + gate2 * _dot(onehot, y)


def _scatter_call(rank_t, y, x1, m3, mod_row_req, n_req, n_tok):
    cap = y.shape[1] // n_req
    e_ids = lax.broadcasted_iota(jnp.int32, (LANES, N_EXPERTS * cap), 0)
    c_ids = lax.broadcasted_iota(jnp.int32, (LANES, N_EXPERTS * cap), 1) // cap
    expand = (e_ids == c_ids).astype(BF)
    return pl.pallas_call(
        functools.partial(_scatter_kernel, cap),
        grid=(n_req,),
        in_specs=[
            pl.BlockSpec((n_tok, LANES), lambda b: (b, 0)),
            pl.BlockSpec((N_EXPERTS, cap, D), lambda b: (0, b, 0)),
            pl.BlockSpec((n_tok, D), lambda b: (b, 0)),
            pl.BlockSpec((1, 1, 6 * D), lambda b: (mod_row_req(b), 0, 0)),
            _const_spec((LANES, N_EXPERTS * cap)),
        ],
        out_specs=pl.BlockSpec((n_tok, D), lambda b: (b, 0)),
        out_shape=jax.ShapeDtypeStruct((n_req * n_tok, D), F32),
        compiler_params=_params(1),
        name="scatter",
    )(rank_t, y, x1, m3, expand)


def _rope_tables(n_tokens):
    rows = n_tokens // GRID_W
    row = jnp.broadcast_to(jnp.arange(rows, dtype=F32)[:, None], (rows, GRID_W)).reshape(-1)
    col = jnp.broadcast_to(jnp.arange(GRID_W, dtype=F32)[None, :], (rows, GRID_W)).reshape(-1)
    half = HEAD_DIM // 4
    inv_freq = ROPE_BASE ** (-jnp.arange(half, dtype=F32) / half)
    ar = row[:, None] * inv_freq
    ac = col[:, None] * inv_freq
    ang = jnp.concatenate([ar, ar, ac, ac], axis=-1)
    cos = jnp.tile(jnp.cos(ang), (1, D // HEAD_DIM))
    sin = jnp.tile(jnp.sin(ang), (1, D // HEAD_DIM))
    lane = jnp.arange(D)
    first_half = (lane % (HEAD_DIM // 2)) < (HEAD_DIM // 4)
    return cos, jnp.where(first_half[None, :], -sin, sin)


def kernel(x_prompt, x_sample, cache_k, cache_v, c, c_ctx, w_ada, b_ada, norm1_g, norm2_g, w_in, q_norm_g, k_norm_g, lambda_q1, lambda_k1, lambda_q2, lambda_k2, subln_g, gmlp_ln_g, gmlp_ln_b, w_spatial, b_spatial, w_out, w_router, w_gate_e, w_up_e, w_down_e):
    n_p, t_p = x_prompt.shape[0], x_prompt.shape[1]
    n_s, t_s = x_sample.shape[0], x_sample.shape[1]
    n_ctx = cache_k.shape[2]
    l = 0

    cond = jnp.concatenate([c_ctx[None, :], c, jnp.zeros((MOD_ROWS - 1 - n_s, D), F32)], axis=0)
    m, lamv = _ada_call(cond, w_ada[l], b_ada[l][None, :], lambda_q1[l][None, :], lambda_k1[l][None, :],
                        lambda_q2[l][None, :], lambda_k2[l][None, :])
    m3 = m.reshape(MOD_ROWS, 1, 6 * D)

    group = jnp.arange(D) // HEAD_DIM
    gsum = (group[:, None] == jnp.arange(LANES)[None, :]).astype(BF)
    wr = jnp.pad(w_router[l], ((0, 0), (0, LANES - N_EXPERTS)))
    wr_hi = wr.astype(BF)
    consts = {
        "n1g": norm1_g[l][None, :],
        "n2g": norm2_g[l][None, :],
        "w_in": w_in[l].astype(BF),
        "qg": jnp.tile(q_norm_g[l], D // HEAD_DIM)[None, :],
        "kg": jnp.tile(k_norm_g[l], D // HEAD_DIM)[None, :],
        "gsum": gsum,
        "gbcast": jnp.concatenate([gsum.T, gsum.T], axis=0),
        "lng": gmlp_ln_g[l][None, :],
        "lnb": gmlp_ln_b[l][None, :],
        "wsp": w_spatial[l].astype(BF),
        "bsp": jnp.repeat(b_spatial[l].T, D // N_GROUPS, axis=1),
        "w_out": w_out[l].astype(BF),
        "wr_hi": wr_hi,
        "wr_lo": (wr - wr_hi.astype(F32)).astype(BF),
    }
    sg = subln_g[l][None, :]

    def prompt_row(i, tm):
        return 0

    def sample_row(i, tm):
        return 1 + i // (t_s // tm)

    outs = []
    kv_out = None
    for x, n_req, n_tok, mod_row, is_sample in (
            (x_prompt, n_p, t_p, prompt_row, False), (x_sample, n_s, t_s, sample_row, True)):
        xf = x.reshape(n_req * n_tok, D)
        tabs = _rope_tables(n_tok) if is_sample else None
        q, k, v, sga, tb = _inproj_call(xf, m3, mod_row, tabs, BF if is_sample else F32, consts)
        if is_sample:
            ctx = (cache_k[:, l].reshape(n_req * n_ctx, D), cache_v[:, l].reshape(n_req * n_ctx, D))
        else:
            ctx = None
            kv_out = (k, v)
        oa = _attn_call(q, k, v, ctx, lamv, sg, n_req, n_tok)
        x1, h2, lg = _outproj_call(oa, sga, tb, xf, m3, mod_row, consts)
        xe, gates, rank_t = _route_call(lg, h2, n_req, n_tok)
        outs.append((x1, xe, gates, rank_t, n_req, n_tok, is_sample))

    (x1p, xep, gp, rtp, _, _, _), (x1s, xes, gs, rts, _, _, _) = outs
    yp, ys = _experts_call(xep, xes, gp, gs, w_gate_e[l], w_up_e[l], w_down_e[l])
    y_prompt = _scatter_call(rtp, yp, x1p, m3, lambda b: 0, n_p, t_p).reshape(x_prompt.shape)
    y_sample = _scatter_call(rts, ys, x1s, m3, lambda b: 1 + b, n_s, t_s).reshape(x_sample.shape)
    new_k = kv_out[0].reshape(n_p, 1, t_p, N_HEADS, 2, HEAD_DIM)
    new_v = kv_out[1].reshape(n_p, 1, t_p, N_HEADS, V_HEAD_DIM)
    return (y_prompt, y_sample, new_k, new_v)
```

```python
import functools
import math

import jax
import jax.numpy as jnp
from jax import lax
from jax.experimental import pallas as pl
from jax.experimental.pallas import tpu as pltpu

D = 1024
N_HEADS = 8
HEAD_DIM = 64
V_HEAD_DIM = 128
GRID_W = 64
ROPE_BASE = 10000.0
CHUNK = 128
N_GROUPS = 8
N_EXPERTS = 16
CAPACITY_FACTOR = 2
D_EXPERT = 2048
N_SEG = 7
EPS = 1e-6
LAMBDA_INIT = 0.8 - 0.6 * math.exp(-0.3 * 0)

LANES = 128
MOD_ROWS = 8
VMEM_LIMIT = 56 * 1024 * 1024

BF = jnp.bfloat16
F32 = jnp.float32


def _dot(a, b):
    return jnp.dot(a, b, preferred_element_type=F32)


def _dot_nt(a, b):
    return lax.dot_general(a, b, (((1,), (1,)), ((), ())), preferred_element_type=F32)


def _split_bf16(x):
    hi = x.astype(BF)
    lo = (x - hi.astype(F32)).astype(BF)
    return hi, lo


def _params(n_grid_dims):
    return pltpu.CompilerParams(
        dimension_semantics=("arbitrary",) * n_grid_dims, vmem_limit_bytes=VMEM_LIMIT)


def _const_spec(shape):
    nd = len(shape)
    return pl.BlockSpec(shape, lambda *_: (0,) * nd)


def _ada_kernel(cond_ref, w_ref, b_ref, lq1_ref, lk1_ref, lq2_ref, lk2_ref, m_ref, lam_ref):
    c = cond_ref[...]
    a = c * jax.nn.sigmoid(c)
    a_hi, a_lo = _split_bf16(a)
    w_hi, w_lo = _split_bf16(w_ref[...])
    m_ref[...] = _dot(a_hi, w_hi) + _dot(a_lo, w_hi) + _dot(a_hi, w_lo) + b_ref[...]
    s1 = jnp.sum(lq1_ref[...] * lk1_ref[...], axis=-1, keepdims=True)
    s2 = jnp.sum(lq2_ref[...] * lk2_ref[...], axis=-1, keepdims=True)
    lam = jnp.exp(s1) - jnp.exp(s2) + LAMBDA_INIT
    lam_ref[...] = jnp.broadcast_to(lam, lam_ref.shape)


def _ada_call(cond, w_ada, b_ada, lq1, lk1, lq2, lk2):
    tn = D
    vec = pl.BlockSpec((1, HEAD_DIM), lambda j: (0, 0))
    return pl.pallas_call(
        _ada_kernel,
        grid=(6 * D // tn,),
        in_specs=[
            pl.BlockSpec((MOD_ROWS, D), lambda j: (0, 0)),
            pl.BlockSpec((D, tn), lambda j: (0, j)),
            pl.BlockSpec((1, tn), lambda j: (0, j)),
            vec, vec, vec, vec,
        ],
        out_specs=[
            pl.BlockSpec((MOD_ROWS, tn), lambda j: (0, j)),
            pl.BlockSpec((MOD_ROWS, LANES), lambda j: (0, 0)),
        ],
        out_shape=[
            jax.ShapeDtypeStruct((MOD_ROWS, 6 * D), F32),
            jax.ShapeDtypeStruct((MOD_ROWS, LANES), F32),
        ],
        compiler_params=_params(1),
        name="ada",
    )(cond, w_ada, b_ada, lq1, lk1, lq2, lk2)


def _rms(x, g):
    return x * lax.rsqrt(jnp.mean(x * x, axis=-1, keepdims=True) + EPS) * g


def _head_norm(t, g, gsum, gbcast):
    ss = _dot((t * t).astype(BF), gsum)
    inv = lax.rsqrt(ss * (1.0 / HEAD_DIM) + EPS)
    hi, lo = _split_bf16(inv)
    bc = _dot(jnp.concatenate([hi, lo], axis=-1), gbcast)
    return t * bc * g


def _rope(t, cos, sin_signed, first_half):
    q = HEAD_DIM // 4
    fwd = pltpu.roll(t, D - q, axis=1)
    bwd = pltpu.roll(t, q, axis=1)
    return t * cos + jnp.where(first_half, fwd, bwd) * sin_signed


def _rope_t(t, cos_t, sin_signed_t):
    n = t.shape[1]
    q = HEAD_DIM // 4
    t3 = t.reshape(D // HEAD_DIM, HEAD_DIM, n)
    rot = jnp.concatenate([t3[:, q:2 * q], t3[:, 0:q], t3[:, 3 * q:4 * q], t3[:, 2 * q:3 * q]], axis=1)
    return t * cos_t + rot.reshape(D, n) * sin_signed_t


def _inproj_kernel(rope, x_ref, m_ref, n1g_ref, w_ref, wkt_ref, qg_ref, kgt_ref, gsum_ref, gbcast_ref,
                   lng_ref, lnb_ref, wsp_ref, bsp_ref, *rest):
    if rope:
        cos_ref, sin_ref, cos_t_ref, sin_t_ref, q_out, kt_out, v_out, sga_out, tb_out = rest
    else:
        q_out, kt_out, v_out, sga_out, tb_out = rest
    tm = x_ref.shape[0]
    m = m_ref[0]
    shift1, scale1 = m[:, 0:D], m[:, D:2 * D]
    h = (_rms(x_ref[...], n1g_ref[...]) * (1.0 + scale1) + shift1).astype(BF)

    def seg(j):
        return _dot(h, w_ref[:, j * D:(j + 1) * D])

    q = _head_norm(seg(0), qg_ref[...], gsum_ref[...], gbcast_ref[...])
    if rope:
        lane = lax.broadcasted_iota(jnp.int32, (1, D), 1)
        q = _rope(q, cos_ref[...], sin_ref[...], (lane % (HEAD_DIM // 2)) < (HEAD_DIM // 4))
    q_out[...] = q.astype(q_out.dtype)

    kt = _dot_nt(wkt_ref[...], h)
    k3 = kt.reshape(D // HEAD_DIM, HEAD_DIM, tm)
    inv = lax.rsqrt(jnp.mean(k3 * k3, axis=1, keepdims=True) + EPS)
    kt = (k3 * inv).reshape(D, tm) * kgt_ref[...]
    if rope:
        kt = _rope_t(kt, cos_t_ref[...], sin_t_ref[...])
    kt_out[0] = kt.astype(kt_out.dtype)

    v_out[...] = seg(2).astype(v_out.dtype)
    sga_out[...] = jax.nn.sigmoid(seg(5)).astype(sga_out.dtype)

    zv = jax.nn.gelu(seg(4))
    mu = jnp.mean(zv, axis=-1, keepdims=True)
    zc = zv - mu
    var = jnp.mean(zc * zc, axis=-1, keepdims=True)
    zvn = (zc * lax.rsqrt(var + EPS) * lng_ref[...] + lnb_ref[...]).astype(BF)
    pre = jax.nn.gelu(seg(3)) * jax.nn.sigmoid(seg(6))
    for c in range(tm // CHUNK):
        rows = slice(c * CHUNK, (c + 1) * CHUNK)
        for g in range(N_GROUPS):
            cols = slice(g * LANES, (g + 1) * LANES)
            sp = _dot(wsp_ref[g], zvn[rows, cols]) + bsp_ref[:, cols]
            tb_out[rows, cols] = (pre[rows, cols] * sp).astype(tb_out.dtype)


TM = 256


def _inproj_call(x, m3, mod_row, rope_tabs, kv_dtype, consts, n_req, n_tok):
    t = x.shape[0]
    tm = TM
    per_req = n_tok // tm
    rope = rope_tabs is not None
    tok = pl.BlockSpec((tm, D), lambda i: (i, 0))
    in_specs = [
        tok,
        pl.BlockSpec((1, 1, 6 * D), lambda i: (mod_row(i, tm), 0, 0)),
        _const_spec((1, D)),
        pl.BlockSpec((D, N_SEG * D), lambda i: (0, 0), pipeline_mode=pl.Buffered(1)),
        _const_spec((D, D)),
        _const_spec((1, D)), _const_spec((D, tm)),
        _const_spec((D, LANES)), _const_spec((2 * LANES, D)),
        _const_spec((1, D)), _const_spec((1, D)),
        _const_spec((N_GROUPS, CHUNK, CHUNK)), _const_spec((CHUNK, D)),
    ]
    args = [x, m3, consts["n1g"], consts["w_in"], consts["wkt"], consts["qg"], consts["kgt"], consts["gsum"],
            consts["gbcast"], consts["lng"], consts["lnb"], consts["wsp"], consts["bsp"]]
    if rope:
        tab = pl.BlockSpec((tm, D), lambda i: (i % per_req, 0))
        tab_t = pl.BlockSpec((D, tm), lambda i: (0, i % per_req))
        in_specs += [tab, tab, tab_t, tab_t]
        args += list(rope_tabs)
    kt_spec = pl.BlockSpec((1, D, tm), lambda i: (i // per_req, 0, i % per_req))
    return pl.pallas_call(
        functools.partial(_inproj_kernel, rope),
        grid=(t // tm,),
        in_specs=in_specs,
        out_specs=[tok, kt_spec, tok, tok, tok],
        out_shape=[
            jax.ShapeDtypeStruct((t, D), BF),
            jax.ShapeDtypeStruct((n_req, D, n_tok), kv_dtype),
            jax.ShapeDtypeStruct((t, D), kv_dtype),
            jax.ShapeDtypeStruct((t, D), BF),
            jax.ShapeDtypeStruct((t, D), BF),
        ],
        compiler_params=_params(1),
        name="inproj_rope" if rope else "inproj",
    )(*args)


def _attn_kernel(has_ctx, q_ref, kt_ref, v_ref, *rest):
    if has_ctx:
        kct_ref, vc_ref, lam_ref, sg_ref, o_ref = rest
    else:
        lam_ref, sg_ref, o_ref = rest
    tq = q_ref.shape[0]
    lam = lam_ref[0:1, 0:1]
    first = lax.broadcasted_iota(jnp.int32, (1, LANES), 1) < HEAD_DIM
    zero = jnp.zeros((), BF)
    for h in range(N_HEADS):
        cols = slice(h * LANES, (h + 1) * LANES)
        q = q_ref[:, cols]
        qz = jnp.concatenate([jnp.where(first, q, zero), jnp.where(first, zero, q)], axis=0)
        scores = [_dot(qz, kt_ref[0, cols, :].astype(BF))]
        vals = [v_ref[:, cols].astype(BF)]
        if has_ctx:
            scores.append(_dot(qz, kct_ref[0, cols, :].astype(BF)))
            vals.append(vc_ref[:, cols].astype(BF))
        mx = functools.reduce(jnp.maximum, [jnp.max(s, axis=-1, keepdims=True) for s in scores])
        exps = [jnp.exp2(s - mx) for s in scores]
        den = functools.reduce(jnp.add, [jnp.sum(e, axis=-1, keepdims=True) for e in exps])
        ob = functools.reduce(jnp.add, [_dot(e.astype(BF), v) for e, v in zip(exps, vals)])
        o = ob[:tq] * (1.0 / den[:tq]) - ob[tq:] * (lam / den[tq:])
        o = _rms(o, sg_ref[...]) * (1.0 - LAMBDA_INIT)
        o_ref[:, cols] = o.astype(o_ref.dtype)


def _attn_call(q, kt, v, ctx, lamv, subln_g, n_req, n_tok):
    tq = 256
    nqb = n_tok // tq
    has_ctx = ctx is not None
    in_specs = [
        pl.BlockSpec((tq, D), lambda b, i: (b * nqb + i, 0)),
        pl.BlockSpec((1, D, n_tok), lambda b, i: (b, 0, 0)),
        pl.BlockSpec((n_tok, D), lambda b, i: (b, 0)),
    ]
    args = [q, kt, v]
    if has_ctx:
        n_ctx = ctx[0].shape[2]
        in_specs += [pl.BlockSpec((1, D, n_ctx), lambda b, i: (b, 0, 0)),
                     pl.BlockSpec((n_ctx, D), lambda b, i: (b, 0))]
        args += list(ctx)
    in_specs += [_const_spec((MOD_ROWS, LANES)), _const_spec((1, V_HEAD_DIM))]
    args += [lamv, subln_g]
    return pl.pallas_call(
        functools.partial(_attn_kernel, has_ctx),
        grid=(n_req, nqb),
        in_specs=in_specs,
        out_specs=pl.BlockSpec((tq, D), lambda b, i: (b * nqb + i, 0)),
        out_shape=jax.ShapeDtypeStruct((n_req * n_tok, D), BF),
        compiler_params=_params(2),
        name="attn_ctx" if has_ctx else "attn",
    )(*args)


def _outproj_kernel(oa_ref, sga_ref, tb_ref, x_ref, m_ref, wo_ref, n2g_ref, wr_hi_ref, wr_lo_ref,
                    x1_out, h2_out, lg_out):
    m = m_ref[0]
    gate1, shift2, scale2 = m[:, 2 * D:3 * D], m[:, 3 * D:4 * D], m[:, 4 * D:5 * D]
    merged = (sga_ref[...].astype(F32) * oa_ref[...].astype(F32) + tb_ref[...].astype(F32)).astype(BF)
    x1 = x_ref[...] + gate1 * _dot(merged, wo_ref[...])
    x1_out[...] = x1
    h2 = _rms(x1, n2g_ref[...]) * (1.0 + scale2) + shift2
    h2_out[...] = h2.astype(BF)
    hi, lo = _split_bf16(h2)
    wr_hi = wr_hi_ref[...]
    lg_out[...] = _dot(hi, wr_hi) + _dot(lo, wr_hi) + _dot(hi, wr_lo_ref[...])


def _outproj_call(oa, sga, tb, x, m3, mod_row, consts):
    t = x.shape[0]
    tm = 256
    tok = pl.BlockSpec((tm, D), lambda i: (i, 0))
    return pl.pallas_call(
        _outproj_kernel,
        grid=(t // tm,),
        in_specs=[
            tok, tok, tok, tok,
            pl.BlockSpec((1, 1, 6 * D), lambda i: (mod_row(i, tm), 0, 0)),
            _const_spec((D, D)), _const_spec((1, D)),
            _const_spec((D, LANES)), _const_spec((D, LANES)),
        ],
        out_specs=[tok, tok, pl.BlockSpec((tm, LANES), lambda i: (i, 0))],
        out_shape=[
            jax.ShapeDtypeStruct((t, D), F32),
            jax.ShapeDtypeStruct((t, D), BF),
            jax.ShapeDtypeStruct((t, LANES), F32),
        ],
        compiler_params=_params(1),
        name="outproj",
    )(oa, sga, tb, x, m3, consts["w_out"], consts["n2g"], consts["wr_hi"], consts["wr_lo"])


RANK_TILE = 256
GATHER_ROWS = 512


def _route_kernel(cap, lg_ref, h2_ref, xe_ref, gate_ref, rank_t_ref, aff_s, aff_t_s, rank_s, p_s):
    n = lg_ref.shape[0]
    lane = lax.broadcasted_iota(jnp.int32, (1, LANES), 1)
    valid = lane < N_EXPERTS
    lg = jnp.where(valid, lg_ref[...], -1e30)
    ex = jnp.where(valid, jnp.exp(lg - jnp.max(lg, axis=-1, keepdims=True)), 0.0)
    aff = ex / jnp.sum(ex, axis=-1, keepdims=True)
    aff_s[...] = aff
    aff_t_s[...] = aff.T
    rank_s[...] = jnp.zeros(rank_s.shape, F32)
    earlier = (lax.broadcasted_iota(jnp.int32, (RANK_TILE, 1), 0)
               < lax.broadcasted_iota(jnp.int32, (1, RANK_TILE), 1)).astype(F32)
    slot = lax.broadcasted_iota(jnp.int32, (cap, 1), 0).astype(F32)

    def per_expert(e, carry):
        a_row = aff_t_s[pl.ds(e, 1), :]
        rank = jnp.zeros((1, n), F32)
        for t in range(n // RANK_TILE):
            lo, hi = t * RANK_TILE, (t + 1) * RANK_TILE
            a_col = jnp.sum(jnp.where(lane == e, aff_s[lo:hi, :], 0.0), axis=-1, keepdims=True)
            parts = []
            if lo > 0:
                parts.append(jnp.where(a_col > a_row[:, :lo], 1.0, 0.0))
            a_diag = a_row[:, lo:hi]
            parts.append(jnp.where(a_col > a_diag, 1.0, jnp.where(a_col == a_diag, earlier, 0.0)))
            if hi < n:
                parts.append(jnp.where(a_col >= a_row[:, hi:], 1.0, 0.0))
            counts = [jnp.sum(p, axis=0, keepdims=True) for p in parts]
            rank = rank + (counts[0] if len(counts) == 1 else jnp.concatenate(counts, axis=1))
        rank_s[pl.ds(e, 1), :] = rank
        chosen = rank == slot
        p_s[pl.ds(pl.multiple_of(e * cap, cap), cap), :] = jnp.where(chosen, 1.0, 0.0).astype(BF)
        gate_ref[e] = jnp.sum(jnp.where(chosen, a_row, 0.0), axis=-1, keepdims=True)
        return carry

    lax.fori_loop(0, N_EXPERTS, per_expert, 0)
    rank_t_ref[...] = rank_s[...].T
    h2 = h2_ref[...]
    rows = min(GATHER_ROWS, N_EXPERTS * cap)
    e_per = rows // cap
    for t in range(N_EXPERTS * cap // rows):
        xe = _dot(p_s[t * rows:(t + 1) * rows, :], h2).astype(BF)
        xe_ref[t * e_per:(t + 1) * e_per] = xe.reshape(e_per, cap, D)


def _route_call(lg, h2, n_req, n_tok):
    cap = max(1, CAPACITY_FACTOR * n_tok // N_EXPERTS)
    return pl.pallas_call(
        functools.partial(_route_kernel, cap),
        grid=(n_req,),
        in_specs=[
            pl.BlockSpec((n_tok, LANES), lambda b: (b, 0)),
            pl.BlockSpec((n_tok, D), lambda b: (b, 0)),
        ],
        out_specs=[
            pl.BlockSpec((N_EXPERTS, cap, D), lambda b: (0, b, 0)),
            pl.BlockSpec((N_EXPERTS, cap, 1), lambda b: (0, b, 0)),
            pl.BlockSpec((n_tok, LANES), lambda b: (b, 0)),
        ],
        out_shape=[
            jax.ShapeDtypeStruct((N_EXPERTS, n_req * cap, D), BF),
            jax.ShapeDtypeStruct((N_EXPERTS, n_req * cap, 1), F32),
            jax.ShapeDtypeStruct((n_req * n_tok, LANES), F32),
        ],
        scratch_shapes=[
            pltpu.VMEM((n_tok, LANES), F32),
            pltpu.VMEM((LANES, n_tok), F32),
            pltpu.VMEM((LANES, n_tok), F32),
            pltpu.VMEM((N_EXPERTS * cap, n_tok), BF),
        ],
        compiler_params=_params(1),
        name="route",
    )(lg, h2)


def _experts_kernel(xa_ref, xb_ref, ga_ref, gb_ref, wg_ref, wu_ref, wd_ref, ya_ref, yb_ref, x_s, acc_s):
    f = pl.program_id(1)
    ra = xa_ref.shape[1]

    @pl.when(f == 0)
    def _():
        x_s[0:ra, :] = xa_ref[0]
        x_s[ra:, :] = xb_ref[0]
        acc_s[...] = jnp.zeros(acc_s.shape, F32)

    x = x_s[...]
    gate = _dot(x, wg_ref[0].astype(BF))
    up = _dot(x, wu_ref[0].astype(BF))
    hid = (gate * jax.nn.sigmoid(gate) * up).astype(BF)
    acc_s[...] += _dot(hid, wd_ref[0].astype(BF))

    @pl.when(f == pl.num_programs(1) - 1)
    def _():
        ya_ref[0] = (acc_s[0:ra, :] * ga_ref[0]).astype(ya_ref.dtype)
        yb_ref[0] = (acc_s[ra:, :] * gb_ref[0]).astype(yb_ref.dtype)


def _experts_call(xa, xb, ga, gb, wg, wu, wd):
    ra, rb = xa.shape[1], xb.shape[1]
    tf = 512
    xa_spec = pl.BlockSpec((1, ra, D), lambda e, f: (e, 0, 0))
    xb_spec = pl.BlockSpec((1, rb, D), lambda e, f: (e, 0, 0))
    return pl.pallas_call(
        _experts_kernel,
        grid=(N_EXPERTS, D_EXPERT // tf),
        in_specs=[
            xa_spec, xb_spec,
            pl.BlockSpec((1, ra, 1), lambda e, f: (e, 0, 0)),
            pl.BlockSpec((1, rb, 1), lambda e, f: (e, 0, 0)),
            pl.BlockSpec((1, D, tf), lambda e, f: (e, 0, f)),
            pl.BlockSpec((1, D, tf), lambda e, f: (e, 0, f)),
            pl.BlockSpec((1, tf, D), lambda e, f: (e, f, 0)),
        ],
        out_specs=[xa_spec, xb_spec],
        out_shape=[
            jax.ShapeDtypeStruct((N_EXPERTS, ra, D), BF),
            jax.ShapeDtypeStruct((N_EXPERTS, rb, D), BF),
        ],
        scratch_shapes=[pltpu.VMEM((ra + rb, D), BF), pltpu.VMEM((ra + rb, D), F32)],
        compiler_params=_params(2),
        name="experts",
    )(xa, xb, ga, gb, wg, wu, wd)


def _scatter_kernel(cap, rank_t_ref, y_ref, x1_ref, m_ref, expand_ref, out_ref):
    gate2 = m_ref[0][:, 5 * D:6 * D]
    rank = jnp.minimum(rank_t_ref[...], float(cap)).astype(BF)
    rank_wide = _dot(rank, expand_ref[...])
    slot = (lax.broadcasted_iota(jnp.int32, (1, N_EXPERTS * cap), 1) % cap).astype(F32)
    onehot = jnp.where(rank_wide == slot, 1.0, 0.0).astype(BF)
    y = y_ref[...].reshape(N_EXPERTS * cap, D)
    out_ref[...] = x1_ref[...] + gate2 * _dot(onehot, y)


def _scatter_call(rank_t, y, x1, m3, mod_row_req, n_req, n_tok):
    cap = y.shape[1] // n_req
    e_ids = lax.broadcasted_iota(jnp.int32, (LANES, N_EXPERTS * cap), 0)
    c_ids = lax.broadcasted_iota(jnp.int32, (LANES, N_EXPERTS * cap), 1) // cap
    expand = (e_ids == c_ids).astype(BF)
    return pl.pallas_call(
        functools.partial(_scatter_kernel, cap),
        grid=(n_req,),
        in_specs=[
            pl.BlockSpec((n_tok, LANES), lambda b: (b, 0)),
            pl.BlockSpec((N_EXPERTS, cap, D), lambda b: (0, b, 0)),
            pl.BlockSpec((n_tok, D), lambda b: (b, 0)),
            pl.BlockSpec((1, 1, 6 * D), lambda b: (mod_row_req(b), 0, 0)),
            _const_spec((LANES, N_EXPERTS * cap)),
        ],
        out_specs=pl.BlockSpec((n_tok, D), lambda b: (b, 0)),
        out_shape=jax.ShapeDtypeStruct((n_req * n_tok, D), F32),
        compiler_params=_params(1),
        name="scatter",
    )(rank_t, y, x1, m3, expand)


def _rope_tables(n_tokens):
    rows = n_tokens // GRID_W
    row = jnp.broadcast_to(jnp.arange(rows, dtype=F32)[:, None], (rows, GRID_W)).reshape(-1)
    col = jnp.broadcast_to(jnp.arange(GRID_W, dtype=F32)[None, :], (rows, GRID_W)).reshape(-1)
    half = HEAD_DIM // 4
    inv_freq = ROPE_BASE ** (-jnp.arange(half, dtype=F32) / half)
    ar = row[:, None] * inv_freq
    ac = col[:, None] * inv_freq
    ang = jnp.concatenate([ar, ar, ac, ac], axis=-1)
    cos = jnp.tile(jnp.cos(ang), (1, D // HEAD_DIM))
    sin = jnp.tile(jnp.sin(ang), (1, D // HEAD_DIM))
    lane = jnp.arange(D)
    first_half = (lane % (HEAD_DIM // 2)) < (HEAD_DIM // 4)
    sin_signed = jnp.where(first_half[None, :], -sin, sin)
    return cos, sin_signed, cos.T, sin_signed.T


def kernel(x_prompt, x_sample, cache_k, cache_v, c, c_ctx, w_ada, b_ada, norm1_g, norm2_g, w_in, q_norm_g, k_norm_g, lambda_q1, lambda_k1, lambda_q2, lambda_k2, subln_g, gmlp_ln_g, gmlp_ln_b, w_spatial, b_spatial, w_out, w_router, w_gate_e, w_up_e, w_down_e):
    n_p, t_p = x_prompt.shape[0], x_prompt.shape[1]
    n_s, t_s = x_sample.shape[0], x_sample.shape[1]
    n_ctx = cache_k.shape[2]
    l = 0

    cond = jnp.concatenate([c_ctx[None, :], c, jnp.zeros((MOD_ROWS - 1 - n_s, D), F32)], axis=0)
    m, lamv = _ada_call(cond, w_ada[l], b_ada[l][None, :], lambda_q1[l][None, :], lambda_k1[l][None, :],
                        lambda_q2[l][None, :], lambda_k2[l][None, :])
    m3 = m.reshape(MOD_ROWS, 1, 6 * D)

    group = jnp.arange(D) // HEAD_DIM
    gsum = (group[:, None] == jnp.arange(LANES)[None, :]).astype(BF)
    wr = jnp.pad(w_router[l], ((0, 0), (0, LANES - N_EXPERTS)))
    wr_hi = wr.astype(BF)
    consts = {
        "n1g": norm1_g[l][None, :],
        "n2g": norm2_g[l][None, :],
        "w_in": w_in[l].astype(BF),
        "wkt": w_in[l][:, D:2 * D].T.astype(BF),
        "qg": jnp.tile(q_norm_g[l] * (HEAD_DIM ** -0.5 * math.log2(math.e)), D // HEAD_DIM)[None, :],
        "kgt": jnp.broadcast_to(jnp.tile(k_norm_g[l], D // HEAD_DIM)[:, None], (D, TM)),
        "gsum": gsum,
        "gbcast": jnp.concatenate([gsum.T, gsum.T], axis=0),
        "lng": gmlp_ln_g[l][None, :],
        "lnb": gmlp_ln_b[l][None, :],
        "wsp": w_spatial[l].astype(BF),
        "bsp": jnp.repeat(b_spatial[l].T, D // N_GROUPS, axis=1),
        "w_out": w_out[l].astype(BF),
        "wr_hi": wr_hi,
        "wr_lo": (wr - wr_hi.astype(F32)).astype(BF),
    }
    sg = subln_g[l][None, :]

    def prompt_row(i, tm):
        return 0

    def sample_row(i, tm):
        return 1 + i // (t_s // tm)

    outs = []
    kv_out = None
    for x, n_req, n_tok, mod_row, is_sample in (
            (x_prompt, n_p, t_p, prompt_row, False), (x_sample, n_s, t_s, sample_row, True)):
        xf = x.reshape(n_req * n_tok, D)
        tabs = _rope_tables(n_tok) if is_sample else None
        q, kt, v, sga, tb = _inproj_call(xf, m3, mod_row, tabs, BF if is_sample else F32, consts, n_req, n_tok)
        if is_sample:
            ctx = (jnp.transpose(cache_k[:, l], (0, 2, 3, 4, 1)).reshape(n_req, D, n_ctx),
                   cache_v[:, l].reshape(n_req * n_ctx, D))
        else:
            ctx = None
            kv_out = (kt, v)
        oa = _attn_call(q, kt, v, ctx, lamv, sg, n_req, n_tok)
        x1, h2, lg = _outproj_call(oa, sga, tb, xf, m3, mod_row, consts)
        xe, gates, rank_t = _route_call(lg, h2, n_req, n_tok)
        outs.append((x1, xe, gates, rank_t, n_req, n_tok, is_sample))

    (x1p, xep, gp, rtp, _, _, _), (x1s, xes, gs, rts, _, _, _) = outs
    yp, ys = _experts_call(xep, xes, gp, gs, w_gate_e[l], w_up_e[l], w_down_e[l])
    y_prompt = _scatter_call(rtp, yp, x1p, m3, lambda b: 0, n_p, t_p).reshape(x_prompt.shape)
    y_sample = _scatter_call(rts, ys, x1s, m3, lambda b: 1 + b, n_s, t_s).reshape(x_sample.shape)
    new_k = jnp.transpose(kv_out[0].reshape(n_p, N_HEADS, 2, HEAD_DIM, t_p), (0, 4, 1, 2, 3))
    new_k = new_k.reshape(n_p, 1, t_p, N_HEADS, 2, HEAD_DIM)
    new_v = kv_out[1].reshape(n_p, 1, t_p, N_HEADS, V_HEAD_DIM)
    return (y_prompt, y_sample, new_k, new_v)
```

```python
import functools
import math

import jax
import jax.numpy as jnp
from jax import lax
from jax.experimental import pallas as pl
from jax.experimental.pallas import tpu as pltpu

D = 1024
N_HEADS = 8
HEAD_DIM = 64
V_HEAD_DIM = 128
GRID_W = 64
ROPE_BASE = 10000.0
CHUNK = 128
N_GROUPS = 8
N_EXPERTS = 16
CAPACITY_FACTOR = 2
D_EXPERT = 2048
N_SEG = 7
EPS = 1e-6
LAMBDA_INIT = 0.8 - 0.6 * math.exp(-0.3 * 0)

LANES = 128
MOD_ROWS = 8
VMEM_LIMIT = 56 * 1024 * 1024

BF = jnp.bfloat16
F32 = jnp.float32


def _dot(a, b):
    return jnp.dot(a, b, preferred_element_type=F32)


def _dot_nt(a, b):
    return lax.dot_general(a, b, (((1,), (1,)), ((), ())), preferred_element_type=F32)


def _split_bf16(x):
    hi = x.astype(BF)
    lo = (x - hi.astype(F32)).astype(BF)
    return hi, lo


def _params(n_grid_dims):
    return pltpu.CompilerParams(
        dimension_semantics=("arbitrary",) * n_grid_dims, vmem_limit_bytes=VMEM_LIMIT)


def _const_spec(shape):
    nd = len(shape)
    return pl.BlockSpec(shape, lambda *_: (0,) * nd)


def _ada_kernel(cond_ref, w_ref, b_ref, lq1_ref, lk1_ref, lq2_ref, lk2_ref, m_ref, lam_ref):
    c = cond_ref[...]
    a = c * jax.nn.sigmoid(c)
    a_hi, a_lo = _split_bf16(a)
    w_hi, w_lo = _split_bf16(w_ref[...])
    m_ref[...] = _dot(a_hi, w_hi) + _dot(a_lo, w_hi) + _dot(a_hi, w_lo) + b_ref[...]
    s1 = jnp.sum(lq1_ref[...] * lk1_ref[...], axis=-1, keepdims=True)
    s2 = jnp.sum(lq2_ref[...] * lk2_ref[...], axis=-1, keepdims=True)
    lam = jnp.exp(s1) - jnp.exp(s2) + LAMBDA_INIT
    lam_ref[...] = jnp.broadcast_to(lam, lam_ref.shape)


def _ada_call(cond, w_ada, b_ada, lq1, lk1, lq2, lk2):
    tn = D
    vec = pl.BlockSpec((1, HEAD_DIM), lambda j: (0, 0))
    return pl.pallas_call(
        _ada_kernel,
        grid=(6 * D // tn,),
        in_specs=[
            pl.BlockSpec((MOD_ROWS, D), lambda j: (0, 0)),
            pl.BlockSpec((D, tn), lambda j: (0, j)),
            pl.BlockSpec((1, tn), lambda j: (0, j)),
            vec, vec, vec, vec,
        ],
        out_specs=[
            pl.BlockSpec((MOD_ROWS, tn), lambda j: (0, j)),
            pl.BlockSpec((MOD_ROWS, LANES), lambda j: (0, 0)),
        ],
        out_shape=[
            jax.ShapeDtypeStruct((MOD_ROWS, 6 * D), F32),
            jax.ShapeDtypeStruct((MOD_ROWS, LANES), F32),
        ],
        compiler_params=_params(1),
        name="ada",
    )(cond, w_ada, b_ada, lq1, lk1, lq2, lk2)


K_SEG = 1


def _wcast_kernel(w_ref, wb_ref, wkt_ref):
    w = w_ref[...]
    wb_ref[...] = w.astype(BF)

    @pl.when(pl.program_id(0) == K_SEG)
    def _():
        wkt_ref[...] = w.T.astype(BF)


def _wcast_call(w_in):
    return pl.pallas_call(
        _wcast_kernel,
        grid=(N_SEG,),
        in_specs=[pl.BlockSpec((D, D), lambda j: (0, j))],
        out_specs=[pl.BlockSpec((D, D), lambda j: (0, j)), pl.BlockSpec((D, D), lambda j: (0, 0))],
        out_shape=[jax.ShapeDtypeStruct((D, N_SEG * D), BF), jax.ShapeDtypeStruct((D, D), BF)],
        compiler_params=_params(1),
        name="wcast",
    )(w_in)


def _rms(x, g):
    return x * lax.rsqrt(jnp.mean(x * x, axis=-1, keepdims=True) + EPS) * g


def _head_norm(t, g, gsum, gbcast):
    ss = _dot((t * t).astype(BF), gsum)
    inv = lax.rsqrt(ss * (1.0 / HEAD_DIM) + EPS)
    hi, lo = _split_bf16(inv)
    bc = _dot(jnp.concatenate([hi, lo], axis=-1), gbcast)
    return t * bc * g


def _rope(t, cos, sin_signed, first_half):
    q = HEAD_DIM // 4
    fwd = pltpu.roll(t, D - q, axis=1)
    bwd = pltpu.roll(t, q, axis=1)
    return t * cos + jnp.where(first_half, fwd, bwd) * sin_signed


def _rope_t(t, cos_t, sin_signed_t):
    n = t.shape[1]
    q = HEAD_DIM // 4
    t3 = t.reshape(D // HEAD_DIM, HEAD_DIM, n)
    rot = jnp.concatenate([t3[:, q:2 * q], t3[:, 0:q], t3[:, 3 * q:4 * q], t3[:, 2 * q:3 * q]], axis=1)
    return t * cos_t + rot.reshape(D, n) * sin_signed_t


def _inproj_kernel(rope, x_ref, m_ref, n1g_ref, w_ref, wkt_ref, qg_ref, kgt_ref, gsum_ref, gbcast_ref,
                   lng_ref, lnb_ref, wsp_ref, bsp_ref, *rest):
    if rope:
        cos_ref, sin_ref, cos_t_ref, sin_t_ref, q_out, kt_out, v_out, sga_out, tb_out = rest
    else:
        q_out, kt_out, v_out, sga_out, tb_out = rest
    tm = x_ref.shape[0]
    m = m_ref[0]
    shift1, scale1 = m[:, 0:D], m[:, D:2 * D]
    h = (_rms(x_ref[...], n1g_ref[...]) * (1.0 + scale1) + shift1).astype(BF)

    def seg(j):
        return _dot(h, w_ref[:, j * D:(j + 1) * D])

    q = _head_norm(seg(0), qg_ref[...], gsum_ref[...], gbcast_ref[...])
    if rope:
        lane = lax.broadcasted_iota(jnp.int32, (1, D), 1)
        q = _rope(q, cos_ref[...], sin_ref[...], (lane % (HEAD_DIM // 2)) < (HEAD_DIM // 4))
    q_out[...] = q.astype(q_out.dtype)

    kt = _dot_nt(wkt_ref[...], h)
    k3 = kt.reshape(D // HEAD_DIM, HEAD_DIM, tm)
    inv = lax.rsqrt(jnp.mean(k3 * k3, axis=1, keepdims=True) + EPS)
    kt = (k3 * inv).reshape(D, tm) * kgt_ref[...]
    if rope:
        kt = _rope_t(kt, cos_t_ref[...], sin_t_ref[...])
    kt_out[0] = kt.astype(kt_out.dtype)

    v_out[...] = seg(2).astype(v_out.dtype)
    sga_out[...] = jax.nn.sigmoid(seg(5)).astype(sga_out.dtype)

    zv = jax.nn.gelu(seg(4))
    mu = jnp.mean(zv, axis=-1, keepdims=True)
    zc = zv - mu
    var = jnp.mean(zc * zc, axis=-1, keepdims=True)
    zvn = (zc * lax.rsqrt(var + EPS) * lng_ref[...] + lnb_ref[...]).astype(BF)
    pre = jax.nn.gelu(seg(3)) * jax.nn.sigmoid(seg(6))
    for c in range(tm // CHUNK):
        rows = slice(c * CHUNK, (c + 1) * CHUNK)
        for g in range(N_GROUPS):
            cols = slice(g * LANES, (g + 1) * LANES)
            sp = _dot(wsp_ref[g], zvn[rows, cols]) + bsp_ref[:, cols]
            tb_out[rows, cols] = (pre[rows, cols] * sp).astype(tb_out.dtype)


TM = 256


def _inproj_call(x, m3, mod_row, rope_tabs, kv_dtype, consts, n_req, n_tok):
    t = x.shape[0]
    tm = TM
    per_req = n_tok // tm
    rope = rope_tabs is not None
    tok = pl.BlockSpec((tm, D), lambda i: (i, 0))
    in_specs = [
        tok,
        pl.BlockSpec((1, 1, 6 * D), lambda i: (mod_row(i, tm), 0, 0)),
        _const_spec((1, D)),
        pl.BlockSpec((D, N_SEG * D), lambda i: (0, 0), pipeline_mode=pl.Buffered(1)),
        _const_spec((D, D)),
        _const_spec((1, D)), _const_spec((D, tm)),
        _const_spec((D, LANES)), _const_spec((2 * LANES, D)),
        _const_spec((1, D)), _const_spec((1, D)),
        _const_spec((N_GROUPS, CHUNK, CHUNK)), _const_spec((CHUNK, D)),
    ]
    args = [x, m3, consts["n1g"], consts["w_in"], consts["wkt"], consts["qg"], consts["kgt"], consts["gsum"],
            consts["gbcast"], consts["lng"], consts["lnb"], consts["wsp"], consts["bsp"]]
    if rope:
        tab = pl.BlockSpec((tm, D), lambda i: (i % per_req, 0))
        tab_t = pl.BlockSpec((D, tm), lambda i: (0, i % per_req))
        in_specs += [tab, tab, tab_t, tab_t]
        args += list(rope_tabs)
    kt_spec = pl.BlockSpec((1, D, tm), lambda i: (i // per_req, 0, i % per_req))
    return pl.pallas_call(
        functools.partial(_inproj_kernel, rope),
        grid=(t // tm,),
        in_specs=in_specs,
        out_specs=[tok, kt_spec, tok, tok, tok],
        out_shape=[
            jax.ShapeDtypeStruct((t, D), BF),
            jax.ShapeDtypeStruct((n_req, D, n_tok), kv_dtype),
            jax.ShapeDtypeStruct((t, D), kv_dtype),
            jax.ShapeDtypeStruct((t, D), BF),
            jax.ShapeDtypeStruct((t, D), BF),
        ],
        compiler_params=_params(1),
        name="inproj_rope" if rope else "inproj",
    )(*args)


def _attn_kernel(has_ctx, q_ref, kt_ref, v_ref, *rest):
    if has_ctx:
        kct_ref, vc_ref, lam_ref, sg_ref, o_ref = rest
    else:
        lam_ref, sg_ref, o_ref = rest
    tq = q_ref.shape[0]
    lam = lam_ref[0:1, 0:1]
    first = lax.broadcasted_iota(jnp.int32, (1, LANES), 1) < HEAD_DIM
    zero = jnp.zeros((), BF)
    for h in range(N_HEADS):
        cols = slice(h * LANES, (h + 1) * LANES)
        q = q_ref[:, cols]
        qz = jnp.concatenate([jnp.where(first, q, zero), jnp.where(first, zero, q)], axis=0)
        scores = [_dot(qz, kt_ref[0, cols, :].astype(BF))]
        vals = [v_ref[:, cols].astype(BF)]
        if has_ctx:
            scores.append(_dot(qz, kct_ref[0, cols, :].astype(BF)))
            vals.append(vc_ref[:, cols].astype(BF))
        mx = functools.reduce(jnp.maximum, [jnp.max(s, axis=-1, keepdims=True) for s in scores])
        exps = [jnp.exp2(s - mx) for s in scores]
        den = functools.reduce(jnp.add, [jnp.sum(e, axis=-1, keepdims=True) for e in exps])
        ob = functools.reduce(jnp.add, [_dot(e.astype(BF), v) for e, v in zip(exps, vals)])
        o = ob[:tq] * (1.0 / den[:tq]) - ob[tq:] * (lam / den[tq:])
        o = _rms(o, sg_ref[...]) * (1.0 - LAMBDA_INIT)
        o_ref[:, cols] = o.astype(o_ref.dtype)


def _attn_call(q, kt, v, ctx, lamv, subln_g, n_req, n_tok):
    tq = 256
    nqb = n_tok // tq
    has_ctx = ctx is not None
    in_specs = [
        pl.BlockSpec((tq, D), lambda b, i: (b * nqb + i, 0)),
        pl.BlockSpec((1, D, n_tok), lambda b, i: (b, 0, 0)),
        pl.BlockSpec((n_tok, D), lambda b, i: (b, 0)),
    ]
    args = [q, kt, v]
    if has_ctx:
        n_ctx = ctx[0].shape[2]
        in_specs += [pl.BlockSpec((1, D, n_ctx), lambda b, i: (b, 0, 0)),
                     pl.BlockSpec((n_ctx, D), lambda b, i: (b, 0))]
        args += list(ctx)
    in_specs += [_const_spec((MOD_ROWS, LANES)), _const_spec((1, V_HEAD_DIM))]
    args += [lamv, subln_g]
    return pl.pallas_call(
        functools.partial(_attn_kernel, has_ctx),
        grid=(n_req, nqb),
        in_specs=in_specs,
        out_specs=pl.BlockSpec((tq, D), lambda b, i: (b * nqb + i, 0)),
        out_shape=jax.ShapeDtypeStruct((n_req * n_tok, D), BF),
        compiler_params=_params(2),
        name="attn_ctx" if has_ctx else "attn",
    )(*args)


def _outproj_kernel(oa_ref, sga_ref, tb_ref, x_ref, m_ref, wo_ref, n2g_ref, wr_hi_ref, wr_lo_ref,
                    x1_out, h2_out, lg_out):
    m = m_ref[0]
    gate1, shift2, scale2 = m[:, 2 * D:3 * D], m[:, 3 * D:4 * D], m[:, 4 * D:5 * D]
    merged = (sga_ref[...].astype(F32) * oa_ref[...].astype(F32) + tb_ref[...].astype(F32)).astype(BF)
    x1 = x_ref[...] + gate1 * _dot(merged, wo_ref[...])
    x1_out[...] = x1
    h2 = _rms(x1, n2g_ref[...]) * (1.0 + scale2) + shift2
    h2_out[...] = h2.astype(BF)
    hi, lo = _split_bf16(h2)
    wr_hi = wr_hi_ref[...]
    lg_out[...] = _dot(hi, wr_hi) + _dot(lo, wr_hi) + _dot(hi, wr_lo_ref[...])


def _outproj_call(oa, sga, tb, x, m3, mod_row, consts):
    t = x.shape[0]
    tm = 256
    tok = pl.BlockSpec((tm, D), lambda i: (i, 0))
    return pl.pallas_call(
        _outproj_kernel,
        grid=(t // tm,),
        in_specs=[
            tok, tok, tok, tok,
            pl.BlockSpec((1, 1, 6 * D), lambda i: (mod_row(i, tm), 0, 0)),
            _const_spec((D, D)), _const_spec((1, D)),
            _const_spec((D, LANES)), _const_spec((D, LANES)),
        ],
        out_specs=[tok, tok, pl.BlockSpec((tm, LANES), lambda i: (i, 0))],
        out_shape=[
            jax.ShapeDtypeStruct((t, D), F32),
            jax.ShapeDtypeStruct((t, D), BF),
            jax.ShapeDtypeStruct((t, LANES), F32),
        ],
        compiler_params=_params(1),
        name="outproj",
    )(oa, sga, tb, x, m3, consts["w_out"], consts["n2g"], consts["wr_hi"], consts["wr_lo"])


GATHER_ROWS = 512
KEY_BITS = 31
ROUTE_UNROLL_ELEMS = 1 << 19


def _route_kernel(cap, unroll, lg_ref, h2_ref, before_ref, xe_ref, gate_ref, slot_ref, aff_t_s, slot_t_s, p_s):
    lane = lax.broadcasted_iota(jnp.int32, (1, LANES), 1)
    valid = lane < N_EXPERTS
    lg = jnp.where(valid, lg_ref[...], -1e30)
    ex = jnp.where(valid, jnp.exp(lg - jnp.max(lg, axis=-1, keepdims=True)), 0.0)
    aff = ex / jnp.sum(ex, axis=-1, keepdims=True)

    def count(mask):
        return jnp.sum(jnp.where(mask, 1.0, 0.0), axis=0, keepdims=True)

    kth_bits = jnp.zeros((1, LANES), jnp.int32)
    for bit in range(KEY_BITS - 1, -1, -1):
        cand = kth_bits | (1 << bit)
        enough = count(aff >= lax.bitcast_convert_type(cand, F32)) >= cap
        kth_bits = jnp.where(enough, cand, kth_bits)
    kth = lax.bitcast_convert_type(kth_bits, F32)
    above = aff > kth
    tied = aff == kth
    need = cap - count(above)
    before = before_ref[...]
    tied_before = _dot(before, jnp.where(tied, 1.0, 0.0).astype(BF))
    chosen = jnp.where(above, 1.0, jnp.where(tied, jnp.where(tied_before < need, 1.0, 0.0), 0.0))
    slot = _dot(before, chosen.astype(BF))
    slot = jnp.where(valid, jnp.where(chosen > 0.0, slot, float(cap)), float(cap))
    slot_ref[...] = slot
    slot_t_s[...] = slot.T
    aff_t_s[...] = aff.T
    slot_ids = lax.broadcasted_iota(jnp.int32, (cap, 1), 0).astype(F32)

    def per_expert(e, carry):
        hit = slot_t_s[pl.ds(e, 1), :] == slot_ids
        p_s[pl.ds(pl.multiple_of(e * cap, cap), cap), :] = jnp.where(hit, 1.0, 0.0).astype(BF)
        gate_ref[e] = jnp.sum(jnp.where(hit, aff_t_s[pl.ds(e, 1), :], 0.0), axis=-1, keepdims=True)
        return carry

    lax.fori_loop(0, N_EXPERTS, per_expert, 0, unroll=unroll)
    h2 = h2_ref[...]
    rows = min(GATHER_ROWS, N_EXPERTS * cap)
    e_per = rows // cap
    for t in range(N_EXPERTS * cap // rows):
        xe = _dot(p_s[t * rows:(t + 1) * rows, :], h2).astype(BF)
        xe_ref[t * e_per:(t + 1) * e_per] = xe.reshape(e_per, cap, D)


def _route_call(lg, h2, n_req, n_tok):
    cap = max(1, CAPACITY_FACTOR * n_tok // N_EXPERTS)
    before = (lax.broadcasted_iota(jnp.int32, (n_tok, n_tok), 1)
              < lax.broadcasted_iota(jnp.int32, (n_tok, n_tok), 0)).astype(BF)
    unroll = max(1, min(N_EXPERTS, ROUTE_UNROLL_ELEMS // (cap * n_tok)))
    return pl.pallas_call(
        functools.partial(_route_kernel, cap, unroll),
        grid=(n_req,),
        in_specs=[
            pl.BlockSpec((n_tok, LANES), lambda b: (b, 0)),
            pl.BlockSpec((n_tok, D), lambda b: (b, 0)),
            _const_spec((n_tok, n_tok)),
        ],
        out_specs=[
            pl.BlockSpec((N_EXPERTS, cap, D), lambda b: (0, b, 0)),
            pl.BlockSpec((N_EXPERTS, cap, 1), lambda b: (0, b, 0)),
            pl.BlockSpec((n_tok, LANES), lambda b: (b, 0)),
        ],
        out_shape=[
            jax.ShapeDtypeStruct((N_EXPERTS, n_req * cap, D), BF),
            jax.ShapeDtypeStruct((N_EXPERTS, n_req * cap, 1), F32),
            jax.ShapeDtypeStruct((n_req * n_tok, LANES), F32),
        ],
        scratch_shapes=[
            pltpu.VMEM((LANES, n_tok), F32),
            pltpu.VMEM((LANES, n_tok), F32),
            pltpu.VMEM((N_EXPERTS * cap, n_tok), BF),
        ],
        compiler_params=_params(1),
        name="route",
    )(lg, h2, before)


def _experts_kernel(xa_ref, xb_ref, ga_ref, gb_ref, wg_ref, wu_ref, wd_ref, ya_ref, yb_ref, x_s, acc_s):
    f = pl.program_id(1)
    ra = xa_ref.shape[1]

    @pl.when(f == 0)
    def _():
        x_s[0:ra, :] = xa_ref[0]
        x_s[ra:, :] = xb_ref[0]
        acc_s[...] = jnp.zeros(acc_s.shape, F32)

    x = x_s[...]
    gate = _dot(x, wg_ref[0].astype(BF))
    up = _dot(x, wu_ref[0].astype(BF))
    hid = (gate * jax.nn.sigmoid(gate) * up).astype(BF)
    acc_s[...] += _dot(hid, wd_ref[0].astype(BF))

    @pl.when(f == pl.num_programs(1) - 1)
    def _():
        ya_ref[0] = (acc_s[0:ra, :] * ga_ref[0]).astype(ya_ref.dtype)
        yb_ref[0] = (acc_s[ra:, :] * gb_ref[0]).astype(yb_ref.dtype)


def _experts_call(xa, xb, ga, gb, wg, wu, wd):
    ra, rb = xa.shape[1], xb.shape[1]
    tf = 512
    xa_spec = pl.BlockSpec((1, ra, D), lambda e, f: (e, 0, 0))
    xb_spec = pl.BlockSpec((1, rb, D), lambda e, f: (e, 0, 0))
    return pl.pallas_call(
        _experts_kernel,
        grid=(N_EXPERTS, D_EXPERT // tf),
        in_specs=[
            xa_spec, xb_spec,
            pl.BlockSpec((1, ra, 1), lambda e, f: (e, 0, 0)),
            pl.BlockSpec((1, rb, 1), lambda e, f: (e, 0, 0)),
            pl.BlockSpec((1, D, tf), lambda e, f: (e, 0, f)),
            pl.BlockSpec((1, D, tf), lambda e, f: (e, 0, f)),
            pl.BlockSpec((1, tf, D), lambda e, f: (e, f, 0)),
        ],
        out_specs=[xa_spec, xb_spec],
        out_shape=[
            jax.ShapeDtypeStruct((N_EXPERTS, ra, D), BF),
            jax.ShapeDtypeStruct((N_EXPERTS, rb, D), BF),
        ],
        scratch_shapes=[pltpu.VMEM((ra + rb, D), BF), pltpu.VMEM((ra + rb, D), F32)],
        compiler_params=_params(2),
        name="experts",
    )(xa, xb, ga, gb, wg, wu, wd)


def _scatter_kernel(cap, slot_ref, y_ref, x1_ref, m_ref, expand_ref, out_ref):
    gate2 = m_ref[0][:, 5 * D:6 * D]
    slot_wide = _dot(slot_ref[...].astype(BF), expand_ref[...])
    slot_ids = (lax.broadcasted_iota(jnp.int32, (1, N_EXPERTS * cap), 1) % cap).astype(F32)
    onehot = jnp.where(slot_wide == slot_ids, 1.0, 0.0).astype(BF)
    y = y_ref[...].reshape(N_EXPERTS * cap, D)
    out_ref[...] = x1_ref[...] + gate2 * _dot(onehot, y)


def _scatter_call(slots, y, x1, m3, mod_row_req, n_req, n_tok):
    cap = y.shape[1] // n_req
    e_ids = lax.broadcasted_iota(jnp.int32, (LANES, N_EXPERTS * cap), 0)
    c_ids = lax.broadcasted_iota(jnp.int32, (LANES, N_EXPERTS * cap), 1) // cap
    expand = (e_ids == c_ids).astype(BF)
    return pl.pallas_call(
        functools.partial(_scatter_kernel, cap),
        grid=(n_req,),
        in_specs=[
            pl.BlockSpec((n_tok, LANES), lambda b: (b, 0)),
            pl.BlockSpec((N_EXPERTS, cap, D), lambda b: (0, b, 0)),
            pl.BlockSpec((n_tok, D), lambda b: (b, 0)),
            pl.BlockSpec((1, 1, 6 * D), lambda b: (mod_row_req(b), 0, 0)),
            _const_spec((LANES, N_EXPERTS * cap)),
        ],
        out_specs=pl.BlockSpec((n_tok, D), lambda b: (b, 0)),
        out_shape=jax.ShapeDtypeStruct((n_req * n_tok, D), F32),
        compiler_params=_params(1),
        name="scatter",
    )(slots, y, x1, m3, expand)


def _rope_tables(n_tokens):
    rows = n_tokens // GRID_W
    row = jnp.broadcast_to(jnp.arange(rows, dtype=F32)[:, None], (rows, GRID_W)).reshape(-1)
    col = jnp.broadcast_to(jnp.arange(GRID_W, dtype=F32)[None, :], (rows, GRID_W)).reshape(-1)
    half = HEAD_DIM // 4
    inv_freq = ROPE_BASE ** (-jnp.arange(half, dtype=F32) / half)
    ar = row[:, None] * inv_freq
    ac = col[:, None] * inv_freq
    ang = jnp.concatenate([ar, ar, ac, ac], axis=-1)
    cos = jnp.tile(jnp.cos(ang), (1, D // HEAD_DIM))
    sin = jnp.tile(jnp.sin(ang), (1, D // HEAD_DIM))
    lane = jnp.arange(D)
    first_half = (lane % (HEAD_DIM // 2)) < (HEAD_DIM // 4)
    sin_signed = jnp.where(first_half[None, :], -sin, sin)
    return cos, sin_signed, cos.T, sin_signed.T


def kernel(x_prompt, x_sample, cache_k, cache_v, c, c_ctx, w_ada, b_ada, norm1_g, norm2_g, w_in, q_norm_g, k_norm_g, lambda_q1, lambda_k1, lambda_q2, lambda_k2, subln_g, gmlp_ln_g, gmlp_ln_b, w_spatial, b_spatial, w_out, w_router, w_gate_e, w_up_e, w_down_e):
    n_p, t_p = x_prompt.shape[0], x_prompt.shape[1]
    n_s, t_s = x_sample.shape[0], x_sample.shape[1]
    n_ctx = cache_k.shape[2]
    l = 0

    cond = jnp.concatenate([c_ctx[None, :], c, jnp.zeros((MOD_ROWS - 1 - n_s, D), F32)], axis=0)
    m, lamv = _ada_call(cond, w_ada[l], b_ada[l][None, :], lambda_q1[l][None, :], lambda_k1[l][None, :],
                        lambda_q2[l][None, :], lambda_k2[l][None, :])
    m3 = m.reshape(MOD_ROWS, 1, 6 * D)
    w_in_bf, wkt = _wcast_call(w_in[l])

    group = jnp.arange(D) // HEAD_DIM
    gsum = (group[:, None] == jnp.arange(LANES)[None, :]).astype(BF)
    wr = jnp.pad(w_router[l], ((0, 0), (0, LANES - N_EXPERTS)))
    wr_hi = wr.astype(BF)
    consts = {
        "n1g": norm1_g[l][None, :],
        "n2g": norm2_g[l][None, :],
        "w_in": w_in_bf,
        "wkt": wkt,
        "qg": jnp.tile(q_norm_g[l] * (HEAD_DIM ** -0.5 * math.log2(math.e)), D // HEAD_DIM)[None, :],
        "kgt": jnp.broadcast_to(jnp.tile(k_norm_g[l], D // HEAD_DIM)[:, None], (D, TM)),
        "gsum": gsum,
        "gbcast": jnp.concatenate([gsum.T, gsum.T], axis=0),
        "lng": gmlp_ln_g[l][None, :],
        "lnb": gmlp_ln_b[l][None, :],
        "wsp": w_spatial[l].astype(BF),
        "bsp": jnp.repeat(b_spatial[l].T, D // N_GROUPS, axis=1),
        "w_out": w_out[l].astype(BF),
        "wr_hi": wr_hi,
        "wr_lo": (wr - wr_hi.astype(F32)).astype(BF),
    }
    sg = subln_g[l][None, :]

    def prompt_row(i, tm):
        return 0

    def sample_row(i, tm):
        return 1 + i // (t_s // tm)

    outs = []
    kv_out = None
    for x, n_req, n_tok, mod_row, is_sample in (
            (x_prompt, n_p, t_p, prompt_row, False), (x_sample, n_s, t_s, sample_row, True)):
        xf = x.reshape(n_req * n_tok, D)
        tabs = _rope_tables(n_tok) if is_sample else None
        q, kt, v, sga, tb = _inproj_call(xf, m3, mod_row, tabs, BF if is_sample else F32, consts, n_req, n_tok)
        if is_sample:
            ctx = (jnp.transpose(cache_k[:, l], (0, 2, 3, 4, 1)).reshape(n_req, D, n_ctx),
                   cache_v[:, l].reshape(n_req * n_ctx, D))
        else:
            ctx = None
            kv_out = (kt, v)
        oa = _attn_call(q, kt, v, ctx, lamv, sg, n_req, n_tok)
        x1, h2, lg = _outproj_call(oa, sga, tb, xf, m3, mod_row, consts)
        xe, gates, slots = _route_call(lg, h2, n_req, n_tok)
        outs.append((x1, xe, gates, slots, n_req, n_tok, is_sample))

    (x1p, xep, gp, rtp, _, _, _), (x1s, xes, gs, rts, _, _, _) = outs
    yp, ys = _experts_call(xep, xes, gp, gs, w_gate_e[l], w_up_e[l], w_down_e[l])
    y_prompt = _scatter_call(rtp, yp, x1p, m3, lambda b: 0, n_p, t_p).reshape(x_prompt.shape)
    y_sample = _scatter_call(rts, ys, x1s, m3, lambda b: 1 + b, n_s, t_s).reshape(x_sample.shape)
    new_k = jnp.transpose(kv_out[0].reshape(n_p, N_HEADS, 2, HEAD_DIM, t_p), (0, 4, 1, 2, 3))
    new_k = new_k.reshape(n_p, 1, t_p, N_HEADS, 2, HEAD_DIM)
    new_v = kv_out[1].reshape(n_p, 1, t_p, N_HEADS, V_HEAD_DIM)
    return (y_prompt, y_sample, new_k, new_v)
```

```python
import functools
import math

import jax
import jax.numpy as jnp
from jax import lax
from jax.experimental import pallas as pl
from jax.experimental.pallas import tpu as pltpu

D = 1024
N_HEADS = 8
HEAD_DIM = 64
V_HEAD_DIM = 128
GRID_W = 64
ROPE_BASE = 10000.0
CHUNK = 128
N_GROUPS = 8
N_EXPERTS = 16
CAPACITY_FACTOR = 2
D_EXPERT = 2048
N_SEG = 7
EPS = 1e-6
LAMBDA_INIT = 0.8 - 0.6 * math.exp(-0.3 * 0)

LANES = 128
MOD_ROWS = 8
VMEM_LIMIT = 56 * 1024 * 1024

BF = jnp.bfloat16
F32 = jnp.float32


def _dot(a, b):
    return jnp.dot(a, b, preferred_element_type=F32)


def _dot_nt(a, b):
    return lax.dot_general(a, b, (((1,), (1,)), ((), ())), preferred_element_type=F32)


def _split_bf16(x):
    hi = x.astype(BF)
    lo = (x - hi.astype(F32)).astype(BF)
    return hi, lo


def _params(n_grid_dims):
    return pltpu.CompilerParams(
        dimension_semantics=("arbitrary",) * n_grid_dims, vmem_limit_bytes=VMEM_LIMIT)


def _const_spec(shape):
    nd = len(shape)
    return pl.BlockSpec(shape, lambda *_: (0,) * nd)


def _ada_kernel(cond_ref, w_ref, b_ref, lq1_ref, lk1_ref, lq2_ref, lk2_ref, m_ref, lam_ref):
    c = cond_ref[...]
    a = c * jax.nn.sigmoid(c)
    a_hi, a_lo = _split_bf16(a)
    w_hi, w_lo = _split_bf16(w_ref[...])
    m_ref[...] = _dot(a_hi, w_hi) + _dot(a_lo, w_hi) + _dot(a_hi, w_lo) + b_ref[...]
    s1 = jnp.sum(lq1_ref[...] * lk1_ref[...], axis=-1, keepdims=True)
    s2 = jnp.sum(lq2_ref[...] * lk2_ref[...], axis=-1, keepdims=True)
    lam = jnp.exp(s1) - jnp.exp(s2) + LAMBDA_INIT
    lam_ref[...] = jnp.broadcast_to(lam, lam_ref.shape)


def _ada_call(cond, w_ada, b_ada, lq1, lk1, lq2, lk2):
    tn = D
    vec = pl.BlockSpec((1, HEAD_DIM), lambda j: (0, 0))
    return pl.pallas_call(
        _ada_kernel,
        grid=(6 * D // tn,),
        in_specs=[
            pl.BlockSpec((MOD_ROWS, D), lambda j: (0, 0)),
            pl.BlockSpec((D, tn), lambda j: (0, j)),
            pl.BlockSpec((1, tn), lambda j: (0, j)),
            vec, vec, vec, vec,
        ],
        out_specs=[
            pl.BlockSpec((MOD_ROWS, tn), lambda j: (0, j)),
            pl.BlockSpec((MOD_ROWS, LANES), lambda j: (0, 0)),
        ],
        out_shape=[
            jax.ShapeDtypeStruct((MOD_ROWS, 6 * D), F32),
            jax.ShapeDtypeStruct((MOD_ROWS, LANES), F32),
        ],
        compiler_params=_params(1),
        name="ada",
    )(cond, w_ada, b_ada, lq1, lk1, lq2, lk2)


K_SEG = 1


def _wcast_kernel(w_ref, wb_ref, wkt_ref):
    w = w_ref[...]
    wb_ref[...] = w.astype(BF)

    @pl.when(pl.program_id(0) == K_SEG)
    def _():
        wkt_ref[...] = w.T.astype(BF)


def _wcast_call(w_in):
    return pl.pallas_call(
        _wcast_kernel,
        grid=(N_SEG,),
        in_specs=[pl.BlockSpec((D, D), lambda j: (0, j))],
        out_specs=[pl.BlockSpec((D, D), lambda j: (0, j)), pl.BlockSpec((D, D), lambda j: (0, 0))],
        out_shape=[jax.ShapeDtypeStruct((D, N_SEG * D), BF), jax.ShapeDtypeStruct((D, D), BF)],
        compiler_params=_params(1),
        name="wcast",
    )(w_in)


def _rms(x, g):
    return x * lax.rsqrt(jnp.mean(x * x, axis=-1, keepdims=True) + EPS) * g


def _head_norm(t, g, gsum, gbcast):
    ss = _dot((t * t).astype(BF), gsum)
    inv = lax.rsqrt(ss * (1.0 / HEAD_DIM) + EPS)
    hi, lo = _split_bf16(inv)
    bc = _dot(jnp.concatenate([hi, lo], axis=-1), gbcast)
    return t * bc * g


def _rope(t, cos, sin_signed, first_half):
    q = HEAD_DIM // 4
    fwd = pltpu.roll(t, D - q, axis=1)
    bwd = pltpu.roll(t, q, axis=1)
    return t * cos + jnp.where(first_half, fwd, bwd) * sin_signed


def _rope_t(t, cos_t, sin_signed_t):
    n = t.shape[1]
    q = HEAD_DIM // 4
    t3 = t.reshape(D // HEAD_DIM, HEAD_DIM, n)
    rot = jnp.concatenate([t3[:, q:2 * q], t3[:, 0:q], t3[:, 3 * q:4 * q], t3[:, 2 * q:3 * q]], axis=1)
    return t * cos_t + rot.reshape(D, n) * sin_signed_t


def _inproj_kernel(rope, x_ref, m_ref, n1g_ref, w_ref, wkt_ref, qg_ref, kgt_ref, gsum_ref, gbcast_ref,
                   lng_ref, lnb_ref, wsp_ref, bsp_ref, *rest):
    if rope:
        cos_ref, sin_ref, cos_t_ref, sin_t_ref, q_out, kt_out, v_out, sga_out, tb_out = rest
    else:
        q_out, kt_out, v_out, sga_out, tb_out = rest
    tm = x_ref.shape[0]
    m = m_ref[0]
    shift1, scale1 = m[:, 0:D], m[:, D:2 * D]
    h = (_rms(x_ref[...], n1g_ref[...]) * (1.0 + scale1) + shift1).astype(BF)

    def seg(j):
        return _dot(h, w_ref[:, j * D:(j + 1) * D])

    q = _head_norm(seg(0), qg_ref[...], gsum_ref[...], gbcast_ref[...])
    if rope:
        lane = lax.broadcasted_iota(jnp.int32, (1, D), 1)
        q = _rope(q, cos_ref[...], sin_ref[...], (lane % (HEAD_DIM // 2)) < (HEAD_DIM // 4))
    q_out[...] = q.astype(q_out.dtype)

    kt = _dot_nt(wkt_ref[...], h)
    k3 = kt.reshape(D // HEAD_DIM, HEAD_DIM, tm)
    inv = lax.rsqrt(jnp.mean(k3 * k3, axis=1, keepdims=True) + EPS)
    kt = (k3 * inv).reshape(D, tm) * kgt_ref[...]
    if rope:
        kt = _rope_t(kt, cos_t_ref[...], sin_t_ref[...])
    kt_out[0] = kt.astype(kt_out.dtype)

    v_out[...] = seg(2).astype(v_out.dtype)
    sga_out[...] = jax.nn.sigmoid(seg(5)).astype(sga_out.dtype)

    zv = jax.nn.gelu(seg(4))
    mu = jnp.mean(zv, axis=-1, keepdims=True)
    zc = zv - mu
    var = jnp.mean(zc * zc, axis=-1, keepdims=True)
    zvn = (zc * lax.rsqrt(var + EPS) * lng_ref[...] + lnb_ref[...]).astype(BF)
    pre = jax.nn.gelu(seg(3)) * jax.nn.sigmoid(seg(6))
    for c in range(tm // CHUNK):
        rows = slice(c * CHUNK, (c + 1) * CHUNK)
        for g in range(N_GROUPS):
            cols = slice(g * LANES, (g + 1) * LANES)
            sp = _dot(wsp_ref[g], zvn[rows, cols]) + bsp_ref[:, cols]
            tb_out[rows, cols] = (pre[rows, cols] * sp).astype(tb_out.dtype)


TM = 256


def _inproj_call(x, m3, mod_row, rope_tabs, kv_dtype, consts, n_req, n_tok):
    t = x.shape[0]
    tm = TM
    per_req = n_tok // tm
    rope = rope_tabs is not None
    tok = pl.BlockSpec((tm, D), lambda i: (i, 0))
    in_specs = [
        tok,
        pl.BlockSpec((1, 1, 6 * D), lambda i: (mod_row(i, tm), 0, 0)),
        _const_spec((1, D)),
        pl.BlockSpec((D, N_SEG * D), lambda i: (0, 0), pipeline_mode=pl.Buffered(1)),
        _const_spec((D, D)),
        _const_spec((1, D)), _const_spec((D, tm)),
        _const_spec((D, LANES)), _const_spec((2 * LANES, D)),
        _const_spec((1, D)), _const_spec((1, D)),
        _const_spec((N_GROUPS, CHUNK, CHUNK)), _const_spec((CHUNK, D)),
    ]
    args = [x, m3, consts["n1g"], consts["w_in"], consts["wkt"], consts["qg"], consts["kgt"], consts["gsum"],
            consts["gbcast"], consts["lng"], consts["lnb"], consts["wsp"], consts["bsp"]]
    if rope:
        tab = pl.BlockSpec((tm, D), lambda i: (i % per_req, 0))
        tab_t = pl.BlockSpec((D, tm), lambda i: (0, i % per_req))
        in_specs += [tab, tab, tab_t, tab_t]
        args += list(rope_tabs)
    kt_spec = pl.BlockSpec((1, D, tm), lambda i: (i // per_req, 0, i % per_req))
    return pl.pallas_call(
        functools.partial(_inproj_kernel, rope),
        grid=(t // tm,),
        in_specs=in_specs,
        out_specs=[tok, kt_spec, tok, tok, tok],
        out_shape=[
            jax.ShapeDtypeStruct((t, D), BF),
            jax.ShapeDtypeStruct((n_req, D, n_tok), kv_dtype),
            jax.ShapeDtypeStruct((t, D), kv_dtype),
            jax.ShapeDtypeStruct((t, D), BF),
            jax.ShapeDtypeStruct((t, D), BF),
        ],
        compiler_params=_params(1),
        name="inproj_rope" if rope else "inproj",
    )(*args)


def _attn_kernel(has_ctx, q_ref, kt_ref, v_ref, *rest):
    if has_ctx:
        kct_ref, vc_ref, *rest = rest
    (lam_ref, sg_ref, sga_ref, tb_ref, x_ref, m_ref, wo_ref, n2g_ref, wr_hi_ref, wr_lo_ref,
     x1_out, h2_out, lg_out, merged_s) = rest
    tq = q_ref.shape[0]
    lam = lam_ref[0:1, 0:1]
    lane = lax.broadcasted_iota(jnp.int32, (1, LANES), 1)
    first = lane < HEAD_DIM
    zero = jnp.zeros((), BF)
    ones_col = jnp.where(lane == 0, 1.0, 0.0).astype(BF)

    def head_values(ref, h):
        v = ref[:, h * V_HEAD_DIM:(h + 1) * V_HEAD_DIM].astype(BF)
        return jnp.concatenate([v, jnp.broadcast_to(ones_col, (v.shape[0], LANES))], axis=1)

    for h in range(N_HEADS):
        cols = slice(h * LANES, (h + 1) * LANES)
        q = q_ref[:, cols]
        qz = jnp.concatenate([jnp.where(first, q, zero), jnp.where(first, zero, q)], axis=0)
        scores = [_dot(qz, kt_ref[0, cols, :].astype(BF))]
        vals = [head_values(v_ref, h)]
        if has_ctx:
            scores.append(_dot(qz, kct_ref[0, cols, :].astype(BF)))
            vals.append(head_values(vc_ref, h))
        mx = functools.reduce(jnp.maximum, [jnp.max(s, axis=-1, keepdims=True) for s in scores])
        ob = functools.reduce(jnp.add, [_dot(jnp.exp2(s - mx).astype(BF), v) for s, v in zip(scores, vals)])
        den = ob[:, V_HEAD_DIM:V_HEAD_DIM + 1]
        o = ob[:tq, :V_HEAD_DIM] * (1.0 / den[:tq]) - ob[tq:, :V_HEAD_DIM] * (lam / den[tq:])
        o = _rms(o, sg_ref[...]) * (1.0 - LAMBDA_INIT)
        merged_s[:, cols] = (sga_ref[:, cols].astype(F32) * o + tb_ref[:, cols].astype(F32)).astype(BF)

    m = m_ref[0]
    gate1, shift2, scale2 = m[:, 2 * D:3 * D], m[:, 3 * D:4 * D], m[:, 4 * D:5 * D]
    x1 = x_ref[...] + gate1 * _dot(merged_s[...], wo_ref[...])
    x1_out[...] = x1
    h2 = _rms(x1, n2g_ref[...]) * (1.0 + scale2) + shift2
    h2_out[...] = h2.astype(BF)
    hi, lo = _split_bf16(h2)
    wr_hi = wr_hi_ref[...]
    lg_out[...] = _dot(hi, wr_hi) + _dot(lo, wr_hi) + _dot(hi, wr_lo_ref[...])


def _attn_call(q, kt, v, ctx, lamv, sga, tb, x, m3, mod_row_req, consts, n_req, n_tok):
    tq = TM
    nqb = n_tok // tq
    has_ctx = ctx is not None
    tok = pl.BlockSpec((tq, D), lambda b, i: (b * nqb + i, 0))

    in_specs = [tok, pl.BlockSpec((1, D, n_tok), lambda b, i: (b, 0, 0)),
                pl.BlockSpec((n_tok, D), lambda b, i: (b, 0))]
    args = [q, kt, v]
    if has_ctx:
        n_ctx = ctx[0].shape[2]
        in_specs += [pl.BlockSpec((1, D, n_ctx), lambda b, i: (b, 0, 0)),
                     pl.BlockSpec((n_ctx, D), lambda b, i: (b, 0))]
        args += list(ctx)
    in_specs += [
        _const_spec((MOD_ROWS, LANES)), _const_spec((1, V_HEAD_DIM)),
        tok, tok, tok,
        pl.BlockSpec((1, 1, 6 * D), lambda b, i: (mod_row_req(b), 0, 0)),
        _const_spec((D, D)), _const_spec((1, D)),
        _const_spec((D, LANES)), _const_spec((D, LANES)),
    ]
    args += [lamv, consts["sg"], sga, tb, x, m3, consts["w_out"], consts["n2g"], consts["wr_hi"], consts["wr_lo"]]
    t = n_req * n_tok
    return pl.pallas_call(
        functools.partial(_attn_kernel, has_ctx),
        grid=(n_req, nqb),
        in_specs=in_specs,
        out_specs=[tok, tok, pl.BlockSpec((tq, LANES), lambda b, i: (b * nqb + i, 0))],
        out_shape=[
            jax.ShapeDtypeStruct((t, D), F32),
            jax.ShapeDtypeStruct((t, D), BF),
            jax.ShapeDtypeStruct((t, LANES), F32),
        ],
        scratch_shapes=[pltpu.VMEM((tq, D), BF)],
        compiler_params=_params(2),
        name="attn_ctx" if has_ctx else "attn",
    )(*args)


GATHER_ROWS = 512
KEY_BITS = 31
ROUTE_UNROLL_ELEMS = 1 << 19


def _route_kernel(cap, unroll, lg_ref, h2_ref, before_ref, xe_ref, gate_ref, slot_ref, aff_t_s, slot_t_s, p_s):
    lane = lax.broadcasted_iota(jnp.int32, (1, LANES), 1)
    valid = lane < N_EXPERTS
    lg = jnp.where(valid, lg_ref[...], -1e30)
    ex = jnp.where(valid, jnp.exp(lg - jnp.max(lg, axis=-1, keepdims=True)), 0.0)
    aff = ex / jnp.sum(ex, axis=-1, keepdims=True)

    def count(mask):
        return jnp.sum(jnp.where(mask, 1.0, 0.0), axis=0, keepdims=True)

    kth_bits = jnp.zeros((1, LANES), jnp.int32)
    for bit in range(KEY_BITS - 1, -1, -1):
        cand = kth_bits | (1 << bit)
        enough = count(aff >= lax.bitcast_convert_type(cand, F32)) >= cap
        kth_bits = jnp.where(enough, cand, kth_bits)
    kth = lax.bitcast_convert_type(kth_bits, F32)
    above = aff > kth
    tied = aff == kth
    need = cap - count(above)
    before = before_ref[...]
    tied_before = _dot(before, jnp.where(tied, 1.0, 0.0).astype(BF))
    chosen = jnp.where(above, 1.0, jnp.where(tied, jnp.where(tied_before < need, 1.0, 0.0), 0.0))
    slot = _dot(before, chosen.astype(BF))
    slot = jnp.where(valid, jnp.where(chosen > 0.0, slot, float(cap)), float(cap))
    slot_ref[...] = slot
    slot_t_s[...] = slot.T
    aff_t_s[...] = aff.T
    slot_ids = lax.broadcasted_iota(jnp.int32, (cap, 1), 0).astype(F32)

    def per_expert(e, carry):
        hit = slot_t_s[pl.ds(e, 1), :] == slot_ids
        p_s[pl.ds(pl.multiple_of(e * cap, cap), cap), :] = jnp.where(hit, 1.0, 0.0).astype(BF)
        gate_ref[e] = jnp.sum(jnp.where(hit, aff_t_s[pl.ds(e, 1), :], 0.0), axis=-1, keepdims=True)
        return carry

    lax.fori_loop(0, N_EXPERTS, per_expert, 0, unroll=unroll)
    h2 = h2_ref[...]
    rows = min(GATHER_ROWS, N_EXPERTS * cap)
    e_per = rows // cap
    for t in range(N_EXPERTS * cap // rows):
        xe = _dot(p_s[t * rows:(t + 1) * rows, :], h2).astype(BF)
        xe_ref[t * e_per:(t + 1) * e_per] = xe.reshape(e_per, cap, D)


def _route_call(lg, h2, n_req, n_tok):
    cap = max(1, CAPACITY_FACTOR * n_tok // N_EXPERTS)
    before = (lax.broadcasted_iota(jnp.int32, (n_tok, n_tok), 1)
              < lax.broadcasted_iota(jnp.int32, (n_tok, n_tok), 0)).astype(BF)
    unroll = max(1, min(N_EXPERTS, ROUTE_UNROLL_ELEMS // (cap * n_tok)))
    return pl.pallas_call(
        functools.partial(_route_kernel, cap, unroll),
        grid=(n_req,),
        in_specs=[
            pl.BlockSpec((n_tok, LANES), lambda b: (b, 0)),
            pl.BlockSpec((n_tok, D), lambda b: (b, 0)),
            _const_spec((n_tok, n_tok)),
        ],
        out_specs=[
            pl.BlockSpec((N_EXPERTS, cap, D), lambda b: (0, b, 0)),
            pl.BlockSpec((N_EXPERTS, cap, 1), lambda b: (0, b, 0)),
            pl.BlockSpec((n_tok, LANES), lambda b: (b, 0)),
        ],
        out_shape=[
            jax.ShapeDtypeStruct((N_EXPERTS, n_req * cap, D), BF),
            jax.ShapeDtypeStruct((N_EXPERTS, n_req * cap, 1), F32),
            jax.ShapeDtypeStruct((n_req * n_tok, LANES), F32),
        ],
        scratch_shapes=[
            pltpu.VMEM((LANES, n_tok), F32),
            pltpu.VMEM((LANES, n_tok), F32),
            pltpu.VMEM((N_EXPERTS * cap, n_tok), BF),
        ],
        compiler_params=_params(1),
        name="route",
    )(lg, h2, before)


def _experts_kernel(xa_ref, xb_ref, ga_ref, gb_ref, wg_ref, wu_ref, wd_ref, ya_ref, yb_ref, x_s, acc_s):
    f = pl.program_id(1)
    ra = xa_ref.shape[1]

    @pl.when(f == 0)
    def _():
        x_s[0:ra, :] = xa_ref[0]
        x_s[ra:, :] = xb_ref[0]
        acc_s[...] = jnp.zeros(acc_s.shape, F32)

    x = x_s[...]
    gate = _dot(x, wg_ref[0].astype(BF))
    up = _dot(x, wu_ref[0].astype(BF))
    hid = (gate * jax.nn.sigmoid(gate) * up).astype(BF)
    acc_s[...] += _dot(hid, wd_ref[0].astype(BF))

    @pl.when(f == pl.num_programs(1) - 1)
    def _():
        ya_ref[0] = (acc_s[0:ra, :] * ga_ref[0]).astype(ya_ref.dtype)
        yb_ref[0] = (acc_s[ra:, :] * gb_ref[0]).astype(yb_ref.dtype)


def _experts_call(xa, xb, ga, gb, wg, wu, wd):
    ra, rb = xa.shape[1], xb.shape[1]
    tf = 512
    xa_spec = pl.BlockSpec((1, ra, D), lambda e, f: (e, 0, 0))
    xb_spec = pl.BlockSpec((1, rb, D), lambda e, f: (e, 0, 0))
    return pl.pallas_call(
        _experts_kernel,
        grid=(N_EXPERTS, D_EXPERT // tf),
        in_specs=[
            xa_spec, xb_spec,
            pl.BlockSpec((1, ra, 1), lambda e, f: (e, 0, 0)),
            pl.BlockSpec((1, rb, 1), lambda e, f: (e, 0, 0)),
            pl.BlockSpec((1, D, tf), lambda e, f: (e, 0, f)),
            pl.BlockSpec((1, D, tf), lambda e, f: (e, 0, f)),
            pl.BlockSpec((1, tf, D), lambda e, f: (e, f, 0)),
        ],
        out_specs=[xa_spec, xb_spec],
        out_shape=[
            jax.ShapeDtypeStruct((N_EXPERTS, ra, D), BF),
            jax.ShapeDtypeStruct((N_EXPERTS, rb, D), BF),
        ],
        scratch_shapes=[pltpu.VMEM((ra + rb, D), BF), pltpu.VMEM((ra + rb, D), F32)],
        compiler_params=_params(2),
        name="experts",
    )(xa, xb, ga, gb, wg, wu, wd)


def _scatter_kernel(cap, slot_ref, y_ref, x1_ref, m_ref, expand_ref, out_ref):
    gate2 = m_ref[0][:, 5 * D:6 * D]
    slot_wide = _dot(slot_ref[...].astype(BF), expand_ref[...])
    slot_ids = (lax.broadcasted_iota(jnp.int32, (1, N_EXPERTS * cap), 1) % cap).astype(F32)
    onehot = jnp.where(slot_wide == slot_ids, 1.0, 0.0).astype(BF)
    y = y_ref[...].reshape(N_EXPERTS * cap, D)
    out_ref[...] = x1_ref[...] + gate2 * _dot(onehot, y)


def _scatter_call(slots, y, x1, m3, mod_row_req, n_req, n_tok):
    cap = y.shape[1] // n_req
    e_ids = lax.broadcasted_iota(jnp.int32, (LANES, N_EXPERTS * cap), 0)
    c_ids = lax.broadcasted_iota(jnp.int32, (LANES, N_EXPERTS * cap), 1) // cap
    expand = (e_ids == c_ids).astype(BF)
    return pl.pallas_call(
        functools.partial(_scatter_kernel, cap),
        grid=(n_req,),
        in_specs=[
            pl.BlockSpec((n_tok, LANES), lambda b: (b, 0)),
            pl.BlockSpec((N_EXPERTS, cap, D), lambda b: (0, b, 0)),
            pl.BlockSpec((n_tok, D), lambda b: (b, 0)),
            pl.BlockSpec((1, 1, 6 * D), lambda b: (mod_row_req(b), 0, 0)),
            _const_spec((LANES, N_EXPERTS * cap)),
        ],
        out_specs=pl.BlockSpec((n_tok, D), lambda b: (b, 0)),
        out_shape=jax.ShapeDtypeStruct((n_req * n_tok, D), F32),
        compiler_params=_params(1),
        name="scatter",
    )(slots, y, x1, m3, expand)


def _rope_tables(n_tokens):
    rows = n_tokens // GRID_W
    row = jnp.broadcast_to(jnp.arange(rows, dtype=F32)[:, None], (rows, GRID_W)).reshape(-1)
    col = jnp.broadcast_to(jnp.arange(GRID_W, dtype=F32)[None, :], (rows, GRID_W)).reshape(-1)
    half = HEAD_DIM // 4
    inv_freq = ROPE_BASE ** (-jnp.arange(half, dtype=F32) / half)
    ar = row[:, None] * inv_freq
    ac = col[:, None] * inv_freq
    ang = jnp.concatenate([ar, ar, ac, ac], axis=-1)
    cos = jnp.tile(jnp.cos(ang), (1, D // HEAD_DIM))
    sin = jnp.tile(jnp.sin(ang), (1, D // HEAD_DIM))
    lane = jnp.arange(D)
    first_half = (lane % (HEAD_DIM // 2)) < (HEAD_DIM // 4)
    sin_signed = jnp.where(first_half[None, :], -sin, sin)
    return cos, sin_signed, cos.T, sin_signed.T


def kernel(x_prompt, x_sample, cache_k, cache_v, c, c_ctx, w_ada, b_ada, norm1_g, norm2_g, w_in, q_norm_g, k_norm_g, lambda_q1, lambda_k1, lambda_q2, lambda_k2, subln_g, gmlp_ln_g, gmlp_ln_b, w_spatial, b_spatial, w_out, w_router, w_gate_e, w_up_e, w_down_e):
    n_p, t_p = x_prompt.shape[0], x_prompt.shape[1]
    n_s, t_s = x_sample.shape[0], x_sample.shape[1]
    n_ctx = cache_k.shape[2]
    l = 0

    cond = jnp.concatenate([c_ctx[None, :], c, jnp.zeros((MOD_ROWS - 1 - n_s, D), F32)], axis=0)
    m, lamv = _ada_call(cond, w_ada[l], b_ada[l][None, :], lambda_q1[l][None, :], lambda_k1[l][None, :],
                        lambda_q2[l][None, :], lambda_k2[l][None, :])
    m3 = m.reshape(MOD_ROWS, 1, 6 * D)
    w_in_bf, wkt = _wcast_call(w_in[l])

    group = jnp.arange(D) // HEAD_DIM
    gsum = (group[:, None] == jnp.arange(LANES)[None, :]).astype(BF)
    wr = jnp.pad(w_router[l], ((0, 0), (0, LANES - N_EXPERTS)))
    wr_hi = wr.astype(BF)
    consts = {
        "n1g": norm1_g[l][None, :],
        "n2g": norm2_g[l][None, :],
        "w_in": w_in_bf,
        "wkt": wkt,
        "qg": jnp.tile(q_norm_g[l] * (HEAD_DIM ** -0.5 * math.log2(math.e)), D // HEAD_DIM)[None, :],
        "kgt": jnp.broadcast_to(jnp.tile(k_norm_g[l], D // HEAD_DIM)[:, None], (D, TM)),
        "gsum": gsum,
        "gbcast": jnp.concatenate([gsum.T, gsum.T], axis=0),
        "lng": gmlp_ln_g[l][None, :],
        "lnb": gmlp_ln_b[l][None, :],
        "wsp": w_spatial[l].astype(BF),
        "bsp": jnp.repeat(b_spatial[l].T, D // N_GROUPS, axis=1),
        "w_out": w_out[l].astype(BF),
        "wr_hi": wr_hi,
        "wr_lo": (wr - wr_hi.astype(F32)).astype(BF),
    }
    consts["sg"] = subln_g[l][None, :]

    def prompt_row(i, tm):
        return 0

    def sample_row(i, tm):
        return 1 + i // (t_s // tm)

    outs = []
    kv_out = None
    for x, n_req, n_tok, mod_row, mod_row_req, is_sample in (
            (x_prompt, n_p, t_p, prompt_row, lambda b: 0, False),
            (x_sample, n_s, t_s, sample_row, lambda b: 1 + b, True)):
        xf = x.reshape(n_req * n_tok, D)
        tabs = _rope_tables(n_tok) if is_sample else None
        q, kt, v, sga, tb = _inproj_call(xf, m3, mod_row, tabs, BF if is_sample else F32, consts, n_req, n_tok)
        if is_sample:
            ctx = (jnp.transpose(cache_k[:, l], (0, 2, 3, 4, 1)).reshape(n_req, D, n_ctx),
                   cache_v[:, l].reshape(n_req * n_ctx, D))
        else:
            ctx = None
            kv_out = (kt, v)
        x1, h2, lg = _attn_call(q, kt, v, ctx, lamv, sga, tb, xf, m3, mod_row_req, consts, n_req, n_tok)
        xe, gates, slots = _route_call(lg, h2, n_req, n_tok)
        outs.append((x1, xe, gates, slots, n_req, n_tok, is_sample))

    (x1p, xep, gp, rtp, _, _, _), (x1s, xes, gs, rts, _, _, _) = outs
    yp, ys = _experts_call(xep, xes, gp, gs, w_gate_e[l], w_up_e[l], w_down_e[l])
    y_prompt = _scatter_call(rtp, yp, x1p, m3, lambda b: 0, n_p, t_p).reshape(x_prompt.shape)
    y_sample = _scatter_call(rts, ys, x1s, m3, lambda b: 1 + b, n_s, t_s).reshape(x_sample.shape)
    new_k = jnp.transpose(kv_out[0].reshape(n_p, N_HEADS, 2, HEAD_DIM, t_p), (0, 4, 1, 2, 3))
    new_k = new_k.reshape(n_p, 1, t_p, N_HEADS, 2, HEAD_DIM)
    new_v = kv_out[1].reshape(n_p, 1, t_p, N_HEADS, V_HEAD_DIM)
    return (y_prompt, y_sample, new_k, new_v)
```

```python
import functools
import math

import jax
import jax.numpy as jnp
from jax import lax
from jax.experimental import pallas as pl
from jax.experimental.pallas import tpu as pltpu

D = 1024
N_HEADS = 8
HEAD_DIM = 64
V_HEAD_DIM = 128
GRID_W = 64
ROPE_BASE = 10000.0
CHUNK = 128
N_GROUPS = 8
N_EXPERTS = 16
CAPACITY_FACTOR = 2
D_EXPERT = 2048
N_SEG = 7
EPS = 1e-6
LAMBDA_INIT = 0.8 - 0.6 * math.exp(-0.3 * 0)

LANES = 128
MOD_ROWS = 8
VMEM_LIMIT = 56 * 1024 * 1024

BF = jnp.bfloat16
F32 = jnp.float32


def _dot(a, b):
    return jnp.dot(a, b, preferred_element_type=F32)


def _dot_nt(a, b):
    return lax.dot_general(a, b, (((1,), (1,)), ((), ())), preferred_element_type=F32)


def _split_bf16(x):
    hi = x.astype(BF)
    lo = (x - hi.astype(F32)).astype(BF)
    return hi, lo


def _params(n_grid_dims):
    return pltpu.CompilerParams(
        dimension_semantics=("arbitrary",) * n_grid_dims, vmem_limit_bytes=VMEM_LIMIT)


def _const_spec(shape):
    nd = len(shape)
    return pl.BlockSpec(shape, lambda *_: (0,) * nd)


def _ada_kernel(cond_ref, w_ref, b_ref, lq1_ref, lk1_ref, lq2_ref, lk2_ref, m_ref, lam_ref):
    c = cond_ref[...]
    a = c * jax.nn.sigmoid(c)
    a_hi, a_lo = _split_bf16(a)
    w_hi, w_lo = _split_bf16(w_ref[...])
    m_ref[...] = _dot(a_hi, w_hi) + _dot(a_lo, w_hi) + _dot(a_hi, w_lo) + b_ref[...]
    s1 = jnp.sum(lq1_ref[...] * lk1_ref[...], axis=-1, keepdims=True)
    s2 = jnp.sum(lq2_ref[...] * lk2_ref[...], axis=-1, keepdims=True)
    lam = jnp.exp(s1) - jnp.exp(s2) + LAMBDA_INIT
    lam_ref[...] = jnp.broadcast_to(lam, lam_ref.shape)


def _ada_call(cond, w_ada, b_ada, lq1, lk1, lq2, lk2):
    tn = D
    vec = pl.BlockSpec((1, HEAD_DIM), lambda j: (0, 0))
    return pl.pallas_call(
        _ada_kernel,
        grid=(6 * D // tn,),
        in_specs=[
            pl.BlockSpec((MOD_ROWS, D), lambda j: (0, 0)),
            pl.BlockSpec((D, tn), lambda j: (0, j)),
            pl.BlockSpec((1, tn), lambda j: (0, j)),
            vec, vec, vec, vec,
        ],
        out_specs=[
            pl.BlockSpec((MOD_ROWS, tn), lambda j: (0, j)),
            pl.BlockSpec((MOD_ROWS, LANES), lambda j: (0, 0)),
        ],
        out_shape=[
            jax.ShapeDtypeStruct((MOD_ROWS, 6 * D), F32),
            jax.ShapeDtypeStruct((MOD_ROWS, LANES), F32),
        ],
        compiler_params=_params(1),
        name="ada",
    )(cond, w_ada, b_ada, lq1, lk1, lq2, lk2)


K_SEG = 1


def _wcast_kernel(w_ref, wb_ref, wkt_ref):
    w = w_ref[...]
    wb_ref[...] = w.astype(BF)

    @pl.when(pl.program_id(0) == K_SEG)
    def _():
        wkt_ref[...] = w.T.astype(BF)


def _wcast_call(w_in):
    return pl.pallas_call(
        _wcast_kernel,
        grid=(N_SEG,),
        in_specs=[pl.BlockSpec((D, D), lambda j: (0, j))],
        out_specs=[pl.BlockSpec((D, D), lambda j: (0, j)), pl.BlockSpec((D, D), lambda j: (0, 0))],
        out_shape=[jax.ShapeDtypeStruct((D, N_SEG * D), BF), jax.ShapeDtypeStruct((D, D), BF)],
        compiler_params=_params(1),
        name="wcast",
    )(w_in)


def _rms(x, g):
    return x * lax.rsqrt(jnp.mean(x * x, axis=-1, keepdims=True) + EPS) * g


def _head_norm(t, g, gsum, gbcast):
    ss = _dot((t * t).astype(BF), gsum)
    inv = lax.rsqrt(ss * (1.0 / HEAD_DIM) + EPS)
    hi, lo = _split_bf16(inv)
    bc = _dot(jnp.concatenate([hi, lo], axis=-1), gbcast)
    return t * bc * g


def _rope(t, cos, sin_signed, first_half):
    q = HEAD_DIM // 4
    fwd = pltpu.roll(t, D - q, axis=1)
    bwd = pltpu.roll(t, q, axis=1)
    return t * cos + jnp.where(first_half, fwd, bwd) * sin_signed


def _rope_t(t, cos_t, sin_signed_t):
    n = t.shape[1]
    q = HEAD_DIM // 4
    t3 = t.reshape(D // HEAD_DIM, HEAD_DIM, n)
    rot = jnp.concatenate([t3[:, q:2 * q], t3[:, 0:q], t3[:, 3 * q:4 * q], t3[:, 2 * q:3 * q]], axis=1)
    return t * cos_t + rot.reshape(D, n) * sin_signed_t


def _inproj_kernel(rope, x_ref, m_ref, n1g_ref, w_ref, wkt_ref, qg_ref, kgt_ref, gsum_ref, gbcast_ref,
                   lng_ref, lnb_ref, wsp_ref, bsp_ref, *rest):
    if rope:
        cos_ref, sin_ref, cos_t_ref, sin_t_ref, q_out, kt_out, v_out, sga_out, tb_out = rest
    else:
        q_out, kt_out, v_out, sga_out, tb_out = rest
    tm = x_ref.shape[0]
    m = m_ref[0]
    shift1, scale1 = m[:, 0:D], m[:, D:2 * D]
    h = (_rms(x_ref[...], n1g_ref[...]) * (1.0 + scale1) + shift1).astype(BF)

    def seg(j):
        return _dot(h, w_ref[:, j * D:(j + 1) * D])

    q = _head_norm(seg(0), qg_ref[...], gsum_ref[...], gbcast_ref[...])
    if rope:
        lane = lax.broadcasted_iota(jnp.int32, (1, D), 1)
        q = _rope(q, cos_ref[...], sin_ref[...], (lane % (HEAD_DIM // 2)) < (HEAD_DIM // 4))
    q_out[...] = q.astype(q_out.dtype)

    kt = _dot_nt(wkt_ref[...], h)
    k3 = kt.reshape(D // HEAD_DIM, HEAD_DIM, tm)
    inv = lax.rsqrt(jnp.mean(k3 * k3, axis=1, keepdims=True) + EPS)
    kt = (k3 * inv).reshape(D, tm) * kgt_ref[...]
    if rope:
        kt = _rope_t(kt, cos_t_ref[...], sin_t_ref[...])
    kt_out[0] = kt.astype(kt_out.dtype)

    v_out[...] = seg(2).astype(v_out.dtype)
    sga_out[...] = jax.nn.sigmoid(seg(5)).astype(sga_out.dtype)

    zv = jax.nn.gelu(seg(4))
    mu = jnp.mean(zv, axis=-1, keepdims=True)
    zc = zv - mu
    var = jnp.mean(zc * zc, axis=-1, keepdims=True)
    zvn = (zc * lax.rsqrt(var + EPS) * lng_ref[...] + lnb_ref[...]).astype(BF)
    pre = jax.nn.gelu(seg(3)) * jax.nn.sigmoid(seg(6))
    for c in range(tm // CHUNK):
        rows = slice(c * CHUNK, (c + 1) * CHUNK)
        for g in range(N_GROUPS):
            cols = slice(g * LANES, (g + 1) * LANES)
            sp = _dot(wsp_ref[g], zvn[rows, cols]) + bsp_ref[:, cols]
            tb_out[rows, cols] = (pre[rows, cols] * sp).astype(tb_out.dtype)


TM = 256


def _inproj_call(x, m3, mod_row, rope_tabs, kv_dtype, consts, n_req, n_tok):
    t = x.shape[0]
    tm = TM
    per_req = n_tok // tm
    rope = rope_tabs is not None
    tok = pl.BlockSpec((tm, D), lambda i: (i, 0))
    in_specs = [
        tok,
        pl.BlockSpec((1, 1, 6 * D), lambda i: (mod_row(i, tm), 0, 0)),
        _const_spec((1, D)),
        pl.BlockSpec((D, N_SEG * D), lambda i: (0, 0), pipeline_mode=pl.Buffered(1)),
        _const_spec((D, D)),
        _const_spec((1, D)), _const_spec((D, tm)),
        _const_spec((D, LANES)), _const_spec((2 * LANES, D)),
        _const_spec((1, D)), _const_spec((1, D)),
        _const_spec((N_GROUPS, CHUNK, CHUNK)), _const_spec((CHUNK, D)),
    ]
    args = [x, m3, consts["n1g"], consts["w_in"], consts["wkt"], consts["qg"], consts["kgt"], consts["gsum"],
            consts["gbcast"], consts["lng"], consts["lnb"], consts["wsp"], consts["bsp"]]
    if rope:
        tab = pl.BlockSpec((tm, D), lambda i: (i % per_req, 0))
        tab_t = pl.BlockSpec((D, tm), lambda i: (0, i % per_req))
        in_specs += [tab, tab, tab_t, tab_t]
        args += list(rope_tabs)
    kt_spec = pl.BlockSpec((1, D, tm), lambda i: (i // per_req, 0, i % per_req))
    return pl.pallas_call(
        functools.partial(_inproj_kernel, rope),
        grid=(t // tm,),
        in_specs=in_specs,
        out_specs=[tok, kt_spec, tok, tok, tok],
        out_shape=[
            jax.ShapeDtypeStruct((t, D), BF),
            jax.ShapeDtypeStruct((n_req, D, n_tok), kv_dtype),
            jax.ShapeDtypeStruct((t, D), kv_dtype),
            jax.ShapeDtypeStruct((t, D), BF),
            jax.ShapeDtypeStruct((t, D), BF),
        ],
        compiler_params=_params(1),
        name="inproj_rope" if rope else "inproj",
    )(*args)


SCORE_GROUP_ELEMS = 1 << 21


def _attn_kernel(has_ctx, heads_per_group, q_ref, kt_ref, v_ref, *rest):
    if has_ctx:
        kct_ref, vc_ref, *rest = rest
    (lam_ref, sg_ref, sga_ref, tb_ref, x_ref, m_ref, wo_ref, n2g_ref, wr_hi_ref, wr_lo_ref,
     x1_out, h2_out, lg_out, merged_s) = rest
    tq = q_ref.shape[0]
    lam = lam_ref[0:1, 0:1]
    lane = lax.broadcasted_iota(jnp.int32, (1, LANES), 1)
    first = lane < HEAD_DIM
    zero = jnp.zeros((), BF)
    ones_col = jnp.where(lane == 0, 1.0, 0.0).astype(BF)

    def head_values(ref, h):
        v = ref[:, h * V_HEAD_DIM:(h + 1) * V_HEAD_DIM].astype(BF)
        return jnp.concatenate([v, jnp.broadcast_to(ones_col, (v.shape[0], LANES))], axis=1)

    def head_cols(h):
        return slice(h * LANES, (h + 1) * LANES)

    def head_scores(h):
        q = q_ref[:, head_cols(h)]
        qz = jnp.concatenate([jnp.where(first, q, zero), jnp.where(first, zero, q)], axis=0)
        parts = [_dot(qz, kt_ref[0, head_cols(h), :].astype(BF))]
        if has_ctx:
            parts.append(_dot(qz, kct_ref[0, head_cols(h), :].astype(BF)))
        return parts

    def row_max(scores):
        return functools.reduce(jnp.maximum, [jnp.max(s, axis=-1, keepdims=True) for s in scores])

    def head_pv(h, scores, mx):
        vals = [head_values(v_ref, h)] + ([head_values(vc_ref, h)] if has_ctx else [])
        return functools.reduce(jnp.add, [_dot(jnp.exp2(s - mx).astype(BF), v) for s, v in zip(scores, vals)])

    def head_finish(h, ob):
        den = ob[:, V_HEAD_DIM:V_HEAD_DIM + 1]
        o = ob[:tq, :V_HEAD_DIM] * (1.0 / den[:tq]) - ob[tq:, :V_HEAD_DIM] * (lam / den[tq:])
        o = _rms(o, sg_ref[...]) * (1.0 - LAMBDA_INIT)
        cols = head_cols(h)
        merged_s[:, cols] = (sga_ref[:, cols].astype(F32) * o + tb_ref[:, cols].astype(F32)).astype(BF)

    for g0 in range(0, N_HEADS, heads_per_group):
        group = range(g0, g0 + heads_per_group)
        scores = [head_scores(h) for h in group]
        maxes = [row_max(s) for s in scores]
        outs = [head_pv(h, s, mx) for h, s, mx in zip(group, scores, maxes)]
        for h, ob in zip(group, outs):
            head_finish(h, ob)

    m = m_ref[0]
    gate1, shift2, scale2 = m[:, 2 * D:3 * D], m[:, 3 * D:4 * D], m[:, 4 * D:5 * D]
    x1 = x_ref[...] + gate1 * _dot(merged_s[...], wo_ref[...])
    x1_out[...] = x1
    h2 = _rms(x1, n2g_ref[...]) * (1.0 + scale2) + shift2
    h2_out[...] = h2.astype(BF)
    hi, lo = _split_bf16(h2)
    wr_hi = wr_hi_ref[...]
    lg_out[...] = _dot(hi, wr_hi) + _dot(lo, wr_hi) + _dot(hi, wr_lo_ref[...])


def _attn_call(q, kt, v, ctx, lamv, sga, tb, x, m3, mod_row_req, consts, n_req, n_tok):
    tq = TM
    nqb = n_tok // tq
    has_ctx = ctx is not None
    tok = pl.BlockSpec((tq, D), lambda b, i: (b * nqb + i, 0))

    in_specs = [tok, pl.BlockSpec((1, D, n_tok), lambda b, i: (b, 0, 0)),
                pl.BlockSpec((n_tok, D), lambda b, i: (b, 0))]
    args = [q, kt, v]
    if has_ctx:
        n_ctx = ctx[0].shape[2]
        in_specs += [pl.BlockSpec((1, D, n_ctx), lambda b, i: (b, 0, 0)),
                     pl.BlockSpec((n_ctx, D), lambda b, i: (b, 0))]
        args += list(ctx)
    in_specs += [
        _const_spec((MOD_ROWS, LANES)), _const_spec((1, V_HEAD_DIM)),
        tok, tok, tok,
        pl.BlockSpec((1, 1, 6 * D), lambda b, i: (mod_row_req(b), 0, 0)),
        _const_spec((D, D)), _const_spec((1, D)),
        _const_spec((D, LANES)), _const_spec((D, LANES)),
    ]
    args += [lamv, consts["sg"], sga, tb, x, m3, consts["w_out"], consts["n2g"], consts["wr_hi"], consts["wr_lo"]]
    t = n_req * n_tok
    n_keys = n_tok + (ctx[0].shape[2] if has_ctx else 0)
    heads_per_group = max(1, min(N_HEADS, SCORE_GROUP_ELEMS // (2 * tq * n_keys)))
    return pl.pallas_call(
        functools.partial(_attn_kernel, has_ctx, heads_per_group),
        grid=(n_req, nqb),
        in_specs=in_specs,
        out_specs=[tok, tok, pl.BlockSpec((tq, LANES), lambda b, i: (b * nqb + i, 0))],
        out_shape=[
            jax.ShapeDtypeStruct((t, D), F32),
            jax.ShapeDtypeStruct((t, D), BF),
            jax.ShapeDtypeStruct((t, LANES), F32),
        ],
        scratch_shapes=[pltpu.VMEM((tq, D), BF)],
        compiler_params=_params(2),
        name="attn_ctx" if has_ctx else "attn",
    )(*args)


GATHER_ROWS = 512
KEY_BITS = 31
ROUTE_UNROLL_ELEMS = 1 << 19


def _route_kernel(cap, unroll, lg_ref, h2_ref, before_ref, xe_ref, gate_ref, slot_ref, aff_t_s, slot_t_s, p_s):
    lane = lax.broadcasted_iota(jnp.int32, (1, LANES), 1)
    valid = lane < N_EXPERTS
    lg = jnp.where(valid, lg_ref[...], -1e30)
    ex = jnp.where(valid, jnp.exp(lg - jnp.max(lg, axis=-1, keepdims=True)), 0.0)
    aff = ex / jnp.sum(ex, axis=-1, keepdims=True)

    def count(mask):
        return jnp.sum(jnp.where(mask, 1.0, 0.0), axis=0, keepdims=True)

    kth_bits = jnp.zeros((1, LANES), jnp.int32)
    for bit in range(KEY_BITS - 1, -1, -1):
        cand = kth_bits | (1 << bit)
        enough = count(aff >= lax.bitcast_convert_type(cand, F32)) >= cap
        kth_bits = jnp.where(enough, cand, kth_bits)
    kth = lax.bitcast_convert_type(kth_bits, F32)
    above = aff > kth
    tied = aff == kth
    need = cap - count(above)
    before = before_ref[...]
    tied_before = _dot(before, jnp.where(tied, 1.0, 0.0).astype(BF))
    chosen = jnp.where(above, 1.0, jnp.where(tied, jnp.where(tied_before < need, 1.0, 0.0), 0.0))
    slot = _dot(before, chosen.astype(BF))
    slot = jnp.where(valid, jnp.where(chosen > 0.0, slot, float(cap)), float(cap))
    slot_ref[...] = slot
    slot_t_s[...] = slot.T
    aff_t_s[...] = aff.T
    slot_ids = lax.broadcasted_iota(jnp.int32, (cap, 1), 0).astype(F32)

    def per_expert(e, carry):
        hit = slot_t_s[pl.ds(e, 1), :] == slot_ids
        p_s[pl.ds(pl.multiple_of(e * cap, cap), cap), :] = jnp.where(hit, 1.0, 0.0).astype(BF)
        gate_ref[e] = jnp.sum(jnp.where(hit, aff_t_s[pl.ds(e, 1), :], 0.0), axis=-1, keepdims=True)
        return carry

    lax.fori_loop(0, N_EXPERTS, per_expert, 0, unroll=unroll)
    h2 = h2_ref[...]
    rows = min(GATHER_ROWS, N_EXPERTS * cap)
    e_per = rows // cap
    for t in range(N_EXPERTS * cap // rows):
        xe = _dot(p_s[t * rows:(t + 1) * rows, :], h2).astype(BF)
        xe_ref[t * e_per:(t + 1) * e_per] = xe.reshape(e_per, cap, D)


def _route_call(lg, h2, n_req, n_tok):
    cap = max(1, CAPACITY_FACTOR * n_tok // N_EXPERTS)
    before = (lax.broadcasted_iota(jnp.int32, (n_tok, n_tok), 1)
              < lax.broadcasted_iota(jnp.int32, (n_tok, n_tok), 0)).astype(BF)
    unroll = max(1, min(N_EXPERTS, ROUTE_UNROLL_ELEMS // (cap * n_tok)))
    return pl.pallas_call(
        functools.partial(_route_kernel, cap, unroll),
        grid=(n_req,),
        in_specs=[
            pl.BlockSpec((n_tok, LANES), lambda b: (b, 0)),
            pl.BlockSpec((n_tok, D), lambda b: (b, 0)),
            _const_spec((n_tok, n_tok)),
        ],
        out_specs=[
            pl.BlockSpec((N_EXPERTS, cap, D), lambda b: (0, b, 0)),
            pl.BlockSpec((N_EXPERTS, cap, 1), lambda b: (0, b, 0)),
            pl.BlockSpec((n_tok, LANES), lambda b: (b, 0)),
        ],
        out_shape=[
            jax.ShapeDtypeStruct((N_EXPERTS, n_req * cap, D), BF),
            jax.ShapeDtypeStruct((N_EXPERTS, n_req * cap, 1), F32),
            jax.ShapeDtypeStruct((n_req * n_tok, LANES), F32),
        ],
        scratch_shapes=[
            pltpu.VMEM((LANES, n_tok), F32),
            pltpu.VMEM((LANES, n_tok), F32),
            pltpu.VMEM((N_EXPERTS * cap, n_tok), BF),
        ],
        compiler_params=_params(1),
        name="route",
    )(lg, h2, before)


EXPERT_BLOCK = 1024
EXPERT_SUB = 512


def _experts_kernel(xa_ref, xb_ref, ga_ref, gb_ref, wg_ref, wu_ref, wd_ref, ya_ref, yb_ref, x_s, acc_s):
    f = pl.program_id(1)
    ra = xa_ref.shape[1]

    @pl.when(f == 0)
    def _():
        x_s[0:ra, :] = xa_ref[0]
        x_s[ra:, :] = xb_ref[0]
        acc_s[...] = jnp.zeros(acc_s.shape, F32)

    x = x_s[...]
    for c in range(wg_ref.shape[2] // EXPERT_SUB):
        cs = slice(c * EXPERT_SUB, (c + 1) * EXPERT_SUB)
        gate = _dot(x, wg_ref[0, :, cs].astype(BF))
        up = _dot(x, wu_ref[0, :, cs].astype(BF))
        hid = (gate * jax.nn.sigmoid(gate) * up).astype(BF)
        acc_s[...] += _dot(hid, wd_ref[0, cs, :].astype(BF))

    @pl.when(f == pl.num_programs(1) - 1)
    def _():
        ya_ref[0] = (acc_s[0:ra, :] * ga_ref[0]).astype(ya_ref.dtype)
        yb_ref[0] = (acc_s[ra:, :] * gb_ref[0]).astype(yb_ref.dtype)


def _experts_call(xa, xb, ga, gb, wg, wu, wd):
    ra, rb = xa.shape[1], xb.shape[1]
    tf = EXPERT_BLOCK
    xa_spec = pl.BlockSpec((1, ra, D), lambda e, f: (e, 0, 0))
    xb_spec = pl.BlockSpec((1, rb, D), lambda e, f: (e, 0, 0))
    return pl.pallas_call(
        _experts_kernel,
        grid=(N_EXPERTS, D_EXPERT // tf),
        in_specs=[
            xa_spec, xb_spec,
            pl.BlockSpec((1, ra, 1), lambda e, f: (e, 0, 0)),
            pl.BlockSpec((1, rb, 1), lambda e, f: (e, 0, 0)),
            pl.BlockSpec((1, D, tf), lambda e, f: (e, 0, f)),
            pl.BlockSpec((1, D, tf), lambda e, f: (e, 0, f)),
            pl.BlockSpec((1, tf, D), lambda e, f: (e, f, 0)),
        ],
        out_specs=[xa_spec, xb_spec],
        out_shape=[
            jax.ShapeDtypeStruct((N_EXPERTS, ra, D), BF),
            jax.ShapeDtypeStruct((N_EXPERTS, rb, D), BF),
        ],
        scratch_shapes=[pltpu.VMEM((ra + rb, D), BF), pltpu.VMEM((ra + rb, D), F32)],
        compiler_params=_params(2),
        name="experts",
    )(xa, xb, ga, gb, wg, wu, wd)


def _scatter_kernel(cap, slot_ref, y_ref, x1_ref, m_ref, expand_ref, out_ref):
    gate2 = m_ref[0][:, 5 * D:6 * D]
    slot_wide = _dot(slot_ref[...].astype(BF), expand_ref[...])
    slot_ids = (lax.broadcasted_iota(jnp.int32, (1, N_EXPERTS * cap), 1) % cap).astype(F32)
    onehot = jnp.where(slot_wide == slot_ids, 1.0, 0.0).astype(BF)
    y = y_ref[...].reshape(N_EXPERTS * cap, D)
    out_ref[...] = x1_ref[...] + gate2 * _dot(onehot, y)


def _scatter_call(slots, y, x1, m3, mod_row_req, n_req, n_tok):
    cap = y.shape[1] // n_req
    e_ids = lax.broadcasted_iota(jnp.int32, (LANES, N_EXPERTS * cap), 0)
    c_ids = lax.broadcasted_iota(jnp.int32, (LANES, N_EXPERTS * cap), 1) // cap
    expand = (e_ids == c_ids).astype(BF)
    return pl.pallas_call(
        functools.partial(_scatter_kernel, cap),
        grid=(n_req,),
        in_specs=[
            pl.BlockSpec((n_tok, LANES), lambda b: (b, 0)),
            pl.BlockSpec((N_EXPERTS, cap, D), lambda b: (0, b, 0)),
            pl.BlockSpec((n_tok, D), lambda b: (b, 0)),
            pl.BlockSpec((1, 1, 6 * D), lambda b: (mod_row_req(b), 0, 0)),
            _const_spec((LANES, N_EXPERTS * cap)),
        ],
        out_specs=pl.BlockSpec((n_tok, D), lambda b: (b, 0)),
        out_shape=jax.ShapeDtypeStruct((n_req * n_tok, D), F32),
        compiler_params=_params(1),
        name="scatter",
    )(slots, y, x1, m3, expand)


def _rope_tables(n_tokens):
    rows = n_tokens // GRID_W
    row = jnp.broadcast_to(jnp.arange(rows, dtype=F32)[:, None], (rows, GRID_W)).reshape(-1)
    col = jnp.broadcast_to(jnp.arange(GRID_W, dtype=F32)[None, :], (rows, GRID_W)).reshape(-1)
    half = HEAD_DIM // 4
    inv_freq = ROPE_BASE ** (-jnp.arange(half, dtype=F32) / half)
    ar = row[:, None] * inv_freq
    ac = col[:, None] * inv_freq
    ang = jnp.concatenate([ar, ar, ac, ac], axis=-1)
    cos = jnp.tile(jnp.cos(ang), (1, D // HEAD_DIM))
    sin = jnp.tile(jnp.sin(ang), (1, D // HEAD_DIM))
    lane = jnp.arange(D)
    first_half = (lane % (HEAD_DIM // 2)) < (HEAD_DIM // 4)
    sin_signed = jnp.where(first_half[None, :], -sin, sin)
    return cos, sin_signed, cos.T, sin_signed.T


def kernel(x_prompt, x_sample, cache_k, cache_v, c, c_ctx, w_ada, b_ada, norm1_g, norm2_g, w_in, q_norm_g, k_norm_g, lambda_q1, lambda_k1, lambda_q2, lambda_k2, subln_g, gmlp_ln_g, gmlp_ln_b, w_spatial, b_spatial, w_out, w_router, w_gate_e, w_up_e, w_down_e):
    n_p, t_p = x_prompt.shape[0], x_prompt.shape[1]
    n_s, t_s = x_sample.shape[0], x_sample.shape[1]
    n_ctx = cache_k.shape[2]
    l = 0

    cond = jnp.concatenate([c_ctx[None, :], c, jnp.zeros((MOD_ROWS - 1 - n_s, D), F32)], axis=0)
    m, lamv = _ada_call(cond, w_ada[l], b_ada[l][None, :], lambda_q1[l][None, :], lambda_k1[l][None, :],
                        lambda_q2[l][None, :], lambda_k2[l][None, :])
    m3 = m.reshape(MOD_ROWS, 1, 6 * D)
    w_in_bf, wkt = _wcast_call(w_in[l])

    group = jnp.arange(D) // HEAD_DIM
    gsum = (group[:, None] == jnp.arange(LANES)[None, :]).astype(BF)
    wr = jnp.pad(w_router[l], ((0, 0), (0, LANES - N_EXPERTS)))
    wr_hi = wr.astype(BF)
    consts = {
        "n1g": norm1_g[l][None, :],
        "n2g": norm2_g[l][None, :],
        "w_in": w_in_bf,
        "wkt": wkt,
        "qg": jnp.tile(q_norm_g[l] * (HEAD_DIM ** -0.5 * math.log2(math.e)), D // HEAD_DIM)[None, :],
        "kgt": jnp.broadcast_to(jnp.tile(k_norm_g[l], D // HEAD_DIM)[:, None], (D, TM)),
        "gsum": gsum,
        "gbcast": jnp.concatenate([gsum.T, gsum.T], axis=0),
        "lng": gmlp_ln_g[l][None, :],
        "lnb": gmlp_ln_b[l][None, :],
        "wsp": w_spatial[l].astype(BF),
        "bsp": jnp.repeat(b_spatial[l].T, D // N_GROUPS, axis=1),
        "w_out": w_out[l].astype(BF),
        "wr_hi": wr_hi,
        "wr_lo": (wr - wr_hi.astype(F32)).astype(BF),
    }
    consts["sg"] = subln_g[l][None, :]

    def prompt_row(i, tm):
        return 0

    def sample_row(i, tm):
        return 1 + i // (t_s // tm)

    outs = []
    kv_out = None
    for x, n_req, n_tok, mod_row, mod_row_req, is_sample in (
            (x_prompt, n_p, t_p, prompt_row, lambda b: 0, False),
            (x_sample, n_s, t_s, sample_row, lambda b: 1 + b, True)):
        xf = x.reshape(n_req * n_tok, D)
        tabs = _rope_tables(n_tok) if is_sample else None
        q, kt, v, sga, tb = _inproj_call(xf, m3, mod_row, tabs, BF if is_sample else F32, consts, n_req, n_tok)
        if is_sample:
            ctx = (jnp.transpose(cache_k[:, l], (0, 2, 3, 4, 1)).reshape(n_req, D, n_ctx),
                   cache_v[:, l].reshape(n_req * n_ctx, D))
        else:
            ctx = None
            kv_out = (kt, v)
        x1, h2, lg = _attn_call(q, kt, v, ctx, lamv, sga, tb, xf, m3, mod_row_req, consts, n_req, n_tok)
        xe, gates, slots = _route_call(lg, h2, n_req, n_tok)
        outs.append((x1, xe, gates, slots, n_req, n_tok, is_sample))

    (x1p, xep, gp, rtp, _, _, _), (x1s, xes, gs, rts, _, _, _) = outs
    yp, ys = _experts_call(xep, xes, gp, gs, w_gate_e[l], w_up_e[l], w_down_e[l])
    y_prompt = _scatter_call(rtp, yp, x1p, m3, lambda b: 0, n_p, t_p).reshape(x_prompt.shape)
    y_sample = _scatter_call(rts, ys, x1s, m3, lambda b: 1 + b, n_s, t_s).reshape(x_sample.shape)
    new_k = jnp.transpose(kv_out[0].reshape(n_p, N_HEADS, 2, HEAD_DIM, t_p), (0, 4, 1, 2, 3))
    new_k = new_k.reshape(n_p, 1, t_p, N_HEADS, 2, HEAD_DIM)
    new_v = kv_out[1].reshape(n_p, 1, t_p, N_HEADS, V_HEAD_DIM)
    return (y_prompt, y_sample, new_k, new_v)
```

```python
import functools
import math

import jax
import jax.numpy as jnp
from jax import lax
from jax.experimental import pallas as pl
from jax.experimental.pallas import tpu as pltpu

D = 1024
N_HEADS = 8
HEAD_DIM = 64
V_HEAD_DIM = 128
GRID_W = 64
ROPE_BASE = 10000.0
CHUNK = 128
N_GROUPS = 8
N_EXPERTS = 16
CAPACITY_FACTOR = 2
D_EXPERT = 2048
N_SEG = 7
K_SEG = 1
EPS = 1e-6
LAMBDA_INIT = 0.8 - 0.6 * math.exp(-0.3 * 0)

LANES = 128
MOD_ROWS = 8
VMEM_LIMIT = 56 * 1024 * 1024
TM = 256

BF = jnp.bfloat16
F32 = jnp.float32


def _dot(a, b):
    return jnp.dot(a, b, preferred_element_type=F32)


def _dot_nt(a, b):
    return lax.dot_general(a, b, (((1,), (1,)), ((), ())), preferred_element_type=F32)


def _split_bf16(x):
    hi = x.astype(BF)
    lo = (x - hi.astype(F32)).astype(BF)
    return hi, lo


def _rms(x, g):
    return x * lax.rsqrt(jnp.mean(x * x, axis=-1, keepdims=True) + EPS) * g


def _params(n_grid_dims):
    return pltpu.CompilerParams(
        dimension_semantics=("arbitrary",) * n_grid_dims, vmem_limit_bytes=VMEM_LIMIT)


def _const_spec(shape):
    nd = len(shape)
    return pl.BlockSpec(shape, lambda *_: (0,) * nd)


def _ada_kernel(cond_ref, w_ref, b_ref, lq1_ref, lk1_ref, lq2_ref, lk2_ref, m_ref, lam_ref):
    c = cond_ref[...]
    a = c * jax.nn.sigmoid(c)
    a_hi, a_lo = _split_bf16(a)
    w_hi, w_lo = _split_bf16(w_ref[...])
    m_ref[...] = _dot(a_hi, w_hi) + _dot(a_lo, w_hi) + _dot(a_hi, w_lo) + b_ref[...]
    s1 = jnp.sum(lq1_ref[...] * lk1_ref[...], axis=-1, keepdims=True)
    s2 = jnp.sum(lq2_ref[...] * lk2_ref[...], axis=-1, keepdims=True)
    lam = jnp.exp(s1) - jnp.exp(s2) + LAMBDA_INIT
    lam_ref[...] = jnp.broadcast_to(lam, lam_ref.shape)


def _ada_call(cond, w_ada, b_ada, lq1, lk1, lq2, lk2):
    tn = D
    vec = pl.BlockSpec((1, HEAD_DIM), lambda j: (0, 0))
    return pl.pallas_call(
        _ada_kernel,
        grid=(6 * D // tn,),
        in_specs=[
            pl.BlockSpec((MOD_ROWS, D), lambda j: (0, 0)),
            pl.BlockSpec((D, tn), lambda j: (0, j)),
            pl.BlockSpec((1, tn), lambda j: (0, j)),
            vec, vec, vec, vec,
        ],
        out_specs=[
            pl.BlockSpec((MOD_ROWS, tn), lambda j: (0, j)),
            pl.BlockSpec((MOD_ROWS, LANES), lambda j: (0, 0)),
        ],
        out_shape=[
            jax.ShapeDtypeStruct((MOD_ROWS, 6 * D), F32),
            jax.ShapeDtypeStruct((MOD_ROWS, LANES), F32),
        ],
        compiler_params=_params(1),
        name="ada",
    )(cond, w_ada, b_ada, lq1, lk1, lq2, lk2)


def _wcast_kernel(w_ref, wb_ref, wkt_ref):
    w = w_ref[...]
    wb_ref[...] = w.astype(BF)

    @pl.when(pl.program_id(0) == K_SEG)
    def _():
        wkt_ref[...] = w.T.astype(BF)


def _wcast_call(w_in):
    return pl.pallas_call(
        _wcast_kernel,
        grid=(N_SEG,),
        in_specs=[pl.BlockSpec((D, D), lambda j: (0, j))],
        out_specs=[pl.BlockSpec((D, D), lambda j: (0, j)), pl.BlockSpec((D, D), lambda j: (0, 0))],
        out_shape=[jax.ShapeDtypeStruct((D, N_SEG * D), BF), jax.ShapeDtypeStruct((D, D), BF)],
        compiler_params=_params(1),
        name="wcast",
    )(w_in)


def _head_norm(t, g, gsum, gbcast):
    ss = _dot((t * t).astype(BF), gsum)
    inv = lax.rsqrt(ss * (1.0 / HEAD_DIM) + EPS)
    hi, lo = _split_bf16(inv)
    bc = _dot(jnp.concatenate([hi, lo], axis=-1), gbcast)
    return t * bc * g


def _lane_tile(tab):
    return jnp.concatenate([tab] * (D // LANES), axis=1)


def _rope(t, cos, sin_signed):
    q = HEAD_DIM // 4
    lane = lax.broadcasted_iota(jnp.int32, (1, D), 1)
    first_half = (lane % (HEAD_DIM // 2)) < q
    fwd = pltpu.roll(t, D - q, axis=1)
    bwd = pltpu.roll(t, q, axis=1)
    return t * _lane_tile(cos) + jnp.where(first_half, fwd, bwd) * _lane_tile(sin_signed)


def _rope_t(t3, cos_t, sin_signed_t):
    q = HEAD_DIM // 4
    rot = jnp.concatenate([t3[:, q:2 * q], t3[:, 0:q], t3[:, 3 * q:4 * q], t3[:, 2 * q:3 * q]], axis=1)
    return t3 * cos_t[None] + rot * sin_signed_t[None]


def _inproj_kernel(rope, row_of_step, x_ref, m_ref, n1g_ref, w_ref, wkt_ref, qg_ref, kgt_ref, gsum_ref,
                   gbcast_ref, lng_ref, lnb_ref, wsp_ref, bsp_ref, *rest):
    if rope:
        cos_ref, sin_ref, cos_t_ref, sin_t_ref, q_out, kt_out, v_out, sga_out, tb_out = rest
    else:
        q_out, kt_out, v_out, sga_out, tb_out = rest
    tm = x_ref.shape[0]
    m = m_ref[pl.ds(row_of_step(pl.program_id(0)), 1), :]
    shift1, scale1 = m[:, 0:D], m[:, D:2 * D]
    h = (_rms(x_ref[...], n1g_ref[...]) * (1.0 + scale1) + shift1).astype(BF)

    def seg(j):
        return _dot(h, w_ref[:, j * D:(j + 1) * D])

    zv = jax.nn.gelu(seg(4))
    mu = jnp.mean(zv, axis=-1, keepdims=True)
    zc = zv - mu
    var = jnp.mean(zc * zc, axis=-1, keepdims=True)
    zvn = (zc * lax.rsqrt(var + EPS) * lng_ref[...] + lnb_ref[...]).astype(BF)
    pre = jax.nn.gelu(seg(3)) * jax.nn.sigmoid(seg(6))
    for c in range(tm // CHUNK):
        rows = slice(c * CHUNK, (c + 1) * CHUNK)
        for g in range(N_GROUPS):
            cols = slice(g * LANES, (g + 1) * LANES)
            sp = _dot(wsp_ref[g], zvn[rows, cols]) + bsp_ref[:, cols]
            tb_out[rows, cols] = (pre[rows, cols] * sp).astype(tb_out.dtype)

    q = _head_norm(seg(0), qg_ref[...], gsum_ref[...], gbcast_ref[...])
    if rope:
        q = _rope(q, cos_ref[...], sin_ref[...])
    q_out[...] = q.astype(q_out.dtype)

    k3 = _dot_nt(wkt_ref[...], h).reshape(D // HEAD_DIM, HEAD_DIM, tm)
    inv = lax.rsqrt(jnp.mean(k3 * k3, axis=1, keepdims=True) + EPS)
    k3 = k3 * inv * kgt_ref[...][None]
    if rope:
        k3 = _rope_t(k3, cos_t_ref[...], sin_t_ref[...])
    kt_out[0] = k3.reshape(D, tm).astype(kt_out.dtype)

    sga_out[...] = jax.nn.sigmoid(seg(5)).astype(sga_out.dtype)
    v_out[...] = seg(2).astype(v_out.dtype)


def _inproj_call(x, m, row_of_step, rope_tabs, kv_dtype, consts, n_req, n_tok):
    t = x.shape[0]
    tm = TM
    per_req = n_tok // tm
    rope = rope_tabs is not None
    tok = pl.BlockSpec((tm, D), lambda i: (i, 0))
    in_specs = [
        tok,
        _const_spec((MOD_ROWS, 6 * D)),
        _const_spec((1, D)),
        pl.BlockSpec((D, N_SEG * D), lambda i: (0, 0), pipeline_mode=pl.Buffered(1)),
        _const_spec((D, D)),
        _const_spec((1, D)), _const_spec((HEAD_DIM, tm)),
        _const_spec((D, LANES)), _const_spec((2 * LANES, D)),
        _const_spec((1, D)), _const_spec((1, D)),
        _const_spec((N_GROUPS, CHUNK, CHUNK)), _const_spec((CHUNK, D)),
    ]
    args = [x, m, consts["n1g"], consts["w_in"], consts["wkt"], consts["qg"], consts["kgt"], consts["gsum"],
            consts["gbcast"], consts["lng"], consts["lnb"], consts["wsp"], consts["bsp"]]
    if rope:
        tab = pl.BlockSpec((tm, LANES), lambda i: (i % per_req, 0))
        tab_t = pl.BlockSpec((HEAD_DIM, tm), lambda i: (0, i % per_req))
        in_specs += [tab, tab, tab_t, tab_t]
        args += list(rope_tabs)
    kt_spec = pl.BlockSpec((1, D, tm), lambda i: (i // per_req, 0, i % per_req))
    return pl.pallas_call(
        functools.partial(_inproj_kernel, rope, row_of_step),
        grid=(t // tm,),
        in_specs=in_specs,
        out_specs=[tok, kt_spec, tok, tok, tok],
        out_shape=[
            jax.ShapeDtypeStruct((t, D), BF),
            jax.ShapeDtypeStruct((n_req, D, n_tok), kv_dtype),
            jax.ShapeDtypeStruct((t, D), kv_dtype),
            jax.ShapeDtypeStruct((t, D), BF),
            jax.ShapeDtypeStruct((t, D), BF),
        ],
        compiler_params=_params(1),
        name="inproj_rope" if rope else "inproj",
    )(*args)


SCORE_GROUP_ELEMS = 1 << 21


def _attn_kernel(has_ctx, heads_per_group, row_of_req, q_ref, kt_ref, v_ref, *rest):
    if has_ctx:
        kct_ref, vc_ref, *rest = rest
    (lam_ref, sg_ref, sga_ref, tb_ref, x_ref, m_ref, wo_ref, n2g_ref, wr_hi_ref, wr_lo_ref,
     x1_out, h2_out, lg_out, merged_s) = rest
    tq = q_ref.shape[0]
    lam = lam_ref[0:1, 0:1]
    lane = lax.broadcasted_iota(jnp.int32, (1, LANES), 1)
    first = lane < HEAD_DIM
    zero = jnp.zeros((), BF)
    ones_col = jnp.where(lane == 0, 1.0, 0.0).astype(BF)

    def head_values(ref, h):
        v = ref[:, h * V_HEAD_DIM:(h + 1) * V_HEAD_DIM].astype(BF)
        return jnp.concatenate([v, jnp.broadcast_to(ones_col, (v.shape[0], LANES))], axis=1)

    def head_cols(h):
        return slice(h * LANES, (h + 1) * LANES)

    def head_scores(h):
        q = q_ref[:, head_cols(h)]
        qz = jnp.concatenate([jnp.where(first, q, zero), jnp.where(first, zero, q)], axis=0)
        parts = [_dot(qz, kt_ref[0, head_cols(h), :].astype(BF))]
        if has_ctx:
            parts.append(_dot(qz, kct_ref[0, head_cols(h), :].astype(BF)))
        return parts

    def row_max(scores):
        return functools.reduce(jnp.maximum, [jnp.max(s, axis=-1, keepdims=True) for s in scores])

    def head_pv(h, scores, mx):
        vals = [head_values(v_ref, h)] + ([head_values(vc_ref, h)] if has_ctx else [])
        return functools.reduce(jnp.add, [_dot(jnp.exp2(s - mx).astype(BF), v) for s, v in zip(scores, vals)])

    def head_finish(h, ob):
        den = ob[:, V_HEAD_DIM:V_HEAD_DIM + 1]
        o = ob[:tq, :V_HEAD_DIM] * (1.0 / den[:tq]) - ob[tq:, :V_HEAD_DIM] * (lam / den[tq:])
        o = _rms(o, sg_ref[...]) * (1.0 - LAMBDA_INIT)
        cols = head_cols(h)
        merged_s[:, cols] = (sga_ref[:, cols].astype(F32) * o + tb_ref[:, cols].astype(F32)).astype(BF)

    for g0 in range(0, N_HEADS, heads_per_group):
        group = range(g0, g0 + heads_per_group)
        scores = [head_scores(h) for h in group]
        maxes = [row_max(s) for s in scores]
        outs = [head_pv(h, s, mx) for h, s, mx in zip(group, scores, maxes)]
        for h, ob in zip(group, outs):
            head_finish(h, ob)

    m = m_ref[pl.ds(row_of_req(pl.program_id(0)), 1), :]
    gate1, shift2, scale2 = m[:, 2 * D:3 * D], m[:, 3 * D:4 * D], m[:, 4 * D:5 * D]
    x1 = x_ref[...] + gate1 * _dot(merged_s[...], wo_ref[...])
    x1_out[...] = x1
    h2 = _rms(x1, n2g_ref[...]) * (1.0 + scale2) + shift2
    h2_out[...] = h2.astype(BF)
    hi, lo = _split_bf16(h2)
    wr_hi = wr_hi_ref[...]
    lg_out[...] = _dot(hi, wr_hi) + _dot(lo, wr_hi) + _dot(hi, wr_lo_ref[...])


def _attn_call(q, kt, v, ctx, lamv, sga, tb, x, m, row_of_req, consts, n_req, n_tok):
    tq = TM
    nqb = n_tok // tq
    has_ctx = ctx is not None
    tok = pl.BlockSpec((tq, D), lambda b, i: (b * nqb + i, 0))
    in_specs = [tok, pl.BlockSpec((1, D, n_tok), lambda b, i: (b, 0, 0)),
                pl.BlockSpec((n_tok, D), lambda b, i: (b, 0))]
    args = [q, kt, v]
    if has_ctx:
        n_ctx = ctx[0].shape[2]
        in_specs += [pl.BlockSpec((1, D, n_ctx), lambda b, i: (b, 0, 0)),
                     pl.BlockSpec((n_ctx, D), lambda b, i: (b, 0))]
        args += list(ctx)
    in_specs += [
        _const_spec((MOD_ROWS, LANES)), _const_spec((1, V_HEAD_DIM)),
        tok, tok, tok,
        _const_spec((MOD_ROWS, 6 * D)),
        _const_spec((D, D)), _const_spec((1, D)),
        _const_spec((D, LANES)), _const_spec((D, LANES)),
    ]
    args += [lamv, consts["sg"], sga, tb, x, m, consts["w_out"], consts["n2g"], consts["wr_hi"], consts["wr_lo"]]
    t = n_req * n_tok
    n_keys = n_tok + (ctx[0].shape[2] if has_ctx else 0)
    heads_per_group = max(1, min(N_HEADS, SCORE_GROUP_ELEMS // (2 * tq * n_keys)))
    return pl.pallas_call(
        functools.partial(_attn_kernel, has_ctx, heads_per_group, row_of_req),
        grid=(n_req, nqb),
        in_specs=in_specs,
        out_specs=[tok, tok, pl.BlockSpec((tq, LANES), lambda b, i: (b * nqb + i, 0))],
        out_shape=[
            jax.ShapeDtypeStruct((t, D), F32),
            jax.ShapeDtypeStruct((t, D), BF),
            jax.ShapeDtypeStruct((t, LANES), F32),
        ],
        scratch_shapes=[pltpu.VMEM((tq, D), BF)],
        compiler_params=_params(2),
        name="attn_ctx" if has_ctx else "attn",
    )(*args)


GATHER_ROWS = 512
KEY_BITS = 31
ROUTE_UNROLL_ELEMS = 1 << 19
ROUTE_STEP_TOKENS = 2048


def _route_kernel(cap, n_tok, unroll, lg_ref, h2_ref, before_ref, xe_ref, gate_ref, slot_ref,
                  aff_t_s, slot_t_s, p_s):
    reqs = range(lg_ref.shape[0] // n_tok)
    lane = lax.broadcasted_iota(jnp.int32, (1, LANES), 1)
    valid = lane < N_EXPERTS

    def rows(r):
        return slice(r * n_tok, (r + 1) * n_tok)

    def affinity(r):
        lg = jnp.where(valid, lg_ref[rows(r), :], -1e30)
        ex = jnp.where(valid, jnp.exp(lg - jnp.max(lg, axis=-1, keepdims=True)), 0.0)
        return ex / jnp.sum(ex, axis=-1, keepdims=True)

    def count(mask):
        return jnp.sum(jnp.where(mask, 1.0, 0.0), axis=0, keepdims=True)

    affs = [affinity(r) for r in reqs]
    kth_bits = [jnp.zeros((1, LANES), jnp.int32) for _ in reqs]
    for bit in range(KEY_BITS - 1, -1, -1):
        for r in reqs:
            cand = kth_bits[r] | (1 << bit)
            enough = count(affs[r] >= lax.bitcast_convert_type(cand, F32)) >= cap
            kth_bits[r] = jnp.where(enough, cand, kth_bits[r])

    before = before_ref[...]
    for r in reqs:
        aff = affs[r]
        kth = lax.bitcast_convert_type(kth_bits[r], F32)
        above = aff > kth
        tied = aff == kth
        need = cap - count(above)
        tied_before = _dot(before, jnp.where(tied, 1.0, 0.0).astype(BF))
        chosen = jnp.where(above, 1.0, jnp.where(tied, jnp.where(tied_before < need, 1.0, 0.0), 0.0))
        slot = _dot(before, chosen.astype(BF))
        slot = jnp.where(valid, jnp.where(chosen > 0.0, slot, float(cap)), float(cap))
        slot_ref[rows(r), :] = slot
        slot_t_s[r] = slot.T
        aff_t_s[r] = aff.T

    slot_ids = lax.broadcasted_iota(jnp.int32, (cap, 1), 0).astype(F32)

    def per_expert(e, carry):
        for r in reqs:
            hit = slot_t_s[r, pl.ds(e, 1), :] == slot_ids
            p_s[r, pl.ds(pl.multiple_of(e * cap, cap), cap), :] = jnp.where(hit, 1.0, 0.0).astype(BF)
            gate = jnp.sum(jnp.where(hit, aff_t_s[r, pl.ds(e, 1), :], 0.0), axis=-1, keepdims=True)
            gate_ref[e, r * cap:(r + 1) * cap, :] = gate
        return carry

    lax.fori_loop(0, N_EXPERTS, per_expert, 0, unroll=unroll)
    n_rows = min(GATHER_ROWS, N_EXPERTS * cap)
    e_per = n_rows // cap
    for r in reqs:
        h2 = h2_ref[rows(r), :]
        for t in range(N_EXPERTS * cap // n_rows):
            xe = _dot(p_s[r, t * n_rows:(t + 1) * n_rows, :], h2).astype(BF)
            xe_ref[t * e_per:(t + 1) * e_per, r * cap:(r + 1) * cap, :] = xe.reshape(e_per, cap, D)


def _route_call(lg, h2, n_req, n_tok):
    cap = max(1, CAPACITY_FACTOR * n_tok // N_EXPERTS)
    per_step = max(1, min(n_req, ROUTE_STEP_TOKENS // n_tok))
    before = (lax.broadcasted_iota(jnp.int32, (n_tok, n_tok), 1)
              < lax.broadcasted_iota(jnp.int32, (n_tok, n_tok), 0)).astype(BF)
    unroll = max(1, min(N_EXPERTS, ROUTE_UNROLL_ELEMS // (per_step * cap * n_tok)))
    return pl.pallas_call(
        functools.partial(_route_kernel, cap, n_tok, unroll),
        grid=(n_req // per_step,),
        in_specs=[
            pl.BlockSpec((per_step * n_tok, LANES), lambda b: (b, 0)),
            pl.BlockSpec((per_step * n_tok, D), lambda b: (b, 0)),
            _const_spec((n_tok, n_tok)),
        ],
        out_specs=[
            pl.BlockSpec((N_EXPERTS, per_step * cap, D), lambda b: (0, b, 0)),
            pl.BlockSpec((N_EXPERTS, per_step * cap, 1), lambda b: (0, b, 0)),
            pl.BlockSpec((per_step * n_tok, LANES), lambda b: (b, 0)),
        ],
        out_shape=[
            jax.ShapeDtypeStruct((N_EXPERTS, n_req * cap, D), BF),
            jax.ShapeDtypeStruct((N_EXPERTS, n_req * cap, 1), F32),
            jax.ShapeDtypeStruct((n_req * n_tok, LANES), F32),
        ],
        scratch_shapes=[
            pltpu.VMEM((per_step, LANES, n_tok), F32),
            pltpu.VMEM((per_step, LANES, n_tok), F32),
            pltpu.VMEM((per_step, N_EXPERTS * cap, n_tok), BF),
        ],
        compiler_params=_params(1),
        name="route",
    )(lg, h2, before)


EXPERT_BLOCK = 1024
EXPERT_SUB = 512


def _experts_kernel(xa_ref, xb_ref, ga_ref, gb_ref, wg_ref, wu_ref, wd_ref, ya_ref, yb_ref, x_s, acc_s):
    f = pl.program_id(1)
    ra = xa_ref.shape[1]

    @pl.when(f == 0)
    def _():
        x_s[0:ra, :] = xa_ref[0]
        x_s[ra:, :] = xb_ref[0]
        acc_s[...] = jnp.zeros(acc_s.shape, F32)

    x = x_s[...]
    for c in range(wg_ref.shape[2] // EXPERT_SUB):
        cs = slice(c * EXPERT_SUB, (c + 1) * EXPERT_SUB)
        gate = _dot(x, wg_ref[0, :, cs].astype(BF))
        up = _dot(x, wu_ref[0, :, cs].astype(BF))
        hid = (gate * jax.nn.sigmoid(gate) * up).astype(BF)
        acc_s[...] += _dot(hid, wd_ref[0, cs, :].astype(BF))

    @pl.when(f == pl.num_programs(1) - 1)
    def _():
        ya_ref[0] = (acc_s[0:ra, :] * ga_ref[0]).astype(ya_ref.dtype)
        yb_ref[0] = (acc_s[ra:, :] * gb_ref[0]).astype(yb_ref.dtype)


def _experts_call(xa, xb, ga, gb, wg, wu, wd):
    ra, rb = xa.shape[1], xb.shape[1]
    tf = EXPERT_BLOCK
    xa_spec = pl.BlockSpec((1, ra, D), lambda e, f: (e, 0, 0))
    xb_spec = pl.BlockSpec((1, rb, D), lambda e, f: (e, 0, 0))
    return pl.pallas_call(
        _experts_kernel,
        grid=(N_EXPERTS, D_EXPERT // tf),
        in_specs=[
            xa_spec, xb_spec,
            pl.BlockSpec((1, ra, 1), lambda e, f: (e, 0, 0)),
            pl.BlockSpec((1, rb, 1), lambda e, f: (e, 0, 0)),
            pl.BlockSpec((1, D, tf), lambda e, f: (e, 0, f)),
            pl.BlockSpec((1, D, tf), lambda e, f: (e, 0, f)),
            pl.BlockSpec((1, tf, D), lambda e, f: (e, f, 0)),
        ],
        out_specs=[xa_spec, xb_spec],
        out_shape=[
            jax.ShapeDtypeStruct((N_EXPERTS, ra, D), BF),
            jax.ShapeDtypeStruct((N_EXPERTS, rb, D), BF),
        ],
        scratch_shapes=[pltpu.VMEM((ra + rb, D), BF), pltpu.VMEM((ra + rb, D), F32)],
        compiler_params=_params(2),
        name="experts",
    )(xa, xb, ga, gb, wg, wu, wd)


def _scatter_kernel(cap, row_of_req, slot_ref, y_ref, x1_ref, m_ref, expand_ref, out_ref):
    m = m_ref[pl.ds(row_of_req(pl.program_id(0)), 1), :]
    gate2 = m[:, 5 * D:6 * D]
    slot_wide = _dot(slot_ref[...].astype(BF), expand_ref[...])
    slot_ids = (lax.broadcasted_iota(jnp.int32, (1, N_EXPERTS * cap), 1) % cap).astype(F32)
    onehot = jnp.where(slot_wide == slot_ids, 1.0, 0.0).astype(BF)
    y = y_ref[...].reshape(N_EXPERTS * cap, D)
    out_ref[...] = x1_ref[...] + gate2 * _dot(onehot, y)


def _scatter_call(slots, y, x1, m, row_of_req, n_req, n_tok):
    cap = y.shape[1] // n_req
    ts = TM
    nt = n_tok // ts
    e_ids = lax.broadcasted_iota(jnp.int32, (LANES, N_EXPERTS * cap), 0)
    c_ids = lax.broadcasted_iota(jnp.int32, (LANES, N_EXPERTS * cap), 1) // cap
    expand = (e_ids == c_ids).astype(BF)
    tok = pl.BlockSpec((ts, D), lambda b, t: (b * nt + t, 0))
    return pl.pallas_call(
        functools.partial(_scatter_kernel, cap, row_of_req),
        grid=(n_req, nt),
        in_specs=[
            pl.BlockSpec((ts, LANES), lambda b, t: (b * nt + t, 0)),
            pl.BlockSpec((N_EXPERTS, cap, D), lambda b, t: (0, b, 0)),
            tok,
            _const_spec((MOD_ROWS, 6 * D)),
            _const_spec((LANES, N_EXPERTS * cap)),
        ],
        out_specs=tok,
        out_shape=jax.ShapeDtypeStruct((n_req * n_tok, D), F32),
        compiler_params=_params(2),
        name="scatter",
    )(slots, y, x1, m, expand)


def _rope_tables(n_tokens):
    rows = n_tokens // GRID_W
    row = jnp.broadcast_to(jnp.arange(rows, dtype=F32)[:, None], (rows, GRID_W)).reshape(-1)
    col = jnp.broadcast_to(jnp.arange(GRID_W, dtype=F32)[None, :], (rows, GRID_W)).reshape(-1)
    half = HEAD_DIM // 4
    inv_freq = ROPE_BASE ** (-jnp.arange(half, dtype=F32) / half)
    ar = row[:, None] * inv_freq
    ac = col[:, None] * inv_freq
    ang = jnp.concatenate([ar, ar, ac, ac], axis=-1)
    cos, sin = jnp.cos(ang), jnp.sin(ang)
    first_half = (jnp.arange(HEAD_DIM) % (HEAD_DIM // 2)) < (HEAD_DIM // 4)
    sin_signed = jnp.where(first_half[None, :], -sin, sin)
    reps = LANES // HEAD_DIM
    return jnp.tile(cos, (1, reps)), jnp.tile(sin_signed, (1, reps)), cos.T, sin_signed.T


def kernel(x_prompt, x_sample, cache_k, cache_v, c, c_ctx, w_ada, b_ada, norm1_g, norm2_g, w_in, q_norm_g, k_norm_g, lambda_q1, lambda_k1, lambda_q2, lambda_k2, subln_g, gmlp_ln_g, gmlp_ln_b, w_spatial, b_spatial, w_out, w_router, w_gate_e, w_up_e, w_down_e):
    n_p, t_p = x_prompt.shape[0], x_prompt.shape[1]
    n_s, t_s = x_sample.shape[0], x_sample.shape[1]
    n_ctx = cache_k.shape[2]
    l = 0

    cond = jnp.concatenate([c_ctx[None, :], c, jnp.zeros((MOD_ROWS - 1 - n_s, D), F32)], axis=0)
    m, lamv = _ada_call(cond, w_ada[l], b_ada[l][None, :], lambda_q1[l][None, :], lambda_k1[l][None, :],
                        lambda_q2[l][None, :], lambda_k2[l][None, :])
    w_in_bf, wkt = _wcast_call(w_in[l])

    group = jnp.arange(D) // HEAD_DIM
    gsum = (group[:, None] == jnp.arange(LANES)[None, :]).astype(BF)
    wr = jnp.pad(w_router[l], ((0, 0), (0, LANES - N_EXPERTS)))
    wr_hi = wr.astype(BF)
    consts = {
        "n1g": norm1_g[l][None, :],
        "n2g": norm2_g[l][None, :],
        "w_in": w_in_bf,
        "wkt": wkt,
        "qg": jnp.tile(q_norm_g[l] * (HEAD_DIM ** -0.5 * math.log2(math.e)), D // HEAD_DIM)[None, :],
        "kgt": jnp.broadcast_to(k_norm_g[l][:, None], (HEAD_DIM, TM)),
        "gsum": gsum,
        "gbcast": jnp.concatenate([gsum.T, gsum.T], axis=0),
        "lng": gmlp_ln_g[l][None, :],
        "lnb": gmlp_ln_b[l][None, :],
        "wsp": w_spatial[l].astype(BF),
        "bsp": jnp.repeat(b_spatial[l].T, D // N_GROUPS, axis=1),
        "w_out": w_out[l].astype(BF),
        "wr_hi": wr_hi,
        "wr_lo": (wr - wr_hi.astype(F32)).astype(BF),
        "sg": subln_g[l][None, :],
    }

    passes = (
        (x_prompt, n_p, t_p, lambda i: 0, lambda b: 0, False),
        (x_sample, n_s, t_s, lambda i: 1 + i // (t_s // TM), lambda b: 1 + b, True),
    )
    outs = []
    kv_out = None
    for x, n_req, n_tok, row_of_step, row_of_req, is_sample in passes:
        xf = x.reshape(n_req * n_tok, D)
        tabs = _rope_tables(n_tok) if is_sample else None
        q, kt, v, sga, tb = _inproj_call(xf, m, row_of_step, tabs, BF if is_sample else F32, consts, n_req, n_tok)
        if is_sample:
            ctx = (jnp.transpose(cache_k[:, l], (0, 2, 3, 4, 1)).reshape(n_req, D, n_ctx),
                   cache_v[:, l].reshape(n_req * n_ctx, D))
        else:
            ctx = None
            kv_out = (kt, v)
        x1, h2, lg = _attn_call(q, kt, v, ctx, lamv, sga, tb, xf, m, row_of_req, consts, n_req, n_tok)
        xe, gates, slots = _route_call(lg, h2, n_req, n_tok)
        outs.append((x1, xe, gates, slots, row_of_req))

    (x1p, xep, gp, slots_p, row_p), (x1s, xes, gs, slots_s, row_s) = outs
    yp, ys = _experts_call(xep, xes, gp, gs, w_gate_e[l], w_up_e[l], w_down_e[l])
    y_prompt = _scatter_call(slots_p, yp, x1p, m, row_p, n_p, t_p).reshape(x_prompt.shape)
    y_sample = _scatter_call(slots_s, ys, x1s, m, row_s, n_s, t_s).reshape(x_sample.shape)
    new_k = jnp.transpose(kv_out[0].reshape(n_p, N_HEADS, 2, HEAD_DIM, t_p), (0, 4, 1, 2, 3))
    new_k = new_k.reshape(n_p, 1, t_p, N_HEADS, 2, HEAD_DIM)
    new_v = kv_out[1].reshape(n_p, 1, t_p, N_HEADS, V_HEAD_DIM)
    return (y_prompt, y_sample, new_k, new_v)
```

```python
import functools
import math

import jax
import jax.numpy as jnp
from jax import lax
from jax.experimental import pallas as pl
from jax.experimental.pallas import tpu as pltpu

D = 1024
N_HEADS = 8
HEAD_DIM = 64
V_HEAD_DIM = 128
GRID_W = 64
ROPE_BASE = 10000.0
CHUNK = 128
N_GROUPS = 8
N_EXPERTS = 16
CAPACITY_FACTOR = 2
D_EXPERT = 2048
N_SEG = 7
K_SEG = 1
EPS = 1e-6
LAMBDA_INIT = 0.8 - 0.6 * math.exp(-0.3 * 0)

LANES = 128
MOD_ROWS = 8
VMEM_LIMIT = 56 * 1024 * 1024
TM = 256

BF = jnp.bfloat16
F32 = jnp.float32


def _dot(a, b):
    return jnp.dot(a, b, preferred_element_type=F32)


def _dot_nt(a, b):
    return lax.dot_general(a, b, (((1,), (1,)), ((), ())), preferred_element_type=F32)


def _split_bf16(x):
    hi = x.astype(BF)
    lo = (x - hi.astype(F32)).astype(BF)
    return hi, lo


def _rms(x, g):
    return x * lax.rsqrt(jnp.mean(x * x, axis=-1, keepdims=True) + EPS) * g


def _params(n_grid_dims):
    return pltpu.CompilerParams(
        dimension_semantics=("arbitrary",) * n_grid_dims, vmem_limit_bytes=VMEM_LIMIT)


def _const_spec(shape):
    nd = len(shape)
    return pl.BlockSpec(shape, lambda *_: (0,) * nd)


N_MOD = 6


def _prep_kernel(cond_ref, wa_ref, ba_ref, lq1_ref, lk1_ref, lq2_ref, lk2_ref, wi_ref, wo_ref,
                 m_ref, lam_ref, wib_ref, wkt_ref, wob_ref):
    j = pl.program_id(0)

    @pl.when(j < N_MOD)
    def _():
        c = cond_ref[...]
        a_hi, a_lo = _split_bf16(c * jax.nn.sigmoid(c))
        w_hi, w_lo = _split_bf16(wa_ref[...])
        m_ref[...] = _dot(a_hi, w_hi) + _dot(a_lo, w_hi) + _dot(a_hi, w_lo) + ba_ref[...]

    w = wi_ref[...]
    wib_ref[...] = w.astype(BF)

    @pl.when(j == K_SEG)
    def _():
        wkt_ref[...] = w.T.astype(BF)

    @pl.when(j == 0)
    def _():
        s1 = jnp.sum(lq1_ref[...] * lk1_ref[...], axis=-1, keepdims=True)
        s2 = jnp.sum(lq2_ref[...] * lk2_ref[...], axis=-1, keepdims=True)
        lam_ref[...] = jnp.broadcast_to(jnp.exp(s1) - jnp.exp(s2) + LAMBDA_INIT, lam_ref.shape)
        wob_ref[...] = wo_ref[...].astype(BF)


def _prep_call(cond, w_ada, b_ada, lq1, lk1, lq2, lk2, w_in, w_out):
    vec = _const_spec((1, HEAD_DIM))
    mod_block = lambda j: (0, jnp.minimum(j, N_MOD - 1))
    return pl.pallas_call(
        _prep_kernel,
        grid=(N_SEG,),
        in_specs=[
            _const_spec((MOD_ROWS, D)),
            pl.BlockSpec((D, D), mod_block),
            pl.BlockSpec((1, D), mod_block),
            vec, vec, vec, vec,
            pl.BlockSpec((D, D), lambda j: (0, j)),
            _const_spec((D, D)),
        ],
        out_specs=[
            pl.BlockSpec((MOD_ROWS, D), mod_block),
            _const_spec((MOD_ROWS, LANES)),
            pl.BlockSpec((D, D), lambda j: (0, j)),
            _const_spec((D, D)),
            _const_spec((D, D)),
        ],
        out_shape=[
            jax.ShapeDtypeStruct((MOD_ROWS, N_MOD * D), F32),
            jax.ShapeDtypeStruct((MOD_ROWS, LANES), F32),
            jax.ShapeDtypeStruct((D, N_SEG * D), BF),
            jax.ShapeDtypeStruct((D, D), BF),
            jax.ShapeDtypeStruct((D, D), BF),
        ],
        compiler_params=_params(1),
        name="prep",
    )(cond, w_ada, b_ada, lq1, lk1, lq2, lk2, w_in, w_out)


def _head_norm(t, g, gsum, gbcast):
    ss = _dot((t * t).astype(BF), gsum)
    inv = lax.rsqrt(ss * (1.0 / HEAD_DIM) + EPS)
    hi, lo = _split_bf16(inv)
    bc = _dot(jnp.concatenate([hi, lo], axis=-1), gbcast)
    return t * bc * g


def _lane_tile(tab):
    return jnp.concatenate([tab] * (D // LANES), axis=1)


def _rope(t, cos, sin_signed):
    q = HEAD_DIM // 4
    lane = lax.broadcasted_iota(jnp.int32, (1, D), 1)
    first_half = (lane % (HEAD_DIM // 2)) < q
    fwd = pltpu.roll(t, D - q, axis=1)
    bwd = pltpu.roll(t, q, axis=1)
    return t * _lane_tile(cos) + jnp.where(first_half, fwd, bwd) * _lane_tile(sin_signed)


def _rope_t(t3, cos_t, sin_signed_t):
    q = HEAD_DIM // 4
    rot = jnp.concatenate([t3[:, q:2 * q], t3[:, 0:q], t3[:, 3 * q:4 * q], t3[:, 2 * q:3 * q]], axis=1)
    return t3 * cos_t[None] + rot * sin_signed_t[None]


def _inproj_kernel(rope, row_of_step, x_ref, m_ref, n1g_ref, w_ref, wkt_ref, qg_ref, kgt_ref, gsum_ref,
                   gbcast_ref, lng_ref, lnb_ref, wsp_ref, bsp_ref, *rest):
    if rope:
        cos_ref, sin_ref, cos_t_ref, sin_t_ref, q_out, kt_out, v_out, sga_out, tb_out = rest
    else:
        q_out, kt_out, v_out, sga_out, tb_out = rest
    tm = x_ref.shape[0]
    m = m_ref[pl.ds(row_of_step(pl.program_id(0)), 1), :]
    shift1, scale1 = m[:, 0:D], m[:, D:2 * D]
    h = (_rms(x_ref[...], n1g_ref[...]) * (1.0 + scale1) + shift1).astype(BF)

    def seg(j):
        return _dot(h, w_ref[:, j * D:(j + 1) * D])

    q = _head_norm(seg(0), qg_ref[...], gsum_ref[...], gbcast_ref[...])
    if rope:
        q = _rope(q, cos_ref[...], sin_ref[...])
    q_out[...] = q.astype(q_out.dtype)

    k3 = _dot_nt(wkt_ref[...], h).reshape(D // HEAD_DIM, HEAD_DIM, tm)
    inv = lax.rsqrt(jnp.mean(k3 * k3, axis=1, keepdims=True) + EPS)
    k3 = k3 * inv * kgt_ref[...][None]
    if rope:
        k3 = _rope_t(k3, cos_t_ref[...], sin_t_ref[...])
    kt_out[0] = k3.reshape(D, tm).astype(kt_out.dtype)

    v_out[...] = seg(2).astype(v_out.dtype)
    sga_out[...] = jax.nn.sigmoid(seg(5)).astype(sga_out.dtype)

    zv = jax.nn.gelu(seg(4))
    mu = jnp.mean(zv, axis=-1, keepdims=True)
    zc = zv - mu
    var = jnp.mean(zc * zc, axis=-1, keepdims=True)
    zvn = (zc * lax.rsqrt(var + EPS) * lng_ref[...] + lnb_ref[...]).astype(BF)
    pre = jax.nn.gelu(seg(3)) * jax.nn.sigmoid(seg(6))
    for c in range(tm // CHUNK):
        rows = slice(c * CHUNK, (c + 1) * CHUNK)
        for g in range(N_GROUPS):
            cols = slice(g * LANES, (g + 1) * LANES)
            sp = _dot(wsp_ref[g], zvn[rows, cols]) + bsp_ref[:, cols]
            tb_out[rows, cols] = (pre[rows, cols] * sp).astype(tb_out.dtype)


def _inproj_call(x, m, row_of_step, rope_tabs, kv_dtype, consts, n_req, n_tok):
    t = x.shape[0]
    tm = TM
    per_req = n_tok // tm
    rope = rope_tabs is not None
    tok = pl.BlockSpec((tm, D), lambda i: (i, 0))
    in_specs = [
        tok,
        _const_spec((MOD_ROWS, 6 * D)),
        _const_spec((1, D)),
        pl.BlockSpec((D, N_SEG * D), lambda i: (0, 0), pipeline_mode=pl.Buffered(1)),
        _const_spec((D, D)),
        _const_spec((1, D)), _const_spec((HEAD_DIM, tm)),
        _const_spec((D, LANES)), _const_spec((2 * LANES, D)),
        _const_spec((1, D)), _const_spec((1, D)),
        _const_spec((N_GROUPS, CHUNK, CHUNK)), _const_spec((CHUNK, D)),
    ]
    args = [x, m, consts["n1g"], consts["w_in"], consts["wkt"], consts["qg"], consts["kgt"], consts["gsum"],
            consts["gbcast"], consts["lng"], consts["lnb"], consts["wsp"], consts["bsp"]]
    if rope:
        tab = pl.BlockSpec((tm, LANES), lambda i: (i % per_req, 0))
        tab_t = pl.BlockSpec((HEAD_DIM, tm), lambda i: (0, i % per_req))
        in_specs += [tab, tab, tab_t, tab_t]
        args += list(rope_tabs)
    kt_spec = pl.BlockSpec((1, D, tm), lambda i: (i // per_req, 0, i % per_req))
    return pl.pallas_call(
        functools.partial(_inproj_kernel, rope, row_of_step),
        grid=(t // tm,),
        in_specs=in_specs,
        out_specs=[tok, kt_spec, tok, tok, tok],
        out_shape=[
            jax.ShapeDtypeStruct((t, D), BF),
            jax.ShapeDtypeStruct((n_req, D, n_tok), kv_dtype),
            jax.ShapeDtypeStruct((t, D), kv_dtype),
            jax.ShapeDtypeStruct((t, D), BF),
            jax.ShapeDtypeStruct((t, D), BF),
        ],
        compiler_params=_params(1),
        name="inproj_rope" if rope else "inproj",
    )(*args)


SCORE_GROUP_ELEMS = 1 << 21


def _attn_kernel(has_ctx, heads_per_group, row_of_req, q_ref, kt_ref, v_ref, *rest):
    if has_ctx:
        kct_ref, vc_ref, *rest = rest
    (lam_ref, sg_ref, sga_ref, tb_ref, x_ref, m_ref, wo_ref, n2g_ref, wr_hi_ref, wr_lo_ref,
     x1_out, h2_out, lg_out, merged_s) = rest
    tq = q_ref.shape[0]
    lam = lam_ref[0:1, 0:1]
    lane = lax.broadcasted_iota(jnp.int32, (1, LANES), 1)
    first = lane < HEAD_DIM
    zero = jnp.zeros((), BF)
    ones_col = jnp.where(lane == 0, 1.0, 0.0).astype(BF)

    def head_values(ref, h):
        v = ref[:, h * V_HEAD_DIM:(h + 1) * V_HEAD_DIM].astype(BF)
        return jnp.concatenate([v, jnp.broadcast_to(ones_col, (v.shape[0], LANES))], axis=1)

    def head_cols(h):
        return slice(h * LANES, (h + 1) * LANES)

    def head_scores(h):
        q = q_ref[:, head_cols(h)]
        qz = jnp.concatenate([jnp.where(first, q, zero), jnp.where(first, zero, q)], axis=0)
        parts = [_dot(qz, kt_ref[0, head_cols(h), :].astype(BF))]
        if has_ctx:
            parts.append(_dot(qz, kct_ref[0, head_cols(h), :].astype(BF)))
        return parts

    def row_max(scores):
        return functools.reduce(jnp.maximum, [jnp.max(s, axis=-1, keepdims=True) for s in scores])

    def head_pv(h, scores, mx):
        vals = [head_values(v_ref, h)] + ([head_values(vc_ref, h)] if has_ctx else [])
        return functools.reduce(jnp.add, [_dot(jnp.exp2(s - mx).astype(BF), v) for s, v in zip(scores, vals)])

    def head_finish(h, ob):
        den = ob[:, V_HEAD_DIM:V_HEAD_DIM + 1]
        o = ob[:tq, :V_HEAD_DIM] * (1.0 / den[:tq]) - ob[tq:, :V_HEAD_DIM] * (lam / den[tq:])
        o = _rms(o, sg_ref[...]) * (1.0 - LAMBDA_INIT)
        cols = head_cols(h)
        merged_s[:, cols] = (sga_ref[:, cols].astype(F32) * o + tb_ref[:, cols].astype(F32)).astype(BF)

    for g0 in range(0, N_HEADS, heads_per_group):
        group = range(g0, g0 + heads_per_group)
        scores = [head_scores(h) for h in group]
        maxes = [row_max(s) for s in scores]
        outs = [head_pv(h, s, mx) for h, s, mx in zip(group, scores, maxes)]
        for h, ob in zip(group, outs):
            head_finish(h, ob)

    m = m_ref[pl.ds(row_of_req(pl.program_id(0)), 1), :]
    gate1, shift2, scale2 = m[:, 2 * D:3 * D], m[:, 3 * D:4 * D], m[:, 4 * D:5 * D]
    x1 = x_ref[...] + gate1 * _dot(merged_s[...], wo_ref[...])
    x1_out[...] = x1
    h2 = _rms(x1, n2g_ref[...]) * (1.0 + scale2) + shift2
    h2_out[...] = h2.astype(BF)
    hi, lo = _split_bf16(h2)
    wr_hi = wr_hi_ref[...]
    lg_out[...] = _dot(hi, wr_hi) + _dot(lo, wr_hi) + _dot(hi, wr_lo_ref[...])


def _attn_call(q, kt, v, ctx, lamv, sga, tb, x, m, row_of_req, consts, n_req, n_tok):
    tq = TM
    nqb = n_tok // tq
    has_ctx = ctx is not None
    tok = pl.BlockSpec((tq, D), lambda b, i: (b * nqb + i, 0))
    in_specs = [tok, pl.BlockSpec((1, D, n_tok), lambda b, i: (b, 0, 0)),
                pl.BlockSpec((n_tok, D), lambda b, i: (b, 0))]
    args = [q, kt, v]
    if has_ctx:
        n_ctx = ctx[0].shape[2]
        in_specs += [pl.BlockSpec((1, D, n_ctx), lambda b, i: (b, 0, 0)),
                     pl.BlockSpec((n_ctx, D), lambda b, i: (b, 0))]
        args += list(ctx)
    in_specs += [
        _const_spec((MOD_ROWS, LANES)), _const_spec((1, V_HEAD_DIM)),
        tok, tok, tok,
        _const_spec((MOD_ROWS, 6 * D)),
        _const_spec((D, D)), _const_spec((1, D)),
        _const_spec((D, LANES)), _const_spec((D, LANES)),
    ]
    args += [lamv, consts["sg"], sga, tb, x, m, consts["w_out"], consts["n2g"], consts["wr_hi"], consts["wr_lo"]]
    t = n_req * n_tok
    n_keys = n_tok + (ctx[0].shape[2] if has_ctx else 0)
    heads_per_group = max(1, min(N_HEADS, SCORE_GROUP_ELEMS // (2 * tq * n_keys)))
    return pl.pallas_call(
        functools.partial(_attn_kernel, has_ctx, heads_per_group, row_of_req),
        grid=(n_req, nqb),
        in_specs=in_specs,
        out_specs=[tok, tok, pl.BlockSpec((tq, LANES), lambda b, i: (b * nqb + i, 0))],
        out_shape=[
            jax.ShapeDtypeStruct((t, D), F32),
            jax.ShapeDtypeStruct((t, D), BF),
            jax.ShapeDtypeStruct((t, LANES), F32),
        ],
        scratch_shapes=[pltpu.VMEM((tq, D), BF)],
        compiler_params=_params(2),
        name="attn_ctx" if has_ctx else "attn",
    )(*args)


GATHER_ROWS = 512
KEY_BITS = 31
ROUTE_UNROLL_ELEMS = 1 << 22
ROUTE_STEP_TOKENS = 2048


def _route_kernel(cap, n_tok, unroll, lg_ref, h2_ref, before_ref, xe_ref, gate_ref, slot_ref,
                  aff_t_s, slot_t_s, p_s):
    reqs = range(lg_ref.shape[0] // n_tok)
    lane = lax.broadcasted_iota(jnp.int32, (1, LANES), 1)
    valid = lane < N_EXPERTS

    def rows(r):
        return slice(r * n_tok, (r + 1) * n_tok)

    def affinity(r):
        lg = jnp.where(valid, lg_ref[rows(r), :], -1e30)
        ex = jnp.where(valid, jnp.exp(lg - jnp.max(lg, axis=-1, keepdims=True)), 0.0)
        return ex / jnp.sum(ex, axis=-1, keepdims=True)

    def count(mask):
        return jnp.sum(jnp.where(mask, 1.0, 0.0), axis=0, keepdims=True)

    affs = [affinity(r) for r in reqs]
    kth_bits = [jnp.zeros((1, LANES), jnp.int32) for _ in reqs]
    for bit in range(KEY_BITS - 1, -1, -1):
        for r in reqs:
            cand = kth_bits[r] | (1 << bit)
            enough = count(affs[r] >= lax.bitcast_convert_type(cand, F32)) >= cap
            kth_bits[r] = jnp.where(enough, cand, kth_bits[r])

    before = before_ref[...]
    for r in reqs:
        aff = affs[r]
        kth = lax.bitcast_convert_type(kth_bits[r], F32)
        above = aff > kth
        tied = aff == kth
        need = cap - count(above)
        tied_before = _dot(before, jnp.where(tied, 1.0, 0.0).astype(BF))
        chosen = jnp.where(above, 1.0, jnp.where(tied, jnp.where(tied_before < need, 1.0, 0.0), 0.0))
        slot = _dot(before, chosen.astype(BF))
        slot = jnp.where(valid, jnp.where(chosen > 0.0, slot, float(cap)), float(cap))
        slot_ref[rows(r), :] = slot
        slot_t_s[r] = slot.T
        aff_t_s[r] = aff.T

    slot_ids = lax.broadcasted_iota(jnp.int32, (cap, 1), 0).astype(F32)

    def per_expert(e, carry):
        for r in reqs:
            hit = slot_t_s[r, pl.ds(e, 1), :] == slot_ids
            p_s[r, pl.ds(pl.multiple_of(e * cap, cap), cap), :] = jnp.where(hit, 1.0, 0.0).astype(BF)
            gate = jnp.sum(jnp.where(hit, aff_t_s[r, pl.ds(e, 1), :], 0.0), axis=-1, keepdims=True)
            gate_ref[e, r * cap:(r + 1) * cap, :] = gate
        return carry

    lax.fori_loop(0, N_EXPERTS, per_expert, 0, unroll=unroll)
    n_rows = min(GATHER_ROWS, N_EXPERTS * cap)
    e_per = n_rows // cap
    for r in reqs:
        h2 = h2_ref[rows(r), :]
        for t in range(N_EXPERTS * cap // n_rows):
            xe = _dot(p_s[r, t * n_rows:(t + 1) * n_rows, :], h2).astype(BF)
            xe_ref[t * e_per:(t + 1) * e_per, r * cap:(r + 1) * cap, :] = xe.reshape(e_per, cap, D)


def _route_call(lg, h2, n_req, n_tok):
    cap = max(1, CAPACITY_FACTOR * n_tok // N_EXPERTS)
    per_step = max(1, min(n_req, ROUTE_STEP_TOKENS // n_tok))
    before = (lax.broadcasted_iota(jnp.int32, (n_tok, n_tok), 1)
              < lax.broadcasted_iota(jnp.int32, (n_tok, n_tok), 0)).astype(BF)
    unroll = max(1, min(N_EXPERTS, ROUTE_UNROLL_ELEMS // (per_step * cap * n_tok)))
    return pl.pallas_call(
        functools.partial(_route_kernel, cap, n_tok, unroll),
        grid=(n_req // per_step,),
        in_specs=[
            pl.BlockSpec((per_step * n_tok, LANES), lambda b: (b, 0)),
            pl.BlockSpec((per_step * n_tok, D), lambda b: (b, 0)),
            _const_spec((n_tok, n_tok)),
        ],
        out_specs=[
            pl.BlockSpec((N_EXPERTS, per_step * cap, D), lambda b: (0, b, 0)),
            pl.BlockSpec((N_EXPERTS, per_step * cap, 1), lambda b: (0, b, 0)),
            pl.BlockSpec((per_step * n_tok, LANES), lambda b: (b, 0)),
        ],
        out_shape=[
            jax.ShapeDtypeStruct((N_EXPERTS, n_req * cap, D), BF),
            jax.ShapeDtypeStruct((N_EXPERTS, n_req * cap, 1), F32),
            jax.ShapeDtypeStruct((n_req * n_tok, LANES), F32),
        ],
        scratch_shapes=[
            pltpu.VMEM((per_step, LANES, n_tok), F32),
            pltpu.VMEM((per_step, LANES, n_tok), F32),
            pltpu.VMEM((per_step, N_EXPERTS * cap, n_tok), BF),
        ],
        compiler_params=_params(1),
        name="route",
    )(lg, h2, before)


EXPERT_BLOCK = 1024
EXPERT_SUB = 512


def _experts_kernel(xa_ref, xb_ref, ga_ref, gb_ref, wg_ref, wu_ref, wd_ref, ya_ref, yb_ref, x_s, acc_s):
    f = pl.program_id(1)
    ra = xa_ref.shape[1]

    @pl.when(f == 0)
    def _():
        x_s[0:ra, :] = xa_ref[0]
        x_s[ra:, :] = xb_ref[0]
        acc_s[...] = jnp.zeros(acc_s.shape, F32)

    x = x_s[...]
    for c in range(wg_ref.shape[2] // EXPERT_SUB):
        cs = slice(c * EXPERT_SUB, (c + 1) * EXPERT_SUB)
        gate = _dot(x, wg_ref[0, :, cs].astype(BF))
        up = _dot(x, wu_ref[0, :, cs].astype(BF))
        hid = (gate * jax.nn.sigmoid(gate) * up).astype(BF)
        acc_s[...] += _dot(hid, wd_ref[0, cs, :].astype(BF))

    @pl.when(f == pl.num_programs(1) - 1)
    def _():
        ya_ref[0] = (acc_s[0:ra, :] * ga_ref[0]).astype(ya_ref.dtype)
        yb_ref[0] = (acc_s[ra:, :] * gb_ref[0]).astype(yb_ref.dtype)


def _experts_call(xa, xb, ga, gb, wg, wu, wd):
    ra, rb = xa.shape[1], xb.shape[1]
    tf = EXPERT_BLOCK
    xa_spec = pl.BlockSpec((1, ra, D), lambda e, f: (e, 0, 0))
    xb_spec = pl.BlockSpec((1, rb, D), lambda e, f: (e, 0, 0))
    return pl.pallas_call(
        _experts_kernel,
        grid=(N_EXPERTS, D_EXPERT // tf),
        in_specs=[
            xa_spec, xb_spec,
            pl.BlockSpec((1, ra, 1), lambda e, f: (e, 0, 0)),
            pl.BlockSpec((1, rb, 1), lambda e, f: (e, 0, 0)),
            pl.BlockSpec((1, D, tf), lambda e, f: (e, 0, f)),
            pl.BlockSpec((1, D, tf), lambda e, f: (e, 0, f)),
            pl.BlockSpec((1, tf, D), lambda e, f: (e, f, 0)),
        ],
        out_specs=[xa_spec, xb_spec],
        out_shape=[
            jax.ShapeDtypeStruct((N_EXPERTS, ra, D), BF),
            jax.ShapeDtypeStruct((N_EXPERTS, rb, D), BF),
        ],
        scratch_shapes=[pltpu.VMEM((ra + rb, D), BF), pltpu.VMEM((ra + rb, D), F32)],
        compiler_params=_params(2),
        name="experts",
    )(xa, xb, ga, gb, wg, wu, wd)


def _scatter_kernel(cap, row_of_req, slot_ref, y_ref, x1_ref, m_ref, expand_ref, out_ref):
    m = m_ref[pl.ds(row_of_req(pl.program_id(0)), 1), :]
    gate2 = m[:, 5 * D:6 * D]
    slot_wide = _dot(slot_ref[...].astype(BF), expand_ref[...])
    slot_ids = (lax.broadcasted_iota(jnp.int32, (1, N_EXPERTS * cap), 1) % cap).astype(F32)
    onehot = jnp.where(slot_wide == slot_ids, 1.0, 0.0).astype(BF)
    y = y_ref[...].reshape(N_EXPERTS * cap, D)
    out_ref[...] = x1_ref[...] + gate2 * _dot(onehot, y)


def _scatter_call(slots, y, x1, m, row_of_req, n_req, n_tok):
    cap = y.shape[1] // n_req
    e_ids = lax.broadcasted_iota(jnp.int32, (LANES, N_EXPERTS * cap), 0)
    c_ids = lax.broadcasted_iota(jnp.int32, (LANES, N_EXPERTS * cap), 1) // cap
    expand = (e_ids == c_ids).astype(BF)
    tok = pl.BlockSpec((n_tok, D), lambda b: (b, 0))
    return pl.pallas_call(
        functools.partial(_scatter_kernel, cap, row_of_req),
        grid=(n_req,),
        in_specs=[
            pl.BlockSpec((n_tok, LANES), lambda b: (b, 0)),
            pl.BlockSpec((N_EXPERTS, cap, D), lambda b: (0, b, 0)),
            tok,
            _const_spec((MOD_ROWS, 6 * D)),
            _const_spec((LANES, N_EXPERTS * cap)),
        ],
        out_specs=tok,
        out_shape=jax.ShapeDtypeStruct((n_req * n_tok, D), F32),
        compiler_params=_params(1),
        name="scatter",
    )(slots, y, x1, m, expand)


def _rope_tables(n_tokens):
    rows = n_tokens // GRID_W
    row = jnp.broadcast_to(jnp.arange(rows, dtype=F32)[:, None], (rows, GRID_W)).reshape(-1)
    col = jnp.broadcast_to(jnp.arange(GRID_W, dtype=F32)[None, :], (rows, GRID_W)).reshape(-1)
    half = HEAD_DIM // 4
    inv_freq = ROPE_BASE ** (-jnp.arange(half, dtype=F32) / half)
    ar = row[:, None] * inv_freq
    ac = col[:, None] * inv_freq
    ang = jnp.concatenate([ar, ar, ac, ac], axis=-1)
    cos, sin = jnp.cos(ang), jnp.sin(ang)
    first_half = (jnp.arange(HEAD_DIM) % (HEAD_DIM // 2)) < (HEAD_DIM // 4)
    sin_signed = jnp.where(first_half[None, :], -sin, sin)
    reps = LANES // HEAD_DIM
    return jnp.tile(cos, (1, reps)), jnp.tile(sin_signed, (1, reps)), cos.T, sin_signed.T


def kernel(x_prompt, x_sample, cache_k, cache_v, c, c_ctx, w_ada, b_ada, norm1_g, norm2_g, w_in, q_norm_g, k_norm_g, lambda_q1, lambda_k1, lambda_q2, lambda_k2, subln_g, gmlp_ln_g, gmlp_ln_b, w_spatial, b_spatial, w_out, w_router, w_gate_e, w_up_e, w_down_e):
    n_p, t_p = x_prompt.shape[0], x_prompt.shape[1]
    n_s, t_s = x_sample.shape[0], x_sample.shape[1]
    n_ctx = cache_k.shape[2]
    l = 0

    cond = jnp.concatenate([c_ctx[None, :], c, jnp.zeros((MOD_ROWS - 1 - n_s, D), F32)], axis=0)
    m, lamv, w_in_bf, wkt, w_out_bf = _prep_call(
        cond, w_ada[l], b_ada[l][None, :], lambda_q1[l][None, :], lambda_k1[l][None, :],
        lambda_q2[l][None, :], lambda_k2[l][None, :], w_in[l], w_out[l])

    group = jnp.arange(D) // HEAD_DIM
    gsum = (group[:, None] == jnp.arange(LANES)[None, :]).astype(BF)
    wr = jnp.pad(w_router[l], ((0, 0), (0, LANES - N_EXPERTS)))
    wr_hi = wr.astype(BF)
    consts = {
        "n1g": norm1_g[l][None, :],
        "n2g": norm2_g[l][None, :],
        "w_in": w_in_bf,
        "wkt": wkt,
        "qg": jnp.tile(q_norm_g[l] * (HEAD_DIM ** -0.5 * math.log2(math.e)), D // HEAD_DIM)[None, :],
        "kgt": jnp.broadcast_to(k_norm_g[l][:, None], (HEAD_DIM, TM)),
        "gsum": gsum,
        "gbcast": jnp.concatenate([gsum.T, gsum.T], axis=0),
        "lng": gmlp_ln_g[l][None, :],
        "lnb": gmlp_ln_b[l][None, :],
        "wsp": w_spatial[l].astype(BF),
        "bsp": jnp.repeat(b_spatial[l].T, D // N_GROUPS, axis=1),
        "w_out": w_out_bf,
        "wr_hi": wr_hi,
        "wr_lo": (wr - wr_hi.astype(F32)).astype(BF),
        "sg": subln_g[l][None, :],
    }

    passes = (
        (x_prompt, n_p, t_p, lambda i: 0, lambda b: 0, False),
        (x_sample, n_s, t_s, lambda i: 1 + i // (t_s // TM), lambda b: 1 + b, True),
    )
    outs = []
    kv_out = None
    for x, n_req, n_tok, row_of_step, row_of_req, is_sample in passes:
        xf = x.reshape(n_req * n_tok, D)
        tabs = _rope_tables(n_tok) if is_sample else None
        q, kt, v, sga, tb = _inproj_call(xf, m, row_of_step, tabs, BF if is_sample else F32, consts, n_req, n_tok)
        if is_sample:
            ctx = (jnp.transpose(cache_k[:, l], (0, 2, 3, 4, 1)).reshape(n_req, D, n_ctx),
                   cache_v[:, l].reshape(n_req * n_ctx, D))
        else:
            ctx = None
            kv_out = (kt, v)
        x1, h2, lg = _attn_call(q, kt, v, ctx, lamv, sga, tb, xf, m, row_of_req, consts, n_req, n_tok)
        xe, gates, slots = _route_call(lg, h2, n_req, n_tok)
        outs.append((x1, xe, gates, slots, row_of_req))

    (x1p, xep, gp, slots_p, row_p), (x1s, xes, gs, slots_s, row_s) = outs
    yp, ys = _experts_call(xep, xes, gp, gs, w_gate_e[l], w_up_e[l], w_down_e[l])
    y_prompt = _scatter_call(slots_p, yp, x1p, m, row_p, n_p, t_p).reshape(x_prompt.shape)
    y_sample = _scatter_call(slots_s, ys, x1s, m, row_s, n_s, t_s).reshape(x_sample.shape)
    new_k = jnp.transpose(kv_out[0].reshape(n_p, N_HEADS, 2, HEAD_DIM, t_p), (0, 4, 1, 2, 3))
    new_k = new_k.reshape(n_p, 1, t_p, N_HEADS, 2, HEAD_DIM)
    new_v = kv_out[1].reshape(n_p, 1, t_p, N_HEADS, V_HEAD_DIM)
    return (y_prompt, y_sample, new_k, new_v)
```

```python
import functools
import math

import jax
import jax.numpy as jnp
from jax import lax
from jax.experimental import pallas as pl
from jax.experimental.pallas import tpu as pltpu

D = 1024
N_HEADS = 8
HEAD_DIM = 64
V_HEAD_DIM = 128
GRID_W = 64
ROPE_BASE = 10000.0
CHUNK = 128
N_GROUPS = 8
N_EXPERTS = 16
CAPACITY_FACTOR = 2
D_EXPERT = 2048
N_SEG = 7
K_SEG = 1
EPS = 1e-6
LAMBDA_INIT = 0.8 - 0.6 * math.exp(-0.3 * 0)

LANES = 128
MOD_ROWS = 8
VMEM_LIMIT = 56 * 1024 * 1024
TM = 256

BF = jnp.bfloat16
F32 = jnp.float32


def _dot(a, b):
    return jnp.dot(a, b, preferred_element_type=F32)


def _dot_nt(a, b):
    return lax.dot_general(a, b, (((1,), (1,)), ((), ())), preferred_element_type=F32)


def _split_bf16(x):
    hi = x.astype(BF)
    lo = (x - hi.astype(F32)).astype(BF)
    return hi, lo


def _rms(x, g):
    return x * lax.rsqrt(jnp.mean(x * x, axis=-1, keepdims=True) + EPS) * g


def _params(n_grid_dims):
    return pltpu.CompilerParams(
        dimension_semantics=("arbitrary",) * n_grid_dims, vmem_limit_bytes=VMEM_LIMIT)


def _const_spec(shape):
    nd = len(shape)
    return pl.BlockSpec(shape, lambda *_: (0,) * nd)


N_MOD = 6


def _prep_kernel(cond_ref, wa_ref, ba_ref, lq1_ref, lk1_ref, lq2_ref, lk2_ref, wi_ref, wo_ref,
                 m_ref, lam_ref, wib_ref, wkt_ref, wob_ref):
    j = pl.program_id(0)

    @pl.when(j < N_MOD)
    def _():
        c = cond_ref[...]
        a_hi, a_lo = _split_bf16(c * jax.nn.sigmoid(c))
        w_hi, w_lo = _split_bf16(wa_ref[...])
        m_ref[...] = _dot(a_hi, w_hi) + _dot(a_lo, w_hi) + _dot(a_hi, w_lo) + ba_ref[...]

    w = wi_ref[...]
    wib_ref[...] = w.astype(BF)

    @pl.when(j == K_SEG)
    def _():
        wkt_ref[...] = w.T.astype(BF)

    @pl.when(j == 0)
    def _():
        s1 = jnp.sum(lq1_ref[...] * lk1_ref[...], axis=-1, keepdims=True)
        s2 = jnp.sum(lq2_ref[...] * lk2_ref[...], axis=-1, keepdims=True)
        lam_ref[...] = jnp.broadcast_to(jnp.exp(s1) - jnp.exp(s2) + LAMBDA_INIT, lam_ref.shape)
        wob_ref[...] = wo_ref[...].astype(BF)


def _prep_call(cond, w_ada, b_ada, lq1, lk1, lq2, lk2, w_in, w_out):
    vec = _const_spec((1, HEAD_DIM))
    mod_block = lambda j: (0, jnp.minimum(j, N_MOD - 1))
    return pl.pallas_call(
        _prep_kernel,
        grid=(N_SEG,),
        in_specs=[
            _const_spec((MOD_ROWS, D)),
            pl.BlockSpec((D, D), mod_block),
            pl.BlockSpec((1, D), mod_block),
            vec, vec, vec, vec,
            pl.BlockSpec((D, D), lambda j: (0, j)),
            _const_spec((D, D)),
        ],
        out_specs=[
            pl.BlockSpec((MOD_ROWS, D), mod_block),
            _const_spec((MOD_ROWS, LANES)),
            pl.BlockSpec((D, D), lambda j: (0, j)),
            _const_spec((D, D)),
            _const_spec((D, D)),
        ],
        out_shape=[
            jax.ShapeDtypeStruct((MOD_ROWS, N_MOD * D), F32),
            jax.ShapeDtypeStruct((MOD_ROWS, LANES), F32),
            jax.ShapeDtypeStruct((D, N_SEG * D), BF),
            jax.ShapeDtypeStruct((D, D), BF),
            jax.ShapeDtypeStruct((D, D), BF),
        ],
        compiler_params=_params(1),
        name="prep",
    )(cond, w_ada, b_ada, lq1, lk1, lq2, lk2, w_in, w_out)


def _head_norm(t, g, gsum, gbcast):
    ss = _dot((t * t).astype(BF), gsum)
    inv = lax.rsqrt(ss * (1.0 / HEAD_DIM) + EPS)
    hi, lo = _split_bf16(inv)
    bc = _dot(jnp.concatenate([hi, lo], axis=-1), gbcast)
    return t * bc * g


def _lane_tile(tab):
    return jnp.concatenate([tab] * (D // LANES), axis=1)


def _rope(t, cos, sin_signed):
    q = HEAD_DIM // 4
    lane = lax.broadcasted_iota(jnp.int32, (1, D), 1)
    first_half = (lane % (HEAD_DIM // 2)) < q
    fwd = pltpu.roll(t, D - q, axis=1)
    bwd = pltpu.roll(t, q, axis=1)
    return t * _lane_tile(cos) + jnp.where(first_half, fwd, bwd) * _lane_tile(sin_signed)


def _rope_t(t3, cos_t, sin_signed_t):
    q = HEAD_DIM // 4
    rot = jnp.concatenate([t3[:, q:2 * q], t3[:, 0:q], t3[:, 3 * q:4 * q], t3[:, 2 * q:3 * q]], axis=1)
    return t3 * cos_t[None] + rot * sin_signed_t[None]


def _inproj_kernel(rope, row_of_step, x_ref, m_ref, vecs_ref, w_ref, wkt_ref, kgt_ref, gsum_ref,
                   gbcast_ref, wsp_ref, bsp_ref, *rest):
    if rope:
        tab_ref, tab_t_ref, qst_out, kt_out, v_out = rest
    else:
        qst_out, kt_out, v_out = rest
    tm = x_ref.shape[0]
    m = m_ref[pl.ds(row_of_step(pl.program_id(0)), 1), :]
    shift1, scale1 = m[:, 0:D], m[:, D:2 * D]
    n1g, qg, lng, lnb = (vecs_ref[r:r + 1, :] for r in range(4))
    h = (_rms(x_ref[...], n1g) * (1.0 + scale1) + shift1).astype(BF)

    def seg(j):
        return _dot(h, w_ref[:, j * D:(j + 1) * D])

    q = _head_norm(seg(0), qg, gsum_ref[...], gbcast_ref[...])
    if rope:
        q = _rope(q, tab_ref[:, 0:LANES], tab_ref[:, LANES:2 * LANES])
    qst_out[:, 0:D] = q.astype(qst_out.dtype)

    k3 = _dot_nt(wkt_ref[...], h).reshape(D // HEAD_DIM, HEAD_DIM, tm)
    inv = lax.rsqrt(jnp.mean(k3 * k3, axis=1, keepdims=True) + EPS)
    k3 = k3 * inv * kgt_ref[...][None]
    if rope:
        k3 = _rope_t(k3, tab_t_ref[0:HEAD_DIM, :], tab_t_ref[HEAD_DIM:2 * HEAD_DIM, :])
    kt = k3.reshape(D, tm).astype(kt_out.dtype)
    width = kt_out.shape[2]
    for r in range(kt_out.shape[0]):
        kt_out[r] = kt[:, r * width:(r + 1) * width]

    v_out[...] = seg(2).astype(v_out.dtype)
    qst_out[:, D:2 * D] = jax.nn.sigmoid(seg(5)).astype(qst_out.dtype)

    zv = jax.nn.gelu(seg(4))
    mu = jnp.mean(zv, axis=-1, keepdims=True)
    zc = zv - mu
    var = jnp.mean(zc * zc, axis=-1, keepdims=True)
    zvn = (zc * lax.rsqrt(var + EPS) * lng + lnb).astype(BF)
    pre = jax.nn.gelu(seg(3)) * jax.nn.sigmoid(seg(6))
    for c in range(tm // CHUNK):
        rows = slice(c * CHUNK, (c + 1) * CHUNK)
        for g in range(N_GROUPS):
            cols = slice(g * LANES, (g + 1) * LANES)
            sp = _dot(wsp_ref[g], zvn[rows, cols]) + bsp_ref[:, cols]
            out_cols = slice(2 * D + g * LANES, 2 * D + (g + 1) * LANES)
            qst_out[rows, out_cols] = (pre[rows, cols] * sp).astype(qst_out.dtype)


INPROJ_ROWS = 512


def _inproj_call(x, m, row_of_req, rope_tabs, kv_dtype, consts, n_req, n_tok):
    t = x.shape[0]
    tm = INPROJ_ROWS
    per_req = max(1, n_tok // tm)
    per_tile = max(1, tm // n_tok)
    rope = rope_tabs is not None

    def row_of_step(i):
        return row_of_req(i * per_tile // per_req)

    tok = pl.BlockSpec((tm, D), lambda i: (i, 0))
    in_specs = [
        tok,
        _const_spec((MOD_ROWS, 6 * D)),
        _const_spec((MOD_ROWS, D)),
        pl.BlockSpec((D, N_SEG * D), lambda i: (0, 0), pipeline_mode=pl.Buffered(1)),
        _const_spec((D, D)),
        _const_spec((HEAD_DIM, tm)),
        _const_spec((D, LANES)), _const_spec((2 * LANES, D)),
        _const_spec((N_GROUPS, CHUNK, CHUNK)), _const_spec((CHUNK, D)),
    ]
    args = [x, m, consts["vecs_in"], consts["w_in"], consts["wkt"], consts["kgt"], consts["gsum"],
            consts["gbcast"], consts["wsp"], consts["bsp"]]
    if rope:
        in_specs += [pl.BlockSpec((tm, 2 * LANES), lambda i: (i % per_req, 0)),
                     pl.BlockSpec((2 * HEAD_DIM, tm), lambda i: (0, i % per_req))]
        args += list(rope_tabs)
    kt_spec = pl.BlockSpec((per_tile, D, tm // per_tile), lambda i: (i // per_req, 0, i % per_req))
    return pl.pallas_call(
        functools.partial(_inproj_kernel, rope, row_of_step),
        grid=(t // tm,),
        in_specs=in_specs,
        out_specs=[pl.BlockSpec((tm, 3 * D), lambda i: (i, 0)), kt_spec, tok],
        out_shape=[
            jax.ShapeDtypeStruct((t, 3 * D), BF),
            jax.ShapeDtypeStruct((n_req, D, n_tok), kv_dtype),
            jax.ShapeDtypeStruct((t, D), kv_dtype),
        ],
        compiler_params=_params(1),
        name="inproj_rope" if rope else "inproj",
    )(*args)


SCORE_GROUP_ELEMS = 1 << 21


def _attn_kernel(has_ctx, heads_per_group, row_of_req, qst_ref, kt_ref, v_ref, *rest):
    if has_ctx:
        kct_ref, vc_ref, *rest = rest
    lam_ref, vecs_ref, x_ref, m_ref, wo_ref, wr_ref, x1_out, h2_out, lg_out, merged_s = rest
    tq = qst_ref.shape[0]
    n2g = vecs_ref[0:1, :]
    sub_g = vecs_ref[1:2, 0:V_HEAD_DIM]
    lam = lam_ref[0:1, 0:1]
    lane = lax.broadcasted_iota(jnp.int32, (1, LANES), 1)
    first = lane < HEAD_DIM
    zero = jnp.zeros((), BF)
    ones_col = jnp.where(lane == 0, 1.0, 0.0).astype(BF)

    def head_values(ref, h):
        v = ref[:, h * V_HEAD_DIM:(h + 1) * V_HEAD_DIM].astype(BF)
        return jnp.concatenate([v, jnp.broadcast_to(ones_col, (v.shape[0], LANES))], axis=1)

    def head_cols(h):
        return slice(h * LANES, (h + 1) * LANES)

    def head_scores(h):
        q = qst_ref[:, head_cols(h)]
        qz = jnp.concatenate([jnp.where(first, q, zero), jnp.where(first, zero, q)], axis=0)
        parts = [_dot(qz, kt_ref[0, head_cols(h), :].astype(BF))]
        if has_ctx:
            parts.append(_dot(qz, kct_ref[0, head_cols(h), :].astype(BF)))
        return parts

    def row_max(scores):
        return functools.reduce(jnp.maximum, [jnp.max(s, axis=-1, keepdims=True) for s in scores])

    def head_pv(h, scores, mx):
        vals = [head_values(v_ref, h)] + ([head_values(vc_ref, h)] if has_ctx else [])
        return functools.reduce(jnp.add, [_dot(jnp.exp2(s - mx).astype(BF), v) for s, v in zip(scores, vals)])

    def head_finish(h, ob):
        den = ob[:, V_HEAD_DIM:V_HEAD_DIM + 1]
        o = ob[:tq, :V_HEAD_DIM] * (1.0 / den[:tq]) - ob[tq:, :V_HEAD_DIM] * (lam / den[tq:])
        o = _rms(o, sub_g) * (1.0 - LAMBDA_INIT)
        cols = head_cols(h)
        sga = qst_ref[:, D + h * LANES:D + (h + 1) * LANES].astype(F32)
        tb = qst_ref[:, 2 * D + h * LANES:2 * D + (h + 1) * LANES].astype(F32)
        merged_s[:, cols] = (sga * o + tb).astype(BF)

    for g0 in range(0, N_HEADS, heads_per_group):
        group = range(g0, g0 + heads_per_group)
        scores = [head_scores(h) for h in group]
        maxes = [row_max(s) for s in scores]
        outs = [head_pv(h, s, mx) for h, s, mx in zip(group, scores, maxes)]
        for h, ob in zip(group, outs):
            head_finish(h, ob)

    m = m_ref[pl.ds(row_of_req(pl.program_id(0)), 1), :]
    gate1, shift2, scale2 = m[:, 2 * D:3 * D], m[:, 3 * D:4 * D], m[:, 4 * D:5 * D]
    x1 = x_ref[...] + gate1 * _dot(merged_s[...], wo_ref[...])
    x1_out[...] = x1
    h2 = _rms(x1, n2g) * (1.0 + scale2) + shift2
    h2_out[...] = h2.astype(BF)
    hi, lo = _split_bf16(h2)
    wr_hi, wr_lo = wr_ref[:, 0:LANES], wr_ref[:, LANES:2 * LANES]
    lg_out[...] = _dot(hi, wr_hi) + _dot(lo, wr_hi) + _dot(hi, wr_lo)


def _attn_call(qst, kt, v, ctx, lamv, x, m, row_of_req, consts, n_req, n_tok):
    tq = TM
    nqb = n_tok // tq
    has_ctx = ctx is not None
    tok = pl.BlockSpec((tq, D), lambda b, i: (b * nqb + i, 0))
    in_specs = [pl.BlockSpec((tq, 3 * D), lambda b, i: (b * nqb + i, 0)),
                pl.BlockSpec((1, D, n_tok), lambda b, i: (b, 0, 0)),
                pl.BlockSpec((n_tok, D), lambda b, i: (b, 0))]
    args = [qst, kt, v]
    if has_ctx:
        n_ctx = ctx[0].shape[2]
        in_specs += [pl.BlockSpec((1, D, n_ctx), lambda b, i: (b, 0, 0)),
                     pl.BlockSpec((n_ctx, D), lambda b, i: (b, 0))]
        args += list(ctx)
    in_specs += [
        _const_spec((MOD_ROWS, LANES)), _const_spec((MOD_ROWS, D)),
        tok,
        _const_spec((MOD_ROWS, 6 * D)),
        _const_spec((D, D)),
        _const_spec((D, 2 * LANES)),
    ]
    args += [lamv, consts["vecs_attn"], x, m, consts["w_out"], consts["wr"]]
    t = n_req * n_tok
    n_keys = n_tok + (ctx[0].shape[2] if has_ctx else 0)
    heads_per_group = max(1, min(N_HEADS, SCORE_GROUP_ELEMS // (2 * tq * n_keys)))
    return pl.pallas_call(
        functools.partial(_attn_kernel, has_ctx, heads_per_group, row_of_req),
        grid=(n_req, nqb),
        in_specs=in_specs,
        out_specs=[tok, tok, pl.BlockSpec((tq, LANES), lambda b, i: (b * nqb + i, 0))],
        out_shape=[
            jax.ShapeDtypeStruct((t, D), F32),
            jax.ShapeDtypeStruct((t, D), BF),
            jax.ShapeDtypeStruct((t, LANES), F32),
        ],
        scratch_shapes=[pltpu.VMEM((tq, D), BF)],
        compiler_params=_params(2),
        name="attn_ctx" if has_ctx else "attn",
    )(*args)


GATHER_ROWS = 512
KEY_BITS = 31
ROUTE_UNROLL_ELEMS = 1 << 22
ROUTE_STEP_TOKENS = 2048


def _route_kernel(cap, n_tok, unroll, lg_ref, h2_ref, before_ref, xe_ref, gate_ref, slot_ref,
                  aff_t_s, slot_t_s, p_s):
    reqs = range(lg_ref.shape[0] // n_tok)
    lane = lax.broadcasted_iota(jnp.int32, (1, LANES), 1)
    valid = lane < N_EXPERTS

    def rows(r):
        return slice(r * n_tok, (r + 1) * n_tok)

    def affinity(r):
        lg = jnp.where(valid, lg_ref[rows(r), :], -1e30)
        ex = jnp.where(valid, jnp.exp(lg - jnp.max(lg, axis=-1, keepdims=True)), 0.0)
        return ex / jnp.sum(ex, axis=-1, keepdims=True)

    def count(mask):
        return jnp.sum(jnp.where(mask, 1.0, 0.0), axis=0, keepdims=True)

    affs = [affinity(r) for r in reqs]
    kth_bits = [jnp.zeros((1, LANES), jnp.int32) for _ in reqs]
    for bit in range(KEY_BITS - 1, -1, -1):
        for r in reqs:
            cand = kth_bits[r] | (1 << bit)
            enough = count(affs[r] >= lax.bitcast_convert_type(cand, F32)) >= cap
            kth_bits[r] = jnp.where(enough, cand, kth_bits[r])

    before = before_ref[...]
    for r in reqs:
        aff = affs[r]
        kth = lax.bitcast_convert_type(kth_bits[r], F32)
        above = aff > kth
        tied = aff == kth
        need = cap - count(above)
        tied_before = _dot(before, jnp.where(tied, 1.0, 0.0).astype(BF))
        chosen = jnp.where(above, 1.0, jnp.where(tied, jnp.where(tied_before < need, 1.0, 0.0), 0.0))
        slot = _dot(before, chosen.astype(BF))
        slot = jnp.where(valid, jnp.where(chosen > 0.0, slot, float(cap)), float(cap))
        slot_ref[rows(r), :] = slot
        slot_t_s[r] = slot.T
        aff_t_s[r] = aff.T

    slot_ids = lax.broadcasted_iota(jnp.int32, (cap, 1), 0).astype(F32)

    def per_expert(e, carry):
        for r in reqs:
            hit = slot_t_s[r, pl.ds(e, 1), :] == slot_ids
            p_s[r, pl.ds(pl.multiple_of(e * cap, cap), cap), :] = jnp.where(hit, 1.0, 0.0).astype(BF)
            gate = jnp.sum(jnp.where(hit, aff_t_s[r, pl.ds(e, 1), :], 0.0), axis=-1, keepdims=True)
            gate_ref[e, r * cap:(r + 1) * cap, :] = gate
        return carry

    lax.fori_loop(0, N_EXPERTS, per_expert, 0, unroll=unroll)
    n_rows = min(GATHER_ROWS, N_EXPERTS * cap)
    e_per = n_rows // cap
    for r in reqs:
        h2 = h2_ref[rows(r), :]
        for t in range(N_EXPERTS * cap // n_rows):
            xe = _dot(p_s[r, t * n_rows:(t + 1) * n_rows, :], h2).astype(BF)
            xe_ref[t * e_per:(t + 1) * e_per, r * cap:(r + 1) * cap, :] = xe.reshape(e_per, cap, D)


def _route_call(lg, h2, n_req, n_tok):
    cap = max(1, CAPACITY_FACTOR * n_tok // N_EXPERTS)
    per_step = max(1, min(n_req, ROUTE_STEP_TOKENS // n_tok))
    before = (lax.broadcasted_iota(jnp.int32, (n_tok, n_tok), 1)
              < lax.broadcasted_iota(jnp.int32, (n_tok, n_tok), 0)).astype(BF)
    unroll = max(1, min(N_EXPERTS, ROUTE_UNROLL_ELEMS // (per_step * cap * n_tok)))
    return pl.pallas_call(
        functools.partial(_route_kernel, cap, n_tok, unroll),
        grid=(n_req // per_step,),
        in_specs=[
            pl.BlockSpec((per_step * n_tok, LANES), lambda b: (b, 0)),
            pl.BlockSpec((per_step * n_tok, D), lambda b: (b, 0)),
            _const_spec((n_tok, n_tok)),
        ],
        out_specs=[
            pl.BlockSpec((N_EXPERTS, per_step * cap, D), lambda b: (0, b, 0)),
            pl.BlockSpec((N_EXPERTS, per_step * cap, 1), lambda b: (0, b, 0)),
            pl.BlockSpec((per_step * n_tok, LANES), lambda b: (b, 0)),
        ],
        out_shape=[
            jax.ShapeDtypeStruct((N_EXPERTS, n_req * cap, D), BF),
            jax.ShapeDtypeStruct((N_EXPERTS, n_req * cap, 1), F32),
            jax.ShapeDtypeStruct((n_req * n_tok, LANES), F32),
        ],
        scratch_shapes=[
            pltpu.VMEM((per_step, LANES, n_tok), F32),
            pltpu.VMEM((per_step, LANES, n_tok), F32),
            pltpu.VMEM((per_step, N_EXPERTS * cap, n_tok), BF),
        ],
        compiler_params=_params(1),
        name="route",
    )(lg, h2, before)


EXPERT_BLOCK = 1024
EXPERT_SUB = 512


def _experts_kernel(xa_ref, xb_ref, ga_ref, gb_ref, wg_ref, wu_ref, wd_ref, ya_ref, yb_ref, x_s, acc_s):
    f = pl.program_id(1)
    ra = xa_ref.shape[1]

    @pl.when(f == 0)
    def _():
        x_s[0:ra, :] = xa_ref[0]
        x_s[ra:, :] = xb_ref[0]
        acc_s[...] = jnp.zeros(acc_s.shape, F32)

    x = x_s[...]
    for c in range(wg_ref.shape[2] // EXPERT_SUB):
        cs = slice(c * EXPERT_SUB, (c + 1) * EXPERT_SUB)
        gate = _dot(x, wg_ref[0, :, cs].astype(BF))
        up = _dot(x, wu_ref[0, :, cs].astype(BF))
        hid = (gate * jax.nn.sigmoid(gate) * up).astype(BF)
        acc_s[...] += _dot(hid, wd_ref[0, cs, :].astype(BF))

    @pl.when(f == pl.num_programs(1) - 1)
    def _():
        ya_ref[0] = (acc_s[0:ra, :] * ga_ref[0]).astype(ya_ref.dtype)
        yb_ref[0] = (acc_s[ra:, :] * gb_ref[0]).astype(yb_ref.dtype)


def _experts_call(xa, xb, ga, gb, wg, wu, wd):
    ra, rb = xa.shape[1], xb.shape[1]
    tf = EXPERT_BLOCK
    xa_spec = pl.BlockSpec((1, ra, D), lambda e, f: (e, 0, 0))
    xb_spec = pl.BlockSpec((1, rb, D), lambda e, f: (e, 0, 0))
    return pl.pallas_call(
        _experts_kernel,
        grid=(N_EXPERTS, D_EXPERT // tf),
        in_specs=[
            xa_spec, xb_spec,
            pl.BlockSpec((1, ra, 1), lambda e, f: (e, 0, 0)),
            pl.BlockSpec((1, rb, 1), lambda e, f: (e, 0, 0)),
            pl.BlockSpec((1, D, tf), lambda e, f: (e, 0, f)),
            pl.BlockSpec((1, D, tf), lambda e, f: (e, 0, f)),
            pl.BlockSpec((1, tf, D), lambda e, f: (e, f, 0)),
        ],
        out_specs=[xa_spec, xb_spec],
        out_shape=[
            jax.ShapeDtypeStruct((N_EXPERTS, ra, D), BF),
            jax.ShapeDtypeStruct((N_EXPERTS, rb, D), BF),
        ],
        scratch_shapes=[pltpu.VMEM((ra + rb, D), BF), pltpu.VMEM((ra + rb, D), F32)],
        compiler_params=_params(2),
        name="experts",
    )(xa, xb, ga, gb, wg, wu, wd)


def _scatter_kernel(cap, row_of_req, slot_ref, y_ref, x1_ref, m_ref, expand_ref, out_ref):
    m = m_ref[pl.ds(row_of_req(pl.program_id(0)), 1), :]
    gate2 = m[:, 5 * D:6 * D]
    slot_wide = _dot(slot_ref[...].astype(BF), expand_ref[...])
    slot_ids = (lax.broadcasted_iota(jnp.int32, (1, N_EXPERTS * cap), 1) % cap).astype(F32)
    onehot = jnp.where(slot_wide == slot_ids, 1.0, 0.0).astype(BF)
    y = y_ref[...].reshape(N_EXPERTS * cap, D)
    out_ref[...] = x1_ref[...] + gate2 * _dot(onehot, y)


def _scatter_call(slots, y, x1, m, row_of_req, n_req, n_tok):
    cap = y.shape[1] // n_req
    e_ids = lax.broadcasted_iota(jnp.int32, (LANES, N_EXPERTS * cap), 0)
    c_ids = lax.broadcasted_iota(jnp.int32, (LANES, N_EXPERTS * cap), 1) // cap
    expand = (e_ids == c_ids).astype(BF)
    tok = pl.BlockSpec((n_tok, D), lambda b: (b, 0))
    return pl.pallas_call(
        functools.partial(_scatter_kernel, cap, row_of_req),
        grid=(n_req,),
        in_specs=[
            pl.BlockSpec((n_tok, LANES), lambda b: (b, 0)),
            pl.BlockSpec((N_EXPERTS, cap, D), lambda b: (0, b, 0)),
            tok,
            _const_spec((MOD_ROWS, 6 * D)),
            _const_spec((LANES, N_EXPERTS * cap)),
        ],
        out_specs=tok,
        out_shape=jax.ShapeDtypeStruct((n_req * n_tok, D), F32),
        compiler_params=_params(1),
        name="scatter",
    )(slots, y, x1, m, expand)


def _rope_tables(n_tokens):
    rows = n_tokens // GRID_W
    row = jnp.broadcast_to(jnp.arange(rows, dtype=F32)[:, None], (rows, GRID_W)).reshape(-1)
    col = jnp.broadcast_to(jnp.arange(GRID_W, dtype=F32)[None, :], (rows, GRID_W)).reshape(-1)
    half = HEAD_DIM // 4
    inv_freq = ROPE_BASE ** (-jnp.arange(half, dtype=F32) / half)
    ar = row[:, None] * inv_freq
    ac = col[:, None] * inv_freq
    ang = jnp.concatenate([ar, ar, ac, ac], axis=-1)
    cos, sin = jnp.cos(ang), jnp.sin(ang)
    first_half = (jnp.arange(HEAD_DIM) % (HEAD_DIM // 2)) < (HEAD_DIM // 4)
    sin_signed = jnp.where(first_half[None, :], -sin, sin)
    reps = LANES // HEAD_DIM
    token_major = jnp.concatenate([jnp.tile(cos, (1, reps)), jnp.tile(sin_signed, (1, reps))], axis=1)
    return token_major, jnp.concatenate([cos.T, sin_signed.T], axis=0)


def kernel(x_prompt, x_sample, cache_k, cache_v, c, c_ctx, w_ada, b_ada, norm1_g, norm2_g, w_in, q_norm_g, k_norm_g, lambda_q1, lambda_k1, lambda_q2, lambda_k2, subln_g, gmlp_ln_g, gmlp_ln_b, w_spatial, b_spatial, w_out, w_router, w_gate_e, w_up_e, w_down_e):
    n_p, t_p = x_prompt.shape[0], x_prompt.shape[1]
    n_s, t_s = x_sample.shape[0], x_sample.shape[1]
    n_ctx = cache_k.shape[2]
    l = 0

    cond = jnp.concatenate([c_ctx[None, :], c, jnp.zeros((MOD_ROWS - 1 - n_s, D), F32)], axis=0)
    m, lamv, w_in_bf, wkt, w_out_bf = _prep_call(
        cond, w_ada[l], b_ada[l][None, :], lambda_q1[l][None, :], lambda_k1[l][None, :],
        lambda_q2[l][None, :], lambda_k2[l][None, :], w_in[l], w_out[l])

    group = jnp.arange(D) // HEAD_DIM
    gsum = (group[:, None] == jnp.arange(LANES)[None, :]).astype(BF)
    wr = jnp.pad(w_router[l], ((0, 0), (0, LANES - N_EXPERTS)))
    wr_hi = wr.astype(BF)
    qg = jnp.tile(q_norm_g[l] * (HEAD_DIM ** -0.5 * math.log2(math.e)), D // HEAD_DIM)

    def rows_of(*vectors):
        rows = [jnp.pad(vec, (0, D - vec.shape[0])) for vec in vectors]
        return jnp.stack(rows + [jnp.zeros((D,), F32)] * (MOD_ROWS - len(rows)))

    consts = {
        "vecs_in": rows_of(norm1_g[l], qg, gmlp_ln_g[l], gmlp_ln_b[l]),
        "vecs_attn": rows_of(norm2_g[l], subln_g[l]),
        "w_in": w_in_bf,
        "wkt": wkt,
        "kgt": jnp.broadcast_to(k_norm_g[l][:, None], (HEAD_DIM, INPROJ_ROWS)),
        "gsum": gsum,
        "gbcast": jnp.concatenate([gsum.T, gsum.T], axis=0),
        "wsp": w_spatial[l].astype(BF),
        "bsp": jnp.repeat(b_spatial[l].T, D // N_GROUPS, axis=1),
        "w_out": w_out_bf,
        "wr": jnp.concatenate([wr_hi, (wr - wr_hi.astype(F32)).astype(BF)], axis=1),
    }

    passes = (
        (x_prompt, n_p, t_p, lambda b: 0, False),
        (x_sample, n_s, t_s, lambda b: 1 + b, True),
    )
    outs = []
    kv_out = None
    for x, n_req, n_tok, row_of_req, is_sample in passes:
        xf = x.reshape(n_req * n_tok, D)
        tabs = _rope_tables(n_tok) if is_sample else None
        qst, kt, v = _inproj_call(xf, m, row_of_req, tabs, BF if is_sample else F32, consts, n_req, n_tok)
        if is_sample:
            ctx = (jnp.transpose(cache_k[:, l], (0, 2, 3, 4, 1)).reshape(n_req, D, n_ctx),
                   cache_v[:, l].reshape(n_req * n_ctx, D))
        else:
            ctx = None
            kv_out = (kt, v)
        x1, h2, lg = _attn_call(qst, kt, v, ctx, lamv, xf, m, row_of_req, consts, n_req, n_tok)
        xe, gates, slots = _route_call(lg, h2, n_req, n_tok)
        outs.append((x1, xe, gates, slots, row_of_req))

    (x1p, xep, gp, slots_p, row_p), (x1s, xes, gs, slots_s, row_s) = outs
    yp, ys = _experts_call(xep, xes, gp, gs, w_gate_e[l], w_up_e[l], w_down_e[l])
    y_prompt = _scatter_call(slots_p, yp, x1p, m, row_p, n_p, t_p).reshape(x_prompt.shape)
    y_sample = _scatter_call(slots_s, ys, x1s, m, row_s, n_s, t_s).reshape(x_sample.shape)
    new_k = jnp.transpose(kv_out[0].reshape(n_p, N_HEADS, 2, HEAD_DIM, t_p), (0, 4, 1, 2, 3))
    new_k = new_k.reshape(n_p, 1, t_p, N_HEADS, 2, HEAD_DIM)
    new_v = kv_out[1].reshape(n_p, 1, t_p, N_HEADS, V_HEAD_DIM)
    return (y_prompt, y_sample, new_k, new_v)
```

```python
import functools
import math

import jax
import jax.numpy as jnp
from jax import lax
from jax.experimental import pallas as pl
from jax.experimental.pallas import tpu as pltpu

D = 1024
N_HEADS = 8
HEAD_DIM = 64
V_HEAD_DIM = 128
GRID_W = 64
ROPE_BASE = 10000.0
CHUNK = 128
N_GROUPS = 8
N_EXPERTS = 16
CAPACITY_FACTOR = 2
D_EXPERT = 2048
N_SEG = 7
K_SEG = 1
EPS = 1e-6
LAMBDA_INIT = 0.8 - 0.6 * math.exp(-0.3 * 0)

LANES = 128
MOD_ROWS = 8
VMEM_LIMIT = 56 * 1024 * 1024
TM = 256

BF = jnp.bfloat16
F32 = jnp.float32


def _dot(a, b):
    return jnp.dot(a, b, preferred_element_type=F32)


def _dot_nt(a, b):
    return lax.dot_general(a, b, (((1,), (1,)), ((), ())), preferred_element_type=F32)


def _split_bf16(x):
    hi = x.astype(BF)
    lo = (x - hi.astype(F32)).astype(BF)
    return hi, lo


def _rms(x, g):
    return x * lax.rsqrt(jnp.mean(x * x, axis=-1, keepdims=True) + EPS) * g


def _params(n_grid_dims):
    return pltpu.CompilerParams(
        dimension_semantics=("arbitrary",) * n_grid_dims, vmem_limit_bytes=VMEM_LIMIT)


def _const_spec(shape):
    nd = len(shape)
    return pl.BlockSpec(shape, lambda *_: (0,) * nd)


N_MOD = 6


def _prep_kernel(cond_ref, wa_ref, ba_ref, lq1_ref, lk1_ref, lq2_ref, lk2_ref, wi_ref, wo_ref,
                 m_ref, lam_ref, wib_ref, wkt_ref, wob_ref):
    j = pl.program_id(0)

    @pl.when(j < N_MOD)
    def _():
        c = cond_ref[...]
        a_hi, a_lo = _split_bf16(c * jax.nn.sigmoid(c))
        w_hi, w_lo = _split_bf16(wa_ref[...])
        m_ref[...] = _dot(a_hi, w_hi) + _dot(a_lo, w_hi) + _dot(a_hi, w_lo) + ba_ref[...]

    w = wi_ref[...]
    wib_ref[...] = w.astype(BF)

    @pl.when(j == K_SEG)
    def _():
        wkt_ref[...] = w.T.astype(BF)

    @pl.when(j == 0)
    def _():
        s1 = jnp.sum(lq1_ref[...] * lk1_ref[...], axis=-1, keepdims=True)
        s2 = jnp.sum(lq2_ref[...] * lk2_ref[...], axis=-1, keepdims=True)
        lam_ref[...] = jnp.broadcast_to(jnp.exp(s1) - jnp.exp(s2) + LAMBDA_INIT, lam_ref.shape)
        wob_ref[...] = wo_ref[...].astype(BF)


def _prep_call(cond, w_ada, b_ada, lq1, lk1, lq2, lk2, w_in, w_out):
    vec = _const_spec((1, HEAD_DIM))
    mod_block = lambda j: (0, jnp.minimum(j, N_MOD - 1))
    return pl.pallas_call(
        _prep_kernel,
        grid=(N_SEG,),
        in_specs=[
            _const_spec((MOD_ROWS, D)),
            pl.BlockSpec((D, D), mod_block),
            pl.BlockSpec((1, D), mod_block),
            vec, vec, vec, vec,
            pl.BlockSpec((D, D), lambda j: (0, j)),
            _const_spec((D, D)),
        ],
        out_specs=[
            pl.BlockSpec((MOD_ROWS, D), mod_block),
            _const_spec((MOD_ROWS, LANES)),
            pl.BlockSpec((D, D), lambda j: (0, j)),
            _const_spec((D, D)),
            _const_spec((D, D)),
        ],
        out_shape=[
            jax.ShapeDtypeStruct((MOD_ROWS, N_MOD * D), F32),
            jax.ShapeDtypeStruct((MOD_ROWS, LANES), F32),
            jax.ShapeDtypeStruct((D, N_SEG * D), BF),
            jax.ShapeDtypeStruct((D, D), BF),
            jax.ShapeDtypeStruct((D, D), BF),
        ],
        compiler_params=_params(1),
        name="prep",
    )(cond, w_ada, b_ada, lq1, lk1, lq2, lk2, w_in, w_out)


def _head_norm(t, g, gsum, gbcast):
    ss = _dot((t * t).astype(BF), gsum)
    inv = lax.rsqrt(ss * (1.0 / HEAD_DIM) + EPS)
    hi, lo = _split_bf16(inv)
    bc = _dot(jnp.concatenate([hi, lo], axis=-1), gbcast)
    return t * bc * g


def _lane_tile(tab):
    return jnp.concatenate([tab] * (D // LANES), axis=1)


def _rope(t, cos, sin_signed):
    q = HEAD_DIM // 4
    lane = lax.broadcasted_iota(jnp.int32, (1, D), 1)
    first_half = (lane % (HEAD_DIM // 2)) < q
    fwd = pltpu.roll(t, D - q, axis=1)
    bwd = pltpu.roll(t, q, axis=1)
    return t * _lane_tile(cos) + jnp.where(first_half, fwd, bwd) * _lane_tile(sin_signed)


def _rope_t(t3, cos_t, sin_signed_t):
    q = HEAD_DIM // 4
    rot = jnp.concatenate([t3[:, q:2 * q], t3[:, 0:q], t3[:, 3 * q:4 * q], t3[:, 2 * q:3 * q]], axis=1)
    return t3 * cos_t[None] + rot * sin_signed_t[None]


def _inproj_kernel(rope, row_of_step, x_ref, m_ref, vecs_ref, w_ref, wkt_ref, kgt_ref, gsum_ref,
                   gbcast_ref, wsp_ref, bsp_ref, *rest):
    if rope:
        tab_ref, tab_t_ref, qst_out, kt_out, v_out = rest
    else:
        qst_out, kt_out, v_out = rest
    tm = x_ref.shape[0]
    m = m_ref[pl.ds(row_of_step(pl.program_id(0)), 1), :]
    shift1, scale1 = m[:, 0:D], m[:, D:2 * D]
    n1g, qg, lng, lnb = (vecs_ref[r:r + 1, :] for r in range(4))
    h = (_rms(x_ref[...], n1g) * (1.0 + scale1) + shift1).astype(BF)

    def seg(j):
        return _dot(h, w_ref[:, j * D:(j + 1) * D])

    q = _head_norm(seg(0), qg, gsum_ref[...], gbcast_ref[...])
    if rope:
        q = _rope(q, tab_ref[:, 0:LANES], tab_ref[:, LANES:2 * LANES])
    qst_out[:, 0:D] = q.astype(qst_out.dtype)

    k3 = _dot_nt(wkt_ref[...], h).reshape(D // HEAD_DIM, HEAD_DIM, tm)
    inv = lax.rsqrt(jnp.mean(k3 * k3, axis=1, keepdims=True) + EPS)
    k3 = k3 * inv * kgt_ref[...][None]
    if rope:
        k3 = _rope_t(k3, tab_t_ref[0:HEAD_DIM, :], tab_t_ref[HEAD_DIM:2 * HEAD_DIM, :])
    kt = k3.reshape(D, tm).astype(kt_out.dtype)
    width = kt_out.shape[2]
    for r in range(kt_out.shape[0]):
        kt_out[r] = kt[:, r * width:(r + 1) * width]

    v_out[...] = seg(2).astype(v_out.dtype)
    qst_out[:, D:2 * D] = jax.nn.sigmoid(seg(5)).astype(qst_out.dtype)

    zv = jax.nn.gelu(seg(4))
    mu = jnp.mean(zv, axis=-1, keepdims=True)
    zc = zv - mu
    var = jnp.mean(zc * zc, axis=-1, keepdims=True)
    zvn = (zc * lax.rsqrt(var + EPS) * lng + lnb).astype(BF)
    pre = jax.nn.gelu(seg(3)) * jax.nn.sigmoid(seg(6))
    for c in range(tm // CHUNK):
        rows = slice(c * CHUNK, (c + 1) * CHUNK)
        for g in range(N_GROUPS):
            cols = slice(g * LANES, (g + 1) * LANES)
            sp = _dot(wsp_ref[g], zvn[rows, cols]) + bsp_ref[:, cols]
            out_cols = slice(2 * D + g * LANES, 2 * D + (g + 1) * LANES)
            qst_out[rows, out_cols] = (pre[rows, cols] * sp).astype(qst_out.dtype)


INPROJ_ROWS = 512


def _inproj_call(x, m, row_of_req, rope_tabs, kv_dtype, consts, n_req, n_tok):
    t = x.shape[0]
    tm = INPROJ_ROWS
    per_req = max(1, n_tok // tm)
    per_tile = max(1, tm // n_tok)
    rope = rope_tabs is not None

    def row_of_step(i):
        return row_of_req(i * per_tile // per_req)

    tok = pl.BlockSpec((tm, D), lambda i: (i, 0))
    in_specs = [
        tok,
        _const_spec((MOD_ROWS, 6 * D)),
        _const_spec((MOD_ROWS, D)),
        pl.BlockSpec((D, N_SEG * D), lambda i: (0, 0), pipeline_mode=pl.Buffered(1)),
        _const_spec((D, D)),
        _const_spec((HEAD_DIM, tm)),
        _const_spec((D, LANES)), _const_spec((2 * LANES, D)),
        _const_spec((N_GROUPS, CHUNK, CHUNK)), _const_spec((CHUNK, D)),
    ]
    args = [x, m, consts["vecs_in"], consts["w_in"], consts["wkt"], consts["kgt"], consts["gsum"],
            consts["gbcast"], consts["wsp"], consts["bsp"]]
    if rope:
        in_specs += [pl.BlockSpec((tm, 2 * LANES), lambda i: (i % per_req, 0)),
                     pl.BlockSpec((2 * HEAD_DIM, tm), lambda i: (0, i % per_req))]
        args += list(rope_tabs)
    kt_spec = pl.BlockSpec((per_tile, D, tm // per_tile), lambda i: (i // per_req, 0, i % per_req))
    return pl.pallas_call(
        functools.partial(_inproj_kernel, rope, row_of_step),
        grid=(t // tm,),
        in_specs=in_specs,
        out_specs=[pl.BlockSpec((tm, 3 * D), lambda i: (i, 0)), kt_spec, tok],
        out_shape=[
            jax.ShapeDtypeStruct((t, 3 * D), BF),
            jax.ShapeDtypeStruct((n_req, D, n_tok), kv_dtype),
            jax.ShapeDtypeStruct((t, D), kv_dtype),
        ],
        compiler_params=_params(1),
        name="inproj_rope" if rope else "inproj",
    )(*args)


SCORE_GROUP_ELEMS = 1 << 22
ATTN_ROWS = 256


def _attn_kernel(has_ctx, heads_per_group, row_of_req, qst_ref, kt_ref, v_ref, *rest):
    if has_ctx:
        kct_ref, vc_ref, *rest = rest
    lam_ref, vecs_ref, x_ref, m_ref, wo_ref, wr_ref, x1_out, h2_out, lg_out, merged_s = rest
    tq = qst_ref.shape[0]
    n2g = vecs_ref[0:1, :]
    sub_g = vecs_ref[1:2, 0:V_HEAD_DIM]
    lam = lam_ref[0:1, 0:1]
    lane = lax.broadcasted_iota(jnp.int32, (1, LANES), 1)
    first = lane < HEAD_DIM
    zero = jnp.zeros((), BF)
    ones_col = jnp.where(lane == 0, 1.0, 0.0).astype(BF)

    def head_values(ref, h):
        v = ref[:, h * V_HEAD_DIM:(h + 1) * V_HEAD_DIM].astype(BF)
        return jnp.concatenate([v, jnp.broadcast_to(ones_col, (v.shape[0], LANES))], axis=1)

    def head_cols(h):
        return slice(h * LANES, (h + 1) * LANES)

    def head_scores(h):
        q = qst_ref[:, head_cols(h)]
        qz = jnp.concatenate([jnp.where(first, q, zero), jnp.where(first, zero, q)], axis=0)
        parts = [_dot(qz, kt_ref[0, head_cols(h), :].astype(BF))]
        if has_ctx:
            parts.append(_dot(qz, kct_ref[0, head_cols(h), :].astype(BF)))
        return parts

    def row_max(scores):
        return functools.reduce(jnp.maximum, [jnp.max(s, axis=-1, keepdims=True) for s in scores])

    def head_pv(h, scores, mx):
        vals = [head_values(v_ref, h)] + ([head_values(vc_ref, h)] if has_ctx else [])
        return functools.reduce(jnp.add, [_dot(jnp.exp2(s - mx).astype(BF), v) for s, v in zip(scores, vals)])

    def head_finish(h, ob):
        den = ob[:, V_HEAD_DIM:V_HEAD_DIM + 1]
        o = ob[:tq, :V_HEAD_DIM] * (1.0 / den[:tq]) - ob[tq:, :V_HEAD_DIM] * (lam / den[tq:])
        o = _rms(o, sub_g) * (1.0 - LAMBDA_INIT)
        cols = head_cols(h)
        sga = qst_ref[:, D + h * LANES:D + (h + 1) * LANES].astype(F32)
        tb = qst_ref[:, 2 * D + h * LANES:2 * D + (h + 1) * LANES].astype(F32)
        merged_s[:, cols] = (sga * o + tb).astype(BF)

    for g0 in range(0, N_HEADS, heads_per_group):
        group = range(g0, g0 + heads_per_group)
        scores = [head_scores(h) for h in group]
        maxes = [row_max(s) for s in scores]
        outs = [head_pv(h, s, mx) for h, s, mx in zip(group, scores, maxes)]
        for h, ob in zip(group, outs):
            head_finish(h, ob)

    m = m_ref[pl.ds(row_of_req(pl.program_id(0)), 1), :]
    gate1, shift2, scale2 = m[:, 2 * D:3 * D], m[:, 3 * D:4 * D], m[:, 4 * D:5 * D]
    x1 = x_ref[...] + gate1 * _dot(merged_s[...], wo_ref[...])
    x1_out[...] = x1
    h2 = _rms(x1, n2g) * (1.0 + scale2) + shift2
    h2_out[...] = h2.astype(BF)
    hi, lo = _split_bf16(h2)
    wr_hi, wr_lo = wr_ref[:, 0:LANES], wr_ref[:, LANES:2 * LANES]
    lg_out[...] = _dot(hi, wr_hi) + _dot(lo, wr_hi) + _dot(hi, wr_lo)


def _attn_call(qst, kt, v, ctx, lamv, x, m, row_of_req, consts, n_req, n_tok):
    tq = min(n_tok, ATTN_ROWS)
    nqb = n_tok // tq
    has_ctx = ctx is not None
    tok = pl.BlockSpec((tq, D), lambda b, i: (b * nqb + i, 0))
    in_specs = [pl.BlockSpec((tq, 3 * D), lambda b, i: (b * nqb + i, 0)),
                pl.BlockSpec((1, D, n_tok), lambda b, i: (b, 0, 0)),
                pl.BlockSpec((n_tok, D), lambda b, i: (b, 0))]
    args = [qst, kt, v]
    if has_ctx:
        n_ctx = ctx[0].shape[2]
        in_specs += [pl.BlockSpec((1, D, n_ctx), lambda b, i: (b, 0, 0)),
                     pl.BlockSpec((n_ctx, D), lambda b, i: (b, 0))]
        args += list(ctx)
    in_specs += [
        _const_spec((MOD_ROWS, LANES)), _const_spec((MOD_ROWS, D)),
        tok,
        _const_spec((MOD_ROWS, 6 * D)),
        _const_spec((D, D)),
        _const_spec((D, 2 * LANES)),
    ]
    args += [lamv, consts["vecs_attn"], x, m, consts["w_out"], consts["wr"]]
    t = n_req * n_tok
    n_keys = n_tok + (ctx[0].shape[2] if has_ctx else 0)
    heads_per_group = max(1, min(N_HEADS, SCORE_GROUP_ELEMS // (2 * tq * n_keys)))
    while N_HEADS % heads_per_group:
        heads_per_group -= 1
    return pl.pallas_call(
        functools.partial(_attn_kernel, has_ctx, heads_per_group, row_of_req),
        grid=(n_req, nqb),
        in_specs=in_specs,
        out_specs=[tok, tok, pl.BlockSpec((tq, LANES), lambda b, i: (b * nqb + i, 0))],
        out_shape=[
            jax.ShapeDtypeStruct((t, D), F32),
            jax.ShapeDtypeStruct((t, D), BF),
            jax.ShapeDtypeStruct((t, LANES), F32),
        ],
        scratch_shapes=[pltpu.VMEM((tq, D), BF)],
        compiler_params=_params(2),
        name="attn_ctx" if has_ctx else "attn",
    )(*args)


GATHER_ROWS = 512
KEY_BITS = 31
ROUTE_UNROLL_ELEMS = 1 << 22
ROUTE_STEP_TOKENS = 2048


def _route_kernel(cap, n_tok, unroll, lg_ref, h2_ref, before_ref, xe_ref, gate_ref, slot_ref,
                  aff_t_s, slot_t_s, p_s):
    reqs = range(lg_ref.shape[0] // n_tok)
    lane = lax.broadcasted_iota(jnp.int32, (1, LANES), 1)
    valid = lane < N_EXPERTS

    def rows(r):
        return slice(r * n_tok, (r + 1) * n_tok)

    def affinity(r):
        lg = jnp.where(valid, lg_ref[rows(r), :], -1e30)
        ex = jnp.where(valid, jnp.exp(lg - jnp.max(lg, axis=-1, keepdims=True)), 0.0)
        return ex / jnp.sum(ex, axis=-1, keepdims=True)

    def count(mask):
        return jnp.sum(jnp.where(mask, 1.0, 0.0), axis=0, keepdims=True)

    affs = [affinity(r) for r in reqs]
    kth_bits = [jnp.zeros((1, LANES), jnp.int32) for _ in reqs]
    for bit in range(KEY_BITS - 1, -1, -1):
        for r in reqs:
            cand = kth_bits[r] | (1 << bit)
            enough = count(affs[r] >= lax.bitcast_convert_type(cand, F32)) >= cap
            kth_bits[r] = jnp.where(enough, cand, kth_bits[r])

    before = before_ref[...]
    for r in reqs:
        aff = affs[r]
        kth = lax.bitcast_convert_type(kth_bits[r], F32)
        above = aff > kth
        tied = aff == kth
        need = cap - count(above)
        tied_before = _dot(before, jnp.where(tied, 1.0, 0.0).astype(BF))
        chosen = jnp.where(above, 1.0, jnp.where(tied, jnp.where(tied_before < need, 1.0, 0.0), 0.0))
        slot = _dot(before, chosen.astype(BF))
        slot = jnp.where(valid, jnp.where(chosen > 0.0, slot, float(cap)), float(cap))
        slot_ref[rows(r), :] = slot
        slot_t_s[r] = slot.T
        aff_t_s[r] = aff.T

    slot_ids = lax.broadcasted_iota(jnp.int32, (cap, 1), 0).astype(F32)

    def per_expert(e, carry):
        for r in reqs:
            hit = slot_t_s[r, pl.ds(e, 1), :] == slot_ids
            p_s[r, pl.ds(pl.multiple_of(e * cap, cap), cap), :] = jnp.where(hit, 1.0, 0.0).astype(BF)
            gate = jnp.sum(jnp.where(hit, aff_t_s[r, pl.ds(e, 1), :], 0.0), axis=-1, keepdims=True)
            gate_ref[e, r * cap:(r + 1) * cap, :] = gate
        return carry

    lax.fori_loop(0, N_EXPERTS, per_expert, 0, unroll=unroll)
    n_rows = min(GATHER_ROWS, N_EXPERTS * cap)
    e_per = n_rows // cap
    for r in reqs:
        h2 = h2_ref[rows(r), :]
        for t in range(N_EXPERTS * cap // n_rows):
            xe = _dot(p_s[r, t * n_rows:(t + 1) * n_rows, :], h2).astype(BF)
            xe_ref[t * e_per:(t + 1) * e_per, r * cap:(r + 1) * cap, :] = xe.reshape(e_per, cap, D)


def _route_call(lg, h2, n_req, n_tok):
    cap = max(1, CAPACITY_FACTOR * n_tok // N_EXPERTS)
    per_step = max(1, min(n_req, ROUTE_STEP_TOKENS // n_tok))
    before = (lax.broadcasted_iota(jnp.int32, (n_tok, n_tok), 1)
              < lax.broadcasted_iota(jnp.int32, (n_tok, n_tok), 0)).astype(BF)
    unroll = max(1, min(N_EXPERTS, ROUTE_UNROLL_ELEMS // (per_step * cap * n_tok)))
    return pl.pallas_call(
        functools.partial(_route_kernel, cap, n_tok, unroll),
        grid=(n_req // per_step,),
        in_specs=[
            pl.BlockSpec((per_step * n_tok, LANES), lambda b: (b, 0)),
            pl.BlockSpec((per_step * n_tok, D), lambda b: (b, 0)),
            _const_spec((n_tok, n_tok)),
        ],
        out_specs=[
            pl.BlockSpec((N_EXPERTS, per_step * cap, D), lambda b: (0, b, 0)),
            pl.BlockSpec((N_EXPERTS, per_step * cap, 1), lambda b: (0, b, 0)),
            pl.BlockSpec((per_step * n_tok, LANES), lambda b: (b, 0)),
        ],
        out_shape=[
            jax.ShapeDtypeStruct((N_EXPERTS, n_req * cap, D), BF),
            jax.ShapeDtypeStruct((N_EXPERTS, n_req * cap, 1), F32),
            jax.ShapeDtypeStruct((n_req * n_tok, LANES), F32),
        ],
        scratch_shapes=[
            pltpu.VMEM((per_step, LANES, n_tok), F32),
            pltpu.VMEM((per_step, LANES, n_tok), F32),
            pltpu.VMEM((per_step, N_EXPERTS * cap, n_tok), BF),
        ],
        compiler_params=_params(1),
        name="route",
    )(lg, h2, before)


EXPERT_BLOCK = 1024
EXPERT_SUB = 256


def _experts_kernel(xa_ref, xb_ref, ga_ref, gb_ref, wg_ref, wu_ref, wd_ref, ya_ref, yb_ref, x_s, acc_s):
    f = pl.program_id(1)
    ra = xa_ref.shape[1]

    @pl.when(f == 0)
    def _():
        x_s[0:ra, :] = xa_ref[0]
        x_s[ra:, :] = xb_ref[0]
        acc_s[...] = jnp.zeros(acc_s.shape, F32)

    x = x_s[...]
    for c in range(wg_ref.shape[2] // EXPERT_SUB):
        cs = slice(c * EXPERT_SUB, (c + 1) * EXPERT_SUB)
        gate = _dot(x, wg_ref[0, :, cs].astype(BF))
        up = _dot(x, wu_ref[0, :, cs].astype(BF))
        hid = (gate * jax.nn.sigmoid(gate) * up).astype(BF)
        acc_s[...] += _dot(hid, wd_ref[0, cs, :].astype(BF))

    @pl.when(f == pl.num_programs(1) - 1)
    def _():
        ya_ref[0] = (acc_s[0:ra, :] * ga_ref[0]).astype(ya_ref.dtype)
        yb_ref[0] = (acc_s[ra:, :] * gb_ref[0]).astype(yb_ref.dtype)


def _experts_call(xa, xb, ga, gb, wg, wu, wd):
    ra, rb = xa.shape[1], xb.shape[1]
    tf = EXPERT_BLOCK
    xa_spec = pl.BlockSpec((1, ra, D), lambda e, f: (e, 0, 0))
    xb_spec = pl.BlockSpec((1, rb, D), lambda e, f: (e, 0, 0))
    return pl.pallas_call(
        _experts_kernel,
        grid=(N_EXPERTS, D_EXPERT // tf),
        in_specs=[
            xa_spec, xb_spec,
            pl.BlockSpec((1, ra, 1), lambda e, f: (e, 0, 0)),
            pl.BlockSpec((1, rb, 1), lambda e, f: (e, 0, 0)),
            pl.BlockSpec((1, D, tf), lambda e, f: (e, 0, f)),
            pl.BlockSpec((1, D, tf), lambda e, f: (e, 0, f)),
            pl.BlockSpec((1, tf, D), lambda e, f: (e, f, 0)),
        ],
        out_specs=[xa_spec, xb_spec],
        out_shape=[
            jax.ShapeDtypeStruct((N_EXPERTS, ra, D), BF),
            jax.ShapeDtypeStruct((N_EXPERTS, rb, D), BF),
        ],
        scratch_shapes=[pltpu.VMEM((ra + rb, D), BF), pltpu.VMEM((ra + rb, D), F32)],
        compiler_params=_params(2),
        name="experts",
    )(xa, xb, ga, gb, wg, wu, wd)


def _scatter_kernel(cap, row_of_req, slot_ref, y_ref, x1_ref, m_ref, expand_ref, out_ref):
    m = m_ref[pl.ds(row_of_req(pl.program_id(0)), 1), :]
    gate2 = m[:, 5 * D:6 * D]
    slot_wide = _dot(slot_ref[...].astype(BF), expand_ref[...])
    slot_ids = (lax.broadcasted_iota(jnp.int32, (1, N_EXPERTS * cap), 1) % cap).astype(F32)
    onehot = jnp.where(slot_wide == slot_ids, 1.0, 0.0).astype(BF)
    y = y_ref[...].reshape(N_EXPERTS * cap, D)
    out_ref[...] = x1_ref[...] + gate2 * _dot(onehot, y)


def _scatter_call(slots, y, x1, m, row_of_req, n_req, n_tok):
    cap = y.shape[1] // n_req
    e_ids = lax.broadcasted_iota(jnp.int32, (LANES, N_EXPERTS * cap), 0)
    c_ids = lax.broadcasted_iota(jnp.int32, (LANES, N_EXPERTS * cap), 1) // cap
    expand = (e_ids == c_ids).astype(BF)
    tok = pl.BlockSpec((n_tok, D), lambda b: (b, 0))
    return pl.pallas_call(
        functools.partial(_scatter_kernel, cap, row_of_req),
        grid=(n_req,),
        in_specs=[
            pl.BlockSpec((n_tok, LANES), lambda b: (b, 0)),
            pl.BlockSpec((N_EXPERTS, cap, D), lambda b: (0, b, 0)),
            tok,
            _const_spec((MOD_ROWS, 6 * D)),
            _const_spec((LANES, N_EXPERTS * cap)),
        ],
        out_specs=tok,
        out_shape=jax.ShapeDtypeStruct((n_req * n_tok, D), F32),
        compiler_params=_params(1),
        name="scatter",
    )(slots, y, x1, m, expand)


def _rope_tables(n_tokens):
    rows = n_tokens // GRID_W
    row = jnp.broadcast_to(jnp.arange(rows, dtype=F32)[:, None], (rows, GRID_W)).reshape(-1)
    col = jnp.broadcast_to(jnp.arange(GRID_W, dtype=F32)[None, :], (rows, GRID_W)).reshape(-1)
    half = HEAD_DIM // 4
    inv_freq = ROPE_BASE ** (-jnp.arange(half, dtype=F32) / half)
    ar = row[:, None] * inv_freq
    ac = col[:, None] * inv_freq
    ang = jnp.concatenate([ar, ar, ac, ac], axis=-1)
    cos, sin = jnp.cos(ang), jnp.sin(ang)
    first_half = (jnp.arange(HEAD_DIM) % (HEAD_DIM // 2)) < (HEAD_DIM // 4)
    sin_signed = jnp.where(first_half[None, :], -sin, sin)
    reps = LANES // HEAD_DIM
    token_major = jnp.concatenate([jnp.tile(cos, (1, reps)), jnp.tile(sin_signed, (1, reps))], axis=1)
    return token_major, jnp.concatenate([cos.T, sin_signed.T], axis=0)


def kernel(x_prompt, x_sample, cache_k, cache_v, c, c_ctx, w_ada, b_ada, norm1_g, norm2_g, w_in, q_norm_g, k_norm_g, lambda_q1, lambda_k1, lambda_q2, lambda_k2, subln_g, gmlp_ln_g, gmlp_ln_b, w_spatial, b_spatial, w_out, w_router, w_gate_e, w_up_e, w_down_e):
    n_p, t_p = x_prompt.shape[0], x_prompt.shape[1]
    n_s, t_s = x_sample.shape[0], x_sample.shape[1]
    n_ctx = cache_k.shape[2]
    l = 0

    cond = jnp.concatenate([c_ctx[None, :], c, jnp.zeros((MOD_ROWS - 1 - n_s, D), F32)], axis=0)
    m, lamv, w_in_bf, wkt, w_out_bf = _prep_call(
        cond, w_ada[l], b_ada[l][None, :], lambda_q1[l][None, :], lambda_k1[l][None, :],
        lambda_q2[l][None, :], lambda_k2[l][None, :], w_in[l], w_out[l])

    group = jnp.arange(D) // HEAD_DIM
    gsum = (group[:, None] == jnp.arange(LANES)[None, :]).astype(BF)
    wr = jnp.pad(w_router[l], ((0, 0), (0, LANES - N_EXPERTS)))
    wr_hi = wr.astype(BF)
    qg = jnp.tile(q_norm_g[l] * (HEAD_DIM ** -0.5 * math.log2(math.e)), D // HEAD_DIM)

    def rows_of(*vectors):
        rows = [jnp.pad(vec, (0, D - vec.shape[0])) for vec in vectors]
        return jnp.stack(rows + [jnp.zeros((D,), F32)] * (MOD_ROWS - len(rows)))

    consts = {
        "vecs_in": rows_of(norm1_g[l], qg, gmlp_ln_g[l], gmlp_ln_b[l]),
        "vecs_attn": rows_of(norm2_g[l], subln_g[l]),
        "w_in": w_in_bf,
        "wkt": wkt,
        "kgt": jnp.broadcast_to(k_norm_g[l][:, None], (HEAD_DIM, INPROJ_ROWS)),
        "gsum": gsum,
        "gbcast": jnp.concatenate([gsum.T, gsum.T], axis=0),
        "wsp": w_spatial[l].astype(BF),
        "bsp": jnp.repeat(b_spatial[l].T, D // N_GROUPS, axis=1),
        "w_out": w_out_bf,
        "wr": jnp.concatenate([wr_hi, (wr - wr_hi.astype(F32)).astype(BF)], axis=1),
    }

    passes = (
        (x_prompt, n_p, t_p, lambda b: 0, False),
        (x_sample, n_s, t_s, lambda b: 1 + b, True),
    )
    outs = []
    kv_out = None
    for x, n_req, n_tok, row_of_req, is_sample in passes:
        xf = x.reshape(n_req * n_tok, D)
        tabs = _rope_tables(n_tok) if is_sample else None
        qst, kt, v = _inproj_call(xf, m, row_of_req, tabs, BF if is_sample else F32, consts, n_req, n_tok)
        if is_sample:
            ctx = (jnp.transpose(cache_k[:, l], (0, 2, 3, 4, 1)).reshape(n_req, D, n_ctx),
                   cache_v[:, l].reshape(n_req * n_ctx, D))
        else:
            ctx = None
            kv_out = (kt, v)
        x1, h2, lg = _attn_call(qst, kt, v, ctx, lamv, xf, m, row_of_req, consts, n_req, n_tok)
        xe, gates, slots = _route_call(lg, h2, n_req, n_tok)
        outs.append((x1, xe, gates, slots, row_of_req))

    (x1p, xep, gp, slots_p, row_p), (x1s, xes, gs, slots_s, row_s) = outs
    yp, ys = _experts_call(xep, xes, gp, gs, w_gate_e[l], w_up_e[l], w_down_e[l])
    y_prompt = _scatter_call(slots_p, yp, x1p, m, row_p, n_p, t_p).reshape(x_prompt.shape)
    y_sample = _scatter_call(slots_s, ys, x1s, m, row_s, n_s, t_s).reshape(x_sample.shape)
    new_k = jnp.transpose(kv_out[0].reshape(n_p, N_HEADS, 2, HEAD_DIM, t_p), (0, 4, 1, 2, 3))
    new_k = new_k.reshape(n_p, 1, t_p, N_HEADS, 2, HEAD_DIM)
    new_v = kv_out[1].reshape(n_p, 1, t_p, N_HEADS, V_HEAD_DIM)
    return (y_prompt, y_sample, new_k, new_v)
```

```python
import functools
import math

import jax
import jax.numpy as jnp
import numpy as np
from jax import lax
from jax.experimental import pallas as pl
from jax.experimental.pallas import tpu as pltpu

D = 1024
N_HEADS = 8
HEAD_DIM = 64
V_HEAD_DIM = 128
GRID_W = 64
ROPE_BASE = 10000.0
CHUNK = 128
N_GROUPS = 8
N_EXPERTS = 16
CAPACITY_FACTOR = 2
D_EXPERT = 2048
N_SEG = 7
K_SEG = 1
EPS = 1e-6
LAMBDA_INIT = 0.8 - 0.6 * math.exp(-0.3 * 0)

LANES = 128
MOD_ROWS = 8
VMEM_LIMIT = 56 * 1024 * 1024
TM = 256

BF = jnp.bfloat16
F32 = jnp.float32


def _dot(a, b):
    return jnp.dot(a, b, preferred_element_type=F32)


def _dot_nt(a, b):
    return lax.dot_general(a, b, (((1,), (1,)), ((), ())), preferred_element_type=F32)


def _split_bf16(x):
    hi = x.astype(BF)
    lo = (x - hi.astype(F32)).astype(BF)
    return hi, lo


def _rms(x, g):
    return x * lax.rsqrt(jnp.mean(x * x, axis=-1, keepdims=True) + EPS) * g


def _params(n_grid_dims):
    return pltpu.CompilerParams(
        dimension_semantics=("arbitrary",) * n_grid_dims, vmem_limit_bytes=VMEM_LIMIT)


def _const_spec(shape):
    nd = len(shape)
    return pl.BlockSpec(shape, lambda *_: (0,) * nd)


N_MOD = 6


def _prep_kernel(cond_ref, wa_ref, ba_ref, lq1_ref, lk1_ref, lq2_ref, lk2_ref, wi_ref, wo_ref,
                 m_ref, lam_ref, wib_ref, wkt_ref, wob_ref):
    j = pl.program_id(0)

    @pl.when(j < N_MOD)
    def _():
        c = cond_ref[...]
        a_hi, a_lo = _split_bf16(c * jax.nn.sigmoid(c))
        w_hi, w_lo = _split_bf16(wa_ref[...])
        m_ref[...] = _dot(a_hi, w_hi) + _dot(a_lo, w_hi) + _dot(a_hi, w_lo) + ba_ref[...]

    w = wi_ref[...]
    wib_ref[...] = w.astype(BF)

    @pl.when(j == K_SEG)
    def _():
        wkt_ref[...] = w.T.astype(BF)

    @pl.when(j == 0)
    def _():
        s1 = jnp.sum(lq1_ref[...] * lk1_ref[...], axis=-1, keepdims=True)
        s2 = jnp.sum(lq2_ref[...] * lk2_ref[...], axis=-1, keepdims=True)
        lam_ref[...] = jnp.broadcast_to(jnp.exp(s1) - jnp.exp(s2) + LAMBDA_INIT, lam_ref.shape)
        wob_ref[...] = wo_ref[...].astype(BF)


def _prep_call(cond, w_ada, b_ada, lq1, lk1, lq2, lk2, w_in, w_out):
    vec = _const_spec((1, HEAD_DIM))
    mod_block = lambda j: (0, jnp.minimum(j, N_MOD - 1))
    return pl.pallas_call(
        _prep_kernel,
        grid=(N_SEG,),
        in_specs=[
            _const_spec((MOD_ROWS, D)),
            pl.BlockSpec((D, D), mod_block),
            pl.BlockSpec((1, D), mod_block),
            vec, vec, vec, vec,
            pl.BlockSpec((D, D), lambda j: (0, j)),
            _const_spec((D, D)),
        ],
        out_specs=[
            pl.BlockSpec((MOD_ROWS, D), mod_block),
            _const_spec((MOD_ROWS, LANES)),
            pl.BlockSpec((D, D), lambda j: (0, j)),
            _const_spec((D, D)),
            _const_spec((D, D)),
        ],
        out_shape=[
            jax.ShapeDtypeStruct((MOD_ROWS, N_MOD * D), F32),
            jax.ShapeDtypeStruct((MOD_ROWS, LANES), F32),
            jax.ShapeDtypeStruct((D, N_SEG * D), BF),
            jax.ShapeDtypeStruct((D, D), BF),
            jax.ShapeDtypeStruct((D, D), BF),
        ],
        compiler_params=_params(1),
        name="prep",
    )(cond, w_ada, b_ada, lq1, lk1, lq2, lk2, w_in, w_out)


def _head_norm(t, g, gsum, gbcast):
    ss = _dot((t * t).astype(BF), gsum)
    inv = lax.rsqrt(ss * (1.0 / HEAD_DIM) + EPS)
    hi, lo = _split_bf16(inv)
    bc = _dot(jnp.concatenate([hi, lo], axis=-1), gbcast)
    return t * bc * g


def _lane_tile(tab):
    return jnp.concatenate([tab] * (D // LANES), axis=1)


def _rope(t, cos, sin_signed):
    q = HEAD_DIM // 4
    lane = lax.broadcasted_iota(jnp.int32, (1, D), 1)
    first_half = (lane % (HEAD_DIM // 2)) < q
    fwd = pltpu.roll(t, D - q, axis=1)
    bwd = pltpu.roll(t, q, axis=1)
    return t * _lane_tile(cos) + jnp.where(first_half, fwd, bwd) * _lane_tile(sin_signed)


def _rope_t(t3, cos_t, sin_signed_t):
    q = HEAD_DIM // 4
    rot = jnp.concatenate([t3[:, q:2 * q], t3[:, 0:q], t3[:, 3 * q:4 * q], t3[:, 2 * q:3 * q]], axis=1)
    return t3 * cos_t[None] + rot * sin_signed_t[None]


def _inproj_kernel(rope, row_of_step, x_ref, m_ref, vecs_ref, w_ref, wkt_ref, kgt_ref, gsum_ref,
                   gbcast_ref, wsp_ref, bsp_ref, *rest):
    if rope:
        tab_ref, tab_t_ref, qst_out, kt_out, v_out = rest
    else:
        qst_out, kt_out, v_out = rest
    tm = x_ref.shape[0]
    m = m_ref[pl.ds(row_of_step(pl.program_id(0)), 1), :]
    shift1, scale1 = m[:, 0:D], m[:, D:2 * D]
    n1g, qg, lng, lnb = (vecs_ref[r:r + 1, :] for r in range(4))
    h = (_rms(x_ref[...], n1g) * (1.0 + scale1) + shift1).astype(BF)

    def seg(j):
        return _dot(h, w_ref[:, j * D:(j + 1) * D])

    q = _head_norm(seg(0), qg, gsum_ref[...], gbcast_ref[...])
    if rope:
        q = _rope(q, tab_ref[:, 0:LANES], tab_ref[:, LANES:2 * LANES])
    qst_out[:, 0:D] = q.astype(qst_out.dtype)

    k3 = _dot_nt(wkt_ref[...], h).reshape(D // HEAD_DIM, HEAD_DIM, tm)
    inv = lax.rsqrt(jnp.mean(k3 * k3, axis=1, keepdims=True) + EPS)
    k3 = k3 * inv * kgt_ref[...][None]
    if rope:
        k3 = _rope_t(k3, tab_t_ref[0:HEAD_DIM, :], tab_t_ref[HEAD_DIM:2 * HEAD_DIM, :])
    kt = k3.reshape(D, tm).astype(kt_out.dtype)
    width = kt_out.shape[2]
    for r in range(kt_out.shape[0]):
        kt_out[r] = kt[:, r * width:(r + 1) * width]

    v_out[...] = seg(2).astype(v_out.dtype)
    qst_out[:, D:2 * D] = jax.nn.sigmoid(seg(5)).astype(qst_out.dtype)

    zv = jax.nn.gelu(seg(4))
    mu = jnp.mean(zv, axis=-1, keepdims=True)
    zc = zv - mu
    var = jnp.mean(zc * zc, axis=-1, keepdims=True)
    zvn = (zc * lax.rsqrt(var + EPS) * lng + lnb).astype(BF)
    pre = jax.nn.gelu(seg(3)) * jax.nn.sigmoid(seg(6))
    for c in range(tm // CHUNK):
        rows = slice(c * CHUNK, (c + 1) * CHUNK)
        for g in range(N_GROUPS):
            cols = slice(g * LANES, (g + 1) * LANES)
            sp = _dot(wsp_ref[g], zvn[rows, cols]) + bsp_ref[:, cols]
            out_cols = slice(2 * D + g * LANES, 2 * D + (g + 1) * LANES)
            qst_out[rows, out_cols] = (pre[rows, cols] * sp).astype(qst_out.dtype)


INPROJ_ROWS = 512


def _inproj_call(x, m, row_of_req, rope_tabs, kv_dtype, consts, n_req, n_tok):
    t = x.shape[0]
    tm = INPROJ_ROWS
    per_req = max(1, n_tok // tm)
    per_tile = max(1, tm // n_tok)
    rope = rope_tabs is not None

    def row_of_step(i):
        return row_of_req(i * per_tile // per_req)

    tok = pl.BlockSpec((tm, D), lambda i: (i, 0))
    in_specs = [
        tok,
        _const_spec((MOD_ROWS, 6 * D)),
        _const_spec((MOD_ROWS, D)),
        pl.BlockSpec((D, N_SEG * D), lambda i: (0, 0), pipeline_mode=pl.Buffered(1)),
        _const_spec((D, D)),
        _const_spec((HEAD_DIM, tm)),
        _const_spec((D, LANES)), _const_spec((2 * LANES, D)),
        _const_spec((N_GROUPS, CHUNK, CHUNK)), _const_spec((CHUNK, D)),
    ]
    args = [x, m, consts["vecs_in"], consts["w_in"], consts["wkt"], consts["kgt"], consts["gsum"],
            consts["gbcast"], consts["wsp"], consts["bsp"]]
    if rope:
        in_specs += [pl.BlockSpec((tm, 2 * LANES), lambda i: (i % per_req, 0)),
                     pl.BlockSpec((2 * HEAD_DIM, tm), lambda i: (0, i % per_req))]
        args += list(rope_tabs)
    kt_spec = pl.BlockSpec((per_tile, D, tm // per_tile), lambda i: (i // per_req, 0, i % per_req))
    return pl.pallas_call(
        functools.partial(_inproj_kernel, rope, row_of_step),
        grid=(t // tm,),
        in_specs=in_specs,
        out_specs=[pl.BlockSpec((tm, 3 * D), lambda i: (i, 0)), kt_spec, tok],
        out_shape=[
            jax.ShapeDtypeStruct((t, 3 * D), BF),
            jax.ShapeDtypeStruct((n_req, D, n_tok), kv_dtype),
            jax.ShapeDtypeStruct((t, D), kv_dtype),
        ],
        compiler_params=_params(1),
        name="inproj_rope" if rope else "inproj",
    )(*args)


SCORE_GROUP_ELEMS = 1 << 22
ATTN_ROWS = 256


def _attn_kernel(has_ctx, heads_per_group, row_of_req, qst_ref, kt_ref, v_ref, *rest):
    if has_ctx:
        kct_ref, vc_ref, *rest = rest
    lam_ref, vecs_ref, x_ref, m_ref, wo_ref, wr_ref, x1_out, h2_out, lg_out, merged_s = rest
    tq = qst_ref.shape[0]
    n2g = vecs_ref[0:1, :]
    sub_g = vecs_ref[1:2, 0:V_HEAD_DIM]
    lam = lam_ref[0:1, 0:1]
    lane = lax.broadcasted_iota(jnp.int32, (1, LANES), 1)
    first = lane < HEAD_DIM
    zero = jnp.zeros((), BF)
    ones_col = jnp.where(lane == 0, 1.0, 0.0).astype(BF)

    def head_values(ref, h):
        v = ref[:, h * V_HEAD_DIM:(h + 1) * V_HEAD_DIM].astype(BF)
        return jnp.concatenate([v, jnp.broadcast_to(ones_col, (v.shape[0], LANES))], axis=1)

    def head_cols(h):
        return slice(h * LANES, (h + 1) * LANES)

    def head_scores(h):
        q = qst_ref[:, head_cols(h)]
        qz = jnp.concatenate([jnp.where(first, q, zero), jnp.where(first, zero, q)], axis=0)
        parts = [_dot(qz, kt_ref[0, head_cols(h), :].astype(BF))]
        if has_ctx:
            parts.append(_dot(qz, kct_ref[0, head_cols(h), :].astype(BF)))
        return parts

    def row_max(scores):
        return functools.reduce(jnp.maximum, [jnp.max(s, axis=-1, keepdims=True) for s in scores])

    def head_pv(h, scores, mx):
        vals = [head_values(v_ref, h)] + ([head_values(vc_ref, h)] if has_ctx else [])
        return functools.reduce(jnp.add, [_dot(jnp.exp2(s - mx).astype(BF), v) for s, v in zip(scores, vals)])

    def head_finish(h, ob):
        den = ob[:, V_HEAD_DIM:V_HEAD_DIM + 1]
        o = ob[:tq, :V_HEAD_DIM] * (1.0 / den[:tq]) - ob[tq:, :V_HEAD_DIM] * (lam / den[tq:])
        o = _rms(o, sub_g) * (1.0 - LAMBDA_INIT)
        cols = head_cols(h)
        sga = qst_ref[:, D + h * LANES:D + (h + 1) * LANES].astype(F32)
        tb = qst_ref[:, 2 * D + h * LANES:2 * D + (h + 1) * LANES].astype(F32)
        merged_s[:, cols] = (sga * o + tb).astype(BF)

    for g0 in range(0, N_HEADS, heads_per_group):
        group = range(g0, g0 + heads_per_group)
        scores = [head_scores(h) for h in group]
        maxes = [row_max(s) for s in scores]
        outs = [head_pv(h, s, mx) for h, s, mx in zip(group, scores, maxes)]
        for h, ob in zip(group, outs):
            head_finish(h, ob)

    m = m_ref[pl.ds(row_of_req(pl.program_id(0)), 1), :]
    gate1, shift2, scale2 = m[:, 2 * D:3 * D], m[:, 3 * D:4 * D], m[:, 4 * D:5 * D]
    x1 = x_ref[...] + gate1 * _dot(merged_s[...], wo_ref[...])
    x1_out[...] = x1
    h2 = _rms(x1, n2g) * (1.0 + scale2) + shift2
    h2_out[...] = h2.astype(BF)
    hi, lo = _split_bf16(h2)
    wr_hi, wr_lo = wr_ref[:, 0:LANES], wr_ref[:, LANES:2 * LANES]
    lg_out[...] = _dot(hi, wr_hi) + _dot(lo, wr_hi) + _dot(hi, wr_lo)


def _attn_call(qst, kt, v, ctx, lamv, x, m, row_of_req, consts, n_req, n_tok):
    tq = min(n_tok, ATTN_ROWS)
    nqb = n_tok // tq
    has_ctx = ctx is not None
    tok = pl.BlockSpec((tq, D), lambda b, i: (b * nqb + i, 0))
    in_specs = [pl.BlockSpec((tq, 3 * D), lambda b, i: (b * nqb + i, 0)),
                pl.BlockSpec((1, D, n_tok), lambda b, i: (b, 0, 0)),
                pl.BlockSpec((n_tok, D), lambda b, i: (b, 0))]
    args = [qst, kt, v]
    if has_ctx:
        n_ctx = ctx[0].shape[2]
        in_specs += [pl.BlockSpec((1, D, n_ctx), lambda b, i: (b, 0, 0)),
                     pl.BlockSpec((n_ctx, D), lambda b, i: (b, 0))]
        args += list(ctx)
    in_specs += [
        _const_spec((MOD_ROWS, LANES)), _const_spec((MOD_ROWS, D)),
        tok,
        _const_spec((MOD_ROWS, 6 * D)),
        _const_spec((D, D)),
        _const_spec((D, 2 * LANES)),
    ]
    args += [lamv, consts["vecs_attn"], x, m, consts["w_out"], consts["wr"]]
    t = n_req * n_tok
    n_keys = n_tok + (ctx[0].shape[2] if has_ctx else 0)
    heads_per_group = max(1, min(N_HEADS, SCORE_GROUP_ELEMS // (2 * tq * n_keys)))
    while N_HEADS % heads_per_group:
        heads_per_group -= 1
    return pl.pallas_call(
        functools.partial(_attn_kernel, has_ctx, heads_per_group, row_of_req),
        grid=(n_req, nqb),
        in_specs=in_specs,
        out_specs=[tok, tok, pl.BlockSpec((tq, LANES), lambda b, i: (b * nqb + i, 0))],
        out_shape=[
            jax.ShapeDtypeStruct((t, D), F32),
            jax.ShapeDtypeStruct((t, D), BF),
            jax.ShapeDtypeStruct((t, LANES), F32),
        ],
        scratch_shapes=[pltpu.VMEM((tq, D), BF)],
        compiler_params=_params(2),
        name="attn_ctx" if has_ctx else "attn",
    )(*args)


GATHER_ROWS = 512
KEY_BITS = 31
ROUTE_UNROLL_ELEMS = 1 << 22
ROUTE_STEP_TOKENS = 2048


def _route_kernel(cap, n_tok, unroll, has_dest, lg_ref, h2_ref, before_ref, *rest):
    xe_ref, gate_ref, slot_ref, aff_t_s, slot_t_s, p_s = rest[2:] if has_dest else rest
    reqs = range(lg_ref.shape[0] // n_tok)
    lane = lax.broadcasted_iota(jnp.int32, (1, LANES), 1)
    valid = lane < N_EXPERTS

    def rows(r):
        return slice(r * n_tok, (r + 1) * n_tok)

    def affinity(r):
        lg = jnp.where(valid, lg_ref[rows(r), :], -1e30)
        ex = jnp.where(valid, jnp.exp(lg - jnp.max(lg, axis=-1, keepdims=True)), 0.0)
        return ex / jnp.sum(ex, axis=-1, keepdims=True)

    def count(mask):
        return jnp.sum(jnp.where(mask, 1.0, 0.0), axis=0, keepdims=True)

    affs = [affinity(r) for r in reqs]
    kth_bits = [jnp.zeros((1, LANES), jnp.int32) for _ in reqs]
    for bit in range(KEY_BITS - 1, -1, -1):
        for r in reqs:
            cand = kth_bits[r] | (1 << bit)
            enough = count(affs[r] >= lax.bitcast_convert_type(cand, F32)) >= cap
            kth_bits[r] = jnp.where(enough, cand, kth_bits[r])

    before = before_ref[...]
    for r in reqs:
        aff = affs[r]
        kth = lax.bitcast_convert_type(kth_bits[r], F32)
        above = aff > kth
        tied = aff == kth
        need = cap - count(above)
        tied_before = _dot(before, jnp.where(tied, 1.0, 0.0).astype(BF))
        chosen = jnp.where(above, 1.0, jnp.where(tied, jnp.where(tied_before < need, 1.0, 0.0), 0.0))
        slot = _dot(before, chosen.astype(BF))
        slot = jnp.where(valid, jnp.where(chosen > 0.0, slot, float(cap)), float(cap))
        slot_ref[rows(r), :] = slot
        slot_t_s[r] = slot.T
        aff_t_s[r] = aff.T

    slot_ids = lax.broadcasted_iota(jnp.int32, (cap, 1), 0).astype(F32)

    def per_expert(e, carry):
        for r in reqs:
            hit = slot_t_s[r, pl.ds(e, 1), :] == slot_ids
            p_s[r, pl.ds(pl.multiple_of(e * cap, cap), cap), :] = jnp.where(hit, 1.0, 0.0).astype(BF)
            gate = jnp.sum(jnp.where(hit, aff_t_s[r, pl.ds(e, 1), :], 0.0), axis=-1, keepdims=True)
            gate_ref[e, r * cap:(r + 1) * cap, :] = gate
        return carry

    lax.fori_loop(0, N_EXPERTS, per_expert, 0, unroll=unroll)
    n_rows = min(GATHER_ROWS, N_EXPERTS * cap)
    e_per = n_rows // cap
    for r in reqs:
        h2 = h2_ref[rows(r), :]
        for t in range(N_EXPERTS * cap // n_rows):
            xe = _dot(p_s[r, t * n_rows:(t + 1) * n_rows, :], h2).astype(BF)
            xe_ref[t * e_per:(t + 1) * e_per, r * cap:(r + 1) * cap, :] = xe.reshape(e_per, cap, D)


def _capacity(n_tok):
    return max(1, CAPACITY_FACTOR * n_tok // N_EXPERTS)


def _route_call(lg, h2, n_req, n_tok, dest, row_offset, total_rows):
    cap = _capacity(n_tok)
    per_step = max(1, min(n_req, ROUTE_STEP_TOKENS // n_tok))
    block0 = row_offset // (per_step * cap)
    before = jnp.asarray((np.arange(n_tok)[None, :] < np.arange(n_tok)[:, None]).astype(BF))
    unroll = max(1, min(N_EXPERTS, ROUTE_UNROLL_ELEMS // (per_step * cap * n_tok)))
    in_specs = [
        pl.BlockSpec((per_step * n_tok, LANES), lambda b: (b, 0)),
        pl.BlockSpec((per_step * n_tok, D), lambda b: (b, 0)),
        _const_spec((n_tok, n_tok)),
    ]
    args = [lg, h2, before]
    aliases = {}
    if dest is not None:
        in_specs += [pl.BlockSpec(memory_space=pl.ANY)] * 2
        aliases = {len(args): 0, len(args) + 1: 1}
        args += list(dest)
    return pl.pallas_call(
        functools.partial(_route_kernel, cap, n_tok, unroll, dest is not None),
        grid=(n_req // per_step,),
        in_specs=in_specs,
        out_specs=[
            pl.BlockSpec((N_EXPERTS, per_step * cap, D), lambda b: (0, block0 + b, 0)),
            pl.BlockSpec((N_EXPERTS, per_step * cap, 1), lambda b: (0, block0 + b, 0)),
            pl.BlockSpec((per_step * n_tok, LANES), lambda b: (b, 0)),
        ],
        out_shape=[
            jax.ShapeDtypeStruct((N_EXPERTS, total_rows, D), BF),
            jax.ShapeDtypeStruct((N_EXPERTS, total_rows, 1), F32),
            jax.ShapeDtypeStruct((n_req * n_tok, LANES), F32),
        ],
        scratch_shapes=[
            pltpu.VMEM((per_step, LANES, n_tok), F32),
            pltpu.VMEM((per_step, LANES, n_tok), F32),
            pltpu.VMEM((per_step, N_EXPERTS * cap, n_tok), BF),
        ],
        input_output_aliases=aliases,
        compiler_params=_params(1),
        name="route",
    )(*args)


EXPERT_BLOCK = 1024
EXPERT_SUB = 256


def _experts_kernel(x_ref, g_ref, wg_ref, wu_ref, wd_ref, y_ref, acc_s):
    f = pl.program_id(1)

    @pl.when(f == 0)
    def _():
        acc_s[...] = jnp.zeros(acc_s.shape, F32)

    x = x_ref[0]
    for c in range(wg_ref.shape[2] // EXPERT_SUB):
        cs = slice(c * EXPERT_SUB, (c + 1) * EXPERT_SUB)
        gate = _dot(x, wg_ref[0, :, cs].astype(BF))
        up = _dot(x, wu_ref[0, :, cs].astype(BF))
        hid = (gate * jax.nn.sigmoid(gate) * up).astype(BF)
        acc_s[...] += _dot(hid, wd_ref[0, cs, :].astype(BF))

    @pl.when(f == pl.num_programs(1) - 1)
    def _():
        y_ref[0] = (acc_s[...] * g_ref[0]).astype(y_ref.dtype)


def _experts_call(x, gates, wg, wu, wd):
    rows = x.shape[1]
    tf = EXPERT_BLOCK
    x_spec = pl.BlockSpec((1, rows, D), lambda e, f: (e, 0, 0))
    return pl.pallas_call(
        _experts_kernel,
        grid=(N_EXPERTS, D_EXPERT // tf),
        in_specs=[
            x_spec,
            pl.BlockSpec((1, rows, 1), lambda e, f: (e, 0, 0)),
            pl.BlockSpec((1, D, tf), lambda e, f: (e, 0, f)),
            pl.BlockSpec((1, D, tf), lambda e, f: (e, 0, f)),
            pl.BlockSpec((1, tf, D), lambda e, f: (e, f, 0)),
        ],
        out_specs=x_spec,
        out_shape=jax.ShapeDtypeStruct((N_EXPERTS, rows, D), BF),
        scratch_shapes=[pltpu.VMEM((rows, D), F32)],
        compiler_params=_params(2),
        name="experts",
    )(x, gates, wg, wu, wd)


def _scatter_kernel(cap, row_of_req, slot_ref, y_ref, x1_ref, m_ref, expand_ref, out_ref):
    m = m_ref[pl.ds(row_of_req(pl.program_id(0)), 1), :]
    gate2 = m[:, 5 * D:6 * D]
    slot_wide = _dot(slot_ref[...].astype(BF), expand_ref[...])
    slot_ids = (lax.broadcasted_iota(jnp.int32, (1, N_EXPERTS * cap), 1) % cap).astype(F32)
    onehot = jnp.where(slot_wide == slot_ids, 1.0, 0.0).astype(BF)
    y = y_ref[...].reshape(N_EXPERTS * cap, D)
    out_ref[...] = x1_ref[...] + gate2 * _dot(onehot, y)


def _scatter_call(slots, y, row_offset, x1, m, row_of_req, n_req, n_tok):
    cap = _capacity(n_tok)
    block0 = row_offset // cap
    expand = jnp.asarray((np.arange(LANES)[:, None] == np.arange(N_EXPERTS * cap)[None, :] // cap).astype(BF))
    tok = pl.BlockSpec((n_tok, D), lambda b: (b, 0))
    return pl.pallas_call(
        functools.partial(_scatter_kernel, cap, row_of_req),
        grid=(n_req,),
        in_specs=[
            pl.BlockSpec((n_tok, LANES), lambda b: (b, 0)),
            pl.BlockSpec((N_EXPERTS, cap, D), lambda b: (0, block0 + b, 0)),
            tok,
            _const_spec((MOD_ROWS, 6 * D)),
            _const_spec((LANES, N_EXPERTS * cap)),
        ],
        out_specs=tok,
        out_shape=jax.ShapeDtypeStruct((n_req * n_tok, D), F32),
        compiler_params=_params(1),
        name="scatter",
    )(slots, y, x1, m, expand)


def _rope_tables(n_tokens):
    rows = n_tokens // GRID_W
    row = np.broadcast_to(np.arange(rows, dtype=np.float32)[:, None], (rows, GRID_W)).reshape(-1)
    col = np.broadcast_to(np.arange(GRID_W, dtype=np.float32)[None, :], (rows, GRID_W)).reshape(-1)
    half = HEAD_DIM // 4
    inv_freq = (np.float32(ROPE_BASE) ** (-np.arange(half, dtype=np.float32) / np.float32(half))).astype(np.float32)
    ar = row[:, None] * inv_freq
    ac = col[:, None] * inv_freq
    ang = np.concatenate([ar, ar, ac, ac], axis=-1).astype(np.float64)
    cos, sin = np.cos(ang).astype(np.float32), np.sin(ang).astype(np.float32)
    first_half = (np.arange(HEAD_DIM) % (HEAD_DIM // 2)) < (HEAD_DIM // 4)
    sin_signed = np.where(first_half[None, :], -sin, sin)
    reps = LANES // HEAD_DIM
    token_major = np.concatenate([np.tile(cos, (1, reps)), np.tile(sin_signed, (1, reps))], axis=1)
    transposed = np.concatenate([cos.T, sin_signed.T], axis=0)
    return jnp.asarray(token_major), jnp.asarray(transposed)


def kernel(x_prompt, x_sample, cache_k, cache_v, c, c_ctx, w_ada, b_ada, norm1_g, norm2_g, w_in, q_norm_g, k_norm_g, lambda_q1, lambda_k1, lambda_q2, lambda_k2, subln_g, gmlp_ln_g, gmlp_ln_b, w_spatial, b_spatial, w_out, w_router, w_gate_e, w_up_e, w_down_e):
    n_p, t_p = x_prompt.shape[0], x_prompt.shape[1]
    n_s, t_s = x_sample.shape[0], x_sample.shape[1]
    n_ctx = cache_k.shape[2]
    l = 0

    cond = jnp.concatenate([c_ctx[None, :], c, jnp.zeros((MOD_ROWS - 1 - n_s, D), F32)], axis=0)
    m, lamv, w_in_bf, wkt, w_out_bf = _prep_call(
        cond, w_ada[l], b_ada[l][None, :], lambda_q1[l][None, :], lambda_k1[l][None, :],
        lambda_q2[l][None, :], lambda_k2[l][None, :], w_in[l], w_out[l])

    gsum_np = (np.arange(D)[:, None] // HEAD_DIM == np.arange(LANES)[None, :]).astype(BF)
    gsum = jnp.asarray(gsum_np)
    wr = jnp.pad(w_router[l], ((0, 0), (0, LANES - N_EXPERTS)))
    wr_hi = wr.astype(BF)
    qg = jnp.tile(q_norm_g[l] * (HEAD_DIM ** -0.5 * math.log2(math.e)), D // HEAD_DIM)

    def rows_of(*vectors):
        rows = [jnp.tile(vec, D // vec.shape[0])[None, :] for vec in vectors]
        return jnp.concatenate(rows + [jnp.zeros((MOD_ROWS - len(rows), D), F32)], axis=0)

    consts = {
        "vecs_in": rows_of(norm1_g[l], qg, gmlp_ln_g[l], gmlp_ln_b[l]),
        "vecs_attn": rows_of(norm2_g[l], subln_g[l]),
        "w_in": w_in_bf,
        "wkt": wkt,
        "kgt": jnp.broadcast_to(k_norm_g[l][:, None], (HEAD_DIM, INPROJ_ROWS)),
        "gsum": gsum,
        "gbcast": jnp.asarray(np.concatenate([gsum_np.T, gsum_np.T], axis=0)),
        "wsp": w_spatial[l].astype(BF),
        "bsp": jnp.repeat(b_spatial[l].T, D // N_GROUPS, axis=1),
        "w_out": w_out_bf,
        "wr": jnp.concatenate([wr_hi, (wr - wr_hi.astype(F32)).astype(BF)], axis=1),
    }

    passes = (
        (x_prompt, n_p, t_p, lambda b: 0, False),
        (x_sample, n_s, t_s, lambda b: 1 + b, True),
    )
    outs = []
    kv_out = None
    dest, row_offset = None, 0
    total_rows = n_p * _capacity(t_p) + n_s * _capacity(t_s)
    for x, n_req, n_tok, row_of_req, is_sample in passes:
        xf = x.reshape(n_req * n_tok, D)
        tabs = _rope_tables(n_tok) if is_sample else None
        qst, kt, v = _inproj_call(xf, m, row_of_req, tabs, BF if is_sample else F32, consts, n_req, n_tok)
        if is_sample:
            ctx = (jnp.transpose(cache_k[:, l], (0, 2, 3, 4, 1)).reshape(n_req, D, n_ctx),
                   cache_v[:, l].reshape(n_req * n_ctx, D))
        else:
            ctx = None
            kv_out = (kt, v)
        x1, h2, lg = _attn_call(qst, kt, v, ctx, lamv, xf, m, row_of_req, consts, n_req, n_tok)
        xe, gates, slots = _route_call(lg, h2, n_req, n_tok, dest, row_offset, total_rows)
        dest = (xe, gates)
        outs.append((x1, slots, row_offset, row_of_req, n_req, n_tok))
        row_offset += n_req * _capacity(n_tok)

    y = _experts_call(xe, gates, w_gate_e[l], w_up_e[l], w_down_e[l])
    y_prompt, y_sample = (
        _scatter_call(slots, y, offset, x1, m, row_of_req, n_req, n_tok)
        for x1, slots, offset, row_of_req, n_req, n_tok in outs)
    y_prompt = y_prompt.reshape(x_prompt.shape)
    y_sample = y_sample.reshape(x_sample.shape)
    new_k = jnp.transpose(kv_out[0].reshape(n_p, N_HEADS, 2, HEAD_DIM, t_p), (0, 4, 1, 2, 3))
    new_k = new_k.reshape(n_p, 1, t_p, N_HEADS, 2, HEAD_DIM)
    new_v = kv_out[1].reshape(n_p, 1, t_p, N_HEADS, V_HEAD_DIM)
    return (y_prompt, y_sample, new_k, new_v)
```

```python
import functools
import math

import jax
import jax.numpy as jnp
import numpy as np
from jax import lax
from jax.experimental import pallas as pl
from jax.experimental.pallas import tpu as pltpu

D = 1024
N_HEADS = 8
HEAD_DIM = 64
V_HEAD_DIM = 128
GRID_W = 64
ROPE_BASE = 10000.0
CHUNK = 128
N_GROUPS = 8
N_EXPERTS = 16
CAPACITY_FACTOR = 2
D_EXPERT = 2048
N_SEG = 7
K_SEG = 1
EPS = 1e-6
LAMBDA_INIT = 0.8 - 0.6 * math.exp(-0.3 * 0)

LANES = 128
MOD_ROWS = 8
VMEM_LIMIT = 56 * 1024 * 1024
TM = 256

BF = jnp.bfloat16
F32 = jnp.float32


def _dot(a, b):
    return jnp.dot(a, b, preferred_element_type=F32)


def _dot_nt(a, b):
    return lax.dot_general(a, b, (((1,), (1,)), ((), ())), preferred_element_type=F32)


def _split_bf16(x):
    hi = x.astype(BF)
    lo = (x - hi.astype(F32)).astype(BF)
    return hi, lo


def _rms(x, g):
    return x * lax.rsqrt(jnp.mean(x * x, axis=-1, keepdims=True) + EPS) * g


def _params(n_grid_dims):
    return pltpu.CompilerParams(
        dimension_semantics=("arbitrary",) * n_grid_dims, vmem_limit_bytes=VMEM_LIMIT)


def _const_spec(shape):
    nd = len(shape)
    return pl.BlockSpec(shape, lambda *_: (0,) * nd)


N_MOD = 6


def _prep_kernel(cctx_ref, c_ref, wa_ref, ba_ref, lq1_ref, lk1_ref, lq2_ref, lk2_ref, wi_ref, wo_ref, wsp_ref,
                 m_ref, lam_ref, wib_ref, wkt_ref, wob_ref, wsp2_ref, cond_s):
    j = pl.program_id(0)
    n_lat = c_ref.shape[0]

    @pl.when(j == 0)
    def _():
        cond_s[...] = jnp.zeros(cond_s.shape, F32)
        cond_s[0:1, :] = cctx_ref[...]
        cond_s[1:1 + n_lat, :] = c_ref[...]
        s1 = jnp.sum(lq1_ref[...] * lk1_ref[...], axis=-1, keepdims=True)
        s2 = jnp.sum(lq2_ref[...] * lk2_ref[...], axis=-1, keepdims=True)
        lam_ref[...] = jnp.broadcast_to(jnp.exp(s1) - jnp.exp(s2) + LAMBDA_INIT, lam_ref.shape)
        wob_ref[...] = wo_ref[...].astype(BF)
        for p in range(N_GROUPS // 2):
            wsp2_ref[p] = jnp.concatenate([wsp_ref[2 * p], wsp_ref[2 * p + 1]], axis=1).astype(BF)

    @pl.when(j < N_MOD)
    def _():
        c = cond_s[...]
        a_hi, a_lo = _split_bf16(c * jax.nn.sigmoid(c))
        w_hi, w_lo = _split_bf16(wa_ref[...])
        m_ref[...] = _dot(a_hi, w_hi) + _dot(a_lo, w_hi) + _dot(a_hi, w_lo) + ba_ref[...]

    w = wi_ref[...]
    wib_ref[...] = w.astype(BF)

    @pl.when(j == K_SEG)
    def _():
        wkt_ref[...] = w.T.astype(BF)


def _prep_call(c_ctx, c, w_ada, b_ada, lq1, lk1, lq2, lk2, w_in, w_out, w_spatial):
    vec = _const_spec((1, HEAD_DIM))
    mod_block = lambda j: (0, jnp.minimum(j, N_MOD - 1))
    pairs = (N_GROUPS // 2, CHUNK, 2 * CHUNK)
    return pl.pallas_call(
        _prep_kernel,
        grid=(N_SEG,),
        in_specs=[
            _const_spec((1, D)), _const_spec(c.shape),
            pl.BlockSpec((D, D), mod_block),
            pl.BlockSpec((1, D), mod_block),
            vec, vec, vec, vec,
            pl.BlockSpec((D, D), lambda j: (0, j)),
            _const_spec((D, D)),
            _const_spec((N_GROUPS, CHUNK, CHUNK)),
        ],
        out_specs=[
            pl.BlockSpec((MOD_ROWS, D), mod_block),
            _const_spec((MOD_ROWS, LANES)),
            pl.BlockSpec((D, D), lambda j: (0, j)),
            _const_spec((D, D)),
            _const_spec((D, D)),
            _const_spec(pairs),
        ],
        out_shape=[
            jax.ShapeDtypeStruct((MOD_ROWS, N_MOD * D), F32),
            jax.ShapeDtypeStruct((MOD_ROWS, LANES), F32),
            jax.ShapeDtypeStruct((D, N_SEG * D), BF),
            jax.ShapeDtypeStruct((D, D), BF),
            jax.ShapeDtypeStruct((D, D), BF),
            jax.ShapeDtypeStruct(pairs, BF),
        ],
        scratch_shapes=[pltpu.VMEM((MOD_ROWS, D), F32)],
        compiler_params=_params(1),
        name="prep",
    )(c_ctx, c, w_ada, b_ada, lq1, lk1, lq2, lk2, w_in, w_out, w_spatial)


def _head_norm(t, g, gsum, gbcast):
    ss = _dot((t * t).astype(BF), gsum)
    inv = lax.rsqrt(ss * (1.0 / HEAD_DIM) + EPS)
    hi, lo = _split_bf16(inv)
    bc = _dot(jnp.concatenate([hi, lo], axis=-1), gbcast)
    return t * bc * g


def _lane_tile(tab):
    return jnp.concatenate([tab] * (D // LANES), axis=1)


def _rope(t, cos, sin_signed):
    q = HEAD_DIM // 4
    lane = lax.broadcasted_iota(jnp.int32, (1, D), 1)
    first_half = (lane % (HEAD_DIM // 2)) < q
    fwd = pltpu.roll(t, D - q, axis=1)
    bwd = pltpu.roll(t, q, axis=1)
    return t * _lane_tile(cos) + jnp.where(first_half, fwd, bwd) * _lane_tile(sin_signed)


def _rope_t(t3, cos_t, sin_signed_t):
    q = HEAD_DIM // 4
    rot = jnp.concatenate([t3[:, q:2 * q], t3[:, 0:q], t3[:, 3 * q:4 * q], t3[:, 2 * q:3 * q]], axis=1)
    return t3 * cos_t[None] + rot * sin_signed_t[None]


def _inproj_kernel(rope, row_of_step, x_ref, m_ref, vecs_ref, w_ref, wkt_ref, kgt_ref, gsum_ref,
                   gbcast_ref, wsp_ref, bsp_ref, *rest):
    if rope:
        tab_ref, tab_t_ref, qst_out, kt_out, v_out = rest
    else:
        qst_out, kt_out, v_out = rest
    tm = x_ref.shape[0]
    m = m_ref[pl.ds(row_of_step(pl.program_id(0)), 1), :]
    shift1, scale1 = m[:, 0:D], m[:, D:2 * D]
    n1g, qg, lng, lnb = (vecs_ref[r:r + 1, :] for r in range(4))
    h = (_rms(x_ref[...], n1g) * (1.0 + scale1) + shift1).astype(BF)

    def seg(j):
        return _dot(h, w_ref[:, j * D:(j + 1) * D])

    q = _head_norm(seg(0), qg, gsum_ref[...], gbcast_ref[...])
    if rope:
        q = _rope(q, tab_ref[:, 0:LANES], tab_ref[:, LANES:2 * LANES])
    qst_out[:, 0:D] = q.astype(qst_out.dtype)

    k3 = _dot_nt(wkt_ref[...], h).reshape(D // HEAD_DIM, HEAD_DIM, tm)
    inv = lax.rsqrt(jnp.mean(k3 * k3, axis=1, keepdims=True) + EPS)
    k3 = k3 * inv * kgt_ref[...][None]
    if rope:
        k3 = _rope_t(k3, tab_t_ref[0:HEAD_DIM, :], tab_t_ref[HEAD_DIM:2 * HEAD_DIM, :])
    kt = k3.reshape(D, tm).astype(kt_out.dtype)
    width = kt_out.shape[2]
    for r in range(kt_out.shape[0]):
        kt_out[r] = kt[:, r * width:(r + 1) * width]

    v_out[...] = seg(2).astype(v_out.dtype)
    qst_out[:, D:2 * D] = jax.nn.sigmoid(seg(5)).astype(qst_out.dtype)

    zv = jax.nn.gelu(seg(4))
    mu = jnp.mean(zv, axis=-1, keepdims=True)
    zc = zv - mu
    var = jnp.mean(zc * zc, axis=-1, keepdims=True)
    zvn = (zc * lax.rsqrt(var + EPS) * lng + lnb).astype(BF)
    pre = jax.nn.gelu(seg(3)) * jax.nn.sigmoid(seg(6))
    blank = jnp.zeros((CHUNK, LANES), BF)
    for c in range(tm // CHUNK):
        rows = slice(c * CHUNK, (c + 1) * CHUNK)
        for p in range(N_GROUPS // 2):
            cols = slice(2 * p * LANES, (2 * p + 2) * LANES)
            z = zvn[rows, cols]
            z_diag = jnp.concatenate([jnp.concatenate([z[:, :LANES], blank], axis=1),
                                      jnp.concatenate([blank, z[:, LANES:]], axis=1)], axis=0)
            sp = _dot(wsp_ref[p], z_diag) + bsp_ref[:, cols]
            out_cols = slice(2 * D + 2 * p * LANES, 2 * D + (2 * p + 2) * LANES)
            qst_out[rows, out_cols] = (pre[rows, cols] * sp).astype(qst_out.dtype)


INPROJ_ROWS = 512


def _inproj_call(x, m, row_of_req, rope_tabs, kv_dtype, consts, n_req, n_tok):
    t = x.shape[0]
    tm = INPROJ_ROWS
    per_req = max(1, n_tok // tm)
    per_tile = max(1, tm // n_tok)
    rope = rope_tabs is not None

    def row_of_step(i):
        return row_of_req(i * per_tile // per_req)

    tok = pl.BlockSpec((tm, D), lambda i: (i, 0))
    in_specs = [
        tok,
        _const_spec((MOD_ROWS, 6 * D)),
        _const_spec((MOD_ROWS, D)),
        pl.BlockSpec((D, N_SEG * D), lambda i: (0, 0), pipeline_mode=pl.Buffered(1)),
        _const_spec((D, D)),
        _const_spec((HEAD_DIM, tm)),
        _const_spec((D, LANES)), _const_spec((2 * LANES, D)),
        _const_spec((N_GROUPS // 2, CHUNK, 2 * CHUNK)), _const_spec((CHUNK, D)),
    ]
    args = [x, m, consts["vecs_in"], consts["w_in"], consts["wkt"], consts["kgt"], consts["gsum"],
            consts["gbcast"], consts["wsp"], consts["bsp"]]
    if rope:
        in_specs += [pl.BlockSpec((tm, 2 * LANES), lambda i: (i % per_req, 0)),
                     pl.BlockSpec((2 * HEAD_DIM, tm), lambda i: (0, i % per_req))]
        args += list(rope_tabs)
    kt_spec = pl.BlockSpec((per_tile, D, tm // per_tile), lambda i: (i // per_req, 0, i % per_req))
    return pl.pallas_call(
        functools.partial(_inproj_kernel, rope, row_of_step),
        grid=(t // tm,),
        in_specs=in_specs,
        out_specs=[pl.BlockSpec((tm, 3 * D), lambda i: (i, 0)), kt_spec, tok],
        out_shape=[
            jax.ShapeDtypeStruct((t, 3 * D), BF),
            jax.ShapeDtypeStruct((n_req, D, n_tok), kv_dtype),
            jax.ShapeDtypeStruct((t, D), kv_dtype),
        ],
        compiler_params=_params(1),
        name="inproj_rope" if rope else "inproj",
    )(*args)


SCORE_GROUP_ELEMS = 1 << 22
ATTN_ROWS = 256
ATTN_PACK_ROWS = 256


def _attn_kernel(has_ctx, units_per_group, row_of_req, qst_ref, kt_ref, v_ref, *rest):
    if has_ctx:
        kct_ref, vc_ref, *rest = rest
    lam_ref, vecs_ref, x_ref, m_ref, wo_ref, wr_ref, x1_out, h2_out, lg_out, merged_s = rest
    n_reqs = kt_ref.shape[0]
    tq = qst_ref.shape[0] // n_reqs
    n_own = v_ref.shape[0] // n_reqs
    n2g = vecs_ref[0:1, :]
    sub_g = vecs_ref[1:2, 0:V_HEAD_DIM]
    lam = lam_ref[0:1, 0:1]
    lane = lax.broadcasted_iota(jnp.int32, (1, LANES), 1)
    first = lane < HEAD_DIM
    zero = jnp.zeros((), BF)
    ones_col = jnp.where(lane == 0, 1.0, 0.0).astype(BF)

    def values(ref, rows, h):
        v = ref[rows, h * V_HEAD_DIM:(h + 1) * V_HEAD_DIM].astype(BF)
        return jnp.concatenate([v, jnp.broadcast_to(ones_col, (v.shape[0], LANES))], axis=1)

    def head_cols(h):
        return slice(h * LANES, (h + 1) * LANES)

    def q_rows(r):
        return slice(r * tq, (r + 1) * tq)

    def unit_scores(r, h):
        q = qst_ref[q_rows(r), head_cols(h)]
        qz = jnp.concatenate([jnp.where(first, q, zero), jnp.where(first, zero, q)], axis=0)
        parts = [_dot(qz, kt_ref[r, head_cols(h), :].astype(BF))]
        if has_ctx:
            parts.append(_dot(qz, kct_ref[r, head_cols(h), :].astype(BF)))
        return parts

    def row_max(scores):
        return functools.reduce(jnp.maximum, [jnp.max(s, axis=-1, keepdims=True) for s in scores])

    def unit_pv(r, h, scores, mx):
        vals = [values(v_ref, slice(r * n_own, (r + 1) * n_own), h)]
        if has_ctx:
            n_ctx = vc_ref.shape[0] // n_reqs
            vals.append(values(vc_ref, slice(r * n_ctx, (r + 1) * n_ctx), h))
        return functools.reduce(jnp.add, [_dot(jnp.exp2(s - mx).astype(BF), v) for s, v in zip(scores, vals)])

    def unit_finish(r, h, ob):
        den = ob[:, V_HEAD_DIM:V_HEAD_DIM + 1]
        o = ob[:tq, :V_HEAD_DIM] * (1.0 / den[:tq]) - ob[tq:, :V_HEAD_DIM] * (lam / den[tq:])
        o = _rms(o, sub_g) * (1.0 - LAMBDA_INIT)
        sga = qst_ref[q_rows(r), D + h * LANES:D + (h + 1) * LANES].astype(F32)
        tb = qst_ref[q_rows(r), 2 * D + h * LANES:2 * D + (h + 1) * LANES].astype(F32)
        merged_s[q_rows(r), head_cols(h)] = (sga * o + tb).astype(BF)

    units = [(r, h) for r in range(n_reqs) for h in range(N_HEADS)]
    for g0 in range(0, len(units), units_per_group):
        group = units[g0:g0 + units_per_group]
        scores = [unit_scores(r, h) for r, h in group]
        maxes = [row_max(s) for s in scores]
        outs = [unit_pv(r, h, s, mx) for (r, h), s, mx in zip(group, scores, maxes)]
        for (r, h), ob in zip(group, outs):
            unit_finish(r, h, ob)

    mix = _dot(merged_s[...], wo_ref[...])
    h2_parts = []
    for r in range(n_reqs):
        m = m_ref[pl.ds(row_of_req(pl.program_id(0) * n_reqs + r), 1), :]
        gate1, shift2, scale2 = m[:, 2 * D:3 * D], m[:, 3 * D:4 * D], m[:, 4 * D:5 * D]
        x1 = x_ref[q_rows(r), :] + gate1 * mix[q_rows(r), :]
        x1_out[q_rows(r), :] = x1
        h2_parts.append(_rms(x1, n2g) * (1.0 + scale2) + shift2)
    h2 = h2_parts[0] if n_reqs == 1 else jnp.concatenate(h2_parts, axis=0)
    h2_out[...] = h2.astype(BF)
    hi, lo = _split_bf16(h2)
    wr_hi, wr_lo = wr_ref[:, 0:LANES], wr_ref[:, LANES:2 * LANES]
    lg_out[...] = _dot(hi, wr_hi) + _dot(lo, wr_hi) + _dot(hi, wr_lo)


def _attn_call(qst, kt, v, ctx, lamv, x, m, row_of_req, consts, n_req, n_tok):
    tq = min(n_tok, ATTN_ROWS)
    nqb = n_tok // tq
    pack = max(1, ATTN_PACK_ROWS // n_tok) if nqb == 1 else 1
    rows = pack * tq
    has_ctx = ctx is not None
    tok = pl.BlockSpec((rows, D), lambda b, i: (b * nqb + i, 0))
    in_specs = [pl.BlockSpec((rows, 3 * D), lambda b, i: (b * nqb + i, 0)),
                pl.BlockSpec((pack, D, n_tok), lambda b, i: (b, 0, 0)),
                pl.BlockSpec((pack * n_tok, D), lambda b, i: (b, 0))]
    args = [qst, kt, v]
    if has_ctx:
        n_ctx = ctx[0].shape[2]
        in_specs += [pl.BlockSpec((pack, D, n_ctx), lambda b, i: (b, 0, 0)),
                     pl.BlockSpec((pack * n_ctx, D), lambda b, i: (b, 0))]
        args += list(ctx)
    in_specs += [
        _const_spec((MOD_ROWS, LANES)), _const_spec((MOD_ROWS, D)),
        tok,
        _const_spec((MOD_ROWS, 6 * D)),
        _const_spec((D, D)),
        _const_spec((D, 2 * LANES)),
    ]
    args += [lamv, consts["vecs_attn"], x, m, consts["w_out"], consts["wr"]]
    t = n_req * n_tok
    n_keys = n_tok + (ctx[0].shape[2] if has_ctx else 0)
    n_units = pack * N_HEADS
    units_per_group = max(1, min(n_units, SCORE_GROUP_ELEMS // (2 * tq * n_keys)))
    while n_units % units_per_group:
        units_per_group -= 1
    return pl.pallas_call(
        functools.partial(_attn_kernel, has_ctx, units_per_group, row_of_req),
        grid=(n_req // pack, nqb),
        in_specs=in_specs,
        out_specs=[tok, tok, pl.BlockSpec((rows, LANES), lambda b, i: (b * nqb + i, 0))],
        out_shape=[
            jax.ShapeDtypeStruct((t, D), F32),
            jax.ShapeDtypeStruct((t, D), BF),
            jax.ShapeDtypeStruct((t, LANES), F32),
        ],
        scratch_shapes=[pltpu.VMEM((rows, D), BF)],
        compiler_params=_params(2),
        name="attn_ctx" if has_ctx else "attn",
    )(*args)


GATHER_ROWS = 512
KEY_BITS = 31
ROUTE_UNROLL_ELEMS = 1 << 22
ROUTE_STEP_TOKENS = 2048


def _route_kernel(cap, n_tok, unroll, lg_ref, h2_ref, before_ref, xe_ref, gate_ref, slot_ref,
                  aff_t_s, slot_t_s, p_s):
    reqs = range(lg_ref.shape[0] // n_tok)
    lane = lax.broadcasted_iota(jnp.int32, (1, LANES), 1)
    valid = lane < N_EXPERTS

    def rows(r):
        return slice(r * n_tok, (r + 1) * n_tok)

    def affinity(r):
        lg = jnp.where(valid, lg_ref[rows(r), :], -1e30)
        ex = jnp.where(valid, jnp.exp(lg - jnp.max(lg, axis=-1, keepdims=True)), 0.0)
        return ex / jnp.sum(ex, axis=-1, keepdims=True)

    def count(mask):
        return jnp.sum(jnp.where(mask, 1.0, 0.0), axis=0, keepdims=True)

    affs = [affinity(r) for r in reqs]
    kth_bits = [jnp.zeros((1, LANES), jnp.int32) for _ in reqs]
    for bit in range(KEY_BITS - 1, -1, -1):
        for r in reqs:
            cand = kth_bits[r] | (1 << bit)
            enough = count(affs[r] >= lax.bitcast_convert_type(cand, F32)) >= cap
            kth_bits[r] = jnp.where(enough, cand, kth_bits[r])

    before = before_ref[...]
    for r in reqs:
        aff = affs[r]
        kth = lax.bitcast_convert_type(kth_bits[r], F32)
        above = aff > kth
        tied = aff == kth
        need = cap - count(above)
        tied_before = _dot(before, jnp.where(tied, 1.0, 0.0).astype(BF))
        chosen = jnp.where(above, 1.0, jnp.where(tied, jnp.where(tied_before < need, 1.0, 0.0), 0.0))
        slot = _dot(before, chosen.astype(BF))
        slot = jnp.where(valid, jnp.where(chosen > 0.0, slot, float(cap)), float(cap))
        slot_ref[rows(r), :] = slot
        slot_t_s[r] = slot.T
        aff_t_s[r] = aff.T

    slot_ids = lax.broadcasted_iota(jnp.int32, (cap, 1), 0).astype(F32)

    def per_expert(e, carry):
        for r in reqs:
            hit = slot_t_s[r, pl.ds(e, 1), :] == slot_ids
            p_s[r, pl.ds(pl.multiple_of(e * cap, cap), cap), :] = jnp.where(hit, 1.0, 0.0).astype(BF)
            gate = jnp.sum(jnp.where(hit, aff_t_s[r, pl.ds(e, 1), :], 0.0), axis=-1, keepdims=True)
            gate_ref[e, r * cap:(r + 1) * cap, :] = gate
        return carry

    lax.fori_loop(0, N_EXPERTS, per_expert, 0, unroll=unroll)
    n_rows = min(GATHER_ROWS, N_EXPERTS * cap)
    e_per = n_rows // cap
    for r in reqs:
        h2 = h2_ref[rows(r), :]
        for t in range(N_EXPERTS * cap // n_rows):
            xe = _dot(p_s[r, t * n_rows:(t + 1) * n_rows, :], h2).astype(BF)
            xe_ref[t * e_per:(t + 1) * e_per, r * cap:(r + 1) * cap, :] = xe.reshape(e_per, cap, D)


def _capacity(n_tok):
    return max(1, CAPACITY_FACTOR * n_tok // N_EXPERTS)


def _route_call(lg, h2, n_req, n_tok):
    cap = _capacity(n_tok)
    per_step = max(1, min(n_req, ROUTE_STEP_TOKENS // n_tok))
    before = jnp.asarray((np.arange(n_tok)[None, :] < np.arange(n_tok)[:, None]).astype(BF))
    unroll = max(1, min(N_EXPERTS, ROUTE_UNROLL_ELEMS // (per_step * cap * n_tok)))
    return pl.pallas_call(
        functools.partial(_route_kernel, cap, n_tok, unroll),
        grid=(n_req // per_step,),
        in_specs=[
            pl.BlockSpec((per_step * n_tok, LANES), lambda b: (b, 0)),
            pl.BlockSpec((per_step * n_tok, D), lambda b: (b, 0)),
            _const_spec((n_tok, n_tok)),
        ],
        out_specs=[
            pl.BlockSpec((N_EXPERTS, per_step * cap, D), lambda b: (0, b, 0)),
            pl.BlockSpec((N_EXPERTS, per_step * cap, 1), lambda b: (0, b, 0)),
            pl.BlockSpec((per_step * n_tok, LANES), lambda b: (b, 0)),
        ],
        out_shape=[
            jax.ShapeDtypeStruct((N_EXPERTS, n_req * cap, D), BF),
            jax.ShapeDtypeStruct((N_EXPERTS, n_req * cap, 1), F32),
            jax.ShapeDtypeStruct((n_req * n_tok, LANES), F32),
        ],
        scratch_shapes=[
            pltpu.VMEM((per_step, LANES, n_tok), F32),
            pltpu.VMEM((per_step, LANES, n_tok), F32),
            pltpu.VMEM((per_step, N_EXPERTS * cap, n_tok), BF),
        ],
        compiler_params=_params(1),
        name="route",
    )(lg, h2, before)


EXPERT_BLOCK = 1024
EXPERT_SUB = 256


def _experts_kernel(xa_ref, xb_ref, ga_ref, gb_ref, wg_ref, wu_ref, wd_ref, ya_ref, yb_ref, acc_s):
    f = pl.program_id(1)
    ra = xa_ref.shape[1]
    x = jnp.concatenate([xa_ref[0], xb_ref[0]], axis=0)
    for c in range(wg_ref.shape[2] // EXPERT_SUB):
        cs = slice(c * EXPERT_SUB, (c + 1) * EXPERT_SUB)
        gate = _dot(x, wg_ref[0, :, cs].astype(BF))
        up = _dot(x, wu_ref[0, :, cs].astype(BF))
        hid = (gate * jax.nn.sigmoid(gate) * up).astype(BF)
        down = _dot(hid, wd_ref[0, cs, :].astype(BF))
        if c == 0:
            acc_s[...] = jnp.where(f == 0, 0.0, acc_s[...]) + down
        else:
            acc_s[...] += down

    @pl.when(f == pl.num_programs(1) - 1)
    def _():
        ya_ref[0] = (acc_s[0:ra, :] * ga_ref[0]).astype(ya_ref.dtype)
        yb_ref[0] = (acc_s[ra:, :] * gb_ref[0]).astype(yb_ref.dtype)


def _experts_call(xa, xb, ga, gb, wg, wu, wd):
    ra, rb = xa.shape[1], xb.shape[1]
    tf = EXPERT_BLOCK
    xa_spec = pl.BlockSpec((1, ra, D), lambda e, f: (e, 0, 0))
    xb_spec = pl.BlockSpec((1, rb, D), lambda e, f: (e, 0, 0))
    return pl.pallas_call(
        _experts_kernel,
        grid=(N_EXPERTS, D_EXPERT // tf),
        in_specs=[
            xa_spec, xb_spec,
            pl.BlockSpec((1, ra, 1), lambda e, f: (e, 0, 0)),
            pl.BlockSpec((1, rb, 1), lambda e, f: (e, 0, 0)),
            pl.BlockSpec((1, D, tf), lambda e, f: (e, 0, f)),
            pl.BlockSpec((1, D, tf), lambda e, f: (e, 0, f)),
            pl.BlockSpec((1, tf, D), lambda e, f: (e, f, 0)),
        ],
        out_specs=[xa_spec, xb_spec],
        out_shape=[
            jax.ShapeDtypeStruct((N_EXPERTS, ra, D), BF),
            jax.ShapeDtypeStruct((N_EXPERTS, rb, D), BF),
        ],
        scratch_shapes=[pltpu.VMEM((ra + rb, D), F32)],
        compiler_params=_params(2),
        name="experts",
    )(xa, xb, ga, gb, wg, wu, wd)


def _scatter_kernel(cap, row_of_req, slot_ref, y_ref, x1_ref, m_ref, expand_ref, out_ref):
    m = m_ref[pl.ds(row_of_req(pl.program_id(0)), 1), :]
    gate2 = m[:, 5 * D:6 * D]
    slot_wide = _dot(slot_ref[...].astype(BF), expand_ref[...])
    slot_ids = (lax.broadcasted_iota(jnp.int32, (1, N_EXPERTS * cap), 1) % cap).astype(F32)
    onehot = jnp.where(slot_wide == slot_ids, 1.0, 0.0).astype(BF)
    y = y_ref[...].reshape(N_EXPERTS * cap, D)
    out_ref[...] = x1_ref[...] + gate2 * _dot(onehot, y)


def _scatter_call(slots, y, x1, m, row_of_req, n_req, n_tok):
    cap = _capacity(n_tok)
    expand = jnp.asarray((np.arange(LANES)[:, None] == np.arange(N_EXPERTS * cap)[None, :] // cap).astype(BF))
    tok = pl.BlockSpec((n_tok, D), lambda b: (b, 0))
    return pl.pallas_call(
        functools.partial(_scatter_kernel, cap, row_of_req),
        grid=(n_req,),
        in_specs=[
            pl.BlockSpec((n_tok, LANES), lambda b: (b, 0)),
            pl.BlockSpec((N_EXPERTS, cap, D), lambda b: (0, b, 0)),
            tok,
            _const_spec((MOD_ROWS, 6 * D)),
            _const_spec((LANES, N_EXPERTS * cap)),
        ],
        out_specs=tok,
        out_shape=jax.ShapeDtypeStruct((n_req * n_tok, D), F32),
        compiler_params=_params(1),
        name="scatter",
    )(slots, y, x1, m, expand)


def _rope_tables(n_tokens):
    rows = n_tokens // GRID_W
    row = np.broadcast_to(np.arange(rows, dtype=np.float32)[:, None], (rows, GRID_W)).reshape(-1)
    col = np.broadcast_to(np.arange(GRID_W, dtype=np.float32)[None, :], (rows, GRID_W)).reshape(-1)
    half = HEAD_DIM // 4
    inv_freq = (np.float32(ROPE_BASE) ** (-np.arange(half, dtype=np.float32) / np.float32(half))).astype(np.float32)
    ar = row[:, None] * inv_freq
    ac = col[:, None] * inv_freq
    ang = np.concatenate([ar, ar, ac, ac], axis=-1).astype(np.float64)
    cos, sin = np.cos(ang).astype(np.float32), np.sin(ang).astype(np.float32)
    first_half = (np.arange(HEAD_DIM) % (HEAD_DIM // 2)) < (HEAD_DIM // 4)
    sin_signed = np.where(first_half[None, :], -sin, sin)
    reps = LANES // HEAD_DIM
    token_major = np.concatenate([np.tile(cos, (1, reps)), np.tile(sin_signed, (1, reps))], axis=1)
    transposed = np.concatenate([cos.T, sin_signed.T], axis=0)
    return jnp.asarray(token_major), jnp.asarray(transposed)


def kernel(x_prompt, x_sample, cache_k, cache_v, c, c_ctx, w_ada, b_ada, norm1_g, norm2_g, w_in, q_norm_g, k_norm_g, lambda_q1, lambda_k1, lambda_q2, lambda_k2, subln_g, gmlp_ln_g, gmlp_ln_b, w_spatial, b_spatial, w_out, w_router, w_gate_e, w_up_e, w_down_e):
    n_p, t_p = x_prompt.shape[0], x_prompt.shape[1]
    n_s, t_s = x_sample.shape[0], x_sample.shape[1]
    n_ctx = cache_k.shape[2]
    l = 0

    m, lamv, w_in_bf, wkt, w_out_bf, wsp_pairs = _prep_call(
        c_ctx[None, :], c, w_ada[l], b_ada[l][None, :], lambda_q1[l][None, :], lambda_k1[l][None, :],
        lambda_q2[l][None, :], lambda_k2[l][None, :], w_in[l], w_out[l], w_spatial[l])

    gsum_np = (np.arange(D)[:, None] // HEAD_DIM == np.arange(LANES)[None, :]).astype(BF)
    gsum = jnp.asarray(gsum_np)
    wr = jnp.pad(w_router[l], ((0, 0), (0, LANES - N_EXPERTS)))
    wr_hi = wr.astype(BF)
    qg = jnp.tile(q_norm_g[l] * (HEAD_DIM ** -0.5 * math.log2(math.e)), D // HEAD_DIM)

    def rows_of(*vectors):
        rows = [jnp.tile(vec, D // vec.shape[0])[None, :] for vec in vectors]
        return jnp.concatenate(rows + [jnp.zeros((MOD_ROWS - len(rows), D), F32)], axis=0)

    consts = {
        "vecs_in": rows_of(norm1_g[l], qg, gmlp_ln_g[l], gmlp_ln_b[l]),
        "vecs_attn": rows_of(norm2_g[l], subln_g[l]),
        "w_in": w_in_bf,
        "wkt": wkt,
        "kgt": jnp.broadcast_to(k_norm_g[l][:, None], (HEAD_DIM, INPROJ_ROWS)),
        "gsum": gsum,
        "gbcast": jnp.asarray(np.concatenate([gsum_np.T, gsum_np.T], axis=0)),
        "wsp": wsp_pairs,
        "bsp": jnp.repeat(b_spatial[l].T, D // N_GROUPS, axis=1),
        "w_out": w_out_bf,
        "wr": jnp.concatenate([wr_hi, (wr - wr_hi.astype(F32)).astype(BF)], axis=1),
    }

    passes = (
        (x_prompt, n_p, t_p, lambda b: 0, False),
        (x_sample, n_s, t_s, lambda b: 1 + b, True),
    )
    outs = []
    kv_out = None
    for x, n_req, n_tok, row_of_req, is_sample in passes:
        xf = x.reshape(n_req * n_tok, D)
        tabs = _rope_tables(n_tok) if is_sample else None
        qst, kt, v = _inproj_call(xf, m, row_of_req, tabs, BF if is_sample else F32, consts, n_req, n_tok)
        if is_sample:
            ctx = (jnp.transpose(cache_k[:, l], (0, 2, 3, 4, 1)).reshape(n_req, D, n_ctx),
                   cache_v[:, l].reshape(n_req * n_ctx, D))
        else:
            ctx = None
            kv_out = (kt, v)
        x1, h2, lg = _attn_call(qst, kt, v, ctx, lamv, xf, m, row_of_req, consts, n_req, n_tok)
        xe, gates, slots = _route_call(lg, h2, n_req, n_tok)
        outs.append((x1, xe, gates, slots, row_of_req))

    (x1p, xep, gp, slots_p, row_p), (x1s, xes, gs, slots_s, row_s) = outs
    yp, ys = _experts_call(xep, xes, gp, gs, w_gate_e[l], w_up_e[l], w_down_e[l])
    y_prompt = _scatter_call(slots_p, yp, x1p, m, row_p, n_p, t_p).reshape(x_prompt.shape)
    y_sample = _scatter_call(slots_s, ys, x1s, m, row_s, n_s, t_s).reshape(x_sample.shape)
    new_k = jnp.transpose(kv_out[0].reshape(n_p, N_HEADS, 2, HEAD_DIM, t_p), (0, 4, 1, 2, 3))
    new_k = new_k.reshape(n_p, 1, t_p, N_HEADS, 2, HEAD_DIM)
    new_v = kv_out[1].reshape(n_p, 1, t_p, N_HEADS, V_HEAD_DIM)
    return (y_prompt, y_sample, new_k, new_v)
```

```python
import functools
import math

import jax
import jax.numpy as jnp
import numpy as np
from jax import lax
from jax.experimental import pallas as pl
from jax.experimental.pallas import tpu as pltpu

D = 1024
N_HEADS = 8
HEAD_DIM = 64
V_HEAD_DIM = 128
GRID_W = 64
ROPE_BASE = 10000.0
CHUNK = 128
N_GROUPS = 8
N_EXPERTS = 16
CAPACITY_FACTOR = 2
D_EXPERT = 2048
N_SEG = 7
K_SEG = 1
EPS = 1e-6
LAMBDA_INIT = 0.8 - 0.6 * math.exp(-0.3 * 0)

LANES = 128
MOD_ROWS = 8
VMEM_LIMIT = 56 * 1024 * 1024
TM = 256

BF = jnp.bfloat16
F32 = jnp.float32


def _dot(a, b):
    return jnp.dot(a, b, preferred_element_type=F32)


def _dot_nt(a, b):
    return lax.dot_general(a, b, (((1,), (1,)), ((), ())), preferred_element_type=F32)


def _split_bf16(x):
    hi = x.astype(BF)
    lo = (x - hi.astype(F32)).astype(BF)
    return hi, lo


def _rms(x, g):
    return x * lax.rsqrt(jnp.mean(x * x, axis=-1, keepdims=True) + EPS) * g


def _params(n_grid_dims):
    return pltpu.CompilerParams(
        dimension_semantics=("arbitrary",) * n_grid_dims, vmem_limit_bytes=VMEM_LIMIT)


def _const_spec(shape):
    nd = len(shape)
    return pl.BlockSpec(shape, lambda *_: (0,) * nd)


N_MOD = 6


def _prep_kernel(cctx_ref, c_ref, wa_ref, ba_ref, lq1_ref, lk1_ref, lq2_ref, lk2_ref, wi_ref, wo_ref, wsp_ref,
                 m_ref, lam_ref, wib_ref, wqkt_ref, wob_ref, wsp2_ref, cond_s):
    j = pl.program_id(0)
    n_lat = c_ref.shape[0]

    @pl.when(j == 0)
    def _():
        cond_s[...] = jnp.zeros(cond_s.shape, F32)
        cond_s[0:1, :] = cctx_ref[...]
        cond_s[1:1 + n_lat, :] = c_ref[...]
        s1 = jnp.sum(lq1_ref[...] * lk1_ref[...], axis=-1, keepdims=True)
        s2 = jnp.sum(lq2_ref[...] * lk2_ref[...], axis=-1, keepdims=True)
        lam_ref[...] = jnp.broadcast_to(jnp.exp(s1) - jnp.exp(s2) + LAMBDA_INIT, lam_ref.shape)
        wob_ref[...] = wo_ref[...].astype(BF)
        for p in range(N_GROUPS // 2):
            wsp2_ref[p] = jnp.concatenate([wsp_ref[2 * p], wsp_ref[2 * p + 1]], axis=1).astype(BF)

    @pl.when(j < N_MOD)
    def _():
        c = cond_s[...]
        a_hi, a_lo = _split_bf16(c * jax.nn.sigmoid(c))
        w_hi, w_lo = _split_bf16(wa_ref[...])
        m_ref[...] = _dot(a_hi, w_hi) + _dot(a_lo, w_hi) + _dot(a_hi, w_lo) + ba_ref[...]

    w = wi_ref[...]
    wib_ref[...] = w.astype(BF)

    @pl.when(j <= K_SEG)
    def _():
        wqkt_ref[0] = w.T.astype(BF)


def _prep_call(c_ctx, c, w_ada, b_ada, lq1, lk1, lq2, lk2, w_in, w_out, w_spatial):
    vec = _const_spec((1, HEAD_DIM))
    mod_block = lambda j: (0, jnp.minimum(j, N_MOD - 1))
    pairs = (N_GROUPS // 2, CHUNK, 2 * CHUNK)
    return pl.pallas_call(
        _prep_kernel,
        grid=(N_SEG,),
        in_specs=[
            _const_spec((1, D)), _const_spec(c.shape),
            pl.BlockSpec((D, D), mod_block),
            pl.BlockSpec((1, D), mod_block),
            vec, vec, vec, vec,
            pl.BlockSpec((D, D), lambda j: (0, j)),
            _const_spec((D, D)),
            _const_spec((N_GROUPS, CHUNK, CHUNK)),
        ],
        out_specs=[
            pl.BlockSpec((MOD_ROWS, D), mod_block),
            _const_spec((MOD_ROWS, LANES)),
            pl.BlockSpec((D, D), lambda j: (0, j)),
            pl.BlockSpec((1, D, D), lambda j: (jnp.minimum(j, K_SEG), 0, 0)),
            _const_spec((D, D)),
            _const_spec(pairs),
        ],
        out_shape=[
            jax.ShapeDtypeStruct((MOD_ROWS, N_MOD * D), F32),
            jax.ShapeDtypeStruct((MOD_ROWS, LANES), F32),
            jax.ShapeDtypeStruct((D, N_SEG * D), BF),
            jax.ShapeDtypeStruct((K_SEG + 1, D, D), BF),
            jax.ShapeDtypeStruct((D, D), BF),
            jax.ShapeDtypeStruct(pairs, BF),
        ],
        scratch_shapes=[pltpu.VMEM((MOD_ROWS, D), F32)],
        compiler_params=_params(1),
        name="prep",
    )(c_ctx, c, w_ada, b_ada, lq1, lk1, lq2, lk2, w_in, w_out, w_spatial)


def _rope_t(t3, cos_t, sin_signed_t):
    q = HEAD_DIM // 4
    rot = jnp.concatenate([t3[:, q:2 * q], t3[:, 0:q], t3[:, 3 * q:4 * q], t3[:, 2 * q:3 * q]], axis=1)
    return t3 * cos_t[None] + rot * sin_signed_t[None]


def _inproj_kernel(rope, row_of_step, x_ref, m_ref, vecs_ref, w_ref, wqkt_ref, gains_t_ref,
                   wsp_ref, bsp_ref, *rest):
    if rope:
        tab_t_ref, st_out, qt_out, kt_out, v_out = rest
    else:
        st_out, qt_out, kt_out, v_out = rest
    tm = x_ref.shape[0]
    m = m_ref[pl.ds(row_of_step(pl.program_id(0)), 1), :]
    shift1, scale1 = m[:, 0:D], m[:, D:2 * D]
    n1g, lng, lnb = (vecs_ref[r:r + 1, :] for r in range(3))
    h = (_rms(x_ref[...], n1g) * (1.0 + scale1) + shift1).astype(BF)

    def seg(j):
        return _dot(h, w_ref[:, j * D:(j + 1) * D])

    def heads_t(which, out_ref):
        gain_t = gains_t_ref[which * HEAD_DIM:(which + 1) * HEAD_DIM, :]
        t3 = _dot_nt(wqkt_ref[which], h).reshape(D // HEAD_DIM, HEAD_DIM, tm)
        inv = lax.rsqrt(jnp.mean(t3 * t3, axis=1, keepdims=True) + EPS)
        t3 = t3 * inv * gain_t[None]
        if rope:
            t3 = _rope_t(t3, tab_t_ref[0:HEAD_DIM, :], tab_t_ref[HEAD_DIM:2 * HEAD_DIM, :])
        t = t3.reshape(D, tm).astype(out_ref.dtype)
        width = out_ref.shape[2]
        for r in range(out_ref.shape[0]):
            out_ref[r] = t[:, r * width:(r + 1) * width]

    heads_t(0, qt_out)
    heads_t(1, kt_out)

    v_out[...] = seg(2).astype(v_out.dtype)
    st_out[:, 0:D] = jax.nn.sigmoid(seg(5)).astype(st_out.dtype)

    zv = jax.nn.gelu(seg(4))
    mu = jnp.mean(zv, axis=-1, keepdims=True)
    zc = zv - mu
    var = jnp.mean(zc * zc, axis=-1, keepdims=True)
    zvn = (zc * lax.rsqrt(var + EPS) * lng + lnb).astype(BF)
    pre = jax.nn.gelu(seg(3)) * jax.nn.sigmoid(seg(6))
    blank = jnp.zeros((CHUNK, LANES), BF)
    for c in range(tm // CHUNK):
        rows = slice(c * CHUNK, (c + 1) * CHUNK)
        for p in range(N_GROUPS // 2):
            cols = slice(2 * p * LANES, (2 * p + 2) * LANES)
            z = zvn[rows, cols]
            z_diag = jnp.concatenate([jnp.concatenate([z[:, :LANES], blank], axis=1),
                                      jnp.concatenate([blank, z[:, LANES:]], axis=1)], axis=0)
            sp = _dot(wsp_ref[p], z_diag) + bsp_ref[:, cols]
            out_cols = slice(D + 2 * p * LANES, D + (2 * p + 2) * LANES)
            st_out[rows, out_cols] = (pre[rows, cols] * sp).astype(st_out.dtype)


INPROJ_ROWS = 512


def _inproj_call(x, m, row_of_req, rope_tabs, kv_dtype, consts, n_req, n_tok):
    t = x.shape[0]
    tm = INPROJ_ROWS
    per_req = max(1, n_tok // tm)
    per_tile = max(1, tm // n_tok)
    rope = rope_tabs is not None

    def row_of_step(i):
        return row_of_req(i * per_tile // per_req)

    tok = pl.BlockSpec((tm, D), lambda i: (i, 0))
    in_specs = [
        tok,
        _const_spec((MOD_ROWS, 6 * D)),
        _const_spec((MOD_ROWS, D)),
        pl.BlockSpec((D, N_SEG * D), lambda i: (0, 0), pipeline_mode=pl.Buffered(1)),
        _const_spec((K_SEG + 1, D, D)),
        _const_spec((2 * HEAD_DIM, tm)),
        _const_spec((N_GROUPS // 2, CHUNK, 2 * CHUNK)), _const_spec((CHUNK, D)),
    ]
    args = [x, m, consts["vecs_in"], consts["w_in"], consts["wqkt"], consts["gains_t"],
            consts["wsp"], consts["bsp"]]
    if rope:
        in_specs += [pl.BlockSpec((2 * HEAD_DIM, tm), lambda i: (0, i % per_req))]
        args += [rope_tabs]
    t_spec = pl.BlockSpec((per_tile, D, tm // per_tile), lambda i: (i // per_req, 0, i % per_req))
    return pl.pallas_call(
        functools.partial(_inproj_kernel, rope, row_of_step),
        grid=(t // tm,),
        in_specs=in_specs,
        out_specs=[pl.BlockSpec((tm, 2 * D), lambda i: (i, 0)), t_spec, t_spec, tok],
        out_shape=[
            jax.ShapeDtypeStruct((t, 2 * D), BF),
            jax.ShapeDtypeStruct((n_req, D, n_tok), BF),
            jax.ShapeDtypeStruct((n_req, D, n_tok), kv_dtype),
            jax.ShapeDtypeStruct((t, D), kv_dtype),
        ],
        compiler_params=_params(1),
        name="inproj_rope" if rope else "inproj",
    )(*args)


SCORE_GROUP_ELEMS = 1 << 22
ATTN_ROWS = 256


def _dot_tn(a, b):
    return lax.dot_general(a, b, (((0,), (0,)), ((), ())), preferred_element_type=F32)


def _attn_kernel(has_ctx, heads_per_group, row_of_req, st_ref, qt_ref, kt_ref, v_ref, *rest):
    if has_ctx:
        kct_ref, vc_ref, *rest = rest
    lam_ref, vecs_ref, x_ref, m_ref, wo_ref, wr_ref, x1_out, h2_out, lg_out, merged_s = rest
    n2g = vecs_ref[0:1, :]
    sub_g = vecs_ref[1:2, 0:V_HEAD_DIM]
    lam = lam_ref[0:1, 0:1]
    lane = lax.broadcasted_iota(jnp.int32, (1, LANES), 1)
    ones_col = jnp.where(lane == 0, 1.0, 0.0).astype(BF)
    first_rows = lax.broadcasted_iota(jnp.int32, (2 * HEAD_DIM, 1), 0) < HEAD_DIM
    zero = jnp.zeros((), BF)

    def values(ref, h):
        v = ref[:, h * V_HEAD_DIM:(h + 1) * V_HEAD_DIM].astype(BF)
        return jnp.concatenate([v, jnp.broadcast_to(ones_col, (v.shape[0], LANES))], axis=1)

    def head_rows(h):
        return slice(h * 2 * HEAD_DIM, (h + 1) * 2 * HEAD_DIM)

    def head_scores(h):
        qt = qt_ref[0, head_rows(h), :]
        keys = [kt_ref[0, head_rows(h), :].astype(BF)]
        if has_ctx:
            keys.append(kct_ref[0, head_rows(h), :].astype(BF))
        subs = (jnp.where(first_rows, qt, zero), jnp.where(first_rows, zero, qt))
        return [[_dot_tn(q, k) for k in keys] for q in subs]

    def row_max(parts):
        return functools.reduce(jnp.maximum, [jnp.max(s, axis=-1, keepdims=True) for s in parts])

    def head_pv(h, parts, mx):
        vals = [values(v_ref, h)] + ([values(vc_ref, h)] if has_ctx else [])
        return functools.reduce(jnp.add, [_dot(jnp.exp2(s - mx).astype(BF), v) for s, v in zip(parts, vals)])

    def head_finish(h, obs):
        inv = [1.0 / ob[:, V_HEAD_DIM:V_HEAD_DIM + 1] for ob in obs]
        o = obs[0][:, :V_HEAD_DIM] * inv[0] - obs[1][:, :V_HEAD_DIM] * (lam * inv[1])
        o = _rms(o, sub_g) * (1.0 - LAMBDA_INIT)
        cols = slice(h * LANES, (h + 1) * LANES)
        sga = st_ref[:, h * LANES:(h + 1) * LANES].astype(F32)
        tb = st_ref[:, D + h * LANES:D + (h + 1) * LANES].astype(F32)
        merged_s[:, cols] = (sga * o + tb).astype(BF)

    for g0 in range(0, N_HEADS, heads_per_group):
        group = range(g0, g0 + heads_per_group)
        scores = [head_scores(h) for h in group]
        maxes = [[row_max(parts) for parts in subs] for subs in scores]
        outs = [[head_pv(h, parts, mx) for parts, mx in zip(subs, mxs)]
                for h, subs, mxs in zip(group, scores, maxes)]
        for h, obs in zip(group, outs):
            head_finish(h, obs)

    m = m_ref[pl.ds(row_of_req(pl.program_id(0)), 1), :]
    gate1, shift2, scale2 = m[:, 2 * D:3 * D], m[:, 3 * D:4 * D], m[:, 4 * D:5 * D]
    x1 = x_ref[...] + gate1 * _dot(merged_s[...], wo_ref[...])
    x1_out[...] = x1
    h2 = _rms(x1, n2g) * (1.0 + scale2) + shift2
    h2_out[...] = h2.astype(BF)
    hi, lo = _split_bf16(h2)
    wr_hi, wr_lo = wr_ref[:, 0:LANES], wr_ref[:, LANES:2 * LANES]
    lg_out[...] = _dot(hi, wr_hi) + _dot(lo, wr_hi) + _dot(hi, wr_lo)


def _attn_call(st, qt, kt, v, ctx, lamv, x, m, row_of_req, consts, n_req, n_tok):
    tq = min(n_tok, ATTN_ROWS)
    nqb = n_tok // tq
    has_ctx = ctx is not None
    tok = pl.BlockSpec((tq, D), lambda b, i: (b * nqb + i, 0))
    in_specs = [pl.BlockSpec((tq, 2 * D), lambda b, i: (b * nqb + i, 0)),
                pl.BlockSpec((1, D, tq), lambda b, i: (b, 0, i)),
                pl.BlockSpec((1, D, n_tok), lambda b, i: (b, 0, 0)),
                pl.BlockSpec((n_tok, D), lambda b, i: (b, 0))]
    args = [st, qt, kt, v]
    if has_ctx:
        n_ctx = ctx[0].shape[2]
        in_specs += [pl.BlockSpec((1, D, n_ctx), lambda b, i: (b, 0, 0)),
                     pl.BlockSpec((n_ctx, D), lambda b, i: (b, 0))]
        args += list(ctx)
    in_specs += [
        _const_spec((MOD_ROWS, LANES)), _const_spec((MOD_ROWS, D)),
        tok,
        _const_spec((MOD_ROWS, 6 * D)),
        _const_spec((D, D)),
        _const_spec((D, 2 * LANES)),
    ]
    args += [lamv, consts["vecs_attn"], x, m, consts["w_out"], consts["wr"]]
    t = n_req * n_tok
    n_keys = n_tok + (ctx[0].shape[2] if has_ctx else 0)
    heads_per_group = max(1, min(N_HEADS, SCORE_GROUP_ELEMS // (2 * tq * n_keys)))
    while N_HEADS % heads_per_group:
        heads_per_group -= 1
    return pl.pallas_call(
        functools.partial(_attn_kernel, has_ctx, heads_per_group, row_of_req),
        grid=(n_req, nqb),
        in_specs=in_specs,
        out_specs=[tok, tok, pl.BlockSpec((tq, LANES), lambda b, i: (b * nqb + i, 0))],
        out_shape=[
            jax.ShapeDtypeStruct((t, D), F32),
            jax.ShapeDtypeStruct((t, D), BF),
            jax.ShapeDtypeStruct((t, LANES), F32),
        ],
        scratch_shapes=[pltpu.VMEM((tq, D), BF)],
        compiler_params=_params(2),
        name="attn_ctx" if has_ctx else "attn",
    )(*args)


GATHER_ROWS = 512
KEY_BITS = 31
ROUTE_UNROLL_ELEMS = 1 << 22
ROUTE_STEP_TOKENS = 2048


def _route_kernel(cap, n_tok, unroll, lg_ref, h2_ref, before_ref, xe_ref, gate_ref, slot_ref,
                  aff_t_s, slot_t_s, p_s):
    reqs = range(lg_ref.shape[0] // n_tok)
    lane = lax.broadcasted_iota(jnp.int32, (1, LANES), 1)
    valid = lane < N_EXPERTS

    def rows(r):
        return slice(r * n_tok, (r + 1) * n_tok)

    def affinity(r):
        lg = jnp.where(valid, lg_ref[rows(r), :], -1e30)
        ex = jnp.where(valid, jnp.exp(lg - jnp.max(lg, axis=-1, keepdims=True)), 0.0)
        return ex / jnp.sum(ex, axis=-1, keepdims=True)

    def count(mask):
        return jnp.sum(jnp.where(mask, 1.0, 0.0), axis=0, keepdims=True)

    affs = [affinity(r) for r in reqs]
    kth_bits = [jnp.zeros((1, LANES), jnp.int32) for _ in reqs]
    for bit in range(KEY_BITS - 1, -1, -1):
        for r in reqs:
            cand = kth_bits[r] | (1 << bit)
            enough = count(affs[r] >= lax.bitcast_convert_type(cand, F32)) >= cap
            kth_bits[r] = jnp.where(enough, cand, kth_bits[r])

    before = before_ref[...]
    for r in reqs:
        aff = affs[r]
        kth = lax.bitcast_convert_type(kth_bits[r], F32)
        above = aff > kth
        tied = aff == kth
        need = cap - count(above)
        tied_before = _dot(before, jnp.where(tied, 1.0, 0.0).astype(BF))
        chosen = jnp.where(above, 1.0, jnp.where(tied, jnp.where(tied_before < need, 1.0, 0.0), 0.0))
        slot = _dot(before, chosen.astype(BF))
        slot = jnp.where(valid, jnp.where(chosen > 0.0, slot, float(cap)), float(cap))
        slot_ref[rows(r), :] = slot
        slot_t_s[r] = slot.T
        aff_t_s[r] = aff.T

    slot_ids = lax.broadcasted_iota(jnp.int32, (cap, 1), 0).astype(F32)

    def per_expert(e, carry):
        for r in reqs:
            hit = slot_t_s[r, pl.ds(e, 1), :] == slot_ids
            p_s[r, pl.ds(pl.multiple_of(e * cap, cap), cap), :] = jnp.where(hit, 1.0, 0.0).astype(BF)
            gate = jnp.sum(jnp.where(hit, aff_t_s[r, pl.ds(e, 1), :], 0.0), axis=-1, keepdims=True)
            gate_ref[e, r * cap:(r + 1) * cap, :] = gate
        return carry

    lax.fori_loop(0, N_EXPERTS, per_expert, 0, unroll=unroll)
    n_rows = min(GATHER_ROWS, N_EXPERTS * cap)
    e_per = n_rows // cap
    for r in reqs:
        h2 = h2_ref[rows(r), :]
        for t in range(N_EXPERTS * cap // n_rows):
            xe = _dot(p_s[r, t * n_rows:(t + 1) * n_rows, :], h2).astype(BF)
            xe_ref[t * e_per:(t + 1) * e_per, r * cap:(r + 1) * cap, :] = xe.reshape(e_per, cap, D)


def _capacity(n_tok):
    return max(1, CAPACITY_FACTOR * n_tok // N_EXPERTS)


def _route_call(lg, h2, n_req, n_tok):
    cap = _capacity(n_tok)
    per_step = max(1, min(n_req, ROUTE_STEP_TOKENS // n_tok))
    before = jnp.asarray((np.arange(n_tok)[None, :] < np.arange(n_tok)[:, None]).astype(BF))
    unroll = max(1, min(N_EXPERTS, ROUTE_UNROLL_ELEMS // (per_step * cap * n_tok)))
    return pl.pallas_call(
        functools.partial(_route_kernel, cap, n_tok, unroll),
        grid=(n_req // per_step,),
        in_specs=[
            pl.BlockSpec((per_step * n_tok, LANES), lambda b: (b, 0)),
            pl.BlockSpec((per_step * n_tok, D), lambda b: (b, 0)),
            _const_spec((n_tok, n_tok)),
        ],
        out_specs=[
            pl.BlockSpec((N_EXPERTS, per_step * cap, D), lambda b: (0, b, 0)),
            pl.BlockSpec((N_EXPERTS, per_step * cap, 1), lambda b: (0, b, 0)),
            pl.BlockSpec((per_step * n_tok, LANES), lambda b: (b, 0)),
        ],
        out_shape=[
            jax.ShapeDtypeStruct((N_EXPERTS, n_req * cap, D), BF),
            jax.ShapeDtypeStruct((N_EXPERTS, n_req * cap, 1), F32),
            jax.ShapeDtypeStruct((n_req * n_tok, LANES), F32),
        ],
        scratch_shapes=[
            pltpu.VMEM((per_step, LANES, n_tok), F32),
            pltpu.VMEM((per_step, LANES, n_tok), F32),
            pltpu.VMEM((per_step, N_EXPERTS * cap, n_tok), BF),
        ],
        compiler_params=_params(1),
        name="route",
    )(lg, h2, before)


EXPERT_BLOCK = 1024
EXPERT_SUB = 256


def _experts_kernel(xa_ref, xb_ref, ga_ref, gb_ref, wg_ref, wu_ref, wd_ref, ya_ref, yb_ref, acc_s):
    f = pl.program_id(1)
    ra = xa_ref.shape[1]
    x = jnp.concatenate([xa_ref[0], xb_ref[0]], axis=0)
    for c in range(wg_ref.shape[2] // EXPERT_SUB):
        cs = slice(c * EXPERT_SUB, (c + 1) * EXPERT_SUB)
        gate = _dot(x, wg_ref[0, :, cs].astype(BF))
        up = _dot(x, wu_ref[0, :, cs].astype(BF))
        hid = (gate * jax.nn.sigmoid(gate) * up).astype(BF)
        down = _dot(hid, wd_ref[0, cs, :].astype(BF))
        if c == 0:
            acc_s[...] = jnp.where(f == 0, 0.0, acc_s[...]) + down
        else:
            acc_s[...] += down

    @pl.when(f == pl.num_programs(1) - 1)
    def _():
        ya_ref[0] = (acc_s[0:ra, :] * ga_ref[0]).astype(ya_ref.dtype)
        yb_ref[0] = (acc_s[ra:, :] * gb_ref[0]).astype(yb_ref.dtype)


def _experts_call(xa, xb, ga, gb, wg, wu, wd):
    ra, rb = xa.shape[1], xb.shape[1]
    tf = EXPERT_BLOCK
    xa_spec = pl.BlockSpec((1, ra, D), lambda e, f: (e, 0, 0))
    xb_spec = pl.BlockSpec((1, rb, D), lambda e, f: (e, 0, 0))
    return pl.pallas_call(
        _experts_kernel,
        grid=(N_EXPERTS, D_EXPERT // tf),
        in_specs=[
            xa_spec, xb_spec,
            pl.BlockSpec((1, ra, 1), lambda e, f: (e, 0, 0)),
            pl.BlockSpec((1, rb, 1), lambda e, f: (e, 0, 0)),
            pl.BlockSpec((1, D, tf), lambda e, f: (e, 0, f)),
            pl.BlockSpec((1, D, tf), lambda e, f: (e, 0, f)),
            pl.BlockSpec((1, tf, D), lambda e, f: (e, f, 0)),
        ],
        out_specs=[xa_spec, xb_spec],
        out_shape=[
            jax.ShapeDtypeStruct((N_EXPERTS, ra, D), BF),
            jax.ShapeDtypeStruct((N_EXPERTS, rb, D), BF),
        ],
        scratch_shapes=[pltpu.VMEM((ra + rb, D), F32)],
        compiler_params=_params(2),
        name="experts",
    )(xa, xb, ga, gb, wg, wu, wd)


def _scatter_kernel(cap, row_of_req, slot_ref, y_ref, x1_ref, m_ref, expand_ref, out_ref):
    m = m_ref[pl.ds(row_of_req(pl.program_id(0)), 1), :]
    gate2 = m[:, 5 * D:6 * D]
    slot_wide = _dot(slot_ref[...].astype(BF), expand_ref[...])
    slot_ids = (lax.broadcasted_iota(jnp.int32, (1, N_EXPERTS * cap), 1) % cap).astype(F32)
    onehot = jnp.where(slot_wide == slot_ids, 1.0, 0.0).astype(BF)
    y = y_ref[...].reshape(N_EXPERTS * cap, D)
    out_ref[...] = x1_ref[...] + gate2 * _dot(onehot, y)


def _scatter_call(slots, y, x1, m, row_of_req, n_req, n_tok):
    cap = _capacity(n_tok)
    expand = jnp.asarray((np.arange(LANES)[:, None] == np.arange(N_EXPERTS * cap)[None, :] // cap).astype(BF))
    tok = pl.BlockSpec((n_tok, D), lambda b: (b, 0))
    return pl.pallas_call(
        functools.partial(_scatter_kernel, cap, row_of_req),
        grid=(n_req,),
        in_specs=[
            pl.BlockSpec((n_tok, LANES), lambda b: (b, 0)),
            pl.BlockSpec((N_EXPERTS, cap, D), lambda b: (0, b, 0)),
            tok,
            _const_spec((MOD_ROWS, 6 * D)),
            _const_spec((LANES, N_EXPERTS * cap)),
        ],
        out_specs=tok,
        out_shape=jax.ShapeDtypeStruct((n_req * n_tok, D), F32),
        compiler_params=_params(1),
        name="scatter",
    )(slots, y, x1, m, expand)


def _rope_tables(n_tokens):
    rows = n_tokens // GRID_W
    row = np.broadcast_to(np.arange(rows, dtype=np.float32)[:, None], (rows, GRID_W)).reshape(-1)
    col = np.broadcast_to(np.arange(GRID_W, dtype=np.float32)[None, :], (rows, GRID_W)).reshape(-1)
    half = HEAD_DIM // 4
    inv_freq = (np.float32(ROPE_BASE) ** (-np.arange(half, dtype=np.float32) / np.float32(half))).astype(np.float32)
    ar = row[:, None] * inv_freq
    ac = col[:, None] * inv_freq
    ang = np.concatenate([ar, ar, ac, ac], axis=-1).astype(np.float64)
    cos, sin = np.cos(ang).astype(np.float32), np.sin(ang).astype(np.float32)
    first_half = (np.arange(HEAD_DIM) % (HEAD_DIM // 2)) < (HEAD_DIM // 4)
    sin_signed = np.where(first_half[None, :], -sin, sin)
    return jnp.asarray(np.concatenate([cos.T, sin_signed.T], axis=0))


def kernel(x_prompt, x_sample, cache_k, cache_v, c, c_ctx, w_ada, b_ada, norm1_g, norm2_g, w_in, q_norm_g, k_norm_g, lambda_q1, lambda_k1, lambda_q2, lambda_k2, subln_g, gmlp_ln_g, gmlp_ln_b, w_spatial, b_spatial, w_out, w_router, w_gate_e, w_up_e, w_down_e):
    n_p, t_p = x_prompt.shape[0], x_prompt.shape[1]
    n_s, t_s = x_sample.shape[0], x_sample.shape[1]
    n_ctx = cache_k.shape[2]
    l = 0

    m, lamv, w_in_bf, wqkt, w_out_bf, wsp_pairs = _prep_call(
        c_ctx[None, :], c, w_ada[l], b_ada[l][None, :], lambda_q1[l][None, :], lambda_k1[l][None, :],
        lambda_q2[l][None, :], lambda_k2[l][None, :], w_in[l], w_out[l], w_spatial[l])

    wr = jnp.pad(w_router[l], ((0, 0), (0, LANES - N_EXPERTS)))
    wr_hi = wr.astype(BF)
    qk_gains = jnp.concatenate([q_norm_g[l] * (HEAD_DIM ** -0.5 * math.log2(math.e)), k_norm_g[l]])

    def rows_of(*vectors):
        rows = [jnp.tile(vec, D // vec.shape[0])[None, :] for vec in vectors]
        return jnp.concatenate(rows + [jnp.zeros((MOD_ROWS - len(rows), D), F32)], axis=0)

    consts = {
        "vecs_in": rows_of(norm1_g[l], gmlp_ln_g[l], gmlp_ln_b[l]),
        "vecs_attn": rows_of(norm2_g[l], subln_g[l]),
        "w_in": w_in_bf,
        "wqkt": wqkt,
        "gains_t": jnp.broadcast_to(qk_gains[:, None], (2 * HEAD_DIM, INPROJ_ROWS)),
        "wsp": wsp_pairs,
        "bsp": jnp.repeat(b_spatial[l].T, D // N_GROUPS, axis=1),
        "w_out": w_out_bf,
        "wr": jnp.concatenate([wr_hi, (wr - wr_hi.astype(F32)).astype(BF)], axis=1),
    }

    passes = (
        (x_prompt, n_p, t_p, lambda b: 0, False),
        (x_sample, n_s, t_s, lambda b: 1 + b, True),
    )
    outs = []
    kv_out = None
    for x, n_req, n_tok, row_of_req, is_sample in passes:
        xf = x.reshape(n_req * n_tok, D)
        tabs = _rope_tables(n_tok) if is_sample else None
        st, qt, kt, v = _inproj_call(xf, m, row_of_req, tabs, BF if is_sample else F32, consts, n_req, n_tok)
        if is_sample:
            ctx = (jnp.transpose(cache_k[:, l], (0, 2, 3, 4, 1)).reshape(n_req, D, n_ctx),
                   cache_v[:, l].reshape(n_req * n_ctx, D))
        else:
            ctx = None
            kv_out = (kt, v)
        x1, h2, lg = _attn_call(st, qt, kt, v, ctx, lamv, xf, m, row_of_req, consts, n_req, n_tok)
        xe, gates, slots = _route_call(lg, h2, n_req, n_tok)
        outs.append((x1, xe, gates, slots, row_of_req))

    (x1p, xep, gp, slots_p, row_p), (x1s, xes, gs, slots_s, row_s) = outs
    yp, ys = _experts_call(xep, xes, gp, gs, w_gate_e[l], w_up_e[l], w_down_e[l])
    y_prompt = _scatter_call(slots_p, yp, x1p, m, row_p, n_p, t_p).reshape(x_prompt.shape)
    y_sample = _scatter_call(slots_s, ys, x1s, m, row_s, n_s, t_s).reshape(x_sample.shape)
    new_k = jnp.transpose(kv_out[0].reshape(n_p, N_HEADS, 2, HEAD_DIM, t_p), (0, 4, 1, 2, 3))
    new_k = new_k.reshape(n_p, 1, t_p, N_HEADS, 2, HEAD_DIM)
    new_v = kv_out[1].reshape(n_p, 1, t_p, N_HEADS, V_HEAD_DIM)
    return (y_prompt, y_sample, new_k, new_v)
```

```python
import functools
import math

import jax
import jax.numpy as jnp
import numpy as np
from jax import lax
from jax.experimental import pallas as pl
from jax.experimental.pallas import tpu as pltpu

D = 1024
N_HEADS = 8
HEAD_DIM = 64
V_HEAD_DIM = 128
GRID_W = 64
ROPE_BASE = 10000.0
CHUNK = 128
N_GROUPS = 8
N_EXPERTS = 16
CAPACITY_FACTOR = 2
D_EXPERT = 2048
N_SEG = 7
K_SEG = 1
EPS = 1e-6
LAMBDA_INIT = 0.8 - 0.6 * math.exp(-0.3 * 0)

LANES = 128
MOD_ROWS = 8
VMEM_LIMIT = 56 * 1024 * 1024
TM = 256

BF = jnp.bfloat16
F32 = jnp.float32


def _dot(a, b):
    return jnp.dot(a, b, preferred_element_type=F32)


def _dot_nt(a, b):
    return lax.dot_general(a, b, (((1,), (1,)), ((), ())), preferred_element_type=F32)


def _split_bf16(x):
    hi = x.astype(BF)
    lo = (x - hi.astype(F32)).astype(BF)
    return hi, lo


def _rms(x, g):
    return x * lax.rsqrt(jnp.mean(x * x, axis=-1, keepdims=True) + EPS) * g


def _params(n_grid_dims):
    return pltpu.CompilerParams(
        dimension_semantics=("arbitrary",) * n_grid_dims, vmem_limit_bytes=VMEM_LIMIT)


def _const_spec(shape):
    nd = len(shape)
    return pl.BlockSpec(shape, lambda *_: (0,) * nd)


N_MOD = 6


def _prep_kernel(cctx_ref, c_ref, wa_ref, ba_ref, lq1_ref, lk1_ref, lq2_ref, lk2_ref, wi_ref, wo_ref, wsp_ref,
                 m_ref, lam_ref, wib_ref, wkt_ref, wob_ref, wsp2_ref, cond_s):
    j = pl.program_id(0)
    n_lat = c_ref.shape[0]

    @pl.when(j == 0)
    def _():
        cond_s[...] = jnp.zeros(cond_s.shape, F32)
        cond_s[0:1, :] = cctx_ref[...]
        cond_s[1:1 + n_lat, :] = c_ref[...]
        s1 = jnp.sum(lq1_ref[...] * lk1_ref[...], axis=-1, keepdims=True)
        s2 = jnp.sum(lq2_ref[...] * lk2_ref[...], axis=-1, keepdims=True)
        lam_ref[...] = jnp.broadcast_to(jnp.exp(s1) - jnp.exp(s2) + LAMBDA_INIT, lam_ref.shape)
        wob_ref[...] = wo_ref[...].astype(BF)
        for p in range(N_GROUPS // 2):
            wsp2_ref[p] = jnp.concatenate([wsp_ref[2 * p], wsp_ref[2 * p + 1]], axis=1).astype(BF)

    @pl.when(j < N_MOD)
    def _():
        c = cond_s[...]
        a_hi, a_lo = _split_bf16(c * jax.nn.sigmoid(c))
        w_hi, w_lo = _split_bf16(wa_ref[...])
        m_ref[...] = _dot(a_hi, w_hi) + _dot(a_lo, w_hi) + _dot(a_hi, w_lo) + ba_ref[...]

    w = wi_ref[...]
    wib_ref[...] = w.astype(BF)

    @pl.when(j == K_SEG)
    def _():
        wkt_ref[...] = w.T.astype(BF)


def _prep_call(c_ctx, c, w_ada, b_ada, lq1, lk1, lq2, lk2, w_in, w_out, w_spatial):
    vec = _const_spec((1, HEAD_DIM))
    mod_block = lambda j: (0, jnp.minimum(j, N_MOD - 1))
    pairs = (N_GROUPS // 2, CHUNK, 2 * CHUNK)
    return pl.pallas_call(
        _prep_kernel,
        grid=(N_SEG,),
        in_specs=[
            _const_spec((1, D)), _const_spec(c.shape),
            pl.BlockSpec((D, D), mod_block),
            pl.BlockSpec((1, D), mod_block),
            vec, vec, vec, vec,
            pl.BlockSpec((D, D), lambda j: (0, j)),
            _const_spec((D, D)),
            _const_spec((N_GROUPS, CHUNK, CHUNK)),
        ],
        out_specs=[
            pl.BlockSpec((MOD_ROWS, D), mod_block),
            _const_spec((MOD_ROWS, LANES)),
            pl.BlockSpec((D, D), lambda j: (0, j)),
            _const_spec((D, D)),
            _const_spec((D, D)),
            _const_spec(pairs),
        ],
        out_shape=[
            jax.ShapeDtypeStruct((MOD_ROWS, N_MOD * D), F32),
            jax.ShapeDtypeStruct((MOD_ROWS, LANES), F32),
            jax.ShapeDtypeStruct((D, N_SEG * D), BF),
            jax.ShapeDtypeStruct((D, D), BF),
            jax.ShapeDtypeStruct((D, D), BF),
            jax.ShapeDtypeStruct(pairs, BF),
        ],
        scratch_shapes=[pltpu.VMEM((MOD_ROWS, D), F32)],
        compiler_params=_params(1),
        name="prep",
    )(c_ctx, c, w_ada, b_ada, lq1, lk1, lq2, lk2, w_in, w_out, w_spatial)


def _head_norm(t, g, gsum, gbcast):
    ss = _dot((t * t).astype(BF), gsum)
    inv = lax.rsqrt(ss * (1.0 / HEAD_DIM) + EPS)
    hi, lo = _split_bf16(inv)
    bc = _dot(jnp.concatenate([hi, lo], axis=-1), gbcast)
    return t * bc * g


def _lane_tile(tab):
    return jnp.concatenate([tab] * (D // LANES), axis=1)


def _rope(t, cos, sin_signed):
    q = HEAD_DIM // 4
    lane = lax.broadcasted_iota(jnp.int32, (1, D), 1)
    first_half = (lane % (HEAD_DIM // 2)) < q
    fwd = pltpu.roll(t, D - q, axis=1)
    bwd = pltpu.roll(t, q, axis=1)
    return t * _lane_tile(cos) + jnp.where(first_half, fwd, bwd) * _lane_tile(sin_signed)


def _rope_t(t3, cos_t, sin_signed_t):
    q = HEAD_DIM // 4
    rot = jnp.concatenate([t3[:, q:2 * q], t3[:, 0:q], t3[:, 3 * q:4 * q], t3[:, 2 * q:3 * q]], axis=1)
    return t3 * cos_t[None] + rot * sin_signed_t[None]


def _inproj_kernel(rope, row_of_step, x_ref, m_ref, vecs_ref, w_ref, wkt_ref, kgt_ref, gsum_ref,
                   gbcast_ref, wsp_ref, bsp_ref, *rest):
    if rope:
        tab_ref, tab_t_ref, qst_out, kt_out, v_out = rest
    else:
        qst_out, kt_out, v_out = rest
    tm = x_ref.shape[0]
    m = m_ref[pl.ds(row_of_step(pl.program_id(0)), 1), :]
    shift1, scale1 = m[:, 0:D], m[:, D:2 * D]
    n1g, qg, lng, lnb = (vecs_ref[r:r + 1, :] for r in range(4))
    h = (_rms(x_ref[...], n1g) * (1.0 + scale1) + shift1).astype(BF)

    def seg(j):
        return _dot(h, w_ref[:, j * D:(j + 1) * D])

    q = _head_norm(seg(0), qg, gsum_ref[...], gbcast_ref[...])
    if rope:
        q = _rope(q, tab_ref[:, 0:LANES], tab_ref[:, LANES:2 * LANES])
    qst_out[:, 0:D] = q.astype(qst_out.dtype)

    k3 = _dot_nt(wkt_ref[...], h).reshape(D // HEAD_DIM, HEAD_DIM, tm)
    inv = lax.rsqrt(jnp.mean(k3 * k3, axis=1, keepdims=True) + EPS)
    k3 = k3 * inv * kgt_ref[...][None]
    if rope:
        k3 = _rope_t(k3, tab_t_ref[0:HEAD_DIM, :], tab_t_ref[HEAD_DIM:2 * HEAD_DIM, :])
    kt = k3.reshape(D, tm).astype(kt_out.dtype)
    width = kt_out.shape[2]
    for r in range(kt_out.shape[0]):
        kt_out[r] = kt[:, r * width:(r + 1) * width]

    v_out[...] = seg(2).astype(v_out.dtype)
    qst_out[:, D:2 * D] = jax.nn.sigmoid(seg(5)).astype(qst_out.dtype)

    zv = jax.nn.gelu(seg(4))
    mu = jnp.mean(zv, axis=-1, keepdims=True)
    zc = zv - mu
    var = jnp.mean(zc * zc, axis=-1, keepdims=True)
    zvn = (zc * lax.rsqrt(var + EPS) * lng + lnb).astype(BF)
    pre = jax.nn.gelu(seg(3)) * jax.nn.sigmoid(seg(6))
    blank = jnp.zeros((CHUNK, LANES), BF)
    for c in range(tm // CHUNK):
        rows = slice(c * CHUNK, (c + 1) * CHUNK)
        for p in range(N_GROUPS // 2):
            cols = slice(2 * p * LANES, (2 * p + 2) * LANES)
            z = zvn[rows, cols]
            z_diag = jnp.concatenate([jnp.concatenate([z[:, :LANES], blank], axis=1),
                                      jnp.concatenate([blank, z[:, LANES:]], axis=1)], axis=0)
            sp = _dot(wsp_ref[p], z_diag) + bsp_ref[:, cols]
            out_cols = slice(2 * D + 2 * p * LANES, 2 * D + (2 * p + 2) * LANES)
            qst_out[rows, out_cols] = (pre[rows, cols] * sp).astype(qst_out.dtype)


INPROJ_ROWS = 512


def _inproj_call(x, m, row_of_req, rope_tabs, kv_dtype, consts, n_req, n_tok):
    t = x.shape[0]
    tm = INPROJ_ROWS
    per_req = max(1, n_tok // tm)
    per_tile = max(1, tm // n_tok)
    rope = rope_tabs is not None

    def row_of_step(i):
        return row_of_req(i * per_tile // per_req)

    tok = pl.BlockSpec((tm, D), lambda i: (i, 0))
    in_specs = [
        tok,
        _const_spec((MOD_ROWS, 6 * D)),
        _const_spec((MOD_ROWS, D)),
        pl.BlockSpec((D, N_SEG * D), lambda i: (0, 0), pipeline_mode=pl.Buffered(1)),
        _const_spec((D, D)),
        _const_spec((HEAD_DIM, tm)),
        _const_spec((D, LANES)), _const_spec((2 * LANES, D)),
        _const_spec((N_GROUPS // 2, CHUNK, 2 * CHUNK)), _const_spec((CHUNK, D)),
    ]
    args = [x, m, consts["vecs_in"], consts["w_in"], consts["wkt"], consts["kgt"], consts["gsum"],
            consts["gbcast"], consts["wsp"], consts["bsp"]]
    if rope:
        in_specs += [pl.BlockSpec((tm, 2 * LANES), lambda i: (i % per_req, 0)),
                     pl.BlockSpec((2 * HEAD_DIM, tm), lambda i: (0, i % per_req))]
        args += list(rope_tabs)
    kt_spec = pl.BlockSpec((per_tile, D, tm // per_tile), lambda i: (i, 0, 0))
    return pl.pallas_call(
        functools.partial(_inproj_kernel, rope, row_of_step),
        grid=(t // tm,),
        in_specs=in_specs,
        out_specs=[pl.BlockSpec((tm, 3 * D), lambda i: (i, 0)), kt_spec, tok],
        out_shape=[
            jax.ShapeDtypeStruct((t, 3 * D), BF),
            jax.ShapeDtypeStruct((n_req * per_req, D, tm // per_tile), kv_dtype),
            jax.ShapeDtypeStruct((t, D), kv_dtype),
        ],
        compiler_params=_params(1),
        name="inproj_rope" if rope else "inproj",
    )(*args)


SCORE_GROUP_ELEMS = 1 << 22
ATTN_ROWS = 256


def _attn_kernel(has_ctx, heads_per_group, row_of_req, qst_ref, kt_ref, v_ref, *rest):
    if has_ctx:
        kct_ref, vc_ref, *rest = rest
    lam_ref, vecs_ref, x_ref, m_ref, wo_ref, wr_ref, x1_out, h2_out, lg_out, merged_s = rest
    tq = qst_ref.shape[0]
    key_blocks, block_keys = kt_ref.shape[0], kt_ref.shape[2]
    n2g = vecs_ref[0:1, :]
    sub_g = vecs_ref[1:2, 0:V_HEAD_DIM]
    lam = lam_ref[0:1, 0:1]
    lane = lax.broadcasted_iota(jnp.int32, (1, LANES), 1)
    first = lane < HEAD_DIM
    zero = jnp.zeros((), BF)
    ones_col = jnp.where(lane == 0, 1.0, 0.0).astype(BF)

    def values(ref, rows, h):
        v = ref[rows, h * V_HEAD_DIM:(h + 1) * V_HEAD_DIM].astype(BF)
        return jnp.concatenate([v, jnp.broadcast_to(ones_col, (v.shape[0], LANES))], axis=1)

    def head_cols(h):
        return slice(h * LANES, (h + 1) * LANES)

    def head_scores(h):
        q = qst_ref[:, head_cols(h)]
        qz = jnp.concatenate([jnp.where(first, q, zero), jnp.where(first, zero, q)], axis=0)
        parts = [_dot(qz, kt_ref[j, head_cols(h), :].astype(BF)) for j in range(key_blocks)]
        if has_ctx:
            parts.append(_dot(qz, kct_ref[0, head_cols(h), :].astype(BF)))
        return parts

    def row_max(scores):
        return functools.reduce(jnp.maximum, [jnp.max(s, axis=-1, keepdims=True) for s in scores])

    def head_pv(h, scores, mx):
        vals = [values(v_ref, slice(j * block_keys, (j + 1) * block_keys), h) for j in range(key_blocks)]
        if has_ctx:
            vals.append(values(vc_ref, slice(None), h))
        return functools.reduce(jnp.add, [_dot(jnp.exp2(s - mx).astype(BF), v) for s, v in zip(scores, vals)])

    def head_finish(h, ob):
        den = ob[:, V_HEAD_DIM:V_HEAD_DIM + 1]
        o = ob[:tq, :V_HEAD_DIM] * (1.0 / den[:tq]) - ob[tq:, :V_HEAD_DIM] * (lam / den[tq:])
        o = _rms(o, sub_g) * (1.0 - LAMBDA_INIT)
        sga = qst_ref[:, D + h * LANES:D + (h + 1) * LANES].astype(F32)
        tb = qst_ref[:, 2 * D + h * LANES:2 * D + (h + 1) * LANES].astype(F32)
        merged_s[:, head_cols(h)] = (sga * o + tb).astype(BF)

    for g0 in range(0, N_HEADS, heads_per_group):
        group = range(g0, g0 + heads_per_group)
        scores = [head_scores(h) for h in group]
        maxes = [row_max(s) for s in scores]
        outs = [head_pv(h, s, mx) for h, s, mx in zip(group, scores, maxes)]
        for h, ob in zip(group, outs):
            head_finish(h, ob)

    m = m_ref[pl.ds(row_of_req(pl.program_id(0)), 1), :]
    gate1, shift2, scale2 = m[:, 2 * D:3 * D], m[:, 3 * D:4 * D], m[:, 4 * D:5 * D]
    x1 = x_ref[...] + gate1 * _dot(merged_s[...], wo_ref[...])
    x1_out[...] = x1
    h2 = _rms(x1, n2g) * (1.0 + scale2) + shift2
    h2_out[...] = h2.astype(BF)
    hi, lo = _split_bf16(h2)
    wr_hi, wr_lo = wr_ref[:, 0:LANES], wr_ref[:, LANES:2 * LANES]
    lg_out[...] = _dot(hi, wr_hi) + _dot(lo, wr_hi) + _dot(hi, wr_lo)


def _attn_call(qst, kt, v, ctx, lamv, x, m, row_of_req, consts, n_req, n_tok):
    tq = min(n_tok, ATTN_ROWS)
    nqb = n_tok // tq
    key_blocks = kt.shape[0] // n_req
    has_ctx = ctx is not None
    tok = pl.BlockSpec((tq, D), lambda b, i: (b * nqb + i, 0))
    in_specs = [pl.BlockSpec((tq, 3 * D), lambda b, i: (b * nqb + i, 0)),
                pl.BlockSpec((key_blocks, D, kt.shape[2]), lambda b, i: (b, 0, 0)),
                pl.BlockSpec((n_tok, D), lambda b, i: (b, 0))]
    args = [qst, kt, v]
    if has_ctx:
        n_ctx = ctx[0].shape[2]
        in_specs += [pl.BlockSpec((1, D, n_ctx), lambda b, i: (b, 0, 0)),
                     pl.BlockSpec((n_ctx, D), lambda b, i: (b, 0))]
        args += list(ctx)
    in_specs += [
        _const_spec((MOD_ROWS, LANES)), _const_spec((MOD_ROWS, D)),
        tok,
        _const_spec((MOD_ROWS, 6 * D)),
        _const_spec((D, D)),
        _const_spec((D, 2 * LANES)),
    ]
    args += [lamv, consts["vecs_attn"], x, m, consts["w_out"], consts["wr"]]
    t = n_req * n_tok
    n_keys = n_tok + (ctx[0].shape[2] if has_ctx else 0)
    heads_per_group = max(1, min(N_HEADS, SCORE_GROUP_ELEMS // (2 * tq * n_keys)))
    while N_HEADS % heads_per_group:
        heads_per_group -= 1
    return pl.pallas_call(
        functools.partial(_attn_kernel, has_ctx, heads_per_group, row_of_req),
        grid=(n_req, nqb),
        in_specs=in_specs,
        out_specs=[tok, tok, pl.BlockSpec((tq, LANES), lambda b, i: (b * nqb + i, 0))],
        out_shape=[
            jax.ShapeDtypeStruct((t, D), F32),
            jax.ShapeDtypeStruct((t, D), BF),
            jax.ShapeDtypeStruct((t, LANES), F32),
        ],
        scratch_shapes=[pltpu.VMEM((tq, D), BF)],
        compiler_params=_params(2),
        name="attn_ctx" if has_ctx else "attn",
    )(*args)


GATHER_ROWS = 512
KEY_BITS = 31
ROUTE_UNROLL_ELEMS = 1 << 22
ROUTE_STEP_TOKENS = 2048


def _route_kernel(cap, n_tok, unroll, lg_ref, h2_ref, before_ref, xe_ref, gate_ref, slot_ref,
                  aff_t_s, slot_t_s, p_s):
    reqs = range(lg_ref.shape[0] // n_tok)
    lane = lax.broadcasted_iota(jnp.int32, (1, LANES), 1)
    valid = lane < N_EXPERTS

    def rows(r):
        return slice(r * n_tok, (r + 1) * n_tok)

    def affinity(r):
        lg = jnp.where(valid, lg_ref[rows(r), :], -1e30)
        ex = jnp.where(valid, jnp.exp(lg - jnp.max(lg, axis=-1, keepdims=True)), 0.0)
        return ex / jnp.sum(ex, axis=-1, keepdims=True)

    def count(mask):
        return jnp.sum(jnp.where(mask, 1.0, 0.0), axis=0, keepdims=True)

    affs = [affinity(r) for r in reqs]
    kth_bits = [jnp.zeros((1, LANES), jnp.int32) for _ in reqs]
    for bit in range(KEY_BITS - 1, -1, -1):
        for r in reqs:
            cand = kth_bits[r] | (1 << bit)
            enough = count(affs[r] >= lax.bitcast_convert_type(cand, F32)) >= cap
            kth_bits[r] = jnp.where(enough, cand, kth_bits[r])

    before = before_ref[...]
    for r in reqs:
        aff = affs[r]
        kth = lax.bitcast_convert_type(kth_bits[r], F32)
        above = aff > kth
        tied = aff == kth
        need = cap - count(above)
        tied_before = _dot(before, jnp.where(tied, 1.0, 0.0).astype(BF))
        chosen = jnp.where(above, 1.0, jnp.where(tied, jnp.where(tied_before < need, 1.0, 0.0), 0.0))
        slot = _dot(before, chosen.astype(BF))
        slot = jnp.where(valid, jnp.where(chosen > 0.0, slot, float(cap)), float(cap))
        slot_ref[rows(r), :] = slot
        slot_t_s[r] = slot.T
        aff_t_s[r] = aff.T

    slot_ids = lax.broadcasted_iota(jnp.int32, (cap, 1), 0).astype(F32)

    def per_expert(e, carry):
        for r in reqs:
            hit = slot_t_s[r, pl.ds(e, 1), :] == slot_ids
            p_s[r, pl.ds(pl.multiple_of(e * cap, cap), cap), :] = jnp.where(hit, 1.0, 0.0).astype(BF)
            gate = jnp.sum(jnp.where(hit, aff_t_s[r, pl.ds(e, 1), :], 0.0), axis=-1, keepdims=True)
            gate_ref[e, r * cap:(r + 1) * cap, :] = gate
        return carry

    lax.fori_loop(0, N_EXPERTS, per_expert, 0, unroll=unroll)
    n_rows = min(GATHER_ROWS, N_EXPERTS * cap)
    e_per = n_rows // cap
    for r in reqs:
        h2 = h2_ref[rows(r), :]
        for t in range(N_EXPERTS * cap // n_rows):
            xe = _dot(p_s[r, t * n_rows:(t + 1) * n_rows, :], h2).astype(BF)
            xe_ref[t * e_per:(t + 1) * e_per, r * cap:(r + 1) * cap, :] = xe.reshape(e_per, cap, D)


def _capacity(n_tok):
    return max(1, CAPACITY_FACTOR * n_tok // N_EXPERTS)


def _route_call(lg, h2, n_req, n_tok):
    cap = _capacity(n_tok)
    per_step = max(1, min(n_req, ROUTE_STEP_TOKENS // n_tok))
    before = jnp.asarray((np.arange(n_tok)[None, :] < np.arange(n_tok)[:, None]).astype(BF))
    unroll = max(1, min(N_EXPERTS, ROUTE_UNROLL_ELEMS // (per_step * cap * n_tok)))
    return pl.pallas_call(
        functools.partial(_route_kernel, cap, n_tok, unroll),
        grid=(n_req // per_step,),
        in_specs=[
            pl.BlockSpec((per_step * n_tok, LANES), lambda b: (b, 0)),
            pl.BlockSpec((per_step * n_tok, D), lambda b: (b, 0)),
            _const_spec((n_tok, n_tok)),
        ],
        out_specs=[
            pl.BlockSpec((N_EXPERTS, per_step * cap, D), lambda b: (0, b, 0)),
            pl.BlockSpec((N_EXPERTS, per_step * cap, 1), lambda b: (0, b, 0)),
            pl.BlockSpec((per_step * n_tok, LANES), lambda b: (b, 0)),
        ],
        out_shape=[
            jax.ShapeDtypeStruct((N_EXPERTS, n_req * cap, D), BF),
            jax.ShapeDtypeStruct((N_EXPERTS, n_req * cap, 1), F32),
            jax.ShapeDtypeStruct((n_req * n_tok, LANES), F32),
        ],
        scratch_shapes=[
            pltpu.VMEM((per_step, LANES, n_tok), F32),
            pltpu.VMEM((per_step, LANES, n_tok), F32),
            pltpu.VMEM((per_step, N_EXPERTS * cap, n_tok), BF),
        ],
        compiler_params=_params(1),
        name="route",
    )(lg, h2, before)


EXPERT_BLOCK = 1024
EXPERT_SUB = 256


def _experts_kernel(xa_ref, xb_ref, ga_ref, gb_ref, wg_ref, wu_ref, wd_ref, ya_ref, yb_ref, acc_s):
    f = pl.program_id(1)
    ra = xa_ref.shape[1]
    x = jnp.concatenate([xa_ref[0], xb_ref[0]], axis=0)
    for c in range(wg_ref.shape[2] // EXPERT_SUB):
        cs = slice(c * EXPERT_SUB, (c + 1) * EXPERT_SUB)
        gate = _dot(x, wg_ref[0, :, cs].astype(BF))
        up = _dot(x, wu_ref[0, :, cs].astype(BF))
        hid = (gate * jax.nn.sigmoid(gate) * up).astype(BF)
        down = _dot(hid, wd_ref[0, cs, :].astype(BF))
        if c == 0:
            acc_s[...] = jnp.where(f == 0, 0.0, acc_s[...]) + down
        else:
            acc_s[...] += down

    @pl.when(f == pl.num_programs(1) - 1)
    def _():
        ya_ref[0] = (acc_s[0:ra, :] * ga_ref[0]).astype(ya_ref.dtype)
        yb_ref[0] = (acc_s[ra:, :] * gb_ref[0]).astype(yb_ref.dtype)


def _experts_call(xa, xb, ga, gb, wg, wu, wd):
    ra, rb = xa.shape[1], xb.shape[1]
    tf = EXPERT_BLOCK
    xa_spec = pl.BlockSpec((1, ra, D), lambda e, f: (e, 0, 0))
    xb_spec = pl.BlockSpec((1, rb, D), lambda e, f: (e, 0, 0))
    return pl.pallas_call(
        _experts_kernel,
        grid=(N_EXPERTS, D_EXPERT // tf),
        in_specs=[
            xa_spec, xb_spec,
            pl.BlockSpec((1, ra, 1), lambda e, f: (e, 0, 0)),
            pl.BlockSpec((1, rb, 1), lambda e, f: (e, 0, 0)),
            pl.BlockSpec((1, D, tf), lambda e, f: (e, 0, f)),
            pl.BlockSpec((1, D, tf), lambda e, f: (e, 0, f)),
            pl.BlockSpec((1, tf, D), lambda e, f: (e, f, 0)),
        ],
        out_specs=[xa_spec, xb_spec],
        out_shape=[
            jax.ShapeDtypeStruct((N_EXPERTS, ra, D), BF),
            jax.ShapeDtypeStruct((N_EXPERTS, rb, D), BF),
        ],
        scratch_shapes=[pltpu.VMEM((ra + rb, D), F32)],
        compiler_params=_params(2),
        name="experts",
    )(xa, xb, ga, gb, wg, wu, wd)


def _scatter_kernel(cap, row_of_req, slot_ref, y_ref, x1_ref, m_ref, expand_ref, out_ref):
    m = m_ref[pl.ds(row_of_req(pl.program_id(0)), 1), :]
    gate2 = m[:, 5 * D:6 * D]
    slot_wide = _dot(slot_ref[...].astype(BF), expand_ref[...])
    slot_ids = (lax.broadcasted_iota(jnp.int32, (1, N_EXPERTS * cap), 1) % cap).astype(F32)
    onehot = jnp.where(slot_wide == slot_ids, 1.0, 0.0).astype(BF)
    y = y_ref[...].reshape(N_EXPERTS * cap, D)
    out_ref[...] = x1_ref[...] + gate2 * _dot(onehot, y)


def _scatter_call(slots, y, x1, m, row_of_req, n_req, n_tok):
    cap = _capacity(n_tok)
    expand = jnp.asarray((np.arange(LANES)[:, None] == np.arange(N_EXPERTS * cap)[None, :] // cap).astype(BF))
    tok = pl.BlockSpec((n_tok, D), lambda b: (b, 0))
    return pl.pallas_call(
        functools.partial(_scatter_kernel, cap, row_of_req),
        grid=(n_req,),
        in_specs=[
            pl.BlockSpec((n_tok, LANES), lambda b: (b, 0)),
            pl.BlockSpec((N_EXPERTS, cap, D), lambda b: (0, b, 0)),
            tok,
            _const_spec((MOD_ROWS, 6 * D)),
            _const_spec((LANES, N_EXPERTS * cap)),
        ],
        out_specs=tok,
        out_shape=jax.ShapeDtypeStruct((n_req * n_tok, D), F32),
        compiler_params=_params(1),
        name="scatter",
    )(slots, y, x1, m, expand)


def _rope_tables(n_tokens):
    rows = n_tokens // GRID_W
    row = np.broadcast_to(np.arange(rows, dtype=np.float32)[:, None], (rows, GRID_W)).reshape(-1)
    col = np.broadcast_to(np.arange(GRID_W, dtype=np.float32)[None, :], (rows, GRID_W)).reshape(-1)
    half = HEAD_DIM // 4
    inv_freq = (np.float32(ROPE_BASE) ** (-np.arange(half, dtype=np.float32) / np.float32(half))).astype(np.float32)
    ar = row[:, None] * inv_freq
    ac = col[:, None] * inv_freq
    ang = np.concatenate([ar, ar, ac, ac], axis=-1).astype(np.float64)
    cos, sin = np.cos(ang).astype(np.float32), np.sin(ang).astype(np.float32)
    first_half = (np.arange(HEAD_DIM) % (HEAD_DIM // 2)) < (HEAD_DIM // 4)
    sin_signed = np.where(first_half[None, :], -sin, sin)
    reps = LANES // HEAD_DIM
    token_major = np.concatenate([np.tile(cos, (1, reps)), np.tile(sin_signed, (1, reps))], axis=1)
    transposed = np.concatenate([cos.T, sin_signed.T], axis=0)
    return jnp.asarray(token_major), jnp.asarray(transposed)


def kernel(x_prompt, x_sample, cache_k, cache_v, c, c_ctx, w_ada, b_ada, norm1_g, norm2_g, w_in, q_norm_g, k_norm_g, lambda_q1, lambda_k1, lambda_q2, lambda_k2, subln_g, gmlp_ln_g, gmlp_ln_b, w_spatial, b_spatial, w_out, w_router, w_gate_e, w_up_e, w_down_e):
    n_p, t_p = x_prompt.shape[0], x_prompt.shape[1]
    n_s, t_s = x_sample.shape[0], x_sample.shape[1]
    n_ctx = cache_k.shape[2]
    l = 0

    m, lamv, w_in_bf, wkt, w_out_bf, wsp_pairs = _prep_call(
        c_ctx[None, :], c, w_ada[l], b_ada[l][None, :], lambda_q1[l][None, :], lambda_k1[l][None, :],
        lambda_q2[l][None, :], lambda_k2[l][None, :], w_in[l], w_out[l], w_spatial[l])

    gsum_np = (np.arange(D)[:, None] // HEAD_DIM == np.arange(LANES)[None, :]).astype(BF)
    gsum = jnp.asarray(gsum_np)
    wr = jnp.pad(w_router[l], ((0, 0), (0, LANES - N_EXPERTS)))
    wr_hi = wr.astype(BF)
    qg = jnp.tile(q_norm_g[l] * (HEAD_DIM ** -0.5 * math.log2(math.e)), D // HEAD_DIM)

    def rows_of(*vectors):
        rows = [jnp.tile(vec, D // vec.shape[0])[None, :] for vec in vectors]
        return jnp.concatenate(rows + [jnp.zeros((MOD_ROWS - len(rows), D), F32)], axis=0)

    consts = {
        "vecs_in": rows_of(norm1_g[l], qg, gmlp_ln_g[l], gmlp_ln_b[l]),
        "vecs_attn": rows_of(norm2_g[l], subln_g[l]),
        "w_in": w_in_bf,
        "wkt": wkt,
        "kgt": jnp.broadcast_to(k_norm_g[l][:, None], (HEAD_DIM, INPROJ_ROWS)),
        "gsum": gsum,
        "gbcast": jnp.asarray(np.concatenate([gsum_np.T, gsum_np.T], axis=0)),
        "wsp": wsp_pairs,
        "bsp": jnp.repeat(b_spatial[l].T, D // N_GROUPS, axis=1),
        "w_out": w_out_bf,
        "wr": jnp.concatenate([wr_hi, (wr - wr_hi.astype(F32)).astype(BF)], axis=1),
    }

    passes = (
        (x_prompt, n_p, t_p, lambda b: 0, False),
        (x_sample, n_s, t_s, lambda b: 1 + b, True),
    )
    outs = []
    kv_out = None
    for x, n_req, n_tok, row_of_req, is_sample in passes:
        xf = x.reshape(n_req * n_tok, D)
        tabs = _rope_tables(n_tok) if is_sample else None
        qst, kt, v = _inproj_call(xf, m, row_of_req, tabs, BF if is_sample else F32, consts, n_req, n_tok)
        if is_sample:
            ctx = (jnp.transpose(cache_k[:, l], (0, 2, 3, 4, 1)).reshape(n_req, D, n_ctx),
                   cache_v[:, l].reshape(n_req * n_ctx, D))
        else:
            ctx = None
            kv_out = (kt, v)
        x1, h2, lg = _attn_call(qst, kt, v, ctx, lamv, xf, m, row_of_req, consts, n_req, n_tok)
        xe, gates, slots = _route_call(lg, h2, n_req, n_tok)
        outs.append((x1, xe, gates, slots, row_of_req))

    (x1p, xep, gp, slots_p, row_p), (x1s, xes, gs, slots_s, row_s) = outs
    yp, ys = _experts_call(xep, xes, gp, gs, w_gate_e[l], w_up_e[l], w_down_e[l])
    y_prompt = _scatter_call(slots_p, yp, x1p, m, row_p, n_p, t_p).reshape(x_prompt.shape)
    y_sample = _scatter_call(slots_s, ys, x1s, m, row_s, n_s, t_s).reshape(x_sample.shape)
    new_k = jnp.transpose(kv_out[0].reshape(n_p, N_HEADS, 2, HEAD_DIM, t_p), (0, 4, 1, 2, 3))
    new_k = new_k.reshape(n_p, 1, t_p, N_HEADS, 2, HEAD_DIM)
    new_v = kv_out[1].reshape(n_p, 1, t_p, N_HEADS, V_HEAD_DIM)
    return (y_prompt, y_sample, new_k, new_v)
```

```python
import functools
import math

import jax
import jax.numpy as jnp
import numpy as np
from jax import lax
from jax.experimental import pallas as pl
from jax.experimental.pallas import tpu as pltpu

D = 1024
N_HEADS = 8
HEAD_DIM = 64
V_HEAD_DIM = 128
GRID_W = 64
ROPE_BASE = 10000.0
CHUNK = 128
N_GROUPS = 8
N_EXPERTS = 16
CAPACITY_FACTOR = 2
D_EXPERT = 2048
N_SEG = 7
K_SEG = 1
EPS = 1e-6
LAMBDA_INIT = 0.8 - 0.6 * math.exp(-0.3 * 0)

LANES = 128
MOD_ROWS = 8
VMEM_LIMIT = 56 * 1024 * 1024

BF = jnp.bfloat16
F32 = jnp.float32


def _dot(a, b):
    return jnp.dot(a, b, preferred_element_type=F32)


def _dot_nt(a, b):
    return lax.dot_general(a, b, (((1,), (1,)), ((), ())), preferred_element_type=F32)


def _split_bf16(x):
    hi = x.astype(BF)
    lo = (x - hi.astype(F32)).astype(BF)
    return hi, lo


def _sigmoid(x):
    return 0.5 * jnp.tanh(0.5 * x) + 0.5


def _rms(x, g):
    return x * lax.rsqrt(jnp.mean(x * x, axis=-1, keepdims=True) + EPS) * g


def _params(n_grid_dims):
    return pltpu.CompilerParams(
        dimension_semantics=("arbitrary",) * n_grid_dims, vmem_limit_bytes=VMEM_LIMIT)


def _const_spec(shape):
    nd = len(shape)
    return pl.BlockSpec(shape, lambda *_: (0,) * nd)


N_MOD = 6


def _prep_kernel(cctx_ref, c_ref, wa_ref, ba_ref, lq1_ref, lk1_ref, lq2_ref, lk2_ref, wi_ref, wo_ref, wsp_ref,
                 m_ref, lam_ref, wib_ref, wkt_ref, wob_ref, wsp2_ref, cond_s):
    j = pl.program_id(0)
    n_lat = c_ref.shape[0]

    @pl.when(j == 0)
    def _():
        cond_s[...] = jnp.zeros(cond_s.shape, F32)
        cond_s[0:1, :] = cctx_ref[...]
        cond_s[1:1 + n_lat, :] = c_ref[...]
        s1 = jnp.sum(lq1_ref[...] * lk1_ref[...], axis=-1, keepdims=True)
        s2 = jnp.sum(lq2_ref[...] * lk2_ref[...], axis=-1, keepdims=True)
        lam_ref[...] = jnp.broadcast_to(jnp.exp(s1) - jnp.exp(s2) + LAMBDA_INIT, lam_ref.shape)
        wob_ref[...] = wo_ref[...].astype(BF)
        for p in range(N_GROUPS // 2):
            wsp2_ref[p] = jnp.concatenate([wsp_ref[2 * p], wsp_ref[2 * p + 1]], axis=1).astype(BF)

    @pl.when(j < N_MOD)
    def _():
        c = cond_s[...]
        a_hi, a_lo = _split_bf16(c * jax.nn.sigmoid(c))
        w_hi, w_lo = _split_bf16(wa_ref[...])
        m_ref[...] = _dot(a_hi, w_hi) + _dot(a_lo, w_hi) + _dot(a_hi, w_lo) + ba_ref[...]

    w = wi_ref[...]
    wib_ref[...] = w.astype(BF)

    @pl.when(j == K_SEG)
    def _():
        wkt_ref[...] = w.T.astype(BF)


def _prep_call(c_ctx, c, w_ada, b_ada, lq1, lk1, lq2, lk2, w_in, w_out, w_spatial):
    vec = _const_spec((1, HEAD_DIM))
    mod_block = lambda j: (0, jnp.minimum(j, N_MOD - 1))
    pairs = (N_GROUPS // 2, CHUNK, 2 * CHUNK)
    return pl.pallas_call(
        _prep_kernel,
        grid=(N_SEG,),
        in_specs=[
            _const_spec((1, D)), _const_spec(c.shape),
            pl.BlockSpec((D, D), mod_block),
            pl.BlockSpec((1, D), mod_block),
            vec, vec, vec, vec,
            pl.BlockSpec((D, D), lambda j: (0, j)),
            _const_spec((D, D)),
            _const_spec((N_GROUPS, CHUNK, CHUNK)),
        ],
        out_specs=[
            pl.BlockSpec((MOD_ROWS, D), mod_block),
            _const_spec((MOD_ROWS, LANES)),
            pl.BlockSpec((D, D), lambda j: (0, j)),
            _const_spec((D, D)),
            _const_spec((D, D)),
            _const_spec(pairs),
        ],
        out_shape=[
            jax.ShapeDtypeStruct((MOD_ROWS, N_MOD * D), F32),
            jax.ShapeDtypeStruct((MOD_ROWS, LANES), F32),
            jax.ShapeDtypeStruct((D, N_SEG * D), BF),
            jax.ShapeDtypeStruct((D, D), BF),
            jax.ShapeDtypeStruct((D, D), BF),
            jax.ShapeDtypeStruct(pairs, BF),
        ],
        scratch_shapes=[pltpu.VMEM((MOD_ROWS, D), F32)],
        compiler_params=_params(1),
        name="prep",
    )(c_ctx, c, w_ada, b_ada, lq1, lk1, lq2, lk2, w_in, w_out, w_spatial)


def _head_norm(t, g, gsum, gbcast):
    ss = _dot((t * t).astype(BF), gsum)
    inv = lax.rsqrt(ss * (1.0 / HEAD_DIM) + EPS)
    hi, lo = _split_bf16(inv)
    bc = _dot(jnp.concatenate([hi, lo], axis=-1), gbcast)
    return t * bc * g


def _lane_tile(tab):
    return jnp.concatenate([tab] * (D // LANES), axis=1)


def _rope(t, cos, sin_signed):
    q = HEAD_DIM // 4
    lane = lax.broadcasted_iota(jnp.int32, (1, D), 1)
    first_half = (lane % (HEAD_DIM // 2)) < q
    fwd = pltpu.roll(t, D - q, axis=1)
    bwd = pltpu.roll(t, q, axis=1)
    return t * _lane_tile(cos) + jnp.where(first_half, fwd, bwd) * _lane_tile(sin_signed)


def _rope_t(t3, cos_t, sin_signed_t):
    q = HEAD_DIM // 4
    rot = jnp.concatenate([t3[:, q:2 * q], t3[:, 0:q], t3[:, 3 * q:4 * q], t3[:, 2 * q:3 * q]], axis=1)
    return t3 * cos_t[None] + rot * sin_signed_t[None]


def _inproj_kernel(rope, row_of_step, x_ref, m_ref, vecs_ref, w_ref, wkt_ref, kgt_ref, gsum_ref,
                   gbcast_ref, wsp_ref, bsp_ref, *rest):
    if rope:
        tab_ref, tab_t_ref, qst_out, kt_out, v_out = rest
    else:
        qst_out, kt_out, v_out = rest
    tm = x_ref.shape[0]
    m = m_ref[pl.ds(row_of_step(pl.program_id(0)), 1), :]
    shift1, scale1 = m[:, 0:D], m[:, D:2 * D]
    n1g, qg, lng, lnb = (vecs_ref[r:r + 1, :] for r in range(4))
    h = (_rms(x_ref[...], n1g) * (1.0 + scale1) + shift1).astype(BF)

    def seg(j):
        return _dot(h, w_ref[:, j * D:(j + 1) * D])

    q = _head_norm(seg(0), qg, gsum_ref[...], gbcast_ref[...])
    if rope:
        q = _rope(q, tab_ref[:, 0:LANES], tab_ref[:, LANES:2 * LANES])
    qst_out[:, 0:D] = q.astype(qst_out.dtype)

    k3 = _dot_nt(wkt_ref[...], h).reshape(D // HEAD_DIM, HEAD_DIM, tm)
    inv = lax.rsqrt(jnp.mean(k3 * k3, axis=1, keepdims=True) + EPS)
    k3 = k3 * inv * kgt_ref[...][None]
    if rope:
        k3 = _rope_t(k3, tab_t_ref[0:HEAD_DIM, :], tab_t_ref[HEAD_DIM:2 * HEAD_DIM, :])
    kt = k3.reshape(D, tm).astype(kt_out.dtype)
    width = kt_out.shape[2]
    for r in range(kt_out.shape[0]):
        kt_out[r] = kt[:, r * width:(r + 1) * width]

    v_out[...] = seg(2).astype(v_out.dtype)
    qst_out[:, D:2 * D] = _sigmoid(seg(5)).astype(qst_out.dtype)

    zv = jax.nn.gelu(seg(4))
    mu = jnp.mean(zv, axis=-1, keepdims=True)
    zc = zv - mu
    var = jnp.mean(zc * zc, axis=-1, keepdims=True)
    zvn = (zc * lax.rsqrt(var + EPS) * lng + lnb).astype(BF)
    pre = jax.nn.gelu(seg(3)) * _sigmoid(seg(6))
    blank = jnp.zeros((CHUNK, LANES), BF)
    for c in range(tm // CHUNK):
        rows = slice(c * CHUNK, (c + 1) * CHUNK)
        for p in range(N_GROUPS // 2):
            cols = slice(2 * p * LANES, (2 * p + 2) * LANES)
            z = zvn[rows, cols]
            z_diag = jnp.concatenate([jnp.concatenate([z[:, :LANES], blank], axis=1),
                                      jnp.concatenate([blank, z[:, LANES:]], axis=1)], axis=0)
            sp = _dot(wsp_ref[p], z_diag) + bsp_ref[:, cols]
            out_cols = slice(2 * D + 2 * p * LANES, 2 * D + (2 * p + 2) * LANES)
            qst_out[rows, out_cols] = (pre[rows, cols] * sp).astype(qst_out.dtype)


INPROJ_ROWS = 512


def _inproj_call(x, m, row_of_req, rope_tabs, kv_dtype, consts, n_req, n_tok):
    t = x.shape[0]
    tm = INPROJ_ROWS
    per_req = max(1, n_tok // tm)
    per_tile = max(1, tm // n_tok)
    rope = rope_tabs is not None

    def row_of_step(i):
        return row_of_req(i * per_tile // per_req)

    tok = pl.BlockSpec((tm, D), lambda i: (i, 0))
    in_specs = [
        tok,
        _const_spec((MOD_ROWS, 6 * D)),
        _const_spec((MOD_ROWS, D)),
        pl.BlockSpec((D, N_SEG * D), lambda i: (0, 0), pipeline_mode=pl.Buffered(1)),
        _const_spec((D, D)),
        _const_spec((HEAD_DIM, tm)),
        _const_spec((D, LANES)), _const_spec((2 * LANES, D)),
        _const_spec((N_GROUPS // 2, CHUNK, 2 * CHUNK)), _const_spec((CHUNK, D)),
    ]
    args = [x, m, consts["vecs_in"], consts["w_in"], consts["wkt"], consts["kgt"], consts["gsum"],
            consts["gbcast"], consts["wsp"], consts["bsp"]]
    if rope:
        in_specs += [pl.BlockSpec((tm, 2 * LANES), lambda i: (i % per_req, 0)),
                     pl.BlockSpec((2 * HEAD_DIM, tm), lambda i: (0, i % per_req))]
        args += list(rope_tabs)
    kt_spec = pl.BlockSpec((per_tile, D, tm // per_tile), lambda i: (i, 0, 0))
    return pl.pallas_call(
        functools.partial(_inproj_kernel, rope, row_of_step),
        grid=(t // tm,),
        in_specs=in_specs,
        out_specs=[pl.BlockSpec((tm, 3 * D), lambda i: (i, 0)), kt_spec, tok],
        out_shape=[
            jax.ShapeDtypeStruct((t, 3 * D), BF),
            jax.ShapeDtypeStruct((n_req * per_req, D, tm // per_tile), kv_dtype),
            jax.ShapeDtypeStruct((t, D), kv_dtype),
        ],
        compiler_params=_params(1),
        name="inproj_rope" if rope else "inproj",
    )(*args)


SCORE_GROUP_ELEMS = 1 << 22
ATTN_ROWS = 256


def _attn_kernel(has_ctx, heads_per_group, row_of_req, qst_ref, kt_ref, v_ref, *rest):
    if has_ctx:
        kct_ref, vc_ref, *rest = rest
    lam_ref, vecs_ref, x_ref, m_ref, wo_ref, wr_ref, x1_out, h2_out, lg_out, merged_s = rest
    tq = qst_ref.shape[0]
    key_blocks, block_keys = kt_ref.shape[0], kt_ref.shape[2]
    n2g = vecs_ref[0:1, :]
    sub_g = vecs_ref[1:2, 0:V_HEAD_DIM]
    lam = lam_ref[0:1, 0:1]
    lane = lax.broadcasted_iota(jnp.int32, (1, LANES), 1)
    first = lane < HEAD_DIM
    zero = jnp.zeros((), BF)
    ones_col = jnp.where(lane == 0, 1.0, 0.0).astype(BF)

    def values(ref, rows, h):
        v = ref[rows, h * V_HEAD_DIM:(h + 1) * V_HEAD_DIM].astype(BF)
        return jnp.concatenate([v, jnp.broadcast_to(ones_col, (v.shape[0], LANES))], axis=1)

    def head_cols(h):
        return slice(h * LANES, (h + 1) * LANES)

    def head_scores(h):
        q = qst_ref[:, head_cols(h)]
        qz = jnp.concatenate([jnp.where(first, q, zero), jnp.where(first, zero, q)], axis=0)
        parts = [_dot(qz, kt_ref[j, head_cols(h), :].astype(BF)) for j in range(key_blocks)]
        if has_ctx:
            parts.append(_dot(qz, kct_ref[0, head_cols(h), :].astype(BF)))
        return parts

    def row_max(scores):
        return functools.reduce(jnp.maximum, [jnp.max(s, axis=-1, keepdims=True) for s in scores])

    def head_pv(h, scores, mx):
        vals = [values(v_ref, slice(j * block_keys, (j + 1) * block_keys), h) for j in range(key_blocks)]
        if has_ctx:
            vals.append(values(vc_ref, slice(None), h))
        return functools.reduce(jnp.add, [_dot(jnp.exp2(s - mx).astype(BF), v) for s, v in zip(scores, vals)])

    def head_finish(h, ob):
        den = ob[:, V_HEAD_DIM:V_HEAD_DIM + 1]
        o = ob[:tq, :V_HEAD_DIM] * (1.0 / den[:tq]) - ob[tq:, :V_HEAD_DIM] * (lam / den[tq:])
        o = _rms(o, sub_g) * (1.0 - LAMBDA_INIT)
        sga = qst_ref[:, D + h * LANES:D + (h + 1) * LANES].astype(F32)
        tb = qst_ref[:, 2 * D + h * LANES:2 * D + (h + 1) * LANES].astype(F32)
        merged_s[:, head_cols(h)] = (sga * o + tb).astype(BF)

    for g0 in range(0, N_HEADS, heads_per_group):
        group = range(g0, g0 + heads_per_group)
        scores = [head_scores(h) for h in group]
        maxes = [row_max(s) for s in scores]
        outs = [head_pv(h, s, mx) for h, s, mx in zip(group, scores, maxes)]
        for h, ob in zip(group, outs):
            head_finish(h, ob)

    m = m_ref[pl.ds(row_of_req(pl.program_id(0)), 1), :]
    gate1, shift2, scale2 = m[:, 2 * D:3 * D], m[:, 3 * D:4 * D], m[:, 4 * D:5 * D]
    x1 = x_ref[...] + gate1 * _dot(merged_s[...], wo_ref[...])
    x1_out[...] = x1
    h2 = _rms(x1, n2g) * (1.0 + scale2) + shift2
    h2_out[...] = h2.astype(BF)
    hi, lo = _split_bf16(h2)
    wr_hi, wr_lo = wr_ref[:, 0:LANES], wr_ref[:, LANES:2 * LANES]
    lg_out[...] = _dot(hi, wr_hi) + _dot(lo, wr_hi) + _dot(hi, wr_lo)


def _attn_call(qst, kt, v, ctx, lamv, x, m, row_of_req, consts, n_req, n_tok):
    tq = min(n_tok, ATTN_ROWS)
    nqb = n_tok // tq
    key_blocks = kt.shape[0] // n_req
    has_ctx = ctx is not None
    tok = pl.BlockSpec((tq, D), lambda b, i: (b * nqb + i, 0))
    in_specs = [pl.BlockSpec((tq, 3 * D), lambda b, i: (b * nqb + i, 0)),
                pl.BlockSpec((key_blocks, D, kt.shape[2]), lambda b, i: (b, 0, 0)),
                pl.BlockSpec((n_tok, D), lambda b, i: (b, 0))]
    args = [qst, kt, v]
    if has_ctx:
        n_ctx = ctx[0].shape[2]
        in_specs += [pl.BlockSpec((1, D, n_ctx), lambda b, i: (b, 0, 0)),
                     pl.BlockSpec((n_ctx, D), lambda b, i: (b, 0))]
        args += list(ctx)
    in_specs += [
        _const_spec((MOD_ROWS, LANES)), _const_spec((MOD_ROWS, D)),
        tok,
        _const_spec((MOD_ROWS, 6 * D)),
        _const_spec((D, D)),
        _const_spec((D, 2 * LANES)),
    ]
    args += [lamv, consts["vecs_attn"], x, m, consts["w_out"], consts["wr"]]
    t = n_req * n_tok
    n_keys = n_tok + (ctx[0].shape[2] if has_ctx else 0)
    heads_per_group = max(1, min(N_HEADS, SCORE_GROUP_ELEMS // (2 * tq * n_keys)))
    while N_HEADS % heads_per_group:
        heads_per_group -= 1
    return pl.pallas_call(
        functools.partial(_attn_kernel, has_ctx, heads_per_group, row_of_req),
        grid=(n_req, nqb),
        in_specs=in_specs,
        out_specs=[tok, tok, pl.BlockSpec((tq, LANES), lambda b, i: (b * nqb + i, 0))],
        out_shape=[
            jax.ShapeDtypeStruct((t, D), F32),
            jax.ShapeDtypeStruct((t, D), BF),
            jax.ShapeDtypeStruct((t, LANES), F32),
        ],
        scratch_shapes=[pltpu.VMEM((tq, D), BF)],
        compiler_params=_params(2),
        name="attn_ctx" if has_ctx else "attn",
    )(*args)


GATHER_ROWS = 512
KEY_BITS = 31
ROUTE_UNROLL_ELEMS = 1 << 22
ROUTE_STEP_TOKENS = 2048


def _route_kernel(cap, n_tok, unroll, lg_ref, h2_ref, before_ref, xe_ref, gate_ref, slot_ref,
                  aff_t_s, slot_t_s, p_s):
    reqs = range(lg_ref.shape[0] // n_tok)
    lane = lax.broadcasted_iota(jnp.int32, (1, LANES), 1)
    valid = lane < N_EXPERTS

    def rows(r):
        return slice(r * n_tok, (r + 1) * n_tok)

    def affinity(r):
        lg = jnp.where(valid, lg_ref[rows(r), :], -1e30)
        ex = jnp.where(valid, jnp.exp(lg - jnp.max(lg, axis=-1, keepdims=True)), 0.0)
        return ex / jnp.sum(ex, axis=-1, keepdims=True)

    def count(mask):
        return jnp.sum(jnp.where(mask, 1.0, 0.0), axis=0, keepdims=True)

    affs = [affinity(r) for r in reqs]
    kth_bits = [jnp.zeros((1, LANES), jnp.int32) for _ in reqs]
    for bit in range(KEY_BITS - 1, -1, -1):
        for r in reqs:
            cand = kth_bits[r] | (1 << bit)
            enough = count(affs[r] >= lax.bitcast_convert_type(cand, F32)) >= cap
            kth_bits[r] = jnp.where(enough, cand, kth_bits[r])

    before = before_ref[...]
    for r in reqs:
        aff = affs[r]
        kth = lax.bitcast_convert_type(kth_bits[r], F32)
        above = aff > kth
        tied = aff == kth
        need = cap - count(above)
        tied_before = _dot(before, jnp.where(tied, 1.0, 0.0).astype(BF))
        chosen = jnp.where(above, 1.0, jnp.where(tied, jnp.where(tied_before < need, 1.0, 0.0), 0.0))
        slot = _dot(before, chosen.astype(BF))
        slot = jnp.where(valid, jnp.where(chosen > 0.0, slot, float(cap)), float(cap))
        slot_ref[rows(r), :] = slot
        slot_t_s[r] = slot.T
        aff_t_s[r] = aff.T

    slot_ids = lax.broadcasted_iota(jnp.int32, (cap, 1), 0).astype(F32)

    def per_expert(e, carry):
        for r in reqs:
            hit = slot_t_s[r, pl.ds(e, 1), :] == slot_ids
            p_s[r, pl.ds(pl.multiple_of(e * cap, cap), cap), :] = jnp.where(hit, 1.0, 0.0).astype(BF)
            gate = jnp.sum(jnp.where(hit, aff_t_s[r, pl.ds(e, 1), :], 0.0), axis=-1, keepdims=True)
            gate_ref[e, r * cap:(r + 1) * cap, :] = gate
        return carry

    lax.fori_loop(0, N_EXPERTS, per_expert, 0, unroll=unroll)
    n_rows = min(GATHER_ROWS, N_EXPERTS * cap)
    e_per = n_rows // cap
    for r in reqs:
        h2 = h2_ref[rows(r), :]
        for t in range(N_EXPERTS * cap // n_rows):
            xe = _dot(p_s[r, t * n_rows:(t + 1) * n_rows, :], h2).astype(BF)
            xe_ref[t * e_per:(t + 1) * e_per, r * cap:(r + 1) * cap, :] = xe.reshape(e_per, cap, D)


def _capacity(n_tok):
    return max(1, CAPACITY_FACTOR * n_tok // N_EXPERTS)


def _route_call(lg, h2, n_req, n_tok):
    cap = _capacity(n_tok)
    per_step = max(1, min(n_req, ROUTE_STEP_TOKENS // n_tok))
    before = jnp.asarray((np.arange(n_tok)[None, :] < np.arange(n_tok)[:, None]).astype(BF))
    unroll = max(1, min(N_EXPERTS, ROUTE_UNROLL_ELEMS // (per_step * cap * n_tok)))
    return pl.pallas_call(
        functools.partial(_route_kernel, cap, n_tok, unroll),
        grid=(n_req // per_step,),
        in_specs=[
            pl.BlockSpec((per_step * n_tok, LANES), lambda b: (b, 0)),
            pl.BlockSpec((per_step * n_tok, D), lambda b: (b, 0)),
            _const_spec((n_tok, n_tok)),
        ],
        out_specs=[
            pl.BlockSpec((N_EXPERTS, per_step * cap, D), lambda b: (0, b, 0)),
            pl.BlockSpec((N_EXPERTS, per_step * cap, 1), lambda b: (0, b, 0)),
            pl.BlockSpec((per_step * n_tok, LANES), lambda b: (b, 0)),
        ],
        out_shape=[
            jax.ShapeDtypeStruct((N_EXPERTS, n_req * cap, D), BF),
            jax.ShapeDtypeStruct((N_EXPERTS, n_req * cap, 1), F32),
            jax.ShapeDtypeStruct((n_req * n_tok, LANES), F32),
        ],
        scratch_shapes=[
            pltpu.VMEM((per_step, LANES, n_tok), F32),
            pltpu.VMEM((per_step, LANES, n_tok), F32),
            pltpu.VMEM((per_step, N_EXPERTS * cap, n_tok), BF),
        ],
        compiler_params=_params(1),
        name="route",
    )(lg, h2, before)


EXPERT_BLOCK = 1024
EXPERT_SUB = 256


def _experts_kernel(xa_ref, xb_ref, ga_ref, gb_ref, wg_ref, wu_ref, wd_ref, ya_ref, yb_ref, acc_s):
    f = pl.program_id(1)
    ra = xa_ref.shape[1]
    x = jnp.concatenate([xa_ref[0], xb_ref[0]], axis=0)
    for c in range(wg_ref.shape[2] // EXPERT_SUB):
        cs = slice(c * EXPERT_SUB, (c + 1) * EXPERT_SUB)
        gate = _dot(x, wg_ref[0, :, cs].astype(BF))
        up = _dot(x, wu_ref[0, :, cs].astype(BF))
        hid = (gate * _sigmoid(gate) * up).astype(BF)
        down = _dot(hid, wd_ref[0, cs, :].astype(BF))
        if c == 0:
            acc_s[...] = jnp.where(f == 0, 0.0, acc_s[...]) + down
        else:
            acc_s[...] += down

    @pl.when(f == pl.num_programs(1) - 1)
    def _():
        ya_ref[0] = (acc_s[0:ra, :] * ga_ref[0]).astype(ya_ref.dtype)
        yb_ref[0] = (acc_s[ra:, :] * gb_ref[0]).astype(yb_ref.dtype)


def _experts_call(xa, xb, ga, gb, wg, wu, wd):
    ra, rb = xa.shape[1], xb.shape[1]
    tf = EXPERT_BLOCK
    xa_spec = pl.BlockSpec((1, ra, D), lambda e, f: (e, 0, 0))
    xb_spec = pl.BlockSpec((1, rb, D), lambda e, f: (e, 0, 0))
    return pl.pallas_call(
        _experts_kernel,
        grid=(N_EXPERTS, D_EXPERT // tf),
        in_specs=[
            xa_spec, xb_spec,
            pl.BlockSpec((1, ra, 1), lambda e, f: (e, 0, 0)),
            pl.BlockSpec((1, rb, 1), lambda e, f: (e, 0, 0)),
            pl.BlockSpec((1, D, tf), lambda e, f: (e, 0, f)),
            pl.BlockSpec((1, D, tf), lambda e, f: (e, 0, f)),
            pl.BlockSpec((1, tf, D), lambda e, f: (e, f, 0)),
        ],
        out_specs=[xa_spec, xb_spec],
        out_shape=[
            jax.ShapeDtypeStruct((N_EXPERTS, ra, D), BF),
            jax.ShapeDtypeStruct((N_EXPERTS, rb, D), BF),
        ],
        scratch_shapes=[pltpu.VMEM((ra + rb, D), F32)],
        compiler_params=_params(2),
        name="experts",
    )(xa, xb, ga, gb, wg, wu, wd)


def _scatter_kernel(cap, row_of_req, slot_ref, y_ref, x1_ref, m_ref, expand_ref, out_ref):
    m = m_ref[pl.ds(row_of_req(pl.program_id(0)), 1), :]
    gate2 = m[:, 5 * D:6 * D]
    slot_wide = _dot(slot_ref[...].astype(BF), expand_ref[...])
    slot_ids = (lax.broadcasted_iota(jnp.int32, (1, N_EXPERTS * cap), 1) % cap).astype(F32)
    onehot = jnp.where(slot_wide == slot_ids, 1.0, 0.0).astype(BF)
    y = y_ref[...].reshape(N_EXPERTS * cap, D)
    out_ref[...] = x1_ref[...] + gate2 * _dot(onehot, y)


def _scatter_call(slots, y, x1, m, row_of_req, n_req, n_tok):
    cap = _capacity(n_tok)
    expand = jnp.asarray((np.arange(LANES)[:, None] == np.arange(N_EXPERTS * cap)[None, :] // cap).astype(BF))
    tok = pl.BlockSpec((n_tok, D), lambda b: (b, 0))
    return pl.pallas_call(
        functools.partial(_scatter_kernel, cap, row_of_req),
        grid=(n_req,),
        in_specs=[
            pl.BlockSpec((n_tok, LANES), lambda b: (b, 0)),
            pl.BlockSpec((N_EXPERTS, cap, D), lambda b: (0, b, 0)),
            tok,
            _const_spec((MOD_ROWS, 6 * D)),
            _const_spec((LANES, N_EXPERTS * cap)),
        ],
        out_specs=tok,
        out_shape=jax.ShapeDtypeStruct((n_req * n_tok, D), F32),
        compiler_params=_params(1),
        name="scatter",
    )(slots, y, x1, m, expand)


def _rope_tables(n_tokens):
    rows = n_tokens // GRID_W
    row = np.broadcast_to(np.arange(rows, dtype=np.float32)[:, None], (rows, GRID_W)).reshape(-1)
    col = np.broadcast_to(np.arange(GRID_W, dtype=np.float32)[None, :], (rows, GRID_W)).reshape(-1)
    half = HEAD_DIM // 4
    inv_freq = (np.float32(ROPE_BASE) ** (-np.arange(half, dtype=np.float32) / np.float32(half))).astype(np.float32)
    ar = row[:, None] * inv_freq
    ac = col[:, None] * inv_freq
    ang = np.concatenate([ar, ar, ac, ac], axis=-1).astype(np.float64)
    cos, sin = np.cos(ang).astype(np.float32), np.sin(ang).astype(np.float32)
    first_half = (np.arange(HEAD_DIM) % (HEAD_DIM // 2)) < (HEAD_DIM // 4)
    sin_signed = np.where(first_half[None, :], -sin, sin)
    reps = LANES // HEAD_DIM
    token_major = np.concatenate([np.tile(cos, (1, reps)), np.tile(sin_signed, (1, reps))], axis=1)
    transposed = np.concatenate([cos.T, sin_signed.T], axis=0)
    return jnp.asarray(token_major), jnp.asarray(transposed)


def kernel(x_prompt, x_sample, cache_k, cache_v, c, c_ctx, w_ada, b_ada, norm1_g, norm2_g, w_in, q_norm_g, k_norm_g, lambda_q1, lambda_k1, lambda_q2, lambda_k2, subln_g, gmlp_ln_g, gmlp_ln_b, w_spatial, b_spatial, w_out, w_router, w_gate_e, w_up_e, w_down_e):
    n_p, t_p = x_prompt.shape[0], x_prompt.shape[1]
    n_s, t_s = x_sample.shape[0], x_sample.shape[1]
    n_ctx = cache_k.shape[2]
    l = 0

    m, lamv, w_in_bf, wkt, w_out_bf, wsp_pairs = _prep_call(
        c_ctx[None, :], c, w_ada[l], b_ada[l][None, :], lambda_q1[l][None, :], lambda_k1[l][None, :],
        lambda_q2[l][None, :], lambda_k2[l][None, :], w_in[l], w_out[l], w_spatial[l])

    gsum_np = (np.arange(D)[:, None] // HEAD_DIM == np.arange(LANES)[None, :]).astype(BF)
    gsum = jnp.asarray(gsum_np)
    wr = jnp.pad(w_router[l], ((0, 0), (0, LANES - N_EXPERTS)))
    wr_hi = wr.astype(BF)
    qg = jnp.tile(q_norm_g[l] * (HEAD_DIM ** -0.5 * math.log2(math.e)), D // HEAD_DIM)

    def rows_of(*vectors):
        rows = [jnp.tile(vec, D // vec.shape[0])[None, :] for vec in vectors]
        return jnp.concatenate(rows + [jnp.zeros((MOD_ROWS - len(rows), D), F32)], axis=0)

    consts = {
        "vecs_in": rows_of(norm1_g[l], qg, gmlp_ln_g[l], gmlp_ln_b[l]),
        "vecs_attn": rows_of(norm2_g[l], subln_g[l]),
        "w_in": w_in_bf,
        "wkt": wkt,
        "kgt": jnp.broadcast_to(k_norm_g[l][:, None], (HEAD_DIM, INPROJ_ROWS)),
        "gsum": gsum,
        "gbcast": jnp.asarray(np.concatenate([gsum_np.T, gsum_np.T], axis=0)),
        "wsp": wsp_pairs,
        "bsp": jnp.repeat(b_spatial[l].T, D // N_GROUPS, axis=1),
        "w_out": w_out_bf,
        "wr": jnp.concatenate([wr_hi, (wr - wr_hi.astype(F32)).astype(BF)], axis=1),
    }

    passes = (
        (x_prompt, n_p, t_p, lambda b: 0, False),
        (x_sample, n_s, t_s, lambda b: 1 + b, True),
    )
    outs = []
    kv_out = None
    for x, n_req, n_tok, row_of_req, is_sample in passes:
        xf = x.reshape(n_req * n_tok, D)
        tabs = _rope_tables(n_tok) if is_sample else None
        qst, kt, v = _inproj_call(xf, m, row_of_req, tabs, BF if is_sample else F32, consts, n_req, n_tok)
        if is_sample:
            ctx = (jnp.transpose(cache_k[:, l], (0, 2, 3, 4, 1)).reshape(n_req, D, n_ctx),
                   cache_v[:, l].reshape(n_req * n_ctx, D))
        else:
            ctx = None
            kv_out = (kt, v)
        x1, h2, lg = _attn_call(qst, kt, v, ctx, lamv, xf, m, row_of_req, consts, n_req, n_tok)
        xe, gates, slots = _route_call(lg, h2, n_req, n_tok)
        outs.append((x1, xe, gates, slots, row_of_req))

    (x1p, xep, gp, slots_p, row_p), (x1s, xes, gs, slots_s, row_s) = outs
    yp, ys = _experts_call(xep, xes, gp, gs, w_gate_e[l], w_up_e[l], w_down_e[l])
    y_prompt = _scatter_call(slots_p, yp, x1p, m, row_p, n_p, t_p).reshape(x_prompt.shape)
    y_sample = _scatter_call(slots_s, ys, x1s, m, row_s, n_s, t_s).reshape(x_sample.shape)
    new_k = jnp.transpose(kv_out[0].reshape(n_p, N_HEADS, 2, HEAD_DIM, t_p), (0, 4, 1, 2, 3))
    new_k = new_k.reshape(n_p, 1, t_p, N_HEADS, 2, HEAD_DIM)
    new_v = kv_out[1].reshape(n_p, 1, t_p, N_HEADS, V_HEAD_DIM)
    return (y_prompt, y_sample, new_k, new_v)
```

```python
import functools
import math

import jax
import jax.numpy as jnp
import numpy as np
from jax import lax
from jax.experimental import pallas as pl
from jax.experimental.pallas import tpu as pltpu

D = 1024
N_HEADS = 8
HEAD_DIM = 64
V_HEAD_DIM = 128
GRID_W = 64
ROPE_BASE = 10000.0
CHUNK = 128
N_GROUPS = 8
N_EXPERTS = 16
CAPACITY_FACTOR = 2
D_EXPERT = 2048
N_SEG = 7
K_SEG = 1
EPS = 1e-6
LAMBDA_INIT = 0.8 - 0.6 * math.exp(-0.3 * 0)

LANES = 128
MOD_ROWS = 8
VMEM_LIMIT = 56 * 1024 * 1024

BF = jnp.bfloat16
F32 = jnp.float32


def _dot(a, b):
    return jnp.dot(a, b, preferred_element_type=F32)


def _dot_nt(a, b):
    return lax.dot_general(a, b, (((1,), (1,)), ((), ())), preferred_element_type=F32)


def _split_bf16(x):
    hi = x.astype(BF)
    lo = (x - hi.astype(F32)).astype(BF)
    return hi, lo


def _sigmoid(x):
    return 0.5 * jnp.tanh(0.5 * x) + 0.5


def _rms(x, g):
    return x * lax.rsqrt(jnp.mean(x * x, axis=-1, keepdims=True) + EPS) * g


def _params(n_grid_dims):
    return pltpu.CompilerParams(
        dimension_semantics=("arbitrary",) * n_grid_dims, vmem_limit_bytes=VMEM_LIMIT)


def _const_spec(shape):
    nd = len(shape)
    return pl.BlockSpec(shape, lambda *_: (0,) * nd)


N_MOD = 6


def _prep_kernel(cctx_ref, c_ref, wa_ref, ba_ref, lq1_ref, lk1_ref, lq2_ref, lk2_ref, wi_ref, wo_ref, wsp_ref,
                 m_ref, lam_ref, wib_ref, wkt_ref, wob_ref, wsp2_ref, cond_s):
    j = pl.program_id(0)
    n_lat = c_ref.shape[0]

    @pl.when(j == 0)
    def _():
        cond_s[...] = jnp.zeros(cond_s.shape, F32)
        cond_s[0:1, :] = cctx_ref[...]
        cond_s[1:1 + n_lat, :] = c_ref[...]
        s1 = jnp.sum(lq1_ref[...] * lk1_ref[...], axis=-1, keepdims=True)
        s2 = jnp.sum(lq2_ref[...] * lk2_ref[...], axis=-1, keepdims=True)
        lam_ref[...] = jnp.broadcast_to(jnp.exp(s1) - jnp.exp(s2) + LAMBDA_INIT, lam_ref.shape)
        wob_ref[...] = wo_ref[...].astype(BF)
        for p in range(N_GROUPS // 2):
            wsp2_ref[p] = jnp.concatenate([wsp_ref[2 * p], wsp_ref[2 * p + 1]], axis=1).astype(BF)

    @pl.when(j < N_MOD)
    def _():
        c = cond_s[...]
        a_hi, a_lo = _split_bf16(c * jax.nn.sigmoid(c))
        w_hi, w_lo = _split_bf16(wa_ref[...])
        m_ref[...] = _dot(a_hi, w_hi) + _dot(a_lo, w_hi) + _dot(a_hi, w_lo) + ba_ref[...]

    w = wi_ref[...]
    wib_ref[...] = w.astype(BF)

    @pl.when(j == K_SEG)
    def _():
        wkt_ref[...] = w.T.astype(BF)


def _prep_call(c_ctx, c, w_ada, b_ada, lq1, lk1, lq2, lk2, w_in, w_out, w_spatial):
    vec = _const_spec((1, HEAD_DIM))
    mod_block = lambda j: (0, jnp.minimum(j, N_MOD - 1))
    pairs = (N_GROUPS // 2, CHUNK, 2 * CHUNK)
    return pl.pallas_call(
        _prep_kernel,
        grid=(N_SEG,),
        in_specs=[
            _const_spec((1, D)), _const_spec(c.shape),
            pl.BlockSpec((D, D), mod_block),
            pl.BlockSpec((1, D), mod_block),
            vec, vec, vec, vec,
            pl.BlockSpec((D, D), lambda j: (0, j)),
            _const_spec((D, D)),
            _const_spec((N_GROUPS, CHUNK, CHUNK)),
        ],
        out_specs=[
            pl.BlockSpec((MOD_ROWS, D), mod_block),
            _const_spec((MOD_ROWS, LANES)),
            pl.BlockSpec((D, D), lambda j: (0, j)),
            _const_spec((D, D)),
            _const_spec((D, D)),
            _const_spec(pairs),
        ],
        out_shape=[
            jax.ShapeDtypeStruct((MOD_ROWS, N_MOD * D), F32),
            jax.ShapeDtypeStruct((MOD_ROWS, LANES), F32),
            jax.ShapeDtypeStruct((D, N_SEG * D), BF),
            jax.ShapeDtypeStruct((D, D), BF),
            jax.ShapeDtypeStruct((D, D), BF),
            jax.ShapeDtypeStruct(pairs, BF),
        ],
        scratch_shapes=[pltpu.VMEM((MOD_ROWS, D), F32)],
        compiler_params=_params(1),
        name="prep",
    )(c_ctx, c, w_ada, b_ada, lq1, lk1, lq2, lk2, w_in, w_out, w_spatial)


def _head_norm(t, g, gsum, gbcast):
    ss = _dot((t * t).astype(BF), gsum)
    inv = lax.rsqrt(ss * (1.0 / HEAD_DIM) + EPS)
    hi, lo = _split_bf16(inv)
    bc = _dot(jnp.concatenate([hi, lo], axis=-1), gbcast)
    return t * bc * g


def _lane_tile(tab):
    return jnp.concatenate([tab] * (D // LANES), axis=1)


def _rope(t, cos, sin_signed):
    q = HEAD_DIM // 4
    lane = lax.broadcasted_iota(jnp.int32, (1, D), 1)
    first_half = (lane % (HEAD_DIM // 2)) < q
    fwd = pltpu.roll(t, D - q, axis=1)
    bwd = pltpu.roll(t, q, axis=1)
    return t * _lane_tile(cos) + jnp.where(first_half, fwd, bwd) * _lane_tile(sin_signed)


def _rope_t(t3, cos_t, sin_signed_t):
    q = HEAD_DIM // 4
    rot = jnp.concatenate([t3[:, q:2 * q], t3[:, 0:q], t3[:, 3 * q:4 * q], t3[:, 2 * q:3 * q]], axis=1)
    return t3 * cos_t[None] + rot * sin_signed_t[None]


def _inproj_kernel(rope, row_of_step, x_ref, m_ref, vecs_ref, w_ref, wkt_ref, kgt_ref, gsum_ref,
                   gbcast_ref, wsp_ref, bsp_ref, *rest):
    if rope:
        tab_ref, tab_t_ref, qst_out, kt_out, v_out = rest
    else:
        qst_out, kt_out, v_out = rest
    tm = x_ref.shape[0]
    m = m_ref[pl.ds(row_of_step(pl.program_id(0)), 1), :]
    shift1, scale1 = m[:, 0:D], m[:, D:2 * D]
    n1g, qg, lng, lnb = (vecs_ref[r:r + 1, :] for r in range(4))
    h = (_rms(x_ref[...], n1g) * (1.0 + scale1) + shift1).astype(BF)

    def seg(j):
        return _dot(h, w_ref[:, j * D:(j + 1) * D])

    q = _head_norm(seg(0), qg, gsum_ref[...], gbcast_ref[...])
    if rope:
        q = _rope(q, tab_ref[:, 0:LANES], tab_ref[:, LANES:2 * LANES])
    qst_out[:, 0:D] = q.astype(qst_out.dtype)

    k3 = _dot_nt(wkt_ref[...], h).reshape(D // HEAD_DIM, HEAD_DIM, tm)
    inv = lax.rsqrt(jnp.mean(k3 * k3, axis=1, keepdims=True) + EPS)
    k3 = k3 * inv * kgt_ref[...][None]
    if rope:
        k3 = _rope_t(k3, tab_t_ref[0:HEAD_DIM, :], tab_t_ref[HEAD_DIM:2 * HEAD_DIM, :])
    kt = k3.reshape(D, tm).astype(kt_out.dtype)
    width = kt_out.shape[2]
    for r in range(kt_out.shape[0]):
        kt_out[r] = kt[:, r * width:(r + 1) * width]

    v_out[...] = seg(2).astype(v_out.dtype)
    qst_out[:, D:2 * D] = _sigmoid(seg(5)).astype(qst_out.dtype)

    zv = jax.nn.gelu(seg(4))
    mu = jnp.mean(zv, axis=-1, keepdims=True)
    zc = zv - mu
    var = jnp.mean(zc * zc, axis=-1, keepdims=True)
    zvn = (zc * lax.rsqrt(var + EPS) * lng + lnb).astype(BF)
    pre = jax.nn.gelu(seg(3)) * _sigmoid(seg(6))
    blank = jnp.zeros((CHUNK, LANES), BF)
    for c in range(tm // CHUNK):
        rows = slice(c * CHUNK, (c + 1) * CHUNK)
        for p in range(N_GROUPS // 2):
            cols = slice(2 * p * LANES, (2 * p + 2) * LANES)
            z = zvn[rows, cols]
            z_diag = jnp.concatenate([jnp.concatenate([z[:, :LANES], blank], axis=1),
                                      jnp.concatenate([blank, z[:, LANES:]], axis=1)], axis=0)
            sp = _dot(wsp_ref[p], z_diag) + bsp_ref[:, cols]
            out_cols = slice(2 * D + 2 * p * LANES, 2 * D + (2 * p + 2) * LANES)
            qst_out[rows, out_cols] = (pre[rows, cols] * sp).astype(qst_out.dtype)


INPROJ_ROWS = 512


def _inproj_call(x, m, row_of_req, rope_tabs, kv_dtype, consts, n_req, n_tok):
    t = x.shape[0]
    tm = INPROJ_ROWS
    per_req = max(1, n_tok // tm)
    per_tile = max(1, tm // n_tok)
    rope = rope_tabs is not None

    def row_of_step(i):
        return row_of_req(i * per_tile // per_req)

    tok = pl.BlockSpec((tm, D), lambda i: (i, 0))
    in_specs = [
        tok,
        _const_spec((MOD_ROWS, 6 * D)),
        _const_spec((MOD_ROWS, D)),
        pl.BlockSpec((D, N_SEG * D), lambda i: (0, 0), pipeline_mode=pl.Buffered(1)),
        _const_spec((D, D)),
        _const_spec((HEAD_DIM, tm)),
        _const_spec((D, LANES)), _const_spec((2 * LANES, D)),
        _const_spec((N_GROUPS // 2, CHUNK, 2 * CHUNK)), _const_spec((CHUNK, D)),
    ]
    args = [x, m, consts["vecs_in"], consts["w_in"], consts["wkt"], consts["kgt"], consts["gsum"],
            consts["gbcast"], consts["wsp"], consts["bsp"]]
    if rope:
        in_specs += [pl.BlockSpec((tm, 2 * LANES), lambda i: (i % per_req, 0)),
                     pl.BlockSpec((2 * HEAD_DIM, tm), lambda i: (0, i % per_req))]
        args += list(rope_tabs)
    kt_spec = pl.BlockSpec((per_tile, D, tm // per_tile), lambda i: (i, 0, 0))
    return pl.pallas_call(
        functools.partial(_inproj_kernel, rope, row_of_step),
        grid=(t // tm,),
        in_specs=in_specs,
        out_specs=[pl.BlockSpec((tm, 3 * D), lambda i: (i, 0)), kt_spec, tok],
        out_shape=[
            jax.ShapeDtypeStruct((t, 3 * D), BF),
            jax.ShapeDtypeStruct((n_req * per_req, D, tm // per_tile), kv_dtype),
            jax.ShapeDtypeStruct((t, D), kv_dtype),
        ],
        compiler_params=_params(1),
        name="inproj_rope" if rope else "inproj",
    )(*args)


SCORE_GROUP_ELEMS = 1 << 22
ATTN_ROWS = 256


def _attn_kernel(has_ctx, heads_per_group, row_of_req, qst_ref, kt_ref, v_ref, *rest):
    if has_ctx:
        kct_ref, vc_ref, *rest = rest
    lam_ref, vecs_ref, x_ref, m_ref, wo_ref, wr_ref, x1_out, h2_out, lg_out, merged_s = rest
    tq = qst_ref.shape[0]
    key_blocks, block_keys = kt_ref.shape[0], kt_ref.shape[2]
    n2g = vecs_ref[0:1, :]
    sub_g = vecs_ref[1:2, 0:V_HEAD_DIM]
    lam = lam_ref[0:1, 0:1]
    lane = lax.broadcasted_iota(jnp.int32, (1, LANES), 1)
    first = lane < HEAD_DIM
    zero = jnp.zeros((), BF)
    ones_col = jnp.where(lane == 0, 1.0, 0.0).astype(BF)

    def values(ref, rows, h):
        v = ref[rows, h * V_HEAD_DIM:(h + 1) * V_HEAD_DIM].astype(BF)
        return jnp.concatenate([v, jnp.broadcast_to(ones_col, (v.shape[0], LANES))], axis=1)

    def head_cols(h):
        return slice(h * LANES, (h + 1) * LANES)

    def head_scores(h):
        q = qst_ref[:, head_cols(h)]
        qz = jnp.concatenate([jnp.where(first, q, zero), jnp.where(first, zero, q)], axis=0)
        parts = [_dot(qz, kt_ref[j, head_cols(h), :].astype(BF)) for j in range(key_blocks)]
        if has_ctx:
            parts.append(_dot(qz, kct_ref[0, head_cols(h), :].astype(BF)))
        return parts

    def row_max(scores):
        return functools.reduce(jnp.maximum, [jnp.max(s, axis=-1, keepdims=True) for s in scores])

    def head_pv(h, scores, mx):
        vals = [values(v_ref, slice(j * block_keys, (j + 1) * block_keys), h) for j in range(key_blocks)]
        if has_ctx:
            vals.append(values(vc_ref, slice(None), h))
        return functools.reduce(jnp.add, [_dot(jnp.exp2(s - mx).astype(BF), v) for s, v in zip(scores, vals)])

    def head_finish(h, ob):
        den = ob[:, V_HEAD_DIM:V_HEAD_DIM + 1]
        o = ob[:tq, :V_HEAD_DIM] * (1.0 / den[:tq]) - ob[tq:, :V_HEAD_DIM] * (lam / den[tq:])
        o = _rms(o, sub_g) * (1.0 - LAMBDA_INIT)
        sga = qst_ref[:, D + h * LANES:D + (h + 1) * LANES].astype(F32)
        tb = qst_ref[:, 2 * D + h * LANES:2 * D + (h + 1) * LANES].astype(F32)
        merged_s[:, head_cols(h)] = (sga * o + tb).astype(BF)

    for g0 in range(0, N_HEADS, heads_per_group):
        group = range(g0, g0 + heads_per_group)
        scores = [head_scores(h) for h in group]
        maxes = [row_max(s) for s in scores]
        outs = [head_pv(h, s, mx) for h, s, mx in zip(group, scores, maxes)]
        for h, ob in zip(group, outs):
            head_finish(h, ob)

    m = m_ref[pl.ds(row_of_req(pl.program_id(0)), 1), :]
    gate1, shift2, scale2 = m[:, 2 * D:3 * D], m[:, 3 * D:4 * D], m[:, 4 * D:5 * D]
    x1 = x_ref[...] + gate1 * _dot(merged_s[...], wo_ref[...])
    x1_out[...] = x1
    h2 = _rms(x1, n2g) * (1.0 + scale2) + shift2
    h2_out[...] = h2.astype(BF)
    hi, lo = _split_bf16(h2)
    wr_hi, wr_lo = wr_ref[:, 0:LANES], wr_ref[:, LANES:2 * LANES]
    lg_out[...] = _dot(hi, wr_hi) + _dot(lo, wr_hi) + _dot(hi, wr_lo)


def _attn_call(qst, kt, v, ctx, lamv, x, m, row_of_req, consts, n_req, n_tok):
    tq = min(n_tok, ATTN_ROWS)
    nqb = n_tok // tq
    key_blocks = kt.shape[0] // n_req
    has_ctx = ctx is not None
    tok = pl.BlockSpec((tq, D), lambda b, i: (b * nqb + i, 0))
    in_specs = [pl.BlockSpec((tq, 3 * D), lambda b, i: (b * nqb + i, 0)),
                pl.BlockSpec((key_blocks, D, kt.shape[2]), lambda b, i: (b, 0, 0)),
                pl.BlockSpec((n_tok, D), lambda b, i: (b, 0))]
    args = [qst, kt, v]
    if has_ctx:
        n_ctx = ctx[0].shape[2]
        in_specs += [pl.BlockSpec((1, D, n_ctx), lambda b, i: (b, 0, 0)),
                     pl.BlockSpec((n_ctx, D), lambda b, i: (b, 0))]
        args += list(ctx)
    in_specs += [
        _const_spec((MOD_ROWS, LANES)), _const_spec((MOD_ROWS, D)),
        tok,
        _const_spec((MOD_ROWS, 6 * D)),
        _const_spec((D, D)),
        _const_spec((D, 2 * LANES)),
    ]
    args += [lamv, consts["vecs_attn"], x, m, consts["w_out"], consts["wr"]]
    t = n_req * n_tok
    n_keys = n_tok + (ctx[0].shape[2] if has_ctx else 0)
    heads_per_group = max(1, min(N_HEADS, SCORE_GROUP_ELEMS // (2 * tq * n_keys)))
    while N_HEADS % heads_per_group:
        heads_per_group -= 1
    return pl.pallas_call(
        functools.partial(_attn_kernel, has_ctx, heads_per_group, row_of_req),
        grid=(n_req, nqb),
        in_specs=in_specs,
        out_specs=[tok, tok, pl.BlockSpec((tq, LANES), lambda b, i: (b * nqb + i, 0))],
        out_shape=[
            jax.ShapeDtypeStruct((t, D), F32),
            jax.ShapeDtypeStruct((t, D), BF),
            jax.ShapeDtypeStruct((t, LANES), F32),
        ],
        scratch_shapes=[pltpu.VMEM((tq, D), BF)],
        compiler_params=_params(2),
        name="attn_ctx" if has_ctx else "attn",
    )(*args)


GATHER_ROWS = 512
KEY_BITS = 31
ROUTE_UNROLL_ELEMS = 1 << 22
ROUTE_STEP_TOKENS = 2048


def _route_kernel(cap, n_tok, unroll, lg_ref, h2_ref, before_ref, xe_ref, gate_ref, slot_ref,
                  aff_t_s, slot_t_s, p_s):
    reqs = range(lg_ref.shape[0] // n_tok)
    lane = lax.broadcasted_iota(jnp.int32, (1, LANES), 1)
    valid = lane < N_EXPERTS

    def rows(r):
        return slice(r * n_tok, (r + 1) * n_tok)

    def affinity(r):
        lg = jnp.where(valid, lg_ref[rows(r), :], -1e30)
        ex = jnp.where(valid, jnp.exp(lg - jnp.max(lg, axis=-1, keepdims=True)), 0.0)
        return ex / jnp.sum(ex, axis=-1, keepdims=True)

    def count(mask):
        return jnp.sum(jnp.where(mask, 1.0, 0.0), axis=0, keepdims=True)

    affs = [affinity(r) for r in reqs]
    kth_bits = [jnp.zeros((1, LANES), jnp.int32) for _ in reqs]
    for bit in range(KEY_BITS - 1, -1, -1):
        for r in reqs:
            cand = kth_bits[r] | (1 << bit)
            enough = count(affs[r] >= lax.bitcast_convert_type(cand, F32)) >= cap
            kth_bits[r] = jnp.where(enough, cand, kth_bits[r])

    before = before_ref[...]
    for r in reqs:
        aff = affs[r]
        kth = lax.bitcast_convert_type(kth_bits[r], F32)
        above = aff > kth
        tied = aff == kth
        need = cap - count(above)
        tied_before = _dot(before, jnp.where(tied, 1.0, 0.0).astype(BF))
        chosen = jnp.where(above, 1.0, jnp.where(tied, jnp.where(tied_before < need, 1.0, 0.0), 0.0))
        slot = _dot(before, chosen.astype(BF))
        slot = jnp.where(valid, jnp.where(chosen > 0.0, slot, float(cap)), float(cap))
        slot_ref[rows(r), :] = slot
        slot_t_s[r] = slot.T
        aff_t_s[r] = aff.T

    slot_ids = lax.broadcasted_iota(jnp.int32, (cap, 1), 0).astype(F32)

    def per_expert(e, carry):
        for r in reqs:
            hit = slot_t_s[r, pl.ds(e, 1), :] == slot_ids
            p_s[r, pl.ds(pl.multiple_of(e * cap, cap), cap), :] = jnp.where(hit, 1.0, 0.0).astype(BF)
            gate = jnp.sum(jnp.where(hit, aff_t_s[r, pl.ds(e, 1), :], 0.0), axis=-1, keepdims=True)
            gate_ref[e, r * cap:(r + 1) * cap, :] = gate
        return carry

    lax.fori_loop(0, N_EXPERTS, per_expert, 0, unroll=unroll)
    n_rows = min(GATHER_ROWS, N_EXPERTS * cap)
    e_per = n_rows // cap
    for r in reqs:
        h2 = h2_ref[rows(r), :]
        for t in range(N_EXPERTS * cap // n_rows):
            xe = _dot(p_s[r, t * n_rows:(t + 1) * n_rows, :], h2).astype(BF)
            xe_ref[t * e_per:(t + 1) * e_per, r * cap:(r + 1) * cap, :] = xe.reshape(e_per, cap, D)


def _capacity(n_tok):
    return max(1, CAPACITY_FACTOR * n_tok // N_EXPERTS)


def _route_call(lg, h2, n_req, n_tok):
    cap = _capacity(n_tok)
    per_step = max(1, min(n_req, ROUTE_STEP_TOKENS // n_tok))
    before = jnp.asarray((np.arange(n_tok)[None, :] < np.arange(n_tok)[:, None]).astype(BF))
    unroll = max(1, min(N_EXPERTS, ROUTE_UNROLL_ELEMS // (per_step * cap * n_tok)))
    return pl.pallas_call(
        functools.partial(_route_kernel, cap, n_tok, unroll),
        grid=(n_req // per_step,),
        in_specs=[
            pl.BlockSpec((per_step * n_tok, LANES), lambda b: (b, 0)),
            pl.BlockSpec((per_step * n_tok, D), lambda b: (b, 0)),
            _const_spec((n_tok, n_tok)),
        ],
        out_specs=[
            pl.BlockSpec((N_EXPERTS, per_step * cap, D), lambda b: (0, b, 0)),
            pl.BlockSpec((N_EXPERTS, per_step * cap, 1), lambda b: (0, b, 0)),
            pl.BlockSpec((per_step * n_tok, LANES), lambda b: (b, 0)),
        ],
        out_shape=[
            jax.ShapeDtypeStruct((N_EXPERTS, n_req * cap, D), BF),
            jax.ShapeDtypeStruct((N_EXPERTS, n_req * cap, 1), F32),
            jax.ShapeDtypeStruct((n_req * n_tok, LANES), F32),
        ],
        scratch_shapes=[
            pltpu.VMEM((per_step, LANES, n_tok), F32),
            pltpu.VMEM((per_step, LANES, n_tok), F32),
            pltpu.VMEM((per_step, N_EXPERTS * cap, n_tok), BF),
        ],
        compiler_params=_params(1),
        name="route",
    )(lg, h2, before)


EXPERT_BLOCK = 1024
EXPERT_SUB = 256


def _experts_kernel(xa_ref, xb_ref, ga_ref, gb_ref, wg_ref, wu_ref, wd_ref, ya_ref, yb_ref, acc_s):
    f = pl.program_id(1)
    ra = xa_ref.shape[1]
    x = jnp.concatenate([xa_ref[0], xb_ref[0]], axis=0)
    for c in range(wg_ref.shape[2] // EXPERT_SUB):
        cs = slice(c * EXPERT_SUB, (c + 1) * EXPERT_SUB)
        gate = _dot(x, wg_ref[0, :, cs].astype(BF))
        up = _dot(x, wu_ref[0, :, cs].astype(BF))
        hid = (gate * _sigmoid(gate) * up).astype(BF)
        down = _dot(hid, wd_ref[0, cs, :].astype(BF))
        if c == 0:
            acc_s[...] = jnp.where(f == 0, 0.0, acc_s[...]) + down
        else:
            acc_s[...] += down

    @pl.when(f == pl.num_programs(1) - 1)
    def _():
        ya_ref[0] = (acc_s[0:ra, :] * ga_ref[0]).astype(ya_ref.dtype)
        yb_ref[0] = (acc_s[ra:, :] * gb_ref[0]).astype(yb_ref.dtype)


def _experts_call(xa, xb, ga, gb, wg, wu, wd):
    ra, rb = xa.shape[1], xb.shape[1]
    tf = EXPERT_BLOCK
    xa_spec = pl.BlockSpec((1, ra, D), lambda e, f: (e, 0, 0))
    xb_spec = pl.BlockSpec((1, rb, D), lambda e, f: (e, 0, 0))
    return pl.pallas_call(
        _experts_kernel,
        grid=(N_EXPERTS, D_EXPERT // tf),
        in_specs=[
            xa_spec, xb_spec,
            pl.BlockSpec((1, ra, 1), lambda e, f: (e, 0, 0)),
            pl.BlockSpec((1, rb, 1), lambda e, f: (e, 0, 0)),
            pl.BlockSpec((1, D, tf), lambda e, f: (e, 0, f)),
            pl.BlockSpec((1, D, tf), lambda e, f: (e, 0, f)),
            pl.BlockSpec((1, tf, D), lambda e, f: (e, f, 0)),
        ],
        out_specs=[xa_spec, xb_spec],
        out_shape=[
            jax.ShapeDtypeStruct((N_EXPERTS, ra, D), BF),
            jax.ShapeDtypeStruct((N_EXPERTS, rb, D), BF),
        ],
        scratch_shapes=[pltpu.VMEM((ra + rb, D), F32)],
        compiler_params=_params(2),
        name="experts",
    )(xa, xb, ga, gb, wg, wu, wd)


SCATTER_STEP_TOKENS = 1024


def _scatter_kernel(cap, n_tok, row_of_req, slot_ref, y_ref, x1_ref, m_ref, expand_ref, out_ref):
    per_step = slot_ref.shape[0] // n_tok
    slot_ids = (lax.broadcasted_iota(jnp.int32, (1, N_EXPERTS * cap), 1) % cap).astype(F32)
    for r in range(per_step):
        rows = slice(r * n_tok, (r + 1) * n_tok)
        m = m_ref[pl.ds(row_of_req(pl.program_id(0) * per_step + r), 1), :]
        gate2 = m[:, 5 * D:6 * D]
        slot_wide = _dot(slot_ref[rows, :].astype(BF), expand_ref[...])
        onehot = jnp.where(slot_wide == slot_ids, 1.0, 0.0).astype(BF)
        y = y_ref[:, r * cap:(r + 1) * cap, :].reshape(N_EXPERTS * cap, D)
        out_ref[rows, :] = x1_ref[rows, :] + gate2 * _dot(onehot, y)


def _scatter_call(slots, y, x1, m, row_of_req, n_req, n_tok):
    cap = _capacity(n_tok)
    per_step = max(1, min(n_req, SCATTER_STEP_TOKENS // n_tok))
    expand = jnp.asarray((np.arange(LANES)[:, None] == np.arange(N_EXPERTS * cap)[None, :] // cap).astype(BF))
    tok = pl.BlockSpec((per_step * n_tok, D), lambda b: (b, 0))
    return pl.pallas_call(
        functools.partial(_scatter_kernel, cap, n_tok, row_of_req),
        grid=(n_req // per_step,),
        in_specs=[
            pl.BlockSpec((per_step * n_tok, LANES), lambda b: (b, 0)),
            pl.BlockSpec((N_EXPERTS, per_step * cap, D), lambda b: (0, b, 0)),
            tok,
            _const_spec((MOD_ROWS, 6 * D)),
            _const_spec((LANES, N_EXPERTS * cap)),
        ],
        out_specs=tok,
        out_shape=jax.ShapeDtypeStruct((n_req * n_tok, D), F32),
        compiler_params=_params(1),
        name="scatter",
    )(slots, y, x1, m, expand)


def _rope_tables(n_tokens):
    rows = n_tokens // GRID_W
    row = np.broadcast_to(np.arange(rows, dtype=np.float32)[:, None], (rows, GRID_W)).reshape(-1)
    col = np.broadcast_to(np.arange(GRID_W, dtype=np.float32)[None, :], (rows, GRID_W)).reshape(-1)
    half = HEAD_DIM // 4
    inv_freq = (np.float32(ROPE_BASE) ** (-np.arange(half, dtype=np.float32) / np.float32(half))).astype(np.float32)
    ar = row[:, None] * inv_freq
    ac = col[:, None] * inv_freq
    ang = np.concatenate([ar, ar, ac, ac], axis=-1).astype(np.float64)
    cos, sin = np.cos(ang).astype(np.float32), np.sin(ang).astype(np.float32)
    first_half = (np.arange(HEAD_DIM) % (HEAD_DIM // 2)) < (HEAD_DIM // 4)
    sin_signed = np.where(first_half[None, :], -sin, sin)
    reps = LANES // HEAD_DIM
    token_major = np.concatenate([np.tile(cos, (1, reps)), np.tile(sin_signed, (1, reps))], axis=1)
    transposed = np.concatenate([cos.T, sin_signed.T], axis=0)
    return jnp.asarray(token_major), jnp.asarray(transposed)


def kernel(x_prompt, x_sample, cache_k, cache_v, c, c_ctx, w_ada, b_ada, norm1_g, norm2_g, w_in, q_norm_g, k_norm_g, lambda_q1, lambda_k1, lambda_q2, lambda_k2, subln_g, gmlp_ln_g, gmlp_ln_b, w_spatial, b_spatial, w_out, w_router, w_gate_e, w_up_e, w_down_e):
    n_p, t_p = x_prompt.shape[0], x_prompt.shape[1]
    n_s, t_s = x_sample.shape[0], x_sample.shape[1]
    n_ctx = cache_k.shape[2]
    l = 0

    m, lamv, w_in_bf, wkt, w_out_bf, wsp_pairs = _prep_call(
        c_ctx[None, :], c, w_ada[l], b_ada[l][None, :], lambda_q1[l][None, :], lambda_k1[l][None, :],
        lambda_q2[l][None, :], lambda_k2[l][None, :], w_in[l], w_out[l], w_spatial[l])

    gsum_np = (np.arange(D)[:, None] // HEAD_DIM == np.arange(LANES)[None, :]).astype(BF)
    gsum = jnp.asarray(gsum_np)
    wr = jnp.pad(w_router[l], ((0, 0), (0, LANES - N_EXPERTS)))
    wr_hi = wr.astype(BF)
    qg = jnp.tile(q_norm_g[l] * (HEAD_DIM ** -0.5 * math.log2(math.e)), D // HEAD_DIM)

    def rows_of(*vectors):
        rows = [jnp.tile(vec, D // vec.shape[0])[None, :] for vec in vectors]
        return jnp.concatenate(rows + [jnp.zeros((MOD_ROWS - len(rows), D), F32)], axis=0)

    consts = {
        "vecs_in": rows_of(norm1_g[l], qg, gmlp_ln_g[l], gmlp_ln_b[l]),
        "vecs_attn": rows_of(norm2_g[l], subln_g[l]),
        "w_in": w_in_bf,
        "wkt": wkt,
        "kgt": jnp.broadcast_to(k_norm_g[l][:, None], (HEAD_DIM, INPROJ_ROWS)),
        "gsum": gsum,
        "gbcast": jnp.asarray(np.concatenate([gsum_np.T, gsum_np.T], axis=0)),
        "wsp": wsp_pairs,
        "bsp": jnp.repeat(b_spatial[l].T, D // N_GROUPS, axis=1),
        "w_out": w_out_bf,
        "wr": jnp.concatenate([wr_hi, (wr - wr_hi.astype(F32)).astype(BF)], axis=1),
    }

    passes = (
        (x_prompt, n_p, t_p, lambda b: 0, False),
        (x_sample, n_s, t_s, lambda b: 1 + b, True),
    )
    outs = []
    kv_out = None
    for x, n_req, n_tok, row_of_req, is_sample in passes:
        xf = x.reshape(n_req * n_tok, D)
        tabs = _rope_tables(n_tok) if is_sample else None
        qst, kt, v = _inproj_call(xf, m, row_of_req, tabs, BF if is_sample else F32, consts, n_req, n_tok)
        if is_sample:
            ctx = (jnp.transpose(cache_k[:, l], (0, 2, 3, 4, 1)).reshape(n_req, D, n_ctx),
                   cache_v[:, l].reshape(n_req * n_ctx, D))
        else:
            ctx = None
            kv_out = (kt, v)
        x1, h2, lg = _attn_call(qst, kt, v, ctx, lamv, xf, m, row_of_req, consts, n_req, n_tok)
        xe, gates, slots = _route_call(lg, h2, n_req, n_tok)
        outs.append((x1, xe, gates, slots, row_of_req))

    (x1p, xep, gp, slots_p, row_p), (x1s, xes, gs, slots_s, row_s) = outs
    yp, ys = _experts_call(xep, xes, gp, gs, w_gate_e[l], w_up_e[l], w_down_e[l])
    y_prompt = _scatter_call(slots_p, yp, x1p, m, row_p, n_p, t_p).reshape(x_prompt.shape)
    y_sample = _scatter_call(slots_s, ys, x1s, m, row_s, n_s, t_s).reshape(x_sample.shape)
    new_k = jnp.transpose(kv_out[0].reshape(n_p, N_HEADS, 2, HEAD_DIM, t_p), (0, 4, 1, 2, 3))
    new_k = new_k.reshape(n_p, 1, t_p, N_HEADS, 2, HEAD_DIM)
    new_v = kv_out[1].reshape(n_p, 1, t_p, N_HEADS, V_HEAD_DIM)
    return (y_prompt, y_sample, new_k, new_v)
```

```python
import functools
import math

import jax
import jax.numpy as jnp
import numpy as np
from jax import lax
from jax.experimental import pallas as pl
from jax.experimental.pallas import tpu as pltpu

D = 1024
N_HEADS = 8
HEAD_DIM = 64
V_HEAD_DIM = 128
GRID_W = 64
ROPE_BASE = 10000.0
CHUNK = 128
N_GROUPS = 8
N_EXPERTS = 16
CAPACITY_FACTOR = 2
D_EXPERT = 2048
N_SEG = 7
K_SEG = 1
EPS = 1e-6
LAMBDA_INIT = 0.8 - 0.6 * math.exp(-0.3 * 0)

LANES = 128
MOD_ROWS = 8
VMEM_LIMIT = 56 * 1024 * 1024

BF = jnp.bfloat16
F32 = jnp.float32


def _dot(a, b):
    return jnp.dot(a, b, preferred_element_type=F32)


def _dot_nt(a, b):
    return lax.dot_general(a, b, (((1,), (1,)), ((), ())), preferred_element_type=F32)


def _split_bf16(x):
    hi = x.astype(BF)
    lo = (x - hi.astype(F32)).astype(BF)
    return hi, lo


def _sigmoid(x):
    return 0.5 * jnp.tanh(0.5 * x) + 0.5


def _rms(x, g):
    return x * lax.rsqrt(jnp.mean(x * x, axis=-1, keepdims=True) + EPS) * g


def _params(n_grid_dims):
    return pltpu.CompilerParams(
        dimension_semantics=("arbitrary",) * n_grid_dims, vmem_limit_bytes=VMEM_LIMIT)


def _const_spec(shape):
    nd = len(shape)
    return pl.BlockSpec(shape, lambda *_: (0,) * nd)


N_MOD = 6


def _prep_kernel(cctx_ref, c_ref, wa_ref, ba_ref, lq1_ref, lk1_ref, lq2_ref, lk2_ref, wi_ref, wo_ref, wsp_ref,
                 m_ref, lam_ref, wib_ref, wkt_ref, wob_ref, wsp2_ref, cond_s):
    j = pl.program_id(0)
    n_lat = c_ref.shape[0]

    @pl.when(j == 0)
    def _():
        cond_s[...] = jnp.zeros(cond_s.shape, F32)
        cond_s[0:1, :] = cctx_ref[...]
        cond_s[1:1 + n_lat, :] = c_ref[...]
        s1 = jnp.sum(lq1_ref[...] * lk1_ref[...], axis=-1, keepdims=True)
        s2 = jnp.sum(lq2_ref[...] * lk2_ref[...], axis=-1, keepdims=True)
        lam_ref[...] = jnp.broadcast_to(jnp.exp(s1) - jnp.exp(s2) + LAMBDA_INIT, lam_ref.shape)
        wob_ref[...] = wo_ref[...].astype(BF)
        for p in range(N_GROUPS // 2):
            wsp2_ref[p] = jnp.concatenate([wsp_ref[2 * p], wsp_ref[2 * p + 1]], axis=1).astype(BF)

    @pl.when(j < N_MOD)
    def _():
        c = cond_s[...]
        a_hi, a_lo = _split_bf16(c * jax.nn.sigmoid(c))
        w_hi, w_lo = _split_bf16(wa_ref[...])
        m_ref[...] = _dot(a_hi, w_hi) + _dot(a_lo, w_hi) + _dot(a_hi, w_lo) + ba_ref[...]

    w = wi_ref[...]
    wib_ref[...] = w.astype(BF)

    @pl.when(j == K_SEG)
    def _():
        wkt_ref[...] = w.T.astype(BF)


def _prep_call(c_ctx, c, w_ada, b_ada, lq1, lk1, lq2, lk2, w_in, w_out, w_spatial):
    vec = _const_spec((1, HEAD_DIM))
    mod_block = lambda j: (0, jnp.minimum(j, N_MOD - 1))
    pairs = (N_GROUPS // 2, CHUNK, 2 * CHUNK)
    return pl.pallas_call(
        _prep_kernel,
        grid=(N_SEG,),
        in_specs=[
            _const_spec((1, D)), _const_spec(c.shape),
            pl.BlockSpec((D, D), mod_block),
            pl.BlockSpec((1, D), mod_block),
            vec, vec, vec, vec,
            pl.BlockSpec((D, D), lambda j: (0, j)),
            _const_spec((D, D)),
            _const_spec((N_GROUPS, CHUNK, CHUNK)),
        ],
        out_specs=[
            pl.BlockSpec((MOD_ROWS, D), mod_block),
            _const_spec((MOD_ROWS, LANES)),
            pl.BlockSpec((D, D), lambda j: (0, j)),
            _const_spec((D, D)),
            _const_spec((D, D)),
            _const_spec(pairs),
        ],
        out_shape=[
            jax.ShapeDtypeStruct((MOD_ROWS, N_MOD * D), F32),
            jax.ShapeDtypeStruct((MOD_ROWS, LANES), F32),
            jax.ShapeDtypeStruct((D, N_SEG * D), BF),
            jax.ShapeDtypeStruct((D, D), BF),
            jax.ShapeDtypeStruct((D, D), BF),
            jax.ShapeDtypeStruct(pairs, BF),
        ],
        scratch_shapes=[pltpu.VMEM((MOD_ROWS, D), F32)],
        compiler_params=_params(1),
        name="prep",
    )(c_ctx, c, w_ada, b_ada, lq1, lk1, lq2, lk2, w_in, w_out, w_spatial)


def _head_norm(t, g, gsum, gbcast):
    ss = _dot((t * t).astype(BF), gsum)
    inv = lax.rsqrt(ss * (1.0 / HEAD_DIM) + EPS)
    hi, lo = _split_bf16(inv)
    bc = _dot(jnp.concatenate([hi, lo], axis=-1), gbcast)
    return t * bc * g


def _lane_tile(tab):
    return jnp.concatenate([tab] * (D // LANES), axis=1)


def _rope(t, cos, sin_signed):
    q = HEAD_DIM // 4
    lane = lax.broadcasted_iota(jnp.int32, (1, D), 1)
    first_half = (lane % (HEAD_DIM // 2)) < q
    fwd = pltpu.roll(t, D - q, axis=1)
    bwd = pltpu.roll(t, q, axis=1)
    return t * _lane_tile(cos) + jnp.where(first_half, fwd, bwd) * _lane_tile(sin_signed)


def _rope_t(t3, cos_t, sin_signed_t):
    q = HEAD_DIM // 4
    rot = jnp.concatenate([t3[:, q:2 * q], t3[:, 0:q], t3[:, 3 * q:4 * q], t3[:, 2 * q:3 * q]], axis=1)
    return t3 * cos_t[None] + rot * sin_signed_t[None]


def _inproj_kernel(rope, row_of_step, x_ref, m_ref, vecs_ref, w_ref, wkt_ref, kgt_ref, gsum_ref,
                   gbcast_ref, wsp_ref, bsp_ref, *rest):
    if rope:
        tab_ref, tab_t_ref, qst_out, kt_out, v_out = rest
    else:
        qst_out, kt_out, v_out = rest
    tm = x_ref.shape[0]
    m = m_ref[pl.ds(row_of_step(pl.program_id(0)), 1), :]
    shift1, scale1 = m[:, 0:D], m[:, D:2 * D]
    n1g, qg, lng, lnb = (vecs_ref[r:r + 1, :] for r in range(4))
    h = (_rms(x_ref[...], n1g) * (1.0 + scale1) + shift1).astype(BF)

    def seg(j):
        return _dot(h, w_ref[:, j * D:(j + 1) * D])

    q = _head_norm(seg(0), qg, gsum_ref[...], gbcast_ref[...])
    if rope:
        q = _rope(q, tab_ref[:, 0:LANES], tab_ref[:, LANES:2 * LANES])
    qst_out[:, 0:D] = q.astype(qst_out.dtype)

    k3 = _dot_nt(wkt_ref[...], h).reshape(D // HEAD_DIM, HEAD_DIM, tm)
    inv = lax.rsqrt(jnp.mean(k3 * k3, axis=1, keepdims=True) + EPS)
    k3 = k3 * inv * kgt_ref[...][None]
    if rope:
        k3 = _rope_t(k3, tab_t_ref[0:HEAD_DIM, :], tab_t_ref[HEAD_DIM:2 * HEAD_DIM, :])
    kt = k3.reshape(D, tm).astype(kt_out.dtype)
    width = kt_out.shape[2]
    for r in range(kt_out.shape[0]):
        kt_out[r] = kt[:, r * width:(r + 1) * width]

    v_out[...] = seg(2).astype(v_out.dtype)
    qst_out[:, D:2 * D] = _sigmoid(seg(5)).astype(qst_out.dtype)

    zv = jax.nn.gelu(seg(4))
    mu = jnp.mean(zv, axis=-1, keepdims=True)
    zc = zv - mu
    var = jnp.mean(zc * zc, axis=-1, keepdims=True)
    zvn = (zc * lax.rsqrt(var + EPS) * lng + lnb).astype(BF)
    pre = jax.nn.gelu(seg(3)) * _sigmoid(seg(6))
    blank = jnp.zeros((CHUNK, LANES), BF)
    for c in range(tm // CHUNK):
        rows = slice(c * CHUNK, (c + 1) * CHUNK)
        for p in range(N_GROUPS // 2):
            cols = slice(2 * p * LANES, (2 * p + 2) * LANES)
            z = zvn[rows, cols]
            z_diag = jnp.concatenate([jnp.concatenate([z[:, :LANES], blank], axis=1),
                                      jnp.concatenate([blank, z[:, LANES:]], axis=1)], axis=0)
            sp = _dot(wsp_ref[p], z_diag) + bsp_ref[:, cols]
            out_cols = slice(2 * D + 2 * p * LANES, 2 * D + (2 * p + 2) * LANES)
            qst_out[rows, out_cols] = (pre[rows, cols] * sp).astype(qst_out.dtype)


INPROJ_ROWS = 512


def _inproj_call(x, m, row_of_req, rope_tabs, kv_dtype, consts, n_req, n_tok):
    t = x.shape[0]
    tm = INPROJ_ROWS
    per_req = max(1, n_tok // tm)
    per_tile = max(1, tm // n_tok)
    rope = rope_tabs is not None

    def row_of_step(i):
        return row_of_req(i * per_tile // per_req)

    tok = pl.BlockSpec((tm, D), lambda i: (i, 0))
    in_specs = [
        tok,
        _const_spec((MOD_ROWS, 6 * D)),
        _const_spec((MOD_ROWS, D)),
        pl.BlockSpec((D, N_SEG * D), lambda i: (0, 0), pipeline_mode=pl.Buffered(1)),
        _const_spec((D, D)),
        _const_spec((HEAD_DIM, tm)),
        _const_spec((D, LANES)), _const_spec((2 * LANES, D)),
        _const_spec((N_GROUPS // 2, CHUNK, 2 * CHUNK)), _const_spec((CHUNK, D)),
    ]
    args = [x, m, consts["vecs_in"], consts["w_in"], consts["wkt"], consts["kgt"], consts["gsum"],
            consts["gbcast"], consts["wsp"], consts["bsp"]]
    if rope:
        in_specs += [pl.BlockSpec((tm, 2 * LANES), lambda i: (i % per_req, 0)),
                     pl.BlockSpec((2 * HEAD_DIM, tm), lambda i: (0, i % per_req))]
        args += list(rope_tabs)
    kt_spec = pl.BlockSpec((per_tile, D, tm // per_tile), lambda i: (i, 0, 0))
    return pl.pallas_call(
        functools.partial(_inproj_kernel, rope, row_of_step),
        grid=(t // tm,),
        in_specs=in_specs,
        out_specs=[pl.BlockSpec((tm, 3 * D), lambda i: (i, 0)), kt_spec, tok],
        out_shape=[
            jax.ShapeDtypeStruct((t, 3 * D), BF),
            jax.ShapeDtypeStruct((n_req * per_req, D, tm // per_tile), kv_dtype),
            jax.ShapeDtypeStruct((t, D), kv_dtype),
        ],
        compiler_params=_params(1),
        name="inproj_rope" if rope else "inproj",
    )(*args)


SCORE_GROUP_ELEMS = 1 << 22
ATTN_ROWS = 256


def _attn_kernel(has_ctx, heads_per_group, row_of_req, qst_ref, kt_ref, v_ref, *rest):
    if has_ctx:
        kct_ref, vc_ref, *rest = rest
    lam_ref, vecs_ref, x_ref, m_ref, wo_ref, wr_ref, x1_out, h2_out, lg_out, merged_s = rest
    tq = qst_ref.shape[0]
    key_blocks, block_keys = kt_ref.shape[0], kt_ref.shape[2]
    n2g = vecs_ref[0:1, :]
    sub_g = vecs_ref[1:2, 0:V_HEAD_DIM]
    lam = lam_ref[0:1, 0:1]
    lane = lax.broadcasted_iota(jnp.int32, (1, LANES), 1)
    first = lane < HEAD_DIM
    zero = jnp.zeros((), BF)
    ones_col = jnp.where(lane == 0, 1.0, 0.0).astype(BF)

    def values(ref, rows, h):
        v = ref[rows, h * V_HEAD_DIM:(h + 1) * V_HEAD_DIM].astype(BF)
        return jnp.concatenate([v, jnp.broadcast_to(ones_col, (v.shape[0], LANES))], axis=1)

    def head_cols(h):
        return slice(h * LANES, (h + 1) * LANES)

    def head_scores(h):
        q = qst_ref[:, head_cols(h)]
        qz = jnp.concatenate([jnp.where(first, q, zero), jnp.where(first, zero, q)], axis=0)
        parts = [_dot(qz, kt_ref[j, head_cols(h), :].astype(BF)) for j in range(key_blocks)]
        if has_ctx:
            parts.append(_dot(qz, kct_ref[0, head_cols(h), :].astype(BF)))
        return parts

    def row_max(scores):
        return functools.reduce(jnp.maximum, [jnp.max(s, axis=-1, keepdims=True) for s in scores])

    def head_pv(h, scores, mx):
        vals = [values(v_ref, slice(j * block_keys, (j + 1) * block_keys), h) for j in range(key_blocks)]
        if has_ctx:
            vals.append(values(vc_ref, slice(None), h))
        return functools.reduce(jnp.add, [_dot(jnp.exp2(s - mx).astype(BF), v) for s, v in zip(scores, vals)])

    def head_finish(h, ob):
        den = ob[:, V_HEAD_DIM:V_HEAD_DIM + 1]
        o = ob[:tq, :V_HEAD_DIM] * (1.0 / den[:tq]) - ob[tq:, :V_HEAD_DIM] * (lam / den[tq:])
        o = _rms(o, sub_g) * (1.0 - LAMBDA_INIT)
        sga = qst_ref[:, D + h * LANES:D + (h + 1) * LANES].astype(F32)
        tb = qst_ref[:, 2 * D + h * LANES:2 * D + (h + 1) * LANES].astype(F32)
        merged_s[:, head_cols(h)] = (sga * o + tb).astype(BF)

    for g0 in range(0, N_HEADS, heads_per_group):
        group = range(g0, g0 + heads_per_group)
        scores = [head_scores(h) for h in group]
        maxes = [row_max(s) for s in scores]
        outs = [head_pv(h, s, mx) for h, s, mx in zip(group, scores, maxes)]
        for h, ob in zip(group, outs):
            head_finish(h, ob)

    m = m_ref[pl.ds(row_of_req(pl.program_id(0)), 1), :]
    gate1, shift2, scale2 = m[:, 2 * D:3 * D], m[:, 3 * D:4 * D], m[:, 4 * D:5 * D]
    x1 = x_ref[...] + gate1 * _dot(merged_s[...], wo_ref[...])
    x1_out[...] = x1
    h2 = _rms(x1, n2g) * (1.0 + scale2) + shift2
    h2_out[...] = h2.astype(BF)
    hi, lo = _split_bf16(h2)
    wr_hi, wr_lo = wr_ref[:, 0:LANES], wr_ref[:, LANES:2 * LANES]
    lg_out[...] = _dot(hi, wr_hi) + _dot(lo, wr_hi) + _dot(hi, wr_lo)


def _attn_call(qst, kt, v, ctx, lamv, x, m, row_of_req, consts, n_req, n_tok):
    tq = min(n_tok, ATTN_ROWS)
    nqb = n_tok // tq
    key_blocks = kt.shape[0] // n_req
    has_ctx = ctx is not None
    tok = pl.BlockSpec((tq, D), lambda b, i: (b * nqb + i, 0))
    in_specs = [pl.BlockSpec((tq, 3 * D), lambda b, i: (b * nqb + i, 0)),
                pl.BlockSpec((key_blocks, D, kt.shape[2]), lambda b, i: (b, 0, 0)),
                pl.BlockSpec((n_tok, D), lambda b, i: (b, 0))]
    args = [qst, kt, v]
    if has_ctx:
        n_ctx = ctx[0].shape[2]
        in_specs += [pl.BlockSpec((1, D, n_ctx), lambda b, i: (b, 0, 0)),
                     pl.BlockSpec((n_ctx, D), lambda b, i: (b, 0))]
        args += list(ctx)
    in_specs += [
        _const_spec((MOD_ROWS, LANES)), _const_spec((MOD_ROWS, D)),
        tok,
        _const_spec((MOD_ROWS, 6 * D)),
        _const_spec((D, D)),
        _const_spec((D, 2 * LANES)),
    ]
    args += [lamv, consts["vecs_attn"], x, m, consts["w_out"], consts["wr"]]
    t = n_req * n_tok
    n_keys = n_tok + (ctx[0].shape[2] if has_ctx else 0)
    heads_per_group = max(1, min(N_HEADS, SCORE_GROUP_ELEMS // (2 * tq * n_keys)))
    while N_HEADS % heads_per_group:
        heads_per_group -= 1
    return pl.pallas_call(
        functools.partial(_attn_kernel, has_ctx, heads_per_group, row_of_req),
        grid=(n_req, nqb),
        in_specs=in_specs,
        out_specs=[tok, tok, pl.BlockSpec((tq, LANES), lambda b, i: (b * nqb + i, 0))],
        out_shape=[
            jax.ShapeDtypeStruct((t, D), F32),
            jax.ShapeDtypeStruct((t, D), BF),
            jax.ShapeDtypeStruct((t, LANES), F32),
        ],
        scratch_shapes=[pltpu.VMEM((tq, D), BF)],
        compiler_params=_params(2),
        name="attn_ctx" if has_ctx else "attn",
    )(*args)


GATHER_ROWS = 512
KEY_BITS = 31
ROUTE_UNROLL_ELEMS = 1 << 22
ROUTE_STEP_TOKENS = 2048
PREFIX_BLOCK = 128


def _route_kernel(cap, n_tok, unroll, lg_ref, h2_ref, before_ref, xe_ref, gate_ref, slot_ref,
                  aff_t_s, slot_t_s, p_s):
    reqs = range(lg_ref.shape[0] // n_tok)
    lane = lax.broadcasted_iota(jnp.int32, (1, LANES), 1)
    valid = lane < N_EXPERTS

    def rows(r):
        return slice(r * n_tok, (r + 1) * n_tok)

    def affinity(r):
        lg = jnp.where(valid, lg_ref[rows(r), :], -1e30)
        ex = jnp.where(valid, jnp.exp(lg - jnp.max(lg, axis=-1, keepdims=True)), 0.0)
        return ex / jnp.sum(ex, axis=-1, keepdims=True)

    def count(mask):
        return jnp.sum(jnp.where(mask, 1.0, 0.0), axis=0, keepdims=True)

    affs = [affinity(r) for r in reqs]
    kth_bits = [jnp.zeros((1, LANES), jnp.int32) for _ in reqs]
    for bit in range(KEY_BITS - 1, -1, -1):
        for r in reqs:
            cand = kth_bits[r] | (1 << bit)
            enough = count(affs[r] >= lax.bitcast_convert_type(cand, F32)) >= cap
            kth_bits[r] = jnp.where(enough, cand, kth_bits[r])

    before = before_ref[...]

    def earlier_count(flags):
        blocks, running = [], jnp.zeros((1, LANES), F32)
        for b in range(n_tok // PREFIX_BLOCK):
            blk = flags[b * PREFIX_BLOCK:(b + 1) * PREFIX_BLOCK]
            blocks.append(_dot(before, blk.astype(BF)) + running)
            running = running + jnp.sum(blk, axis=0, keepdims=True)
        return jnp.concatenate(blocks, axis=0)

    for r in reqs:
        aff = affs[r]
        kth = lax.bitcast_convert_type(kth_bits[r], F32)
        above = aff > kth
        tied = aff == kth
        need = cap - count(above)
        tied_before = earlier_count(jnp.where(tied, 1.0, 0.0))
        chosen = jnp.where(above, 1.0, jnp.where(tied, jnp.where(tied_before < need, 1.0, 0.0), 0.0))
        slot = earlier_count(chosen)
        slot = jnp.where(valid, jnp.where(chosen > 0.0, slot, float(cap)), float(cap))
        slot_ref[rows(r), :] = slot
        slot_t_s[r] = slot.T
        aff_t_s[r] = aff.T

    slot_ids = lax.broadcasted_iota(jnp.int32, (cap, 1), 0).astype(F32)

    def per_expert(e, carry):
        for r in reqs:
            hit = slot_t_s[r, pl.ds(e, 1), :] == slot_ids
            p_s[r, pl.ds(pl.multiple_of(e * cap, cap), cap), :] = jnp.where(hit, 1.0, 0.0).astype(BF)
            gate = jnp.sum(jnp.where(hit, aff_t_s[r, pl.ds(e, 1), :], 0.0), axis=-1, keepdims=True)
            gate_ref[e, r * cap:(r + 1) * cap, :] = gate
        return carry

    lax.fori_loop(0, N_EXPERTS, per_expert, 0, unroll=unroll)
    n_rows = min(GATHER_ROWS, N_EXPERTS * cap)
    e_per = n_rows // cap
    for r in reqs:
        h2 = h2_ref[rows(r), :]
        for t in range(N_EXPERTS * cap // n_rows):
            xe = _dot(p_s[r, t * n_rows:(t + 1) * n_rows, :], h2).astype(BF)
            xe_ref[t * e_per:(t + 1) * e_per, r * cap:(r + 1) * cap, :] = xe.reshape(e_per, cap, D)


def _capacity(n_tok):
    return max(1, CAPACITY_FACTOR * n_tok // N_EXPERTS)


def _route_call(lg, h2, n_req, n_tok):
    cap = _capacity(n_tok)
    per_step = max(1, min(n_req, ROUTE_STEP_TOKENS // n_tok))
    before = jnp.asarray((np.arange(PREFIX_BLOCK)[None, :] < np.arange(PREFIX_BLOCK)[:, None]).astype(BF))
    unroll = max(1, min(N_EXPERTS, ROUTE_UNROLL_ELEMS // (per_step * cap * n_tok)))
    return pl.pallas_call(
        functools.partial(_route_kernel, cap, n_tok, unroll),
        grid=(n_req // per_step,),
        in_specs=[
            pl.BlockSpec((per_step * n_tok, LANES), lambda b: (b, 0)),
            pl.BlockSpec((per_step * n_tok, D), lambda b: (b, 0)),
            _const_spec((PREFIX_BLOCK, PREFIX_BLOCK)),
        ],
        out_specs=[
            pl.BlockSpec((N_EXPERTS, per_step * cap, D), lambda b: (0, b, 0)),
            pl.BlockSpec((N_EXPERTS, per_step * cap, 1), lambda b: (0, b, 0)),
            pl.BlockSpec((per_step * n_tok, LANES), lambda b: (b, 0)),
        ],
        out_shape=[
            jax.ShapeDtypeStruct((N_EXPERTS, n_req * cap, D), BF),
            jax.ShapeDtypeStruct((N_EXPERTS, n_req * cap, 1), F32),
            jax.ShapeDtypeStruct((n_req * n_tok, LANES), F32),
        ],
        scratch_shapes=[
            pltpu.VMEM((per_step, LANES, n_tok), F32),
            pltpu.VMEM((per_step, LANES, n_tok), F32),
            pltpu.VMEM((per_step, N_EXPERTS * cap, n_tok), BF),
        ],
        compiler_params=_params(1),
        name="route",
    )(lg, h2, before)


EXPERT_BLOCK = 1024
EXPERT_SUB = 256


def _experts_kernel(xa_ref, xb_ref, ga_ref, gb_ref, wg_ref, wu_ref, wd_ref, ya_ref, yb_ref, acc_s):
    f = pl.program_id(1)
    ra = xa_ref.shape[1]
    x = jnp.concatenate([xa_ref[0], xb_ref[0]], axis=0)
    for c in range(wg_ref.shape[2] // EXPERT_SUB):
        cs = slice(c * EXPERT_SUB, (c + 1) * EXPERT_SUB)
        gate = _dot(x, wg_ref[0, :, cs].astype(BF))
        up = _dot(x, wu_ref[0, :, cs].astype(BF))
        hid = (gate * _sigmoid(gate) * up).astype(BF)
        down = _dot(hid, wd_ref[0, cs, :].astype(BF))
        if c == 0:
            acc_s[...] = jnp.where(f == 0, 0.0, acc_s[...]) + down
        else:
            acc_s[...] += down

    @pl.when(f == pl.num_programs(1) - 1)
    def _():
        ya_ref[0] = (acc_s[0:ra, :] * ga_ref[0]).astype(ya_ref.dtype)
        yb_ref[0] = (acc_s[ra:, :] * gb_ref[0]).astype(yb_ref.dtype)


def _experts_call(xa, xb, ga, gb, wg, wu, wd):
    ra, rb = xa.shape[1], xb.shape[1]
    tf = EXPERT_BLOCK
    xa_spec = pl.BlockSpec((1, ra, D), lambda e, f: (e, 0, 0))
    xb_spec = pl.BlockSpec((1, rb, D), lambda e, f: (e, 0, 0))
    return pl.pallas_call(
        _experts_kernel,
        grid=(N_EXPERTS, D_EXPERT // tf),
        in_specs=[
            xa_spec, xb_spec,
            pl.BlockSpec((1, ra, 1), lambda e, f: (e, 0, 0)),
            pl.BlockSpec((1, rb, 1), lambda e, f: (e, 0, 0)),
            pl.BlockSpec((1, D, tf), lambda e, f: (e, 0, f)),
            pl.BlockSpec((1, D, tf), lambda e, f: (e, 0, f)),
            pl.BlockSpec((1, tf, D), lambda e, f: (e, f, 0)),
        ],
        out_specs=[xa_spec, xb_spec],
        out_shape=[
            jax.ShapeDtypeStruct((N_EXPERTS, ra, D), BF),
            jax.ShapeDtypeStruct((N_EXPERTS, rb, D), BF),
        ],
        scratch_shapes=[pltpu.VMEM((ra + rb, D), F32)],
        compiler_params=_params(2),
        name="experts",
    )(xa, xb, ga, gb, wg, wu, wd)


SCATTER_STEP_TOKENS = 1024


def _scatter_kernel(cap, n_tok, row_of_req, slot_ref, y_ref, x1_ref, m_ref, expand_ref, out_ref):
    per_step = slot_ref.shape[0] // n_tok
    slot_ids = (lax.broadcasted_iota(jnp.int32, (1, N_EXPERTS * cap), 1) % cap).astype(F32)
    for r in range(per_step):
        rows = slice(r * n_tok, (r + 1) * n_tok)
        m = m_ref[pl.ds(row_of_req(pl.program_id(0) * per_step + r), 1), :]
        gate2 = m[:, 5 * D:6 * D]
        slot_wide = _dot(slot_ref[rows, :].astype(BF), expand_ref[...])
        onehot = jnp.where(slot_wide == slot_ids, 1.0, 0.0).astype(BF)
        y = y_ref[:, r * cap:(r + 1) * cap, :].reshape(N_EXPERTS * cap, D)
        out_ref[rows, :] = x1_ref[rows, :] + gate2 * _dot(onehot, y)


def _scatter_call(slots, y, x1, m, row_of_req, n_req, n_tok):
    cap = _capacity(n_tok)
    per_step = max(1, min(n_req, SCATTER_STEP_TOKENS // n_tok))
    expand = jnp.asarray((np.arange(LANES)[:, None] == np.arange(N_EXPERTS * cap)[None, :] // cap).astype(BF))
    tok = pl.BlockSpec((per_step * n_tok, D), lambda b: (b, 0))
    return pl.pallas_call(
        functools.partial(_scatter_kernel, cap, n_tok, row_of_req),
        grid=(n_req // per_step,),
        in_specs=[
            pl.BlockSpec((per_step * n_tok, LANES), lambda b: (b, 0)),
            pl.BlockSpec((N_EXPERTS, per_step * cap, D), lambda b: (0, b, 0)),
            tok,
            _const_spec((MOD_ROWS, 6 * D)),
            _const_spec((LANES, N_EXPERTS * cap)),
        ],
        out_specs=tok,
        out_shape=jax.ShapeDtypeStruct((n_req * n_tok, D), F32),
        compiler_params=_params(1),
        name="scatter",
    )(slots, y, x1, m, expand)


def _rope_tables(n_tokens):
    rows = n_tokens // GRID_W
    row = np.broadcast_to(np.arange(rows, dtype=np.float32)[:, None], (rows, GRID_W)).reshape(-1)
    col = np.broadcast_to(np.arange(GRID_W, dtype=np.float32)[None, :], (rows, GRID_W)).reshape(-1)
    half = HEAD_DIM // 4
    inv_freq = (np.float32(ROPE_BASE) ** (-np.arange(half, dtype=np.float32) / np.float32(half))).astype(np.float32)
    ar = row[:, None] * inv_freq
    ac = col[:, None] * inv_freq
    ang = np.concatenate([ar, ar, ac, ac], axis=-1).astype(np.float64)
    cos, sin = np.cos(ang).astype(np.float32), np.sin(ang).astype(np.float32)
    first_half = (np.arange(HEAD_DIM) % (HEAD_DIM // 2)) < (HEAD_DIM // 4)
    sin_signed = np.where(first_half[None, :], -sin, sin)
    reps = LANES // HEAD_DIM
    token_major = np.concatenate([np.tile(cos, (1, reps)), np.tile(sin_signed, (1, reps))], axis=1)
    transposed = np.concatenate([cos.T, sin_signed.T], axis=0)
    return jnp.asarray(token_major), jnp.asarray(transposed)


def kernel(x_prompt, x_sample, cache_k, cache_v, c, c_ctx, w_ada, b_ada, norm1_g, norm2_g, w_in, q_norm_g, k_norm_g, lambda_q1, lambda_k1, lambda_q2, lambda_k2, subln_g, gmlp_ln_g, gmlp_ln_b, w_spatial, b_spatial, w_out, w_router, w_gate_e, w_up_e, w_down_e):
    n_p, t_p = x_prompt.shape[0], x_prompt.shape[1]
    n_s, t_s = x_sample.shape[0], x_sample.shape[1]
    n_ctx = cache_k.shape[2]
    l = 0

    m, lamv, w_in_bf, wkt, w_out_bf, wsp_pairs = _prep_call(
        c_ctx[None, :], c, w_ada[l], b_ada[l][None, :], lambda_q1[l][None, :], lambda_k1[l][None, :],
        lambda_q2[l][None, :], lambda_k2[l][None, :], w_in[l], w_out[l], w_spatial[l])

    gsum_np = (np.arange(D)[:, None] // HEAD_DIM == np.arange(LANES)[None, :]).astype(BF)
    gsum = jnp.asarray(gsum_np)
    wr = jnp.pad(w_router[l], ((0, 0), (0, LANES - N_EXPERTS)))
    wr_hi = wr.astype(BF)
    qg = jnp.tile(q_norm_g[l] * (HEAD_DIM ** -0.5 * math.log2(math.e)), D // HEAD_DIM)

    def rows_of(*vectors):
        rows = [jnp.tile(vec, D // vec.shape[0])[None, :] for vec in vectors]
        return jnp.concatenate(rows + [jnp.zeros((MOD_ROWS - len(rows), D), F32)], axis=0)

    consts = {
        "vecs_in": rows_of(norm1_g[l], qg, gmlp_ln_g[l], gmlp_ln_b[l]),
        "vecs_attn": rows_of(norm2_g[l], subln_g[l]),
        "w_in": w_in_bf,
        "wkt": wkt,
        "kgt": jnp.broadcast_to(k_norm_g[l][:, None], (HEAD_DIM, INPROJ_ROWS)),
        "gsum": gsum,
        "gbcast": jnp.asarray(np.concatenate([gsum_np.T, gsum_np.T], axis=0)),
        "wsp": wsp_pairs,
        "bsp": jnp.repeat(b_spatial[l].T, D // N_GROUPS, axis=1),
        "w_out": w_out_bf,
        "wr": jnp.concatenate([wr_hi, (wr - wr_hi.astype(F32)).astype(BF)], axis=1),
    }

    passes = (
        (x_prompt, n_p, t_p, lambda b: 0, False),
        (x_sample, n_s, t_s, lambda b: 1 + b, True),
    )
    outs = []
    kv_out = None
    for x, n_req, n_tok, row_of_req, is_sample in passes:
        xf = x.reshape(n_req * n_tok, D)
        tabs = _rope_tables(n_tok) if is_sample else None
        qst, kt, v = _inproj_call(xf, m, row_of_req, tabs, BF if is_sample else F32, consts, n_req, n_tok)
        if is_sample:
            ctx = (jnp.transpose(cache_k[:, l], (0, 2, 3, 4, 1)).reshape(n_req, D, n_ctx),
                   cache_v[:, l].reshape(n_req * n_ctx, D))
        else:
            ctx = None
            kv_out = (kt, v)
        x1, h2, lg = _attn_call(qst, kt, v, ctx, lamv, xf, m, row_of_req, consts, n_req, n_tok)
        xe, gates, slots = _route_call(lg, h2, n_req, n_tok)
        outs.append((x1, xe, gates, slots, row_of_req))

    (x1p, xep, gp, slots_p, row_p), (x1s, xes, gs, slots_s, row_s) = outs
    yp, ys = _experts_call(xep, xes, gp, gs, w_gate_e[l], w_up_e[l], w_down_e[l])
    y_prompt = _scatter_call(slots_p, yp, x1p, m, row_p, n_p, t_p).reshape(x_prompt.shape)
    y_sample = _scatter_call(slots_s, ys, x1s, m, row_s, n_s, t_s).reshape(x_sample.shape)
    new_k = jnp.transpose(kv_out[0].reshape(n_p, N_HEADS, 2, HEAD_DIM, t_p), (0, 4, 1, 2, 3))
    new_k = new_k.reshape(n_p, 1, t_p, N_HEADS, 2, HEAD_DIM)
    new_v = kv_out[1].reshape(n_p, 1, t_p, N_HEADS, V_HEAD_DIM)
    return (y_prompt, y_sample, new_k, new_v)
```

```python
import functools
import math

import jax
import jax.numpy as jnp
import numpy as np
from jax import lax
from jax.experimental import pallas as pl
from jax.experimental.pallas import tpu as pltpu

D = 1024
N_HEADS = 8
HEAD_DIM = 64
V_HEAD_DIM = 128
GRID_W = 64
ROPE_BASE = 10000.0
CHUNK = 128
N_GROUPS = 8
N_EXPERTS = 16
CAPACITY_FACTOR = 2
D_EXPERT = 2048
N_SEG = 7
K_SEG = 1
EPS = 1e-6
LAMBDA_INIT = 0.8 - 0.6 * math.exp(-0.3 * 0)

LANES = 128
MOD_ROWS = 8
VMEM_LIMIT = 56 * 1024 * 1024

BF = jnp.bfloat16
F32 = jnp.float32


def _dot(a, b):
    return jnp.dot(a, b, preferred_element_type=F32)


def _dot_nt(a, b):
    return lax.dot_general(a, b, (((1,), (1,)), ((), ())), preferred_element_type=F32)


def _split_bf16(x):
    hi = x.astype(BF)
    lo = (x - hi.astype(F32)).astype(BF)
    return hi, lo


def _sigmoid(x):
    return 0.5 * jnp.tanh(0.5 * x) + 0.5


def _rms(x, g):
    return x * lax.rsqrt(jnp.mean(x * x, axis=-1, keepdims=True) + EPS) * g


def _params(n_grid_dims):
    return pltpu.CompilerParams(
        dimension_semantics=("arbitrary",) * n_grid_dims, vmem_limit_bytes=VMEM_LIMIT)


def _const_spec(shape):
    nd = len(shape)
    return pl.BlockSpec(shape, lambda *_: (0,) * nd)


N_MOD = 6


def _prep_kernel(cctx_ref, c_ref, wa_ref, ba_ref, lq1_ref, lk1_ref, lq2_ref, lk2_ref, wi_ref, wo_ref, wsp_ref,
                 m_ref, lam_ref, wib_ref, wkt_ref, wob_ref, wsp2_ref, cond_s):
    j = pl.program_id(0)
    n_lat = c_ref.shape[0]

    @pl.when(j == 0)
    def _():
        cond_s[...] = jnp.zeros(cond_s.shape, F32)
        cond_s[0:1, :] = cctx_ref[...]
        cond_s[1:1 + n_lat, :] = c_ref[...]
        s1 = jnp.sum(lq1_ref[...] * lk1_ref[...], axis=-1, keepdims=True)
        s2 = jnp.sum(lq2_ref[...] * lk2_ref[...], axis=-1, keepdims=True)
        lam_ref[...] = jnp.broadcast_to(jnp.exp(s1) - jnp.exp(s2) + LAMBDA_INIT, lam_ref.shape)
        wob_ref[...] = wo_ref[...].astype(BF)
        for p in range(N_GROUPS // 2):
            wsp2_ref[p] = jnp.concatenate([wsp_ref[2 * p], wsp_ref[2 * p + 1]], axis=1).astype(BF)

    @pl.when(j < N_MOD)
    def _():
        c = cond_s[...]
        a_hi, a_lo = _split_bf16(c * jax.nn.sigmoid(c))
        w_hi, w_lo = _split_bf16(wa_ref[...])
        m_ref[...] = _dot(a_hi, w_hi) + _dot(a_lo, w_hi) + _dot(a_hi, w_lo) + ba_ref[...]

    w = wi_ref[...]
    wib_ref[...] = w.astype(BF)

    @pl.when(j == K_SEG)
    def _():
        wkt_ref[...] = w.T.astype(BF)


def _prep_call(c_ctx, c, w_ada, b_ada, lq1, lk1, lq2, lk2, w_in, w_out, w_spatial):
    vec = _const_spec((1, HEAD_DIM))
    mod_block = lambda j: (0, jnp.minimum(j, N_MOD - 1))
    pairs = (N_GROUPS // 2, CHUNK, 2 * CHUNK)
    return pl.pallas_call(
        _prep_kernel,
        grid=(N_SEG,),
        in_specs=[
            _const_spec((1, D)), _const_spec(c.shape),
            pl.BlockSpec((D, D), mod_block),
            pl.BlockSpec((1, D), mod_block),
            vec, vec, vec, vec,
            pl.BlockSpec((D, D), lambda j: (0, j)),
            _const_spec((D, D)),
            _const_spec((N_GROUPS, CHUNK, CHUNK)),
        ],
        out_specs=[
            pl.BlockSpec((MOD_ROWS, D), mod_block),
            _const_spec((MOD_ROWS, LANES)),
            pl.BlockSpec((D, D), lambda j: (0, j)),
            _const_spec((D, D)),
            _const_spec((D, D)),
            _const_spec(pairs),
        ],
        out_shape=[
            jax.ShapeDtypeStruct((MOD_ROWS, N_MOD * D), F32),
            jax.ShapeDtypeStruct((MOD_ROWS, LANES), F32),
            jax.ShapeDtypeStruct((D, N_SEG * D), BF),
            jax.ShapeDtypeStruct((D, D), BF),
            jax.ShapeDtypeStruct((D, D), BF),
            jax.ShapeDtypeStruct(pairs, BF),
        ],
        scratch_shapes=[pltpu.VMEM((MOD_ROWS, D), F32)],
        compiler_params=_params(1),
        name="prep",
    )(c_ctx, c, w_ada, b_ada, lq1, lk1, lq2, lk2, w_in, w_out, w_spatial)


def _head_norm(t, g, gsum, gbcast):
    ss = _dot((t * t).astype(BF), gsum)
    inv = lax.rsqrt(ss * (1.0 / HEAD_DIM) + EPS)
    hi, lo = _split_bf16(inv)
    bc = _dot(jnp.concatenate([hi, lo], axis=-1), gbcast)
    return t * bc * g


def _lane_tile(tab):
    return jnp.concatenate([tab] * (D // LANES), axis=1)


def _rope(t, cos, sin_signed):
    q = HEAD_DIM // 4
    lane = lax.broadcasted_iota(jnp.int32, (1, D), 1)
    first_half = (lane % (HEAD_DIM // 2)) < q
    fwd = pltpu.roll(t, D - q, axis=1)
    bwd = pltpu.roll(t, q, axis=1)
    return t * _lane_tile(cos) + jnp.where(first_half, fwd, bwd) * _lane_tile(sin_signed)


def _rope_t(t3, cos_t, sin_signed_t):
    q = HEAD_DIM // 4
    rot = jnp.concatenate([t3[:, q:2 * q], t3[:, 0:q], t3[:, 3 * q:4 * q], t3[:, 2 * q:3 * q]], axis=1)
    return t3 * cos_t[None] + rot * sin_signed_t[None]


def _inproj_kernel(rope, row_of_step, x_ref, m_ref, vecs_ref, w_ref, wkt_ref, kgt_ref, gsum_ref,
                   gbcast_ref, wsp_ref, bsp_ref, *rest):
    if rope:
        tab_ref, tab_t_ref, qst_out, kt_out, v_out = rest
    else:
        qst_out, kt_out, v_out = rest
    tm = x_ref.shape[0]
    m = m_ref[pl.ds(row_of_step(pl.program_id(0)), 1), :]
    shift1, scale1 = m[:, 0:D], m[:, D:2 * D]
    n1g, qg, lng, lnb = (vecs_ref[r:r + 1, :] for r in range(4))
    h = (_rms(x_ref[...], n1g) * (1.0 + scale1) + shift1).astype(BF)

    def seg(j):
        return _dot(h, w_ref[:, j * D:(j + 1) * D])

    q = _head_norm(seg(0), qg, gsum_ref[...], gbcast_ref[...])
    if rope:
        q = _rope(q, tab_ref[:, 0:LANES], tab_ref[:, LANES:2 * LANES])
    qst_out[:, 0:D] = q.astype(qst_out.dtype)

    k3 = _dot_nt(wkt_ref[...], h).reshape(D // HEAD_DIM, HEAD_DIM, tm)
    inv = lax.rsqrt(jnp.mean(k3 * k3, axis=1, keepdims=True) + EPS)
    k3 = k3 * inv * kgt_ref[...][None]
    if rope:
        k3 = _rope_t(k3, tab_t_ref[0:HEAD_DIM, :], tab_t_ref[HEAD_DIM:2 * HEAD_DIM, :])
    kt = k3.reshape(D, tm).astype(kt_out.dtype)
    width = kt_out.shape[2]
    for r in range(kt_out.shape[0]):
        kt_out[r] = kt[:, r * width:(r + 1) * width]

    v_out[...] = seg(2).astype(v_out.dtype)
    qst_out[:, D:2 * D] = _sigmoid(seg(5)).astype(qst_out.dtype)

    zv = jax.nn.gelu(seg(4))
    mu = jnp.mean(zv, axis=-1, keepdims=True)
    zc = zv - mu
    var = jnp.mean(zc * zc, axis=-1, keepdims=True)
    zvn = (zc * lax.rsqrt(var + EPS) * lng + lnb).astype(BF)
    pre = jax.nn.gelu(seg(3)) * _sigmoid(seg(6))
    blank = jnp.zeros((CHUNK, LANES), BF)
    for c in range(tm // CHUNK):
        rows = slice(c * CHUNK, (c + 1) * CHUNK)
        for p in range(N_GROUPS // 2):
            cols = slice(2 * p * LANES, (2 * p + 2) * LANES)
            z = zvn[rows, cols]
            z_diag = jnp.concatenate([jnp.concatenate([z[:, :LANES], blank], axis=1),
                                      jnp.concatenate([blank, z[:, LANES:]], axis=1)], axis=0)
            sp = _dot(wsp_ref[p], z_diag) + bsp_ref[:, cols]
            out_cols = slice(2 * D + 2 * p * LANES, 2 * D + (2 * p + 2) * LANES)
            qst_out[rows, out_cols] = (pre[rows, cols] * sp).astype(qst_out.dtype)


INPROJ_ROWS = 512


def _inproj_call(x, m, row_of_req, rope_tabs, kv_dtype, consts, n_req, n_tok):
    t = x.shape[0]
    tm = INPROJ_ROWS
    per_req = max(1, n_tok // tm)
    per_tile = max(1, tm // n_tok)
    rope = rope_tabs is not None

    def row_of_step(i):
        return row_of_req(i * per_tile // per_req)

    tok = pl.BlockSpec((tm, D), lambda i: (i, 0))
    in_specs = [
        tok,
        _const_spec((MOD_ROWS, 6 * D)),
        _const_spec((MOD_ROWS, D)),
        pl.BlockSpec((D, N_SEG * D), lambda i: (0, 0), pipeline_mode=pl.Buffered(1)),
        _const_spec((D, D)),
        _const_spec((HEAD_DIM, tm)),
        _const_spec((D, LANES)), _const_spec((2 * LANES, D)),
        _const_spec((N_GROUPS // 2, CHUNK, 2 * CHUNK)), _const_spec((CHUNK, D)),
    ]
    args = [x, m, consts["vecs_in"], consts["w_in"], consts["wkt"], consts["kgt"], consts["gsum"],
            consts["gbcast"], consts["wsp"], consts["bsp"]]
    if rope:
        in_specs += [pl.BlockSpec((tm, 2 * LANES), lambda i: (i % per_req, 0)),
                     pl.BlockSpec((2 * HEAD_DIM, tm), lambda i: (0, i % per_req))]
        args += list(rope_tabs)
    kt_spec = pl.BlockSpec((per_tile, D, tm // per_tile), lambda i: (i, 0, 0))
    return pl.pallas_call(
        functools.partial(_inproj_kernel, rope, row_of_step),
        grid=(t // tm,),
        in_specs=in_specs,
        out_specs=[pl.BlockSpec((tm, 3 * D), lambda i: (i, 0)), kt_spec, tok],
        out_shape=[
            jax.ShapeDtypeStruct((t, 3 * D), BF),
            jax.ShapeDtypeStruct((n_req * per_req, D, tm // per_tile), kv_dtype),
            jax.ShapeDtypeStruct((t, D), kv_dtype),
        ],
        compiler_params=_params(1),
        name="inproj_rope" if rope else "inproj",
    )(*args)


SCORE_GROUP_ELEMS = 1 << 22
ATTN_ROWS = 256


def _attn_kernel(has_ctx, heads_per_group, row_of_req, qst_ref, kt_ref, v_ref, *rest):
    if has_ctx:
        kct_ref, vc_ref, *rest = rest
    lam_ref, vecs_ref, x_ref, m_ref, wo_ref, wr_ref, x1_out, h2_out, lg_out, merged_s = rest
    tq = qst_ref.shape[0]
    key_blocks, block_keys = kt_ref.shape[0], kt_ref.shape[2]
    n2g = vecs_ref[0:1, :]
    sub_g = vecs_ref[1:2, 0:V_HEAD_DIM]
    lam = lam_ref[0:1, 0:1]
    lane = lax.broadcasted_iota(jnp.int32, (1, LANES), 1)
    first = lane < HEAD_DIM
    zero = jnp.zeros((), BF)
    ones_col = jnp.where(lane == 0, 1.0, 0.0).astype(BF)

    def values(ref, rows, h):
        v = ref[rows, h * V_HEAD_DIM:(h + 1) * V_HEAD_DIM].astype(BF)
        return jnp.concatenate([v, jnp.broadcast_to(ones_col, (v.shape[0], LANES))], axis=1)

    def head_cols(h):
        return slice(h * LANES, (h + 1) * LANES)

    def head_scores(h):
        q = qst_ref[:, head_cols(h)]
        qz = jnp.concatenate([jnp.where(first, q, zero), jnp.where(first, zero, q)], axis=0)
        parts = [_dot(qz, kt_ref[j, head_cols(h), :].astype(BF)) for j in range(key_blocks)]
        if has_ctx:
            parts.append(_dot(qz, kct_ref[0, head_cols(h), :].astype(BF)))
        return parts

    def row_max(scores):
        return functools.reduce(jnp.maximum, [jnp.max(s, axis=-1, keepdims=True) for s in scores])

    def head_pv(h, scores, mx):
        vals = [values(v_ref, slice(j * block_keys, (j + 1) * block_keys), h) for j in range(key_blocks)]
        if has_ctx:
            vals.append(values(vc_ref, slice(None), h))
        return functools.reduce(jnp.add, [_dot(jnp.exp2(s - mx).astype(BF), v) for s, v in zip(scores, vals)])

    def head_finish(h, ob):
        den = ob[:, V_HEAD_DIM:V_HEAD_DIM + 1]
        o = ob[:tq, :V_HEAD_DIM] * (1.0 / den[:tq]) - ob[tq:, :V_HEAD_DIM] * (lam / den[tq:])
        o = _rms(o, sub_g) * (1.0 - LAMBDA_INIT)
        sga = qst_ref[:, D + h * LANES:D + (h + 1) * LANES].astype(F32)
        tb = qst_ref[:, 2 * D + h * LANES:2 * D + (h + 1) * LANES].astype(F32)
        merged_s[:, head_cols(h)] = (sga * o + tb).astype(BF)

    for g0 in range(0, N_HEADS, heads_per_group):
        group = range(g0, g0 + heads_per_group)
        scores = [head_scores(h) for h in group]
        maxes = [row_max(s) for s in scores]
        outs = [head_pv(h, s, mx) for h, s, mx in zip(group, scores, maxes)]
        for h, ob in zip(group, outs):
            head_finish(h, ob)

    m = m_ref[pl.ds(row_of_req(pl.program_id(0)), 1), :]
    gate1, shift2, scale2 = m[:, 2 * D:3 * D], m[:, 3 * D:4 * D], m[:, 4 * D:5 * D]
    x1 = x_ref[...] + gate1 * _dot(merged_s[...], wo_ref[...])
    x1_out[...] = x1
    h2 = _rms(x1, n2g) * (1.0 + scale2) + shift2
    h2_out[...] = h2.astype(BF)
    hi, lo = _split_bf16(h2)
    wr_hi, wr_lo = wr_ref[:, 0:LANES], wr_ref[:, LANES:2 * LANES]
    lg_out[...] = _dot(hi, wr_hi) + _dot(lo, wr_hi) + _dot(hi, wr_lo)


def _attn_call(qst, kt, v, ctx, lamv, x, m, row_of_req, consts, n_req, n_tok):
    tq = min(n_tok, ATTN_ROWS)
    nqb = n_tok // tq
    key_blocks = kt.shape[0] // n_req
    has_ctx = ctx is not None
    tok = pl.BlockSpec((tq, D), lambda b, i: (b * nqb + i, 0))
    in_specs = [pl.BlockSpec((tq, 3 * D), lambda b, i: (b * nqb + i, 0)),
                pl.BlockSpec((key_blocks, D, kt.shape[2]), lambda b, i: (b, 0, 0)),
                pl.BlockSpec((n_tok, D), lambda b, i: (b, 0))]
    args = [qst, kt, v]
    if has_ctx:
        n_ctx = ctx[0].shape[2]
        in_specs += [pl.BlockSpec((1, D, n_ctx), lambda b, i: (b, 0, 0)),
                     pl.BlockSpec((n_ctx, D), lambda b, i: (b, 0))]
        args += list(ctx)
    in_specs += [
        _const_spec((MOD_ROWS, LANES)), _const_spec((MOD_ROWS, D)),
        tok,
        _const_spec((MOD_ROWS, 6 * D)),
        _const_spec((D, D)),
        _const_spec((D, 2 * LANES)),
    ]
    args += [lamv, consts["vecs_attn"], x, m, consts["w_out"], consts["wr"]]
    t = n_req * n_tok
    n_keys = n_tok + (ctx[0].shape[2] if has_ctx else 0)
    heads_per_group = max(1, min(N_HEADS, SCORE_GROUP_ELEMS // (2 * tq * n_keys)))
    while N_HEADS % heads_per_group:
        heads_per_group -= 1
    return pl.pallas_call(
        functools.partial(_attn_kernel, has_ctx, heads_per_group, row_of_req),
        grid=(n_req, nqb),
        in_specs=in_specs,
        out_specs=[tok, tok, pl.BlockSpec((tq, LANES), lambda b, i: (b * nqb + i, 0))],
        out_shape=[
            jax.ShapeDtypeStruct((t, D), F32),
            jax.ShapeDtypeStruct((t, D), BF),
            jax.ShapeDtypeStruct((t, LANES), F32),
        ],
        scratch_shapes=[pltpu.VMEM((tq, D), BF)],
        compiler_params=_params(2),
        name="attn_ctx" if has_ctx else "attn",
    )(*args)


GATHER_ROWS = 512
KEY_BITS = 31
ROUTE_UNROLL_ELEMS = 1 << 22
ROUTE_STEP_TOKENS = 2048
PREFIX_BLOCK = 128


def _route_kernel(cap, n_tok, unroll, lg_ref, h2_ref, before_ref, xe_ref, gate_ref, slot_ref,
                  aff_t_s, slot_t_s, p_s):
    reqs = range(lg_ref.shape[0] // n_tok)
    lane = lax.broadcasted_iota(jnp.int32, (1, LANES), 1)
    valid = lane < N_EXPERTS

    def rows(r):
        return slice(r * n_tok, (r + 1) * n_tok)

    def affinity(r):
        lg = jnp.where(valid, lg_ref[rows(r), :], -1e30)
        ex = jnp.where(valid, jnp.exp(lg - jnp.max(lg, axis=-1, keepdims=True)), 0.0)
        return ex / jnp.sum(ex, axis=-1, keepdims=True)

    def count(mask):
        return jnp.sum(jnp.where(mask, 1.0, 0.0), axis=0, keepdims=True)

    affs = [affinity(r) for r in reqs]
    kth_bits = [jnp.zeros((1, LANES), jnp.int32) for _ in reqs]
    for bit in range(KEY_BITS - 1, -1, -1):
        for r in reqs:
            cand = kth_bits[r] | (1 << bit)
            enough = count(affs[r] >= lax.bitcast_convert_type(cand, F32)) >= cap
            kth_bits[r] = jnp.where(enough, cand, kth_bits[r])

    before = before_ref[...]

    def earlier_count(flags):
        blocks, running = [], jnp.zeros((1, LANES), F32)
        for b in range(n_tok // PREFIX_BLOCK):
            blk = flags[b * PREFIX_BLOCK:(b + 1) * PREFIX_BLOCK]
            blocks.append(_dot(before, blk.astype(BF)) + running)
            running = running + jnp.sum(blk, axis=0, keepdims=True)
        return jnp.concatenate(blocks, axis=0)

    for r in reqs:
        aff = affs[r]
        kth = lax.bitcast_convert_type(kth_bits[r], F32)
        above = aff > kth
        tied = aff == kth
        need = cap - count(above)
        tied_before = earlier_count(jnp.where(tied, 1.0, 0.0))
        chosen = jnp.where(above, 1.0, jnp.where(tied, jnp.where(tied_before < need, 1.0, 0.0), 0.0))
        slot = earlier_count(chosen)
        slot = jnp.where(valid, jnp.where(chosen > 0.0, slot, float(cap)), float(cap))
        slot_ref[rows(r), :] = slot
        slot_t_s[r] = slot.T
        aff_t_s[r] = aff.T

    slot_ids = lax.broadcasted_iota(jnp.int32, (cap, 1), 0).astype(F32)

    def per_expert(e, carry):
        for r in reqs:
            hit = slot_t_s[r, pl.ds(e, 1), :] == slot_ids
            p_s[r, pl.ds(pl.multiple_of(e * cap, cap), cap), :] = jnp.where(hit, 1.0, 0.0).astype(BF)
            gate = jnp.sum(jnp.where(hit, aff_t_s[r, pl.ds(e, 1), :], 0.0), axis=-1, keepdims=True)
            gate_ref[e, r * cap:(r + 1) * cap, :] = gate
        return carry

    lax.fori_loop(0, N_EXPERTS, per_expert, 0, unroll=unroll)
    n_rows = min(GATHER_ROWS, N_EXPERTS * cap)
    e_per = n_rows // cap
    for r in reqs:
        h2 = h2_ref[rows(r), :]
        for t in range(N_EXPERTS * cap // n_rows):
            xe = _dot(p_s[r, t * n_rows:(t + 1) * n_rows, :], h2).astype(BF)
            xe_ref[t * e_per:(t + 1) * e_per, r * cap:(r + 1) * cap, :] = xe.reshape(e_per, cap, D)


def _capacity(n_tok):
    return max(1, CAPACITY_FACTOR * n_tok // N_EXPERTS)


def _route_call(lg, h2, n_req, n_tok):
    cap = _capacity(n_tok)
    per_step = max(1, min(n_req, ROUTE_STEP_TOKENS // n_tok))
    before = jnp.asarray((np.arange(PREFIX_BLOCK)[None, :] < np.arange(PREFIX_BLOCK)[:, None]).astype(BF))
    unroll = max(1, min(N_EXPERTS, ROUTE_UNROLL_ELEMS // (per_step * cap * n_tok)))
    return pl.pallas_call(
        functools.partial(_route_kernel, cap, n_tok, unroll),
        grid=(n_req // per_step,),
        in_specs=[
            pl.BlockSpec((per_step * n_tok, LANES), lambda b: (b, 0)),
            pl.BlockSpec((per_step * n_tok, D), lambda b: (b, 0)),
            _const_spec((PREFIX_BLOCK, PREFIX_BLOCK)),
        ],
        out_specs=[
            pl.BlockSpec((N_EXPERTS, per_step * cap, D), lambda b: (0, b, 0)),
            pl.BlockSpec((N_EXPERTS, per_step * cap, 1), lambda b: (0, b, 0)),
            pl.BlockSpec((per_step * n_tok, LANES), lambda b: (b, 0)),
        ],
        out_shape=[
            jax.ShapeDtypeStruct((N_EXPERTS, n_req * cap, D), BF),
            jax.ShapeDtypeStruct((N_EXPERTS, n_req * cap, 1), F32),
            jax.ShapeDtypeStruct((n_req * n_tok, LANES), F32),
        ],
        scratch_shapes=[
            pltpu.VMEM((per_step, LANES, n_tok), F32),
            pltpu.VMEM((per_step, LANES, n_tok), F32),
            pltpu.VMEM((per_step, N_EXPERTS * cap, n_tok), BF),
        ],
        compiler_params=_params(1),
        name="route",
    )(lg, h2, before)


EXPERT_BLOCK = 1024
EXPERT_SUB = 256


def _experts_kernel(xa_ref, xb_ref, ga_ref, gb_ref, wg_ref, wu_ref, wd_ref, ya_ref, yb_ref, acc_s):
    f = pl.program_id(1)
    ra = xa_ref.shape[1]
    x = jnp.concatenate([xa_ref[0], xb_ref[0]], axis=0)
    for c in range(wg_ref.shape[2] // EXPERT_SUB):
        cs = slice(c * EXPERT_SUB, (c + 1) * EXPERT_SUB)
        gate = _dot(x, wg_ref[0, :, cs].astype(BF))
        up = _dot(x, wu_ref[0, :, cs].astype(BF))
        hid = (gate * _sigmoid(gate) * up).astype(BF)
        down = _dot(hid, wd_ref[0, cs, :].astype(BF))
        if c == 0:
            acc_s[...] = jnp.where(f == 0, 0.0, acc_s[...]) + down
        else:
            acc_s[...] += down

    @pl.when(f == pl.num_programs(1) - 1)
    def _():
        ya_ref[0] = (acc_s[0:ra, :] * ga_ref[0]).astype(ya_ref.dtype)
        yb_ref[0] = (acc_s[ra:, :] * gb_ref[0]).astype(yb_ref.dtype)


def _experts_call(xa, xb, ga, gb, wg, wu, wd):
    ra, rb = xa.shape[1], xb.shape[1]
    tf = EXPERT_BLOCK
    xa_spec = pl.BlockSpec((1, ra, D), lambda e, f: (e, 0, 0))
    xb_spec = pl.BlockSpec((1, rb, D), lambda e, f: (e, 0, 0))
    return pl.pallas_call(
        _experts_kernel,
        grid=(N_EXPERTS, D_EXPERT // tf),
        in_specs=[
            xa_spec, xb_spec,
            pl.BlockSpec((1, ra, 1), lambda e, f: (e, 0, 0)),
            pl.BlockSpec((1, rb, 1), lambda e, f: (e, 0, 0)),
            pl.BlockSpec((1, D, tf), lambda e, f: (e, 0, f)),
            pl.BlockSpec((1, D, tf), lambda e, f: (e, 0, f)),
            pl.BlockSpec((1, tf, D), lambda e, f: (e, f, 0)),
        ],
        out_specs=[xa_spec, xb_spec],
        out_shape=[
            jax.ShapeDtypeStruct((N_EXPERTS, ra, D), BF),
            jax.ShapeDtypeStruct((N_EXPERTS, rb, D), BF),
        ],
        scratch_shapes=[pltpu.VMEM((ra + rb, D), F32)],
        compiler_params=_params(2),
        name="experts",
    )(xa, xb, ga, gb, wg, wu, wd)


SCATTER_STEP_TOKENS = 1024


def _scatter_kernel(cap, n_tok, row_of_req, slot_ref, y_ref, x1_ref, m_ref, expand_ref, out_ref):
    per_step = slot_ref.shape[0] // n_tok
    slot_ids = (lax.broadcasted_iota(jnp.int32, (1, N_EXPERTS * cap), 1) % cap).astype(F32)
    for r in range(per_step):
        rows = slice(r * n_tok, (r + 1) * n_tok)
        m = m_ref[pl.ds(row_of_req(pl.program_id(0) * per_step + r), 1), :]
        gate2 = m[:, 5 * D:6 * D]
        slot_wide = _dot(slot_ref[rows, :].astype(BF), expand_ref[...])
        onehot = jnp.where(slot_wide == slot_ids, 1.0, 0.0).astype(BF)
        y = y_ref[:, r * cap:(r + 1) * cap, :].reshape(N_EXPERTS * cap, D)
        out_ref[rows, :] = x1_ref[rows, :] + gate2 * _dot(onehot, y)


def _scatter_call(slots, y, x1, m, row_of_req, n_req, n_tok):
    cap = _capacity(n_tok)
    per_step = max(1, min(n_req, SCATTER_STEP_TOKENS // n_tok))
    expand = jnp.asarray((np.arange(LANES)[:, None] == np.arange(N_EXPERTS * cap)[None, :] // cap).astype(BF))
    tok = pl.BlockSpec((per_step * n_tok, D), lambda b: (b, 0))
    return pl.pallas_call(
        functools.partial(_scatter_kernel, cap, n_tok, row_of_req),
        grid=(n_req // per_step,),
        in_specs=[
            pl.BlockSpec((per_step * n_tok, LANES), lambda b: (b, 0)),
            pl.BlockSpec((N_EXPERTS, per_step * cap, D), lambda b: (0, b, 0)),
            tok,
            _const_spec((MOD_ROWS, 6 * D)),
            _const_spec((LANES, N_EXPERTS * cap)),
        ],
        out_specs=tok,
        out_shape=jax.ShapeDtypeStruct((n_req * n_tok, D), F32),
        compiler_params=_params(1),
        name="scatter",
    )(slots, y, x1, m, expand)


def _rope_tables(n_tokens):
    rows = n_tokens // GRID_W
    row = np.broadcast_to(np.arange(rows, dtype=np.float32)[:, None], (rows, GRID_W)).reshape(-1)
    col = np.broadcast_to(np.arange(GRID_W, dtype=np.float32)[None, :], (rows, GRID_W)).reshape(-1)
    half = HEAD_DIM // 4
    inv_freq = (np.float32(ROPE_BASE) ** (-np.arange(half, dtype=np.float32) / np.float32(half))).astype(np.float32)
    ar = row[:, None] * inv_freq
    ac = col[:, None] * inv_freq
    ang = np.concatenate([ar, ar, ac, ac], axis=-1).astype(np.float64)
    cos, sin = np.cos(ang).astype(np.float32), np.sin(ang).astype(np.float32)
    first_half = (np.arange(HEAD_DIM) % (HEAD_DIM // 2)) < (HEAD_DIM // 4)
    sin_signed = np.where(first_half[None, :], -sin, sin)
    reps = LANES // HEAD_DIM
    token_major = np.concatenate([np.tile(cos, (1, reps)), np.tile(sin_signed, (1, reps))], axis=1)
    transposed = np.concatenate([cos.T, sin_signed.T], axis=0)
    return jnp.asarray(token_major), jnp.asarray(transposed)


def kernel(x_prompt, x_sample, cache_k, cache_v, c, c_ctx, w_ada, b_ada, norm1_g, norm2_g, w_in, q_norm_g, k_norm_g, lambda_q1, lambda_k1, lambda_q2, lambda_k2, subln_g, gmlp_ln_g, gmlp_ln_b, w_spatial, b_spatial, w_out, w_router, w_gate_e, w_up_e, w_down_e):
    n_p, t_p = x_prompt.shape[0], x_prompt.shape[1]
    n_s, t_s = x_sample.shape[0], x_sample.shape[1]
    n_ctx = cache_k.shape[2]
    l = 0

    m, lamv, w_in_bf, wkt, w_out_bf, wsp_pairs = _prep_call(
        c_ctx[None, :], c, w_ada[l], b_ada[l][None, :], lambda_q1[l][None, :], lambda_k1[l][None, :],
        lambda_q2[l][None, :], lambda_k2[l][None, :], w_in[l], w_out[l], w_spatial[l])

    gsum_np = (np.arange(D)[:, None] // HEAD_DIM == np.arange(LANES)[None, :]).astype(BF)
    gsum = jnp.asarray(gsum_np)
    wr = jnp.pad(w_router[l], ((0, 0), (0, LANES - N_EXPERTS)))
    wr_hi = wr.astype(BF)
    qg = jnp.tile(q_norm_g[l] * (HEAD_DIM ** -0.5 * math.log2(math.e)), D // HEAD_DIM)

    def rows_of(*vectors):
        rows = [jnp.tile(vec, D // vec.shape[0])[None, :] for vec in vectors]
        return jnp.concatenate(rows + [jnp.zeros((MOD_ROWS - len(rows), D), F32)], axis=0)

    consts = {
        "vecs_in": rows_of(norm1_g[l], qg, gmlp_ln_g[l], gmlp_ln_b[l]),
        "vecs_attn": rows_of(norm2_g[l], subln_g[l]),
        "w_in": w_in_bf,
        "wkt": wkt,
        "kgt": jnp.broadcast_to(k_norm_g[l][:, None], (HEAD_DIM, INPROJ_ROWS)),
        "gsum": gsum,
        "gbcast": jnp.asarray(np.concatenate([gsum_np.T, gsum_np.T], axis=0)),
        "wsp": wsp_pairs,
        "bsp": jnp.repeat(b_spatial[l].T, D // N_GROUPS, axis=1),
        "w_out": w_out_bf,
        "wr": jnp.concatenate([wr_hi, (wr - wr_hi.astype(F32)).astype(BF)], axis=1),
    }

    passes = (
        (x_prompt, n_p, t_p, lambda b: 0, False),
        (x_sample, n_s, t_s, lambda b: 1 + b, True),
    )
    outs = []
    kv_out = None
    for x, n_req, n_tok, row_of_req, is_sample in passes:
        xf = x.reshape(n_req * n_tok, D)
        tabs = _rope_tables(n_tok) if is_sample else None
        qst, kt, v = _inproj_call(xf, m, row_of_req, tabs, BF if is_sample else F32, consts, n_req, n_tok)
        if is_sample:
            ctx = (jnp.transpose(cache_k[:, l], (0, 2, 3, 4, 1)).reshape(n_req, D, n_ctx),
                   cache_v[:, l].reshape(n_req * n_ctx, D).astype(BF))
        else:
            ctx = None
            kv_out = (kt, v)
        x1, h2, lg = _attn_call(qst, kt, v, ctx, lamv, xf, m, row_of_req, consts, n_req, n_tok)
        xe, gates, slots = _route_call(lg, h2, n_req, n_tok)
        outs.append((x1, xe, gates, slots, row_of_req))

    (x1p, xep, gp, slots_p, row_p), (x1s, xes, gs, slots_s, row_s) = outs
    yp, ys = _experts_call(xep, xes, gp, gs, w_gate_e[l], w_up_e[l], w_down_e[l])
    y_prompt = _scatter_call(slots_p, yp, x1p, m, row_p, n_p, t_p).reshape(x_prompt.shape)
    y_sample = _scatter_call(slots_s, ys, x1s, m, row_s, n_s, t_s).reshape(x_sample.shape)
    new_k = jnp.transpose(kv_out[0].reshape(n_p, N_HEADS, 2, HEAD_DIM, t_p), (0, 4, 1, 2, 3))
    new_k = new_k.reshape(n_p, 1, t_p, N_HEADS, 2, HEAD_DIM)
    new_v = kv_out[1].reshape(n_p, 1, t_p, N_HEADS, V_HEAD_DIM)
    return (y_prompt, y_sample, new_k, new_v)
```

```python
import functools
import math

import jax
import jax.numpy as jnp
import numpy as np
from jax import lax
from jax.experimental import pallas as pl
from jax.experimental.pallas import tpu as pltpu

D = 1024
N_HEADS = 8
HEAD_DIM = 64
V_HEAD_DIM = 128
GRID_W = 64
ROPE_BASE = 10000.0
CHUNK = 128
N_GROUPS = 8
N_EXPERTS = 16
CAPACITY_FACTOR = 2
D_EXPERT = 2048
N_SEG = 7
K_SEG = 1
EPS = 1e-6
LAMBDA_INIT = 0.8 - 0.6 * math.exp(-0.3 * 0)

LANES = 128
MOD_ROWS = 8
VMEM_LIMIT = 56 * 1024 * 1024

BF = jnp.bfloat16
F32 = jnp.float32


def _dot(a, b):
    return jnp.dot(a, b, preferred_element_type=F32)


def _dot_nt(a, b):
    return lax.dot_general(a, b, (((1,), (1,)), ((), ())), preferred_element_type=F32)


def _split_bf16(x):
    hi = x.astype(BF)
    lo = (x - hi.astype(F32)).astype(BF)
    return hi, lo


def _sigmoid(x):
    return 0.5 * jnp.tanh(0.5 * x) + 0.5


def _rms(x, g):
    return x * lax.rsqrt(jnp.mean(x * x, axis=-1, keepdims=True) + EPS) * g


def _params(n_grid_dims):
    return pltpu.CompilerParams(
        dimension_semantics=("arbitrary",) * n_grid_dims, vmem_limit_bytes=VMEM_LIMIT)


def _const_spec(shape):
    nd = len(shape)
    return pl.BlockSpec(shape, lambda *_: (0,) * nd)


N_MOD = 6


def _prep_kernel(cctx_ref, c_ref, wa_ref, ba_ref, lq1_ref, lk1_ref, lq2_ref, lk2_ref, wi_ref, wo_ref, wsp_ref,
                 m_ref, lam_ref, wib_ref, wkt_ref, wob_ref, wsp2_ref, cond_s):
    j = pl.program_id(0)
    n_lat = c_ref.shape[0]

    @pl.when(j == 0)
    def _():
        cond_s[...] = jnp.zeros(cond_s.shape, F32)
        cond_s[0:1, :] = cctx_ref[...]
        cond_s[1:1 + n_lat, :] = c_ref[...]
        s1 = jnp.sum(lq1_ref[...] * lk1_ref[...], axis=-1, keepdims=True)
        s2 = jnp.sum(lq2_ref[...] * lk2_ref[...], axis=-1, keepdims=True)
        lam_ref[...] = jnp.broadcast_to(jnp.exp(s1) - jnp.exp(s2) + LAMBDA_INIT, lam_ref.shape)
        wob_ref[...] = wo_ref[...].astype(BF)
        for p in range(N_GROUPS // 2):
            wsp2_ref[p] = jnp.concatenate([wsp_ref[2 * p], wsp_ref[2 * p + 1]], axis=1).astype(BF)

    @pl.when(j < N_MOD)
    def _():
        c = cond_s[...]
        a_hi, a_lo = _split_bf16(c * jax.nn.sigmoid(c))
        w_hi, w_lo = _split_bf16(wa_ref[...])
        m_ref[...] = _dot(a_hi, w_hi) + _dot(a_lo, w_hi) + _dot(a_hi, w_lo) + ba_ref[...]

    w = wi_ref[...]
    wib_ref[...] = w.astype(BF)

    @pl.when(j == K_SEG)
    def _():
        wkt_ref[...] = w.T.astype(BF)


def _prep_call(c_ctx, c, w_ada, b_ada, lq1, lk1, lq2, lk2, w_in, w_out, w_spatial):
    vec = _const_spec((1, HEAD_DIM))
    mod_block = lambda j: (0, jnp.minimum(j, N_MOD - 1))
    pairs = (N_GROUPS // 2, CHUNK, 2 * CHUNK)
    return pl.pallas_call(
        _prep_kernel,
        grid=(N_SEG,),
        in_specs=[
            _const_spec((1, D)), _const_spec(c.shape),
            pl.BlockSpec((D, D), mod_block),
            pl.BlockSpec((1, D), mod_block),
            vec, vec, vec, vec,
            pl.BlockSpec((D, D), lambda j: (0, j)),
            _const_spec((D, D)),
            _const_spec((N_GROUPS, CHUNK, CHUNK)),
        ],
        out_specs=[
            pl.BlockSpec((MOD_ROWS, D), mod_block),
            _const_spec((MOD_ROWS, LANES)),
            pl.BlockSpec((D, D), lambda j: (0, j)),
            _const_spec((D, D)),
            _const_spec((D, D)),
            _const_spec(pairs),
        ],
        out_shape=[
            jax.ShapeDtypeStruct((MOD_ROWS, N_MOD * D), F32),
            jax.ShapeDtypeStruct((MOD_ROWS, LANES), F32),
            jax.ShapeDtypeStruct((D, N_SEG * D), BF),
            jax.ShapeDtypeStruct((D, D), BF),
            jax.ShapeDtypeStruct((D, D), BF),
            jax.ShapeDtypeStruct(pairs, BF),
        ],
        scratch_shapes=[pltpu.VMEM((MOD_ROWS, D), F32)],
        compiler_params=_params(1),
        name="prep",
    )(c_ctx, c, w_ada, b_ada, lq1, lk1, lq2, lk2, w_in, w_out, w_spatial)


def _head_norm(t, g, gsum, gbcast):
    ss = _dot((t * t).astype(BF), gsum)
    inv = lax.rsqrt(ss * (1.0 / HEAD_DIM) + EPS)
    hi, lo = _split_bf16(inv)
    bc = _dot(jnp.concatenate([hi, lo], axis=-1), gbcast)
    return t * bc * g


def _lane_tile(tab):
    return jnp.concatenate([tab] * (D // LANES), axis=1)


def _rope(t, cos, sin_signed):
    q = HEAD_DIM // 4
    lane = lax.broadcasted_iota(jnp.int32, (1, D), 1)
    first_half = (lane % (HEAD_DIM // 2)) < q
    fwd = pltpu.roll(t, D - q, axis=1)
    bwd = pltpu.roll(t, q, axis=1)
    return t * _lane_tile(cos) + jnp.where(first_half, fwd, bwd) * _lane_tile(sin_signed)


def _rope_t(t3, cos_t, sin_signed_t):
    q = HEAD_DIM // 4
    rot = jnp.concatenate([t3[:, q:2 * q], t3[:, 0:q], t3[:, 3 * q:4 * q], t3[:, 2 * q:3 * q]], axis=1)
    return t3 * cos_t[None] + rot * sin_signed_t[None]


def _inproj_kernel(rope, row_of_step, x_ref, m_ref, vecs_ref, w_ref, wkt_ref, kgt_ref, gsum_ref,
                   gbcast_ref, wsp_ref, bsp_ref, *rest):
    if rope:
        tab_ref, tab_t_ref, qst_out, kt_out, v_out = rest
    else:
        qst_out, kt_out, v_out = rest
    tm = x_ref.shape[0]
    m = m_ref[pl.ds(row_of_step(pl.program_id(0)), 1), :]
    shift1, scale1 = m[:, 0:D], m[:, D:2 * D]
    n1g, qg, lng, lnb = (vecs_ref[r:r + 1, :] for r in range(4))
    h = (_rms(x_ref[...], n1g) * (1.0 + scale1) + shift1).astype(BF)

    def seg(j):
        return _dot(h, w_ref[:, j * D:(j + 1) * D])

    q = _head_norm(seg(0), qg, gsum_ref[...], gbcast_ref[...])
    if rope:
        q = _rope(q, tab_ref[:, 0:LANES], tab_ref[:, LANES:2 * LANES])
    qst_out[:, 0:D] = q.astype(qst_out.dtype)

    k3 = _dot_nt(wkt_ref[...], h).reshape(D // HEAD_DIM, HEAD_DIM, tm)
    inv = lax.rsqrt(jnp.mean(k3 * k3, axis=1, keepdims=True) + EPS)
    k3 = k3 * inv * kgt_ref[...][None]
    if rope:
        k3 = _rope_t(k3, tab_t_ref[0:HEAD_DIM, :], tab_t_ref[HEAD_DIM:2 * HEAD_DIM, :])
    kt = k3.reshape(D, tm).astype(kt_out.dtype)
    width = kt_out.shape[2]
    for r in range(kt_out.shape[0]):
        kt_out[r] = kt[:, r * width:(r + 1) * width]

    v_out[...] = seg(2).astype(v_out.dtype)
    qst_out[:, D:2 * D] = _sigmoid(seg(5)).astype(qst_out.dtype)

    zv = jax.nn.gelu(seg(4))
    mu = jnp.mean(zv, axis=-1, keepdims=True)
    zc = zv - mu
    var = jnp.mean(zc * zc, axis=-1, keepdims=True)
    zvn = (zc * lax.rsqrt(var + EPS) * lng + lnb).astype(BF)
    pre = jax.nn.gelu(seg(3)) * _sigmoid(seg(6))
    blank = jnp.zeros((CHUNK, LANES), BF)
    for c in range(tm // CHUNK):
        rows = slice(c * CHUNK, (c + 1) * CHUNK)
        for p in range(N_GROUPS // 2):
            cols = slice(2 * p * LANES, (2 * p + 2) * LANES)
            z = zvn[rows, cols]
            z_diag = jnp.concatenate([jnp.concatenate([z[:, :LANES], blank], axis=1),
                                      jnp.concatenate([blank, z[:, LANES:]], axis=1)], axis=0)
            sp = _dot(wsp_ref[p], z_diag) + bsp_ref[:, cols]
            out_cols = slice(2 * D + 2 * p * LANES, 2 * D + (2 * p + 2) * LANES)
            qst_out[rows, out_cols] = (pre[rows, cols] * sp).astype(qst_out.dtype)


INPROJ_ROWS = 512


def _inproj_call(x, m, row_of_req, rope_tabs, kv_dtype, consts, n_req, n_tok):
    t = x.shape[0]
    tm = INPROJ_ROWS
    per_req = max(1, n_tok // tm)
    per_tile = max(1, tm // n_tok)
    rope = rope_tabs is not None

    def row_of_step(i):
        return row_of_req(i * per_tile // per_req)

    tok = pl.BlockSpec((tm, D), lambda i: (i, 0))
    in_specs = [
        tok,
        _const_spec((MOD_ROWS, 6 * D)),
        _const_spec((MOD_ROWS, D)),
        pl.BlockSpec((D, N_SEG * D), lambda i: (0, 0), pipeline_mode=pl.Buffered(1)),
        _const_spec((D, D)),
        _const_spec((HEAD_DIM, tm)),
        _const_spec((D, LANES)), _const_spec((2 * LANES, D)),
        _const_spec((N_GROUPS // 2, CHUNK, 2 * CHUNK)), _const_spec((CHUNK, D)),
    ]
    args = [x, m, consts["vecs"], consts["w_in"], consts["wkt"], consts["kgt"], consts["gsum"],
            consts["gbcast"], consts["wsp"], consts["bsp"]]
    if rope:
        in_specs += [pl.BlockSpec((tm, 2 * LANES), lambda i: (i % per_req, 0)),
                     pl.BlockSpec((2 * HEAD_DIM, tm), lambda i: (0, i % per_req))]
        args += list(rope_tabs)
    kt_spec = pl.BlockSpec((per_tile, D, tm // per_tile), lambda i: (i, 0, 0))
    return pl.pallas_call(
        functools.partial(_inproj_kernel, rope, row_of_step),
        grid=(t // tm,),
        in_specs=in_specs,
        out_specs=[pl.BlockSpec((tm, 3 * D), lambda i: (i, 0)), kt_spec, tok],
        out_shape=[
            jax.ShapeDtypeStruct((t, 3 * D), BF),
            jax.ShapeDtypeStruct((n_req * per_req, D, tm // per_tile), kv_dtype),
            jax.ShapeDtypeStruct((t, D), kv_dtype),
        ],
        compiler_params=_params(1),
        name="inproj_rope" if rope else "inproj",
    )(*args)


SCORE_GROUP_ELEMS = 1 << 22
ATTN_ROWS = 256


def _attn_kernel(has_ctx, heads_per_group, row_of_req, qst_ref, kt_ref, v_ref, *rest):
    if has_ctx:
        kct_ref, vc_ref, *rest = rest
    lam_ref, vecs_ref, x_ref, m_ref, wo_ref, wr_ref, x1_out, h2_out, lg_out, merged_s = rest
    tq = qst_ref.shape[0]
    key_blocks, block_keys = kt_ref.shape[0], kt_ref.shape[2]
    n2g = vecs_ref[4:5, :]
    sub_g = vecs_ref[5:6, 0:V_HEAD_DIM]
    lam = lam_ref[0:1, 0:1]
    lane = lax.broadcasted_iota(jnp.int32, (1, LANES), 1)
    first = lane < HEAD_DIM
    zero = jnp.zeros((), BF)
    ones_col = jnp.where(lane == 0, 1.0, 0.0).astype(BF)

    def values(ref, rows, h):
        v = ref[rows, h * V_HEAD_DIM:(h + 1) * V_HEAD_DIM].astype(BF)
        return jnp.concatenate([v, jnp.broadcast_to(ones_col, (v.shape[0], LANES))], axis=1)

    def head_cols(h):
        return slice(h * LANES, (h + 1) * LANES)

    def head_scores(h):
        q = qst_ref[:, head_cols(h)]
        qz = jnp.concatenate([jnp.where(first, q, zero), jnp.where(first, zero, q)], axis=0)
        parts = [_dot(qz, kt_ref[j, head_cols(h), :].astype(BF)) for j in range(key_blocks)]
        if has_ctx:
            parts.append(_dot(qz, kct_ref[0, head_cols(h), :].astype(BF)))
        return parts

    def row_max(scores):
        return functools.reduce(jnp.maximum, [jnp.max(s, axis=-1, keepdims=True) for s in scores])

    def head_pv(h, scores, mx):
        vals = [values(v_ref, slice(j * block_keys, (j + 1) * block_keys), h) for j in range(key_blocks)]
        if has_ctx:
            vals.append(values(vc_ref, slice(None), h))
        return functools.reduce(jnp.add, [_dot(jnp.exp2(s - mx).astype(BF), v) for s, v in zip(scores, vals)])

    def head_finish(h, ob):
        den = ob[:, V_HEAD_DIM:V_HEAD_DIM + 1]
        o = ob[:tq, :V_HEAD_DIM] * (1.0 / den[:tq]) - ob[tq:, :V_HEAD_DIM] * (lam / den[tq:])
        o = _rms(o, sub_g) * (1.0 - LAMBDA_INIT)
        sga = qst_ref[:, D + h * LANES:D + (h + 1) * LANES].astype(F32)
        tb = qst_ref[:, 2 * D + h * LANES:2 * D + (h + 1) * LANES].astype(F32)
        merged_s[:, head_cols(h)] = (sga * o + tb).astype(BF)

    for g0 in range(0, N_HEADS, heads_per_group):
        group = range(g0, g0 + heads_per_group)
        scores = [head_scores(h) for h in group]
        maxes = [row_max(s) for s in scores]
        outs = [head_pv(h, s, mx) for h, s, mx in zip(group, scores, maxes)]
        for h, ob in zip(group, outs):
            head_finish(h, ob)

    m = m_ref[pl.ds(row_of_req(pl.program_id(0)), 1), :]
    gate1, shift2, scale2 = m[:, 2 * D:3 * D], m[:, 3 * D:4 * D], m[:, 4 * D:5 * D]
    x1 = x_ref[...] + gate1 * _dot(merged_s[...], wo_ref[...])
    x1_out[...] = x1
    h2 = _rms(x1, n2g) * (1.0 + scale2) + shift2
    h2_out[...] = h2.astype(BF)
    hi, lo = _split_bf16(h2)
    wr_hi, wr_lo = wr_ref[:, 0:LANES], wr_ref[:, LANES:2 * LANES]
    lg_out[...] = _dot(hi, wr_hi) + _dot(lo, wr_hi) + _dot(hi, wr_lo)


def _attn_call(qst, kt, v, ctx, lamv, x, m, row_of_req, consts, n_req, n_tok):
    tq = min(n_tok, ATTN_ROWS)
    nqb = n_tok // tq
    key_blocks = kt.shape[0] // n_req
    has_ctx = ctx is not None
    tok = pl.BlockSpec((tq, D), lambda b, i: (b * nqb + i, 0))
    in_specs = [pl.BlockSpec((tq, 3 * D), lambda b, i: (b * nqb + i, 0)),
                pl.BlockSpec((key_blocks, D, kt.shape[2]), lambda b, i: (b, 0, 0)),
                pl.BlockSpec((n_tok, D), lambda b, i: (b, 0))]
    args = [qst, kt, v]
    if has_ctx:
        n_ctx = ctx[0].shape[2]
        in_specs += [pl.BlockSpec((1, D, n_ctx), lambda b, i: (b, 0, 0)),
                     pl.BlockSpec((n_ctx, D), lambda b, i: (b, 0))]
        args += list(ctx)
    in_specs += [
        _const_spec((MOD_ROWS, LANES)), _const_spec((MOD_ROWS, D)),
        tok,
        _const_spec((MOD_ROWS, 6 * D)),
        _const_spec((D, D)),
        _const_spec((D, 2 * LANES)),
    ]
    args += [lamv, consts["vecs"], x, m, consts["w_out"], consts["wr"]]
    t = n_req * n_tok
    n_keys = n_tok + (ctx[0].shape[2] if has_ctx else 0)
    heads_per_group = max(1, min(N_HEADS, SCORE_GROUP_ELEMS // (2 * tq * n_keys)))
    while N_HEADS % heads_per_group:
        heads_per_group -= 1
    return pl.pallas_call(
        functools.partial(_attn_kernel, has_ctx, heads_per_group, row_of_req),
        grid=(n_req, nqb),
        in_specs=in_specs,
        out_specs=[tok, tok, pl.BlockSpec((tq, LANES), lambda b, i: (b * nqb + i, 0))],
        out_shape=[
            jax.ShapeDtypeStruct((t, D), F32),
            jax.ShapeDtypeStruct((t, D), BF),
            jax.ShapeDtypeStruct((t, LANES), F32),
        ],
        scratch_shapes=[pltpu.VMEM((tq, D), BF)],
        compiler_params=_params(2),
        name="attn_ctx" if has_ctx else "attn",
    )(*args)


GATHER_ROWS = 512
KEY_BITS = 31
ROUTE_UNROLL_ELEMS = 1 << 22
ROUTE_STEP_TOKENS = 2048
PREFIX_BLOCK = 128


def _route_kernel(cap, n_tok, unroll, lg_ref, h2_ref, before_ref, xe_ref, gate_ref, slot_ref,
                  aff_t_s, slot_t_s, p_s):
    reqs = range(lg_ref.shape[0] // n_tok)
    lane = lax.broadcasted_iota(jnp.int32, (1, LANES), 1)
    valid = lane < N_EXPERTS

    def rows(r):
        return slice(r * n_tok, (r + 1) * n_tok)

    def affinity(r):
        lg = jnp.where(valid, lg_ref[rows(r), :], -1e30)
        ex = jnp.where(valid, jnp.exp(lg - jnp.max(lg, axis=-1, keepdims=True)), 0.0)
        return ex / jnp.sum(ex, axis=-1, keepdims=True)

    def count(mask):
        return jnp.sum(jnp.where(mask, 1.0, 0.0), axis=0, keepdims=True)

    affs = [affinity(r) for r in reqs]
    kth_bits = [jnp.zeros((1, LANES), jnp.int32) for _ in reqs]
    for bit in range(KEY_BITS - 1, -1, -1):
        for r in reqs:
            cand = kth_bits[r] | (1 << bit)
            enough = count(affs[r] >= lax.bitcast_convert_type(cand, F32)) >= cap
            kth_bits[r] = jnp.where(enough, cand, kth_bits[r])

    before = before_ref[...]

    def earlier_count(flags):
        blocks, running = [], jnp.zeros((1, LANES), F32)
        for b in range(n_tok // PREFIX_BLOCK):
            blk = flags[b * PREFIX_BLOCK:(b + 1) * PREFIX_BLOCK]
            blocks.append(_dot(before, blk.astype(BF)) + running)
            running = running + jnp.sum(blk, axis=0, keepdims=True)
        return jnp.concatenate(blocks, axis=0)

    for r in reqs:
        aff = affs[r]
        kth = lax.bitcast_convert_type(kth_bits[r], F32)
        above = aff > kth
        tied = aff == kth
        need = cap - count(above)
        tied_before = earlier_count(jnp.where(tied, 1.0, 0.0))
        chosen = jnp.where(above, 1.0, jnp.where(tied, jnp.where(tied_before < need, 1.0, 0.0), 0.0))
        slot = earlier_count(chosen)
        slot = jnp.where(valid, jnp.where(chosen > 0.0, slot, float(cap)), float(cap))
        slot_ref[rows(r), :] = slot
        slot_t_s[r] = slot.T
        aff_t_s[r] = aff.T

    slot_ids = lax.broadcasted_iota(jnp.int32, (cap, 1), 0).astype(F32)

    def per_expert(e, carry):
        for r in reqs:
            hit = slot_t_s[r, pl.ds(e, 1), :] == slot_ids
            p_s[r, pl.ds(pl.multiple_of(e * cap, cap), cap), :] = jnp.where(hit, 1.0, 0.0).astype(BF)
            gate = jnp.sum(jnp.where(hit, aff_t_s[r, pl.ds(e, 1), :], 0.0), axis=-1, keepdims=True)
            gate_ref[e, r * cap:(r + 1) * cap, :] = gate
        return carry

    lax.fori_loop(0, N_EXPERTS, per_expert, 0, unroll=unroll)
    n_rows = min(GATHER_ROWS, N_EXPERTS * cap)
    e_per = n_rows // cap
    for r in reqs:
        h2 = h2_ref[rows(r), :]
        for t in range(N_EXPERTS * cap // n_rows):
            xe = _dot(p_s[r, t * n_rows:(t + 1) * n_rows, :], h2).astype(BF)
            xe_ref[t * e_per:(t + 1) * e_per, r * cap:(r + 1) * cap, :] = xe.reshape(e_per, cap, D)


def _capacity(n_tok):
    return max(1, CAPACITY_FACTOR * n_tok // N_EXPERTS)


def _route_call(lg, h2, n_req, n_tok):
    cap = _capacity(n_tok)
    per_step = max(1, min(n_req, ROUTE_STEP_TOKENS // n_tok))
    before = jnp.asarray((np.arange(PREFIX_BLOCK)[None, :] < np.arange(PREFIX_BLOCK)[:, None]).astype(BF))
    unroll = max(1, min(N_EXPERTS, ROUTE_UNROLL_ELEMS // (per_step * cap * n_tok)))
    return pl.pallas_call(
        functools.partial(_route_kernel, cap, n_tok, unroll),
        grid=(n_req // per_step,),
        in_specs=[
            pl.BlockSpec((per_step * n_tok, LANES), lambda b: (b, 0)),
            pl.BlockSpec((per_step * n_tok, D), lambda b: (b, 0)),
            _const_spec((PREFIX_BLOCK, PREFIX_BLOCK)),
        ],
        out_specs=[
            pl.BlockSpec((N_EXPERTS, per_step * cap, D), lambda b: (0, b, 0)),
            pl.BlockSpec((N_EXPERTS, per_step * cap, 1), lambda b: (0, b, 0)),
            pl.BlockSpec((per_step * n_tok, LANES), lambda b: (b, 0)),
        ],
        out_shape=[
            jax.ShapeDtypeStruct((N_EXPERTS, n_req * cap, D), BF),
            jax.ShapeDtypeStruct((N_EXPERTS, n_req * cap, 1), F32),
            jax.ShapeDtypeStruct((n_req * n_tok, LANES), F32),
        ],
        scratch_shapes=[
            pltpu.VMEM((per_step, LANES, n_tok), F32),
            pltpu.VMEM((per_step, LANES, n_tok), F32),
            pltpu.VMEM((per_step, N_EXPERTS * cap, n_tok), BF),
        ],
        compiler_params=_params(1),
        name="route",
    )(lg, h2, before)


EXPERT_BLOCK = 1024
EXPERT_SUB = 256


def _experts_kernel(xa_ref, xb_ref, ga_ref, gb_ref, wg_ref, wu_ref, wd_ref, ya_ref, yb_ref, acc_s):
    f = pl.program_id(1)
    ra = xa_ref.shape[1]
    x = jnp.concatenate([xa_ref[0], xb_ref[0]], axis=0)
    for c in range(wg_ref.shape[2] // EXPERT_SUB):
        cs = slice(c * EXPERT_SUB, (c + 1) * EXPERT_SUB)
        gate = _dot(x, wg_ref[0, :, cs].astype(BF))
        up = _dot(x, wu_ref[0, :, cs].astype(BF))
        hid = (gate * _sigmoid(gate) * up).astype(BF)
        down = _dot(hid, wd_ref[0, cs, :].astype(BF))
        if c == 0:
            acc_s[...] = jnp.where(f == 0, 0.0, acc_s[...]) + down
        else:
            acc_s[...] += down

    @pl.when(f == pl.num_programs(1) - 1)
    def _():
        ya_ref[0] = (acc_s[0:ra, :] * ga_ref[0]).astype(ya_ref.dtype)
        yb_ref[0] = (acc_s[ra:, :] * gb_ref[0]).astype(yb_ref.dtype)


def _experts_call(xa, xb, ga, gb, wg, wu, wd):
    ra, rb = xa.shape[1], xb.shape[1]
    tf = EXPERT_BLOCK
    xa_spec = pl.BlockSpec((1, ra, D), lambda e, f: (e, 0, 0))
    xb_spec = pl.BlockSpec((1, rb, D), lambda e, f: (e, 0, 0))
    return pl.pallas_call(
        _experts_kernel,
        grid=(N_EXPERTS, D_EXPERT // tf),
        in_specs=[
            xa_spec, xb_spec,
            pl.BlockSpec((1, ra, 1), lambda e, f: (e, 0, 0)),
            pl.BlockSpec((1, rb, 1), lambda e, f: (e, 0, 0)),
            pl.BlockSpec((1, D, tf), lambda e, f: (e, 0, f)),
            pl.BlockSpec((1, D, tf), lambda e, f: (e, 0, f)),
            pl.BlockSpec((1, tf, D), lambda e, f: (e, f, 0)),
        ],
        out_specs=[xa_spec, xb_spec],
        out_shape=[
            jax.ShapeDtypeStruct((N_EXPERTS, ra, D), BF),
            jax.ShapeDtypeStruct((N_EXPERTS, rb, D), BF),
        ],
        scratch_shapes=[pltpu.VMEM((ra + rb, D), F32)],
        compiler_params=_params(2),
        name="experts",
    )(xa, xb, ga, gb, wg, wu, wd)


SCATTER_STEP_TOKENS = 1024


def _scatter_kernel(cap, n_tok, row_of_req, slot_ref, y_ref, x1_ref, m_ref, expand_ref, out_ref):
    per_step = slot_ref.shape[0] // n_tok
    slot_ids = (lax.broadcasted_iota(jnp.int32, (1, N_EXPERTS * cap), 1) % cap).astype(F32)
    for r in range(per_step):
        rows = slice(r * n_tok, (r + 1) * n_tok)
        m = m_ref[pl.ds(row_of_req(pl.program_id(0) * per_step + r), 1), :]
        gate2 = m[:, 5 * D:6 * D]
        slot_wide = _dot(slot_ref[rows, :].astype(BF), expand_ref[...])
        onehot = jnp.where(slot_wide == slot_ids, 1.0, 0.0).astype(BF)
        y = y_ref[:, r * cap:(r + 1) * cap, :].reshape(N_EXPERTS * cap, D)
        out_ref[rows, :] = x1_ref[rows, :] + gate2 * _dot(onehot, y)


def _scatter_call(slots, y, x1, m, row_of_req, n_req, n_tok):
    cap = _capacity(n_tok)
    per_step = max(1, min(n_req, SCATTER_STEP_TOKENS // n_tok))
    expand = jnp.asarray((np.arange(LANES)[:, None] == np.arange(N_EXPERTS * cap)[None, :] // cap).astype(BF))
    tok = pl.BlockSpec((per_step * n_tok, D), lambda b: (b, 0))
    return pl.pallas_call(
        functools.partial(_scatter_kernel, cap, n_tok, row_of_req),
        grid=(n_req // per_step,),
        in_specs=[
            pl.BlockSpec((per_step * n_tok, LANES), lambda b: (b, 0)),
            pl.BlockSpec((N_EXPERTS, per_step * cap, D), lambda b: (0, b, 0)),
            tok,
            _const_spec((MOD_ROWS, 6 * D)),
            _const_spec((LANES, N_EXPERTS * cap)),
        ],
        out_specs=tok,
        out_shape=jax.ShapeDtypeStruct((n_req * n_tok, D), F32),
        compiler_params=_params(1),
        name="scatter",
    )(slots, y, x1, m, expand)


def _rope_tables(n_tokens):
    rows = n_tokens // GRID_W
    row = np.broadcast_to(np.arange(rows, dtype=np.float32)[:, None], (rows, GRID_W)).reshape(-1)
    col = np.broadcast_to(np.arange(GRID_W, dtype=np.float32)[None, :], (rows, GRID_W)).reshape(-1)
    half = HEAD_DIM // 4
    inv_freq = (np.float32(ROPE_BASE) ** (-np.arange(half, dtype=np.float32) / np.float32(half))).astype(np.float32)
    ar = row[:, None] * inv_freq
    ac = col[:, None] * inv_freq
    ang = np.concatenate([ar, ar, ac, ac], axis=-1).astype(np.float64)
    cos, sin = np.cos(ang).astype(np.float32), np.sin(ang).astype(np.float32)
    first_half = (np.arange(HEAD_DIM) % (HEAD_DIM // 2)) < (HEAD_DIM // 4)
    sin_signed = np.where(first_half[None, :], -sin, sin)
    reps = LANES // HEAD_DIM
    token_major = np.concatenate([np.tile(cos, (1, reps)), np.tile(sin_signed, (1, reps))], axis=1)
    transposed = np.concatenate([cos.T, sin_signed.T], axis=0)
    return jnp.asarray(token_major), jnp.asarray(transposed)


def kernel(x_prompt, x_sample, cache_k, cache_v, c, c_ctx, w_ada, b_ada, norm1_g, norm2_g, w_in, q_norm_g, k_norm_g, lambda_q1, lambda_k1, lambda_q2, lambda_k2, subln_g, gmlp_ln_g, gmlp_ln_b, w_spatial, b_spatial, w_out, w_router, w_gate_e, w_up_e, w_down_e):
    n_p, t_p = x_prompt.shape[0], x_prompt.shape[1]
    n_s, t_s = x_sample.shape[0], x_sample.shape[1]
    n_ctx = cache_k.shape[2]
    l = 0

    m, lamv, w_in_bf, wkt, w_out_bf, wsp_pairs = _prep_call(
        c_ctx[None, :], c, w_ada[l], b_ada[l][None, :], lambda_q1[l][None, :], lambda_k1[l][None, :],
        lambda_q2[l][None, :], lambda_k2[l][None, :], w_in[l], w_out[l], w_spatial[l])

    gsum_np = (np.arange(D)[:, None] // HEAD_DIM == np.arange(LANES)[None, :]).astype(BF)
    gsum = jnp.asarray(gsum_np)
    wr = jnp.pad(w_router[l], ((0, 0), (0, LANES - N_EXPERTS)))
    wr_hi = wr.astype(BF)
    qg = jnp.tile(q_norm_g[l] * (HEAD_DIM ** -0.5 * math.log2(math.e)), D // HEAD_DIM)

    def rows_of(*vectors):
        rows = [jnp.tile(vec, D // vec.shape[0])[None, :] for vec in vectors]
        return jnp.concatenate(rows + [jnp.zeros((MOD_ROWS - len(rows), D), F32)], axis=0)

    consts = {
        "vecs": rows_of(norm1_g[l], qg, gmlp_ln_g[l], gmlp_ln_b[l], norm2_g[l], subln_g[l]),
        "w_in": w_in_bf,
        "wkt": wkt,
        "kgt": jnp.broadcast_to(k_norm_g[l][:, None], (HEAD_DIM, INPROJ_ROWS)),
        "gsum": gsum,
        "gbcast": jnp.asarray(np.concatenate([gsum_np.T, gsum_np.T], axis=0)),
        "wsp": wsp_pairs,
        "bsp": jnp.repeat(b_spatial[l].T, D // N_GROUPS, axis=1),
        "w_out": w_out_bf,
        "wr": jnp.concatenate([wr_hi, (wr - wr_hi.astype(F32)).astype(BF)], axis=1),
    }

    passes = (
        (x_prompt, n_p, t_p, lambda b: 0, False),
        (x_sample, n_s, t_s, lambda b: 1 + b, True),
    )
    outs = []
    kv_out = None
    for x, n_req, n_tok, row_of_req, is_sample in passes:
        xf = x.reshape(n_req * n_tok, D)
        tabs = _rope_tables(n_tok) if is_sample else None
        qst, kt, v = _inproj_call(xf, m, row_of_req, tabs, BF if is_sample else F32, consts, n_req, n_tok)
        if is_sample:
            ctx = (jnp.transpose(cache_k[:, l], (0, 2, 3, 4, 1)).reshape(n_req, D, n_ctx),
                   cache_v[:, l].reshape(n_req * n_ctx, D).astype(BF))
        else:
            ctx = None
            kv_out = (kt, v)
        x1, h2, lg = _attn_call(qst, kt, v, ctx, lamv, xf, m, row_of_req, consts, n_req, n_tok)
        xe, gates, slots = _route_call(lg, h2, n_req, n_tok)
        outs.append((x1, xe, gates, slots, row_of_req))

    (x1p, xep, gp, slots_p, row_p), (x1s, xes, gs, slots_s, row_s) = outs
    yp, ys = _experts_call(xep, xes, gp, gs, w_gate_e[l], w_up_e[l], w_down_e[l])
    y_prompt = _scatter_call(slots_p, yp, x1p, m, row_p, n_p, t_p).reshape(x_prompt.shape)
    y_sample = _scatter_call(slots_s, ys, x1s, m, row_s, n_s, t_s).reshape(x_sample.shape)
    new_k = jnp.transpose(kv_out[0].reshape(n_p, N_HEADS, 2, HEAD_DIM, t_p), (0, 4, 1, 2, 3))
    new_k = new_k.reshape(n_p, 1, t_p, N_HEADS, 2, HEAD_DIM)
    new_v = kv_out[1].reshape(n_p, 1, t_p, N_HEADS, V_HEAD_DIM)
    return (y_prompt, y_sample, new_k, new_v)
```

```python
import functools
import math

import jax
import jax.numpy as jnp
import numpy as np
from jax import lax
from jax.experimental import pallas as pl
from jax.experimental.pallas import tpu as pltpu

D = 1024
N_HEADS = 8
HEAD_DIM = 64
V_HEAD_DIM = 128
GRID_W = 64
ROPE_BASE = 10000.0
CHUNK = 128
N_GROUPS = 8
N_EXPERTS = 16
CAPACITY_FACTOR = 2
D_EXPERT = 2048
N_SEG = 7
K_SEG = 1
EPS = 1e-6
LAMBDA_INIT = 0.8 - 0.6 * math.exp(-0.3 * 0)

LANES = 128
MOD_ROWS = 8
VMEM_LIMIT = 56 * 1024 * 1024

BF = jnp.bfloat16
F32 = jnp.float32


def _dot(a, b):
    return jnp.dot(a, b, preferred_element_type=F32)


def _dot_nt(a, b):
    return lax.dot_general(a, b, (((1,), (1,)), ((), ())), preferred_element_type=F32)


def _split_bf16(x):
    hi = x.astype(BF)
    lo = (x - hi.astype(F32)).astype(BF)
    return hi, lo


def _sigmoid(x):
    return 0.5 * jnp.tanh(0.5 * x) + 0.5


def _rms(x, g):
    return x * lax.rsqrt(jnp.mean(x * x, axis=-1, keepdims=True) + EPS) * g


def _params(n_grid_dims):
    return pltpu.CompilerParams(
        dimension_semantics=("arbitrary",) * n_grid_dims, vmem_limit_bytes=VMEM_LIMIT)


def _const_spec(shape):
    nd = len(shape)
    return pl.BlockSpec(shape, lambda *_: (0,) * nd)


N_MOD = 6


def _prep_kernel(cctx_ref, c_ref, wa_ref, ba_ref, lq1_ref, lk1_ref, lq2_ref, lk2_ref, wi_ref, wo_ref, wsp_ref,
                 m_ref, lam_ref, wib_ref, wkt_ref, wob_ref, wsp2_ref, cond_s):
    j = pl.program_id(0)
    n_lat = c_ref.shape[0]

    @pl.when(j == 0)
    def _():
        cond_s[...] = jnp.zeros(cond_s.shape, F32)
        cond_s[0:1, :] = cctx_ref[...]
        cond_s[1:1 + n_lat, :] = c_ref[...]
        s1 = jnp.sum(lq1_ref[...] * lk1_ref[...], axis=-1, keepdims=True)
        s2 = jnp.sum(lq2_ref[...] * lk2_ref[...], axis=-1, keepdims=True)
        lam_ref[...] = jnp.broadcast_to(jnp.exp(s1) - jnp.exp(s2) + LAMBDA_INIT, lam_ref.shape)
        wob_ref[...] = wo_ref[...].astype(BF)
        for p in range(N_GROUPS // 2):
            wsp2_ref[p] = jnp.concatenate([wsp_ref[2 * p], wsp_ref[2 * p + 1]], axis=1).astype(BF)

    @pl.when(j < N_MOD)
    def _():
        c = cond_s[...]
        a_hi, a_lo = _split_bf16(c * jax.nn.sigmoid(c))
        w_hi, w_lo = _split_bf16(wa_ref[...])
        m_ref[...] = _dot(a_hi, w_hi) + _dot(a_lo, w_hi) + _dot(a_hi, w_lo) + ba_ref[...]

    w = wi_ref[...]
    wib_ref[...] = w.astype(BF)

    @pl.when(j == K_SEG)
    def _():
        wkt_ref[...] = w.T.astype(BF)


def _prep_call(c_ctx, c, w_ada, b_ada, lq1, lk1, lq2, lk2, w_in, w_out, w_spatial):
    vec = _const_spec((1, HEAD_DIM))
    mod_block = lambda j: (0, jnp.minimum(j, N_MOD - 1))
    pairs = (N_GROUPS // 2, CHUNK, 2 * CHUNK)
    return pl.pallas_call(
        _prep_kernel,
        grid=(N_SEG,),
        in_specs=[
            _const_spec((1, D)), _const_spec(c.shape),
            pl.BlockSpec((D, D), mod_block),
            pl.BlockSpec((1, D), mod_block),
            vec, vec, vec, vec,
            pl.BlockSpec((D, D), lambda j: (0, j)),
            _const_spec((D, D)),
            _const_spec((N_GROUPS, CHUNK, CHUNK)),
        ],
        out_specs=[
            pl.BlockSpec((MOD_ROWS, D), mod_block),
            _const_spec((MOD_ROWS, LANES)),
            pl.BlockSpec((D, D), lambda j: (0, j)),
            _const_spec((D, D)),
            _const_spec((D, D)),
            _const_spec(pairs),
        ],
        out_shape=[
            jax.ShapeDtypeStruct((MOD_ROWS, N_MOD * D), F32),
            jax.ShapeDtypeStruct((MOD_ROWS, LANES), F32),
            jax.ShapeDtypeStruct((D, N_SEG * D), BF),
            jax.ShapeDtypeStruct((D, D), BF),
            jax.ShapeDtypeStruct((D, D), BF),
            jax.ShapeDtypeStruct(pairs, BF),
        ],
        scratch_shapes=[pltpu.VMEM((MOD_ROWS, D), F32)],
        compiler_params=_params(1),
        name="prep",
    )(c_ctx, c, w_ada, b_ada, lq1, lk1, lq2, lk2, w_in, w_out, w_spatial)


def _head_norm(t, g, gsum, gbcast):
    ss = _dot((t * t).astype(BF), gsum)
    inv = lax.rsqrt(ss * (1.0 / HEAD_DIM) + EPS)
    hi, lo = _split_bf16(inv)
    bc = _dot(jnp.concatenate([hi, lo], axis=-1), gbcast)
    return t * bc * g


def _lane_tile(tab):
    return jnp.concatenate([tab] * (D // LANES), axis=1)


def _rope(t, cos, sin_signed):
    q = HEAD_DIM // 4
    lane = lax.broadcasted_iota(jnp.int32, (1, D), 1)
    first_half = (lane % (HEAD_DIM // 2)) < q
    fwd = pltpu.roll(t, D - q, axis=1)
    bwd = pltpu.roll(t, q, axis=1)
    return t * _lane_tile(cos) + jnp.where(first_half, fwd, bwd) * _lane_tile(sin_signed)


def _rope_t(t3, cos_t, sin_signed_t):
    q = HEAD_DIM // 4
    rot = jnp.concatenate([t3[:, q:2 * q], t3[:, 0:q], t3[:, 3 * q:4 * q], t3[:, 2 * q:3 * q]], axis=1)
    return t3 * cos_t[None] + rot * sin_signed_t[None]


def _inproj_kernel(rope, row_of_step, x_ref, m_ref, vecs_ref, w_ref, wkt_ref, kgt_ref, gsum_ref,
                   gbcast_ref, wsp_ref, bsp_ref, *rest):
    if rope:
        tab_ref, tab_t_ref, *rest = rest
    qst_out, kt_out, v_out = rest[-3:]
    tm = x_ref.shape[0]
    m = m_ref[pl.ds(row_of_step(pl.program_id(0)), 1), :]
    shift1, scale1 = m[:, 0:D], m[:, D:2 * D]
    n1g, qg, lng, lnb = (vecs_ref[r:r + 1, :] for r in range(4))
    h = (_rms(x_ref[...], n1g) * (1.0 + scale1) + shift1).astype(BF)

    def seg(j):
        return _dot(h, w_ref[:, j * D:(j + 1) * D])

    q = _head_norm(seg(0), qg, gsum_ref[...], gbcast_ref[...])
    if rope:
        q = _rope(q, tab_ref[:, 0:LANES], tab_ref[:, LANES:2 * LANES])
    qst_out[:, 0:D] = q.astype(qst_out.dtype)

    k3 = _dot_nt(wkt_ref[...], h).reshape(D // HEAD_DIM, HEAD_DIM, tm)
    inv = lax.rsqrt(jnp.mean(k3 * k3, axis=1, keepdims=True) + EPS)
    k3 = k3 * inv * kgt_ref[...][None]
    if rope:
        k3 = _rope_t(k3, tab_t_ref[0:HEAD_DIM, :], tab_t_ref[HEAD_DIM:2 * HEAD_DIM, :])
    kt = k3.reshape(D, tm).astype(kt_out.dtype)
    width = kt_out.shape[2]
    for r in range(kt_out.shape[0]):
        kt_out[r] = kt[:, r * width:(r + 1) * width]

    v_out[...] = seg(2).astype(v_out.dtype)
    qst_out[:, D:2 * D] = _sigmoid(seg(5)).astype(qst_out.dtype)

    zv = jax.nn.gelu(seg(4))
    mu = jnp.mean(zv, axis=-1, keepdims=True)
    zc = zv - mu
    var = jnp.mean(zc * zc, axis=-1, keepdims=True)
    zvn = (zc * lax.rsqrt(var + EPS) * lng + lnb).astype(BF)
    pre = jax.nn.gelu(seg(3)) * _sigmoid(seg(6))
    blank = jnp.zeros((CHUNK, LANES), BF)
    for c in range(tm // CHUNK):
        rows = slice(c * CHUNK, (c + 1) * CHUNK)
        for p in range(N_GROUPS // 2):
            cols = slice(2 * p * LANES, (2 * p + 2) * LANES)
            z = zvn[rows, cols]
            z_diag = jnp.concatenate([jnp.concatenate([z[:, :LANES], blank], axis=1),
                                      jnp.concatenate([blank, z[:, LANES:]], axis=1)], axis=0)
            sp = _dot(wsp_ref[p], z_diag) + bsp_ref[:, cols]
            out_cols = slice(2 * D + 2 * p * LANES, 2 * D + (2 * p + 2) * LANES)
            qst_out[rows, out_cols] = (pre[rows, cols] * sp).astype(qst_out.dtype)


INPROJ_ROWS = 512


def _inproj_call(x, m, row_of_req, rope_tabs, kv_dtype, consts, n_req, n_tok, after=None):
    t = x.shape[0]
    tm = INPROJ_ROWS
    per_req = max(1, n_tok // tm)
    per_tile = max(1, tm // n_tok)
    rope = rope_tabs is not None

    def row_of_step(i):
        return row_of_req(i * per_tile // per_req)

    tok = pl.BlockSpec((tm, D), lambda i: (i, 0))
    in_specs = [
        tok,
        _const_spec((MOD_ROWS, 6 * D)),
        _const_spec((MOD_ROWS, D)),
        pl.BlockSpec((D, N_SEG * D), lambda i: (0, 0), pipeline_mode=pl.Buffered(1)),
        _const_spec((D, D)),
        _const_spec((HEAD_DIM, tm)),
        _const_spec((D, LANES)), _const_spec((2 * LANES, D)),
        _const_spec((N_GROUPS // 2, CHUNK, 2 * CHUNK)), _const_spec((CHUNK, D)),
    ]
    args = [x, m, consts["vecs"], consts["w_in"], consts["wkt"], consts["kgt"], consts["gsum"],
            consts["gbcast"], consts["wsp"], consts["bsp"]]
    if rope:
        in_specs += [pl.BlockSpec((tm, 2 * LANES), lambda i: (i % per_req, 0)),
                     pl.BlockSpec((2 * HEAD_DIM, tm), lambda i: (0, i % per_req))]
        args += list(rope_tabs)
    if after is not None:
        in_specs += [pl.BlockSpec(memory_space=pl.ANY)]
        args += [after]
    kt_spec = pl.BlockSpec((per_tile, D, tm // per_tile), lambda i: (i, 0, 0))
    return pl.pallas_call(
        functools.partial(_inproj_kernel, rope, row_of_step),
        grid=(t // tm,),
        in_specs=in_specs,
        out_specs=[pl.BlockSpec((tm, 3 * D), lambda i: (i, 0)), kt_spec, tok],
        out_shape=[
            jax.ShapeDtypeStruct((t, 3 * D), BF),
            jax.ShapeDtypeStruct((n_req * per_req, D, tm // per_tile), kv_dtype),
            jax.ShapeDtypeStruct((t, D), kv_dtype),
        ],
        compiler_params=_params(1),
        name="inproj_rope" if rope else "inproj",
    )(*args)


SCORE_GROUP_ELEMS = 1 << 22
ATTN_ROWS = 256


def _attn_kernel(has_ctx, heads_per_group, row_of_req, qst_ref, kt_ref, v_ref, *rest):
    if has_ctx:
        kct_ref, vc_ref, *rest = rest
    lam_ref, vecs_ref, x_ref, m_ref, wo_ref, wr_ref, x1_out, h2_out, lg_out, merged_s = rest
    tq = qst_ref.shape[0]
    key_blocks, block_keys = kt_ref.shape[0], kt_ref.shape[2]
    n2g = vecs_ref[4:5, :]
    sub_g = vecs_ref[5:6, 0:V_HEAD_DIM]
    lam = lam_ref[0:1, 0:1]
    lane = lax.broadcasted_iota(jnp.int32, (1, LANES), 1)
    first = lane < HEAD_DIM
    zero = jnp.zeros((), BF)
    ones_col = jnp.where(lane == 0, 1.0, 0.0).astype(BF)

    def values(ref, rows, h):
        v = ref[rows, h * V_HEAD_DIM:(h + 1) * V_HEAD_DIM].astype(BF)
        return jnp.concatenate([v, jnp.broadcast_to(ones_col, (v.shape[0], LANES))], axis=1)

    def head_cols(h):
        return slice(h * LANES, (h + 1) * LANES)

    def head_scores(h):
        q = qst_ref[:, head_cols(h)]
        qz = jnp.concatenate([jnp.where(first, q, zero), jnp.where(first, zero, q)], axis=0)
        parts = [_dot(qz, kt_ref[j, head_cols(h), :].astype(BF)) for j in range(key_blocks)]
        if has_ctx:
            parts.append(_dot(qz, kct_ref[0, head_cols(h), :].astype(BF)))
        return parts

    def row_max(scores):
        return functools.reduce(jnp.maximum, [jnp.max(s, axis=-1, keepdims=True) for s in scores])

    def head_pv(h, scores, mx):
        vals = [values(v_ref, slice(j * block_keys, (j + 1) * block_keys), h) for j in range(key_blocks)]
        if has_ctx:
            vals.append(values(vc_ref, slice(None), h))
        return functools.reduce(jnp.add, [_dot(jnp.exp2(s - mx).astype(BF), v) for s, v in zip(scores, vals)])

    def head_finish(h, ob):
        den = ob[:, V_HEAD_DIM:V_HEAD_DIM + 1]
        o = ob[:tq, :V_HEAD_DIM] * (1.0 / den[:tq]) - ob[tq:, :V_HEAD_DIM] * (lam / den[tq:])
        o = _rms(o, sub_g) * (1.0 - LAMBDA_INIT)
        sga = qst_ref[:, D + h * LANES:D + (h + 1) * LANES].astype(F32)
        tb = qst_ref[:, 2 * D + h * LANES:2 * D + (h + 1) * LANES].astype(F32)
        merged_s[:, head_cols(h)] = (sga * o + tb).astype(BF)

    for g0 in range(0, N_HEADS, heads_per_group):
        group = range(g0, g0 + heads_per_group)
        scores = [head_scores(h) for h in group]
        maxes = [row_max(s) for s in scores]
        outs = [head_pv(h, s, mx) for h, s, mx in zip(group, scores, maxes)]
        for h, ob in zip(group, outs):
            head_finish(h, ob)

    m = m_ref[pl.ds(row_of_req(pl.program_id(0)), 1), :]
    gate1, shift2, scale2 = m[:, 2 * D:3 * D], m[:, 3 * D:4 * D], m[:, 4 * D:5 * D]
    x1 = x_ref[...] + gate1 * _dot(merged_s[...], wo_ref[...])
    x1_out[...] = x1
    h2 = _rms(x1, n2g) * (1.0 + scale2) + shift2
    h2_out[...] = h2.astype(BF)
    hi, lo = _split_bf16(h2)
    wr_hi, wr_lo = wr_ref[:, 0:LANES], wr_ref[:, LANES:2 * LANES]
    lg_out[...] = _dot(hi, wr_hi) + _dot(lo, wr_hi) + _dot(hi, wr_lo)


def _attn_call(qst, kt, v, ctx, lamv, x, m, row_of_req, consts, n_req, n_tok):
    tq = min(n_tok, ATTN_ROWS)
    nqb = n_tok // tq
    key_blocks = kt.shape[0] // n_req
    has_ctx = ctx is not None
    tok = pl.BlockSpec((tq, D), lambda b, i: (b * nqb + i, 0))
    in_specs = [pl.BlockSpec((tq, 3 * D), lambda b, i: (b * nqb + i, 0)),
                pl.BlockSpec((key_blocks, D, kt.shape[2]), lambda b, i: (b, 0, 0)),
                pl.BlockSpec((n_tok, D), lambda b, i: (b, 0))]
    args = [qst, kt, v]
    if has_ctx:
        n_ctx = ctx[0].shape[2]
        in_specs += [pl.BlockSpec((1, D, n_ctx), lambda b, i: (b, 0, 0)),
                     pl.BlockSpec((n_ctx, D), lambda b, i: (b, 0))]
        args += list(ctx)
    in_specs += [
        _const_spec((MOD_ROWS, LANES)), _const_spec((MOD_ROWS, D)),
        tok,
        _const_spec((MOD_ROWS, 6 * D)),
        _const_spec((D, D)),
        _const_spec((D, 2 * LANES)),
    ]
    args += [lamv, consts["vecs"], x, m, consts["w_out"], consts["wr"]]
    t = n_req * n_tok
    n_keys = n_tok + (ctx[0].shape[2] if has_ctx else 0)
    heads_per_group = max(1, min(N_HEADS, SCORE_GROUP_ELEMS // (2 * tq * n_keys)))
    while N_HEADS % heads_per_group:
        heads_per_group -= 1
    return pl.pallas_call(
        functools.partial(_attn_kernel, has_ctx, heads_per_group, row_of_req),
        grid=(n_req, nqb),
        in_specs=in_specs,
        out_specs=[tok, tok, pl.BlockSpec((tq, LANES), lambda b, i: (b * nqb + i, 0))],
        out_shape=[
            jax.ShapeDtypeStruct((t, D), F32),
            jax.ShapeDtypeStruct((t, D), BF),
            jax.ShapeDtypeStruct((t, LANES), F32),
        ],
        scratch_shapes=[pltpu.VMEM((tq, D), BF)],
        compiler_params=_params(2),
        name="attn_ctx" if has_ctx else "attn",
    )(*args)


GATHER_ROWS = 512
KEY_BITS = 31
ROUTE_UNROLL_ELEMS = 1 << 22
ROUTE_STEP_TOKENS = 2048
PREFIX_BLOCK = 128


def _route_kernel(cap, n_tok, unroll, lg_ref, h2_ref, before_ref, xe_ref, gate_ref, slot_ref,
                  aff_t_s, slot_t_s, p_s):
    reqs = range(lg_ref.shape[0] // n_tok)
    lane = lax.broadcasted_iota(jnp.int32, (1, LANES), 1)
    valid = lane < N_EXPERTS

    def rows(r):
        return slice(r * n_tok, (r + 1) * n_tok)

    def affinity(r):
        lg = jnp.where(valid, lg_ref[rows(r), :], -1e30)
        ex = jnp.where(valid, jnp.exp(lg - jnp.max(lg, axis=-1, keepdims=True)), 0.0)
        return ex / jnp.sum(ex, axis=-1, keepdims=True)

    def count(mask):
        return jnp.sum(jnp.where(mask, 1.0, 0.0), axis=0, keepdims=True)

    affs = [affinity(r) for r in reqs]
    kth_bits = [jnp.zeros((1, LANES), jnp.int32) for _ in reqs]
    for bit in range(KEY_BITS - 1, -1, -1):
        for r in reqs:
            cand = kth_bits[r] | (1 << bit)
            enough = count(affs[r] >= lax.bitcast_convert_type(cand, F32)) >= cap
            kth_bits[r] = jnp.where(enough, cand, kth_bits[r])

    before = before_ref[...]

    def earlier_count(flags):
        blocks, running = [], jnp.zeros((1, LANES), F32)
        for b in range(n_tok // PREFIX_BLOCK):
            blk = flags[b * PREFIX_BLOCK:(b + 1) * PREFIX_BLOCK]
            blocks.append(_dot(before, blk.astype(BF)) + running)
            running = running + jnp.sum(blk, axis=0, keepdims=True)
        return jnp.concatenate(blocks, axis=0)

    for r in reqs:
        aff = affs[r]
        kth = lax.bitcast_convert_type(kth_bits[r], F32)
        above = aff > kth
        tied = aff == kth
        need = cap - count(above)
        tied_before = earlier_count(jnp.where(tied, 1.0, 0.0))
        chosen = jnp.where(above, 1.0, jnp.where(tied, jnp.where(tied_before < need, 1.0, 0.0), 0.0))
        slot = earlier_count(chosen)
        slot = jnp.where(valid, jnp.where(chosen > 0.0, slot, float(cap)), float(cap))
        slot_ref[rows(r), :] = slot
        slot_t_s[r] = slot.T
        aff_t_s[r] = aff.T

    slot_ids = lax.broadcasted_iota(jnp.int32, (cap, 1), 0).astype(F32)

    def per_expert(e, carry):
        for r in reqs:
            hit = slot_t_s[r, pl.ds(e, 1), :] == slot_ids
            p_s[r, pl.ds(pl.multiple_of(e * cap, cap), cap), :] = jnp.where(hit, 1.0, 0.0).astype(BF)
            gate = jnp.sum(jnp.where(hit, aff_t_s[r, pl.ds(e, 1), :], 0.0), axis=-1, keepdims=True)
            gate_ref[e, r * cap:(r + 1) * cap, :] = gate
        return carry

    lax.fori_loop(0, N_EXPERTS, per_expert, 0, unroll=unroll)
    n_rows = min(GATHER_ROWS, N_EXPERTS * cap)
    e_per = n_rows // cap
    for r in reqs:
        h2 = h2_ref[rows(r), :]
        for t in range(N_EXPERTS * cap // n_rows):
            xe = _dot(p_s[r, t * n_rows:(t + 1) * n_rows, :], h2).astype(BF)
            xe_ref[t * e_per:(t + 1) * e_per, r * cap:(r + 1) * cap, :] = xe.reshape(e_per, cap, D)


def _capacity(n_tok):
    return max(1, CAPACITY_FACTOR * n_tok // N_EXPERTS)


def _route_call(lg, h2, n_req, n_tok):
    cap = _capacity(n_tok)
    per_step = max(1, min(n_req, ROUTE_STEP_TOKENS // n_tok))
    before = jnp.asarray((np.arange(PREFIX_BLOCK)[None, :] < np.arange(PREFIX_BLOCK)[:, None]).astype(BF))
    unroll = max(1, min(N_EXPERTS, ROUTE_UNROLL_ELEMS // (per_step * cap * n_tok)))
    return pl.pallas_call(
        functools.partial(_route_kernel, cap, n_tok, unroll),
        grid=(n_req // per_step,),
        in_specs=[
            pl.BlockSpec((per_step * n_tok, LANES), lambda b: (b, 0)),
            pl.BlockSpec((per_step * n_tok, D), lambda b: (b, 0)),
            _const_spec((PREFIX_BLOCK, PREFIX_BLOCK)),
        ],
        out_specs=[
            pl.BlockSpec((N_EXPERTS, per_step * cap, D), lambda b: (0, b, 0)),
            pl.BlockSpec((N_EXPERTS, per_step * cap, 1), lambda b: (0, b, 0)),
            pl.BlockSpec((per_step * n_tok, LANES), lambda b: (b, 0)),
        ],
        out_shape=[
            jax.ShapeDtypeStruct((N_EXPERTS, n_req * cap, D), BF),
            jax.ShapeDtypeStruct((N_EXPERTS, n_req * cap, 1), F32),
            jax.ShapeDtypeStruct((n_req * n_tok, LANES), F32),
        ],
        scratch_shapes=[
            pltpu.VMEM((per_step, LANES, n_tok), F32),
            pltpu.VMEM((per_step, LANES, n_tok), F32),
            pltpu.VMEM((per_step, N_EXPERTS * cap, n_tok), BF),
        ],
        compiler_params=_params(1),
        name="route",
    )(lg, h2, before)


EXPERT_BLOCK = 1024
EXPERT_SUB = 256


def _experts_kernel(xa_ref, xb_ref, ga_ref, gb_ref, wg_ref, wu_ref, wd_ref, ya_ref, yb_ref, acc_s):
    f = pl.program_id(1)
    ra = xa_ref.shape[1]
    x = jnp.concatenate([xa_ref[0], xb_ref[0]], axis=0)
    for c in range(wg_ref.shape[2] // EXPERT_SUB):
        cs = slice(c * EXPERT_SUB, (c + 1) * EXPERT_SUB)
        gate = _dot(x, wg_ref[0, :, cs].astype(BF))
        up = _dot(x, wu_ref[0, :, cs].astype(BF))
        hid = (gate * _sigmoid(gate) * up).astype(BF)
        down = _dot(hid, wd_ref[0, cs, :].astype(BF))
        if c == 0:
            acc_s[...] = jnp.where(f == 0, 0.0, acc_s[...]) + down
        else:
            acc_s[...] += down

    @pl.when(f == pl.num_programs(1) - 1)
    def _():
        ya_ref[0] = (acc_s[0:ra, :] * ga_ref[0]).astype(ya_ref.dtype)
        yb_ref[0] = (acc_s[ra:, :] * gb_ref[0]).astype(yb_ref.dtype)


def _experts_call(xa, xb, ga, gb, wg, wu, wd):
    ra, rb = xa.shape[1], xb.shape[1]
    tf = EXPERT_BLOCK
    xa_spec = pl.BlockSpec((1, ra, D), lambda e, f: (e, 0, 0))
    xb_spec = pl.BlockSpec((1, rb, D), lambda e, f: (e, 0, 0))
    return pl.pallas_call(
        _experts_kernel,
        grid=(N_EXPERTS, D_EXPERT // tf),
        in_specs=[
            xa_spec, xb_spec,
            pl.BlockSpec((1, ra, 1), lambda e, f: (e, 0, 0)),
            pl.BlockSpec((1, rb, 1), lambda e, f: (e, 0, 0)),
            pl.BlockSpec((1, D, tf), lambda e, f: (e, 0, f)),
            pl.BlockSpec((1, D, tf), lambda e, f: (e, 0, f)),
            pl.BlockSpec((1, tf, D), lambda e, f: (e, f, 0)),
        ],
        out_specs=[xa_spec, xb_spec],
        out_shape=[
            jax.ShapeDtypeStruct((N_EXPERTS, ra, D), BF),
            jax.ShapeDtypeStruct((N_EXPERTS, rb, D), BF),
        ],
        scratch_shapes=[pltpu.VMEM((ra + rb, D), F32)],
        compiler_params=_params(2),
        name="experts",
    )(xa, xb, ga, gb, wg, wu, wd)


SCATTER_STEP_TOKENS = 1024


def _scatter_kernel(cap, n_tok, row_of_req, slot_ref, y_ref, x1_ref, m_ref, expand_ref, out_ref):
    per_step = slot_ref.shape[0] // n_tok
    slot_ids = (lax.broadcasted_iota(jnp.int32, (1, N_EXPERTS * cap), 1) % cap).astype(F32)
    for r in range(per_step):
        rows = slice(r * n_tok, (r + 1) * n_tok)
        m = m_ref[pl.ds(row_of_req(pl.program_id(0) * per_step + r), 1), :]
        gate2 = m[:, 5 * D:6 * D]
        slot_wide = _dot(slot_ref[rows, :].astype(BF), expand_ref[...])
        onehot = jnp.where(slot_wide == slot_ids, 1.0, 0.0).astype(BF)
        y = y_ref[:, r * cap:(r + 1) * cap, :].reshape(N_EXPERTS * cap, D)
        out_ref[rows, :] = x1_ref[rows, :] + gate2 * _dot(onehot, y)


def _scatter_call(slots, y, x1, m, row_of_req, n_req, n_tok):
    cap = _capacity(n_tok)
    per_step = max(1, min(n_req, SCATTER_STEP_TOKENS // n_tok))
    expand = jnp.asarray((np.arange(LANES)[:, None] == np.arange(N_EXPERTS * cap)[None, :] // cap).astype(BF))
    tok = pl.BlockSpec((per_step * n_tok, D), lambda b: (b, 0))
    return pl.pallas_call(
        functools.partial(_scatter_kernel, cap, n_tok, row_of_req),
        grid=(n_req // per_step,),
        in_specs=[
            pl.BlockSpec((per_step * n_tok, LANES), lambda b: (b, 0)),
            pl.BlockSpec((N_EXPERTS, per_step * cap, D), lambda b: (0, b, 0)),
            tok,
            _const_spec((MOD_ROWS, 6 * D)),
            _const_spec((LANES, N_EXPERTS * cap)),
        ],
        out_specs=tok,
        out_shape=jax.ShapeDtypeStruct((n_req * n_tok, D), F32),
        compiler_params=_params(1),
        name="scatter",
    )(slots, y, x1, m, expand)


def _rope_tables(n_tokens):
    rows = n_tokens // GRID_W
    row = np.broadcast_to(np.arange(rows, dtype=np.float32)[:, None], (rows, GRID_W)).reshape(-1)
    col = np.broadcast_to(np.arange(GRID_W, dtype=np.float32)[None, :], (rows, GRID_W)).reshape(-1)
    half = HEAD_DIM // 4
    inv_freq = (np.float32(ROPE_BASE) ** (-np.arange(half, dtype=np.float32) / np.float32(half))).astype(np.float32)
    ar = row[:, None] * inv_freq
    ac = col[:, None] * inv_freq
    ang = np.concatenate([ar, ar, ac, ac], axis=-1).astype(np.float64)
    cos, sin = np.cos(ang).astype(np.float32), np.sin(ang).astype(np.float32)
    first_half = (np.arange(HEAD_DIM) % (HEAD_DIM // 2)) < (HEAD_DIM // 4)
    sin_signed = np.where(first_half[None, :], -sin, sin)
    reps = LANES // HEAD_DIM
    token_major = np.concatenate([np.tile(cos, (1, reps)), np.tile(sin_signed, (1, reps))], axis=1)
    transposed = np.concatenate([cos.T, sin_signed.T], axis=0)
    return jnp.asarray(token_major), jnp.asarray(transposed)


def kernel(x_prompt, x_sample, cache_k, cache_v, c, c_ctx, w_ada, b_ada, norm1_g, norm2_g, w_in, q_norm_g, k_norm_g, lambda_q1, lambda_k1, lambda_q2, lambda_k2, subln_g, gmlp_ln_g, gmlp_ln_b, w_spatial, b_spatial, w_out, w_router, w_gate_e, w_up_e, w_down_e):
    n_p, t_p = x_prompt.shape[0], x_prompt.shape[1]
    n_s, t_s = x_sample.shape[0], x_sample.shape[1]
    n_ctx = cache_k.shape[2]
    l = 0

    m, lamv, w_in_bf, wkt, w_out_bf, wsp_pairs = _prep_call(
        c_ctx[None, :], c, w_ada[l], b_ada[l][None, :], lambda_q1[l][None, :], lambda_k1[l][None, :],
        lambda_q2[l][None, :], lambda_k2[l][None, :], w_in[l], w_out[l], w_spatial[l])

    gsum_np = (np.arange(D)[:, None] // HEAD_DIM == np.arange(LANES)[None, :]).astype(BF)
    gsum = jnp.asarray(gsum_np)
    wr = jnp.pad(w_router[l], ((0, 0), (0, LANES - N_EXPERTS)))
    wr_hi = wr.astype(BF)
    qg = jnp.tile(q_norm_g[l] * (HEAD_DIM ** -0.5 * math.log2(math.e)), D // HEAD_DIM)

    def rows_of(*vectors):
        rows = [jnp.tile(vec, D // vec.shape[0])[None, :] for vec in vectors]
        return jnp.concatenate(rows + [jnp.zeros((MOD_ROWS - len(rows), D), F32)], axis=0)

    consts = {
        "vecs": rows_of(norm1_g[l], qg, gmlp_ln_g[l], gmlp_ln_b[l], norm2_g[l], subln_g[l]),
        "w_in": w_in_bf,
        "wkt": wkt,
        "kgt": jnp.broadcast_to(k_norm_g[l][:, None], (HEAD_DIM, INPROJ_ROWS)),
        "gsum": gsum,
        "gbcast": jnp.asarray(np.concatenate([gsum_np.T, gsum_np.T], axis=0)),
        "wsp": wsp_pairs,
        "bsp": jnp.repeat(b_spatial[l].T, D // N_GROUPS, axis=1),
        "w_out": w_out_bf,
        "wr": jnp.concatenate([wr_hi, (wr - wr_hi.astype(F32)).astype(BF)], axis=1),
    }

    passes = (
        (x_prompt, n_p, t_p, lambda b: 0, False),
        (x_sample, n_s, t_s, lambda b: 1 + b, True),
    )
    outs = []
    kv_out = None
    lg = None
    for x, n_req, n_tok, row_of_req, is_sample in passes:
        xf = x.reshape(n_req * n_tok, D)
        tabs = _rope_tables(n_tok) if is_sample else None
        qst, kt, v = _inproj_call(xf, m, row_of_req, tabs, BF if is_sample else F32, consts, n_req, n_tok,
                                  after=lg)
        if is_sample:
            ctx = (jnp.transpose(cache_k[:, l], (0, 2, 3, 4, 1)).reshape(n_req, D, n_ctx),
                   cache_v[:, l].reshape(n_req * n_ctx, D))
        else:
            ctx = None
            kv_out = (kt, v)
        x1, h2, lg = _attn_call(qst, kt, v, ctx, lamv, xf, m, row_of_req, consts, n_req, n_tok)
        xe, gates, slots = _route_call(lg, h2, n_req, n_tok)
        outs.append((x1, xe, gates, slots, row_of_req))

    (x1p, xep, gp, slots_p, row_p), (x1s, xes, gs, slots_s, row_s) = outs
    yp, ys = _experts_call(xep, xes, gp, gs, w_gate_e[l], w_up_e[l], w_down_e[l])
    y_prompt = _scatter_call(slots_p, yp, x1p, m, row_p, n_p, t_p).reshape(x_prompt.shape)
    y_sample = _scatter_call(slots_s, ys, x1s, m, row_s, n_s, t_s).reshape(x_sample.shape)
    new_k = jnp.transpose(kv_out[0].reshape(n_p, N_HEADS, 2, HEAD_DIM, t_p), (0, 4, 1, 2, 3))
    new_k = new_k.reshape(n_p, 1, t_p, N_HEADS, 2, HEAD_DIM)
    new_v = kv_out[1].reshape(n_p, 1, t_p, N_HEADS, V_HEAD_DIM)
    return (y_prompt, y_sample, new_k, new_v)
```

```python
import functools
import math

import jax
import jax.numpy as jnp
import numpy as np
from jax import lax
from jax.experimental import pallas as pl
from jax.experimental.pallas import tpu as pltpu

D = 1024
N_HEADS = 8
HEAD_DIM = 64
V_HEAD_DIM = 128
GRID_W = 64
ROPE_BASE = 10000.0
CHUNK = 128
N_GROUPS = 8
N_EXPERTS = 16
CAPACITY_FACTOR = 2
D_EXPERT = 2048
N_SEG = 7
K_SEG = 1
EPS = 1e-6
LAMBDA_INIT = 0.8 - 0.6 * math.exp(-0.3 * 0)

LANES = 128
MOD_ROWS = 8
VMEM_LIMIT = 56 * 1024 * 1024

BF = jnp.bfloat16
F32 = jnp.float32


def _dot(a, b):
    return jnp.dot(a, b, preferred_element_type=F32)


def _dot_nt(a, b):
    return lax.dot_general(a, b, (((1,), (1,)), ((), ())), preferred_element_type=F32)


def _split_bf16(x):
    hi = x.astype(BF)
    lo = (x - hi.astype(F32)).astype(BF)
    return hi, lo


def _sigmoid(x):
    return 0.5 * jnp.tanh(0.5 * x) + 0.5


def _rms(x, g):
    return x * lax.rsqrt(jnp.mean(x * x, axis=-1, keepdims=True) + EPS) * g


def _params(n_grid_dims):
    return pltpu.CompilerParams(
        dimension_semantics=("arbitrary",) * n_grid_dims, vmem_limit_bytes=VMEM_LIMIT)


def _const_spec(shape):
    nd = len(shape)
    return pl.BlockSpec(shape, lambda *_: (0,) * nd)


N_MOD = 6


def _prep_kernel(cctx_ref, c_ref, wa_ref, ba_ref, lq1_ref, lk1_ref, lq2_ref, lk2_ref, wi_ref, wo_ref, wsp_ref,
                 m_ref, lam_ref, wib_ref, wkt_ref, wob_ref, wsp2_ref, cond_s):
    j = pl.program_id(0)
    n_lat = c_ref.shape[0]

    @pl.when(j == 0)
    def _():
        cond_s[...] = jnp.zeros(cond_s.shape, F32)
        cond_s[0:1, :] = cctx_ref[...]
        cond_s[1:1 + n_lat, :] = c_ref[...]
        s1 = jnp.sum(lq1_ref[...] * lk1_ref[...], axis=-1, keepdims=True)
        s2 = jnp.sum(lq2_ref[...] * lk2_ref[...], axis=-1, keepdims=True)
        lam_ref[...] = jnp.broadcast_to(jnp.exp(s1) - jnp.exp(s2) + LAMBDA_INIT, lam_ref.shape)
        wob_ref[...] = wo_ref[...].astype(BF)
        for p in range(N_GROUPS // 2):
            wsp2_ref[p] = jnp.concatenate([wsp_ref[2 * p], wsp_ref[2 * p + 1]], axis=1).astype(BF)

    @pl.when(j < N_MOD)
    def _():
        c = cond_s[...]
        a_hi, a_lo = _split_bf16(c * jax.nn.sigmoid(c))
        w_hi, w_lo = _split_bf16(wa_ref[...])
        m_ref[...] = _dot(a_hi, w_hi) + _dot(a_lo, w_hi) + _dot(a_hi, w_lo) + ba_ref[...]

    w = wi_ref[...]
    wib_ref[...] = w.astype(BF)

    @pl.when(j == K_SEG)
    def _():
        wkt_ref[...] = w.T.astype(BF)


def _prep_call(c_ctx, c, w_ada, b_ada, lq1, lk1, lq2, lk2, w_in, w_out, w_spatial):
    vec = _const_spec((1, HEAD_DIM))
    mod_block = lambda j: (0, jnp.minimum(j, N_MOD - 1))
    pairs = (N_GROUPS // 2, CHUNK, 2 * CHUNK)
    return pl.pallas_call(
        _prep_kernel,
        grid=(N_SEG,),
        in_specs=[
            _const_spec((1, D)), _const_spec(c.shape),
            pl.BlockSpec((D, D), mod_block),
            pl.BlockSpec((1, D), mod_block),
            vec, vec, vec, vec,
            pl.BlockSpec((D, D), lambda j: (0, j)),
            _const_spec((D, D)),
            _const_spec((N_GROUPS, CHUNK, CHUNK)),
        ],
        out_specs=[
            pl.BlockSpec((MOD_ROWS, D), mod_block),
            _const_spec((MOD_ROWS, LANES)),
            pl.BlockSpec((D, D), lambda j: (0, j)),
            _const_spec((D, D)),
            _const_spec((D, D)),
            _const_spec(pairs),
        ],
        out_shape=[
            jax.ShapeDtypeStruct((MOD_ROWS, N_MOD * D), F32),
            jax.ShapeDtypeStruct((MOD_ROWS, LANES), F32),
            jax.ShapeDtypeStruct((D, N_SEG * D), BF),
            jax.ShapeDtypeStruct((D, D), BF),
            jax.ShapeDtypeStruct((D, D), BF),
            jax.ShapeDtypeStruct(pairs, BF),
        ],
        scratch_shapes=[pltpu.VMEM((MOD_ROWS, D), F32)],
        compiler_params=_params(1),
        name="prep",
    )(c_ctx, c, w_ada, b_ada, lq1, lk1, lq2, lk2, w_in, w_out, w_spatial)


def _head_norm(t, g, gsum, gbcast):
    ss = _dot((t * t).astype(BF), gsum)
    inv = lax.rsqrt(ss * (1.0 / HEAD_DIM) + EPS)
    hi, lo = _split_bf16(inv)
    bc = _dot(jnp.concatenate([hi, lo], axis=-1), gbcast)
    return t * bc * g


def _lane_tile(tab):
    return jnp.concatenate([tab] * (D // LANES), axis=1)


def _rope(t, cos, sin_signed):
    q = HEAD_DIM // 4
    lane = lax.broadcasted_iota(jnp.int32, (1, D), 1)
    first_half = (lane % (HEAD_DIM // 2)) < q
    fwd = pltpu.roll(t, D - q, axis=1)
    bwd = pltpu.roll(t, q, axis=1)
    return t * _lane_tile(cos) + jnp.where(first_half, fwd, bwd) * _lane_tile(sin_signed)


def _rope_t(t3, cos_t, sin_signed_t):
    q = HEAD_DIM // 4
    rot = jnp.concatenate([t3[:, q:2 * q], t3[:, 0:q], t3[:, 3 * q:4 * q], t3[:, 2 * q:3 * q]], axis=1)
    return t3 * cos_t[None] + rot * sin_signed_t[None]


def _inproj_kernel(rope, row_of_step, x_ref, m_ref, vecs_ref, w_ref, wkt_ref, kgt_ref, gsum_ref,
                   gbcast_ref, wsp_ref, bsp_ref, *rest):
    if rope:
        tab_ref, tab_t_ref, *rest = rest
    qst_out, kt_out, v_out = rest[-3:]
    tm = x_ref.shape[0]
    m = m_ref[pl.ds(row_of_step(pl.program_id(0)), 1), :]
    shift1, scale1 = m[:, 0:D], m[:, D:2 * D]
    n1g, qg, lng, lnb = (vecs_ref[r:r + 1, :] for r in range(4))
    h = (_rms(x_ref[...], n1g) * (1.0 + scale1) + shift1).astype(BF)

    def seg(j):
        return _dot(h, w_ref[:, j * D:(j + 1) * D])

    q = _head_norm(seg(0), qg, gsum_ref[...], gbcast_ref[...])
    if rope:
        q = _rope(q, tab_ref[:, 0:LANES], tab_ref[:, LANES:2 * LANES])
    qst_out[:, 0:D] = q.astype(qst_out.dtype)

    k3 = _dot_nt(wkt_ref[...], h).reshape(D // HEAD_DIM, HEAD_DIM, tm)
    inv = lax.rsqrt(jnp.mean(k3 * k3, axis=1, keepdims=True) + EPS)
    k3 = k3 * inv * kgt_ref[...][None]
    if rope:
        k3 = _rope_t(k3, tab_t_ref[0:HEAD_DIM, :], tab_t_ref[HEAD_DIM:2 * HEAD_DIM, :])
    kt = k3.reshape(D, tm).astype(kt_out.dtype)
    width = kt_out.shape[2]
    for r in range(kt_out.shape[0]):
        kt_out[r] = kt[:, r * width:(r + 1) * width]

    v_out[...] = seg(2).astype(v_out.dtype)
    qst_out[:, D:2 * D] = _sigmoid(seg(5)).astype(qst_out.dtype)

    zv = jax.nn.gelu(seg(4))
    mu = jnp.mean(zv, axis=-1, keepdims=True)
    zc = zv - mu
    var = jnp.mean(zc * zc, axis=-1, keepdims=True)
    zvn = (zc * lax.rsqrt(var + EPS) * lng + lnb).astype(BF)
    pre = jax.nn.gelu(seg(3)) * _sigmoid(seg(6))
    blank = jnp.zeros((CHUNK, LANES), BF)
    for c in range(tm // CHUNK):
        rows = slice(c * CHUNK, (c + 1) * CHUNK)
        for p in range(N_GROUPS // 2):
            cols = slice(2 * p * LANES, (2 * p + 2) * LANES)
            z = zvn[rows, cols]
            z_diag = jnp.concatenate([jnp.concatenate([z[:, :LANES], blank], axis=1),
                                      jnp.concatenate([blank, z[:, LANES:]], axis=1)], axis=0)
            sp = _dot(wsp_ref[p], z_diag) + bsp_ref[:, cols]
            out_cols = slice(2 * D + 2 * p * LANES, 2 * D + (2 * p + 2) * LANES)
            qst_out[rows, out_cols] = (pre[rows, cols] * sp).astype(qst_out.dtype)


INPROJ_ROWS = 512


def _inproj_call(x, m, row_of_req, rope_tabs, kv_dtype, consts, n_req, n_tok, after=None):
    t = x.shape[0]
    tm = INPROJ_ROWS
    per_req = max(1, n_tok // tm)
    per_tile = max(1, tm // n_tok)
    rope = rope_tabs is not None

    def row_of_step(i):
        return row_of_req(i * per_tile // per_req)

    tok = pl.BlockSpec((tm, D), lambda i: (i, 0))
    in_specs = [
        tok,
        _const_spec((MOD_ROWS, 6 * D)),
        _const_spec((MOD_ROWS, D)),
        pl.BlockSpec((D, N_SEG * D), lambda i: (0, 0), pipeline_mode=pl.Buffered(1)),
        _const_spec((D, D)),
        _const_spec((HEAD_DIM, tm)),
        _const_spec((D, LANES)), _const_spec((2 * LANES, D)),
        _const_spec((N_GROUPS // 2, CHUNK, 2 * CHUNK)), _const_spec((CHUNK, D)),
    ]
    args = [x, m, consts["vecs"], consts["w_in"], consts["wkt"], consts["kgt"], consts["gsum"],
            consts["gbcast"], consts["wsp"], consts["bsp"]]
    if rope:
        in_specs += [pl.BlockSpec((tm, 2 * LANES), lambda i: (i % per_req, 0)),
                     pl.BlockSpec((2 * HEAD_DIM, tm), lambda i: (0, i % per_req))]
        args += list(rope_tabs)
    if after is not None:
        in_specs += [pl.BlockSpec(memory_space=pl.ANY)]
        args += [after]
    kt_spec = pl.BlockSpec((per_tile, D, tm // per_tile), lambda i: (i, 0, 0))
    return pl.pallas_call(
        functools.partial(_inproj_kernel, rope, row_of_step),
        grid=(t // tm,),
        in_specs=in_specs,
        out_specs=[pl.BlockSpec((tm, 3 * D), lambda i: (i, 0)), kt_spec, tok],
        out_shape=[
            jax.ShapeDtypeStruct((t, 3 * D), BF),
            jax.ShapeDtypeStruct((n_req * per_req, D, tm // per_tile), kv_dtype),
            jax.ShapeDtypeStruct((t, D), kv_dtype),
        ],
        compiler_params=_params(1),
        name="inproj_rope" if rope else "inproj",
    )(*args)


SCORE_GROUP_ELEMS = 1 << 22
ATTN_ROWS = 256


def _attn_kernel(has_ctx, heads_per_group, row_of_req, qst_ref, kt_ref, v_ref, *rest):
    if has_ctx:
        kct_ref, vc_ref, *rest = rest
    lam_ref, vecs_ref, x_ref, m_ref, wo_ref, wr_ref, x1_out, h2_out, lg_out, merged_s = rest
    tq = qst_ref.shape[0]
    key_blocks, block_keys = kt_ref.shape[0], kt_ref.shape[2]
    n2g = vecs_ref[4:5, :]
    sub_g = vecs_ref[5:6, 0:V_HEAD_DIM]
    lam = lam_ref[0:1, 0:1]
    lane = lax.broadcasted_iota(jnp.int32, (1, LANES), 1)
    first = lane < HEAD_DIM
    zero = jnp.zeros((), BF)
    ones_col = jnp.where(lane == 0, 1.0, 0.0).astype(BF)

    def values(ref, rows, h):
        v = ref[rows, h * V_HEAD_DIM:(h + 1) * V_HEAD_DIM].astype(BF)
        return jnp.concatenate([v, jnp.broadcast_to(ones_col, (v.shape[0], LANES))], axis=1)

    def head_cols(h):
        return slice(h * LANES, (h + 1) * LANES)

    def head_scores(h):
        q = qst_ref[:, head_cols(h)]
        qz = jnp.concatenate([jnp.where(first, q, zero), jnp.where(first, zero, q)], axis=0)
        parts = [_dot(qz, kt_ref[j, head_cols(h), :].astype(BF)) for j in range(key_blocks)]
        if has_ctx:
            parts.append(_dot(qz, kct_ref[0, head_cols(h), :].astype(BF)))
        return parts

    def row_max(scores):
        return functools.reduce(jnp.maximum, [jnp.max(s, axis=-1, keepdims=True) for s in scores])

    def head_pv(h, scores, mx):
        vals = [values(v_ref, slice(j * block_keys, (j + 1) * block_keys), h) for j in range(key_blocks)]
        if has_ctx:
            vals.append(values(vc_ref, slice(None), h))
        return functools.reduce(jnp.add, [_dot(jnp.exp2(s - mx).astype(BF), v) for s, v in zip(scores, vals)])

    def head_finish(h, ob):
        den = ob[:, V_HEAD_DIM:V_HEAD_DIM + 1]
        o = ob[:tq, :V_HEAD_DIM] * (1.0 / den[:tq]) - ob[tq:, :V_HEAD_DIM] * (lam / den[tq:])
        o = _rms(o, sub_g) * (1.0 - LAMBDA_INIT)
        sga = qst_ref[:, D + h * LANES:D + (h + 1) * LANES].astype(F32)
        tb = qst_ref[:, 2 * D + h * LANES:2 * D + (h + 1) * LANES].astype(F32)
        merged_s[:, head_cols(h)] = (sga * o + tb).astype(BF)

    for g0 in range(0, N_HEADS, heads_per_group):
        group = range(g0, g0 + heads_per_group)
        scores = [head_scores(h) for h in group]
        maxes = [row_max(s) for s in scores]
        outs = [head_pv(h, s, mx) for h, s, mx in zip(group, scores, maxes)]
        for h, ob in zip(group, outs):
            head_finish(h, ob)

    m = m_ref[pl.ds(row_of_req(pl.program_id(0)), 1), :]
    gate1, shift2, scale2 = m[:, 2 * D:3 * D], m[:, 3 * D:4 * D], m[:, 4 * D:5 * D]
    x1 = x_ref[...] + gate1 * _dot(merged_s[...], wo_ref[...])
    x1_out[...] = x1
    h2 = _rms(x1, n2g) * (1.0 + scale2) + shift2
    h2_out[...] = h2.astype(BF)
    hi, lo = _split_bf16(h2)
    wr_hi, wr_lo = wr_ref[:, 0:LANES], wr_ref[:, LANES:2 * LANES]
    lg_out[...] = _dot(hi, wr_hi) + _dot(lo, wr_hi) + _dot(hi, wr_lo)


def _attn_call(qst, kt, v, ctx, lamv, x, m, row_of_req, consts, n_req, n_tok):
    tq = min(n_tok, ATTN_ROWS)
    nqb = n_tok // tq
    key_blocks = kt.shape[0] // n_req
    has_ctx = ctx is not None
    tok = pl.BlockSpec((tq, D), lambda b, i: (b * nqb + i, 0))
    in_specs = [pl.BlockSpec((tq, 3 * D), lambda b, i: (b * nqb + i, 0)),
                pl.BlockSpec((key_blocks, D, kt.shape[2]), lambda b, i: (b, 0, 0)),
                pl.BlockSpec((n_tok, D), lambda b, i: (b, 0))]
    args = [qst, kt, v]
    if has_ctx:
        n_ctx = ctx[0].shape[2]
        in_specs += [pl.BlockSpec((1, D, n_ctx), lambda b, i: (b, 0, 0)),
                     pl.BlockSpec((n_ctx, D), lambda b, i: (b, 0))]
        args += list(ctx)
    in_specs += [
        _const_spec((MOD_ROWS, LANES)), _const_spec((MOD_ROWS, D)),
        tok,
        _const_spec((MOD_ROWS, 6 * D)),
        _const_spec((D, D)),
        _const_spec((D, 2 * LANES)),
    ]
    args += [lamv, consts["vecs"], x, m, consts["w_out"], consts["wr"]]
    t = n_req * n_tok
    n_keys = n_tok + (ctx[0].shape[2] if has_ctx else 0)
    heads_per_group = max(1, min(N_HEADS, SCORE_GROUP_ELEMS // (2 * tq * n_keys)))
    while N_HEADS % heads_per_group:
        heads_per_group -= 1
    return pl.pallas_call(
        functools.partial(_attn_kernel, has_ctx, heads_per_group, row_of_req),
        grid=(n_req, nqb),
        in_specs=in_specs,
        out_specs=[tok, tok, pl.BlockSpec((tq, LANES), lambda b, i: (b * nqb + i, 0))],
        out_shape=[
            jax.ShapeDtypeStruct((t, D), F32),
            jax.ShapeDtypeStruct((t, D), BF),
            jax.ShapeDtypeStruct((t, LANES), F32),
        ],
        scratch_shapes=[pltpu.VMEM((tq, D), BF)],
        compiler_params=_params(2),
        name="attn_ctx" if has_ctx else "attn",
    )(*args)


GATHER_ROWS = 512
KEY_BITS = 31
ROUTE_UNROLL_ELEMS = 1 << 22
ROUTE_STEP_TOKENS = 2048
PREFIX_BLOCK = 128


def _route_kernel(cap, n_tok, unroll, lg_ref, h2_ref, before_ref, xe_ref, gate_ref, slot_ref,
                  aff_t_s, slot_t_s, p_s):
    reqs = range(lg_ref.shape[0] // n_tok)
    lane = lax.broadcasted_iota(jnp.int32, (1, LANES), 1)
    valid = lane < N_EXPERTS

    def rows(r):
        return slice(r * n_tok, (r + 1) * n_tok)

    def affinity(r):
        lg = jnp.where(valid, lg_ref[rows(r), :], -1e30)
        ex = jnp.where(valid, jnp.exp(lg - jnp.max(lg, axis=-1, keepdims=True)), 0.0)
        return ex / jnp.sum(ex, axis=-1, keepdims=True)

    def count(mask):
        return jnp.sum(jnp.where(mask, 1.0, 0.0), axis=0, keepdims=True)

    affs = [affinity(r) for r in reqs]
    kth_bits = [jnp.zeros((1, LANES), jnp.int32) for _ in reqs]
    for bit in range(KEY_BITS - 1, -1, -1):
        for r in reqs:
            cand = kth_bits[r] | (1 << bit)
            enough = count(affs[r] >= lax.bitcast_convert_type(cand, F32)) >= cap
            kth_bits[r] = jnp.where(enough, cand, kth_bits[r])

    before = before_ref[...]

    def earlier_count(flags):
        blocks, running = [], jnp.zeros((1, LANES), F32)
        for b in range(n_tok // PREFIX_BLOCK):
            blk = flags[b * PREFIX_BLOCK:(b + 1) * PREFIX_BLOCK]
            blocks.append(_dot(before, blk.astype(BF)) + running)
            running = running + jnp.sum(blk, axis=0, keepdims=True)
        return jnp.concatenate(blocks, axis=0)

    for r in reqs:
        aff = affs[r]
        kth = lax.bitcast_convert_type(kth_bits[r], F32)
        above = aff > kth
        tied = aff == kth
        need = cap - count(above)
        tied_before = earlier_count(jnp.where(tied, 1.0, 0.0))
        chosen = jnp.where(above, 1.0, jnp.where(tied, jnp.where(tied_before < need, 1.0, 0.0), 0.0))
        slot = earlier_count(chosen)
        slot = jnp.where(valid, jnp.where(chosen > 0.0, slot, float(cap)), float(cap))
        slot_ref[rows(r), :] = slot
        slot_t_s[r] = slot.T
        aff_t_s[r] = aff.T

    slot_ids = lax.broadcasted_iota(jnp.int32, (cap, 1), 0).astype(F32)

    def per_expert(e, carry):
        for r in reqs:
            hit = slot_t_s[r, pl.ds(e, 1), :] == slot_ids
            p_s[r, pl.ds(pl.multiple_of(e * cap, cap), cap), :] = jnp.where(hit, 1.0, 0.0).astype(BF)
            gate = jnp.sum(jnp.where(hit, aff_t_s[r, pl.ds(e, 1), :], 0.0), axis=-1, keepdims=True)
            gate_ref[e, r * cap:(r + 1) * cap, :] = gate
        return carry

    lax.fori_loop(0, N_EXPERTS, per_expert, 0, unroll=unroll)
    n_rows = min(GATHER_ROWS, N_EXPERTS * cap)
    e_per = n_rows // cap
    for r in reqs:
        h2 = h2_ref[rows(r), :]
        for t in range(N_EXPERTS * cap // n_rows):
            xe = _dot(p_s[r, t * n_rows:(t + 1) * n_rows, :], h2).astype(BF)
            xe_ref[t * e_per:(t + 1) * e_per, r * cap:(r + 1) * cap, :] = xe.reshape(e_per, cap, D)


def _capacity(n_tok):
    return max(1, CAPACITY_FACTOR * n_tok // N_EXPERTS)


def _route_call(lg, h2, n_req, n_tok):
    cap = _capacity(n_tok)
    per_step = max(1, min(n_req, ROUTE_STEP_TOKENS // n_tok))
    before = jnp.asarray((np.arange(PREFIX_BLOCK)[None, :] < np.arange(PREFIX_BLOCK)[:, None]).astype(BF))
    unroll = max(1, min(N_EXPERTS, ROUTE_UNROLL_ELEMS // (per_step * cap * n_tok)))
    return pl.pallas_call(
        functools.partial(_route_kernel, cap, n_tok, unroll),
        grid=(n_req // per_step,),
        in_specs=[
            pl.BlockSpec((per_step * n_tok, LANES), lambda b: (b, 0)),
            pl.BlockSpec((per_step * n_tok, D), lambda b: (b, 0)),
            _const_spec((PREFIX_BLOCK, PREFIX_BLOCK)),
        ],
        out_specs=[
            pl.BlockSpec((N_EXPERTS, per_step * cap, D), lambda b: (0, b, 0)),
            pl.BlockSpec((N_EXPERTS, per_step * cap, 1), lambda b: (0, b, 0)),
            pl.BlockSpec((per_step * n_tok, LANES), lambda b: (b, 0)),
        ],
        out_shape=[
            jax.ShapeDtypeStruct((N_EXPERTS, n_req * cap, D), BF),
            jax.ShapeDtypeStruct((N_EXPERTS, n_req * cap, 1), F32),
            jax.ShapeDtypeStruct((n_req * n_tok, LANES), F32),
        ],
        scratch_shapes=[
            pltpu.VMEM((per_step, LANES, n_tok), F32),
            pltpu.VMEM((per_step, LANES, n_tok), F32),
            pltpu.VMEM((per_step, N_EXPERTS * cap, n_tok), BF),
        ],
        compiler_params=_params(1),
        name="route",
    )(lg, h2, before)


EXPERT_BLOCK = 1024
EXPERT_SUB = 256


def _experts_kernel(xa_ref, xb_ref, ga_ref, gb_ref, wg_ref, wu_ref, wd_ref, ya_ref, yb_ref, acc_s):
    f = pl.program_id(1)
    ra = xa_ref.shape[1]
    x = jnp.concatenate([xa_ref[0], xb_ref[0]], axis=0)
    for c in range(wg_ref.shape[2] // EXPERT_SUB):
        cs = slice(c * EXPERT_SUB, (c + 1) * EXPERT_SUB)
        gate = _dot(x, wg_ref[0, :, cs].astype(BF))
        up = _dot(x, wu_ref[0, :, cs].astype(BF))
        hid = (gate * _sigmoid(gate) * up).astype(BF)
        down = _dot(hid, wd_ref[0, cs, :].astype(BF))
        if c == 0:
            acc_s[...] = jnp.where(f == 0, 0.0, acc_s[...]) + down
        else:
            acc_s[...] += down

    @pl.when(f == pl.num_programs(1) - 1)
    def _():
        ya_ref[0] = (acc_s[0:ra, :] * ga_ref[0]).astype(ya_ref.dtype)
        yb_ref[0] = (acc_s[ra:, :] * gb_ref[0]).astype(yb_ref.dtype)


def _experts_call(xa, xb, ga, gb, wg, wu, wd):
    ra, rb = xa.shape[1], xb.shape[1]
    tf = EXPERT_BLOCK
    xa_spec = pl.BlockSpec((1, ra, D), lambda e, f: (e, 0, 0))
    xb_spec = pl.BlockSpec((1, rb, D), lambda e, f: (e, 0, 0))
    return pl.pallas_call(
        _experts_kernel,
        grid=(N_EXPERTS, D_EXPERT // tf),
        in_specs=[
            xa_spec, xb_spec,
            pl.BlockSpec((1, ra, 1), lambda e, f: (e, 0, 0)),
            pl.BlockSpec((1, rb, 1), lambda e, f: (e, 0, 0)),
            pl.BlockSpec((1, D, tf), lambda e, f: (e, 0, f)),
            pl.BlockSpec((1, D, tf), lambda e, f: (e, 0, f)),
            pl.BlockSpec((1, tf, D), lambda e, f: (e, f, 0)),
        ],
        out_specs=[xa_spec, xb_spec],
        out_shape=[
            jax.ShapeDtypeStruct((N_EXPERTS, ra, D), BF),
            jax.ShapeDtypeStruct((N_EXPERTS, rb, D), BF),
        ],
        scratch_shapes=[pltpu.VMEM((ra + rb, D), F32)],
        compiler_params=_params(2),
        name="experts",
    )(xa, xb, ga, gb, wg, wu, wd)


SCATTER_STEP_TOKENS = 1024


def _scatter_kernel(cap, n_tok, row_of_req, slot_ref, y_ref, x1_ref, m_ref, expand_ref, out_ref):
    per_step = slot_ref.shape[0] // n_tok
    slot_ids = (lax.broadcasted_iota(jnp.int32, (1, N_EXPERTS * cap), 1) % cap).astype(F32)
    for r in range(per_step):
        rows = slice(r * n_tok, (r + 1) * n_tok)
        m = m_ref[pl.ds(row_of_req(pl.program_id(0) * per_step + r), 1), :]
        gate2 = m[:, 5 * D:6 * D]
        slot_wide = _dot(slot_ref[rows, :].astype(BF), expand_ref[...])
        onehot = jnp.where(slot_wide == slot_ids, 1.0, 0.0).astype(BF)
        y = y_ref[:, r * cap:(r + 1) * cap, :].reshape(N_EXPERTS * cap, D)
        out_ref[rows, :] = x1_ref[rows, :] + gate2 * _dot(onehot, y)


def _scatter_call(slots, y, x1, m, row_of_req, n_req, n_tok):
    cap = _capacity(n_tok)
    per_step = max(1, min(n_req, SCATTER_STEP_TOKENS // n_tok))
    expand = jnp.asarray((np.arange(LANES)[:, None] == np.arange(N_EXPERTS * cap)[None, :] // cap).astype(BF))
    tok = pl.BlockSpec((per_step * n_tok, D), lambda b: (b, 0))
    return pl.pallas_call(
        functools.partial(_scatter_kernel, cap, n_tok, row_of_req),
        grid=(n_req // per_step,),
        in_specs=[
            pl.BlockSpec((per_step * n_tok, LANES), lambda b: (b, 0)),
            pl.BlockSpec((N_EXPERTS, per_step * cap, D), lambda b: (0, b, 0)),
            tok,
            _const_spec((MOD_ROWS, 6 * D)),
            _const_spec((LANES, N_EXPERTS * cap)),
        ],
        out_specs=tok,
        out_shape=jax.ShapeDtypeStruct((n_req * n_tok, D), F32),
        compiler_params=_params(1),
        name="scatter",
    )(slots, y, x1, m, expand)


def _rope_tables(n_tokens):
    rows = n_tokens // GRID_W
    row = np.broadcast_to(np.arange(rows, dtype=np.float32)[:, None], (rows, GRID_W)).reshape(-1)
    col = np.broadcast_to(np.arange(GRID_W, dtype=np.float32)[None, :], (rows, GRID_W)).reshape(-1)
    half = HEAD_DIM // 4
    inv_freq = (np.float32(ROPE_BASE) ** (-np.arange(half, dtype=np.float32) / np.float32(half))).astype(np.float32)
    ar = row[:, None] * inv_freq
    ac = col[:, None] * inv_freq
    ang = np.concatenate([ar, ar, ac, ac], axis=-1).astype(np.float64)
    cos, sin = np.cos(ang).astype(np.float32), np.sin(ang).astype(np.float32)
    first_half = (np.arange(HEAD_DIM) % (HEAD_DIM // 2)) < (HEAD_DIM // 4)
    sin_signed = np.where(first_half[None, :], -sin, sin)
    reps = LANES // HEAD_DIM
    token_major = np.concatenate([np.tile(cos, (1, reps)), np.tile(sin_signed, (1, reps))], axis=1)
    transposed = np.concatenate([cos.T, sin_signed.T], axis=0)
    return jnp.asarray(token_major), jnp.asarray(transposed)


def kernel(x_prompt, x_sample, cache_k, cache_v, c, c_ctx, w_ada, b_ada, norm1_g, norm2_g, w_in, q_norm_g, k_norm_g, lambda_q1, lambda_k1, lambda_q2, lambda_k2, subln_g, gmlp_ln_g, gmlp_ln_b, w_spatial, b_spatial, w_out, w_router, w_gate_e, w_up_e, w_down_e):
    n_p, t_p = x_prompt.shape[0], x_prompt.shape[1]
    n_s, t_s = x_sample.shape[0], x_sample.shape[1]
    n_ctx = cache_k.shape[2]
    l = 0

    m, lamv, w_in_bf, wkt, w_out_bf, wsp_pairs = _prep_call(
        c_ctx[None, :], c, w_ada[l], b_ada[l][None, :], lambda_q1[l][None, :], lambda_k1[l][None, :],
        lambda_q2[l][None, :], lambda_k2[l][None, :], w_in[l], w_out[l], w_spatial[l])

    gsum_np = (np.arange(D)[:, None] // HEAD_DIM == np.arange(LANES)[None, :]).astype(BF)
    gsum = jnp.asarray(gsum_np)
    wr = jnp.pad(w_router[l], ((0, 0), (0, LANES - N_EXPERTS)))
    wr_hi = wr.astype(BF)
    qg = jnp.tile(q_norm_g[l] * (HEAD_DIM ** -0.5 * math.log2(math.e)), D // HEAD_DIM)

    def rows_of(*vectors):
        rows = [jnp.tile(vec, D // vec.shape[0])[None, :] for vec in vectors]
        return jnp.concatenate(rows + [jnp.zeros((MOD_ROWS - len(rows), D), F32)], axis=0)

    consts = {
        "vecs": rows_of(norm1_g[l], qg, gmlp_ln_g[l], gmlp_ln_b[l], norm2_g[l], subln_g[l]),
        "w_in": w_in_bf,
        "wkt": wkt,
        "kgt": jnp.broadcast_to(k_norm_g[l][:, None], (HEAD_DIM, INPROJ_ROWS)),
        "gsum": gsum,
        "gbcast": jnp.asarray(np.concatenate([gsum_np.T, gsum_np.T], axis=0)),
        "wsp": wsp_pairs,
        "bsp": jnp.repeat(b_spatial[l].T, D // N_GROUPS, axis=1),
        "w_out": w_out_bf,
        "wr": jnp.concatenate([wr_hi, (wr - wr_hi.astype(F32)).astype(BF)], axis=1),
    }

    passes = (
        (x_prompt, n_p, t_p, lambda b: 0, False),
        (x_sample, n_s, t_s, lambda b: 1 + b, True),
    )
    outs = []
    kv_out = None
    lg = None
    for x, n_req, n_tok, row_of_req, is_sample in passes:
        xf = x.reshape(n_req * n_tok, D)
        tabs = _rope_tables(n_tok) if is_sample else None
        qst, kt, v = _inproj_call(xf, m, row_of_req, tabs, BF if is_sample else F32, consts, n_req, n_tok,
                                  after=lg)
        if is_sample:
            ctx = (jnp.transpose(cache_k[:, l], (0, 2, 3, 4, 1)).reshape(n_req, D, n_ctx),
                   cache_v[:, l].reshape(n_req * n_ctx, D))
        else:
            ctx = None
            kv_out = (kt, v)
        x1, h2, lg = _attn_call(qst, kt, v, ctx, lamv, xf, m, row_of_req, consts, n_req, n_tok)
        xe, gates, slots = _route_call(lg, h2, n_req, n_tok)
        outs.append((x1, xe, gates, slots, row_of_req))

    kv_out = (kv_out[0], lax.optimization_barrier((kv_out[1], qst))[0])
    (x1p, xep, gp, slots_p, row_p), (x1s, xes, gs, slots_s, row_s) = outs
    yp, ys = _experts_call(xep, xes, gp, gs, w_gate_e[l], w_up_e[l], w_down_e[l])
    y_prompt = _scatter_call(slots_p, yp, x1p, m, row_p, n_p, t_p).reshape(x_prompt.shape)
    y_sample = _scatter_call(slots_s, ys, x1s, m, row_s, n_s, t_s).reshape(x_sample.shape)
    new_k = jnp.transpose(kv_out[0].reshape(n_p, N_HEADS, 2, HEAD_DIM, t_p), (0, 4, 1, 2, 3))
    new_k = new_k.reshape(n_p, 1, t_p, N_HEADS, 2, HEAD_DIM)
    new_v = kv_out[1].reshape(n_p, 1, t_p, N_HEADS, V_HEAD_DIM)
    return (y_prompt, y_sample, new_k, new_v)
```

```python
import functools
import math

import jax
import jax.numpy as jnp
import numpy as np
from jax import lax
from jax.experimental import pallas as pl
from jax.experimental.pallas import tpu as pltpu

D = 1024
N_HEADS = 8
HEAD_DIM = 64
V_HEAD_DIM = 128
GRID_W = 64
ROPE_BASE = 10000.0
CHUNK = 128
N_GROUPS = 8
N_EXPERTS = 16
CAPACITY_FACTOR = 2
D_EXPERT = 2048
N_SEG = 7
K_SEG = 1
EPS = 1e-6
LAMBDA_INIT = 0.8 - 0.6 * math.exp(-0.3 * 0)

LANES = 128
MOD_ROWS = 8
VMEM_LIMIT = 56 * 1024 * 1024

BF = jnp.bfloat16
F32 = jnp.float32


def _dot(a, b):
    return jnp.dot(a, b, preferred_element_type=F32)


def _dot_nt(a, b):
    return lax.dot_general(a, b, (((1,), (1,)), ((), ())), preferred_element_type=F32)


def _split_bf16(x):
    hi = x.astype(BF)
    lo = (x - hi.astype(F32)).astype(BF)
    return hi, lo


def _sigmoid(x):
    return 0.5 * jnp.tanh(0.5 * x) + 0.5


def _rms(x, g):
    return x * lax.rsqrt(jnp.mean(x * x, axis=-1, keepdims=True) + EPS) * g


def _params(n_grid_dims):
    return pltpu.CompilerParams(
        dimension_semantics=("arbitrary",) * n_grid_dims, vmem_limit_bytes=VMEM_LIMIT)


def _const_spec(shape):
    nd = len(shape)
    return pl.BlockSpec(shape, lambda *_: (0,) * nd)


N_MOD = 6


def _prep_kernel(cctx_ref, c_ref, wa_ref, ba_ref, lq1_ref, lk1_ref, lq2_ref, lk2_ref, wi_ref, wo_ref, wsp_ref,
                 m_ref, lam_ref, wib_ref, wkt_ref, wob_ref, wsp2_ref, cond_s):
    j = pl.program_id(0)
    n_lat = c_ref.shape[0]

    @pl.when(j == 0)
    def _():
        cond_s[...] = jnp.zeros(cond_s.shape, F32)
        cond_s[0:1, :] = cctx_ref[...]
        cond_s[1:1 + n_lat, :] = c_ref[...]
        s1 = jnp.sum(lq1_ref[...] * lk1_ref[...], axis=-1, keepdims=True)
        s2 = jnp.sum(lq2_ref[...] * lk2_ref[...], axis=-1, keepdims=True)
        lam_ref[...] = jnp.broadcast_to(jnp.exp(s1) - jnp.exp(s2) + LAMBDA_INIT, lam_ref.shape)
        wob_ref[...] = wo_ref[...].astype(BF)
        for p in range(N_GROUPS // 2):
            wsp2_ref[p] = jnp.concatenate([wsp_ref[2 * p], wsp_ref[2 * p + 1]], axis=1).astype(BF)

    @pl.when(j < N_MOD)
    def _():
        c = cond_s[...]
        a_hi, a_lo = _split_bf16(c * jax.nn.sigmoid(c))
        w_hi, w_lo = _split_bf16(wa_ref[...])
        m_ref[...] = _dot(a_hi, w_hi) + _dot(a_lo, w_hi) + _dot(a_hi, w_lo) + ba_ref[...]

    w = wi_ref[...]
    wib_ref[...] = w.astype(BF)

    @pl.when(j == K_SEG)
    def _():
        wkt_ref[...] = w.T.astype(BF)


def _prep_call(c_ctx, c, w_ada, b_ada, lq1, lk1, lq2, lk2, w_in, w_out, w_spatial):
    vec = _const_spec((1, HEAD_DIM))
    mod_block = lambda j: (0, jnp.minimum(j, N_MOD - 1))
    pairs = (N_GROUPS // 2, CHUNK, 2 * CHUNK)
    return pl.pallas_call(
        _prep_kernel,
        grid=(N_SEG,),
        in_specs=[
            _const_spec((1, D)), _const_spec(c.shape),
            pl.BlockSpec((D, D), mod_block),
            pl.BlockSpec((1, D), mod_block),
            vec, vec, vec, vec,
            pl.BlockSpec((D, D), lambda j: (0, j)),
            _const_spec((D, D)),
            _const_spec((N_GROUPS, CHUNK, CHUNK)),
        ],
        out_specs=[
            pl.BlockSpec((MOD_ROWS, D), mod_block),
            _const_spec((MOD_ROWS, LANES)),
            pl.BlockSpec((D, D), lambda j: (0, j)),
            _const_spec((D, D)),
            _const_spec((D, D)),
            _const_spec(pairs),
        ],
        out_shape=[
            jax.ShapeDtypeStruct((MOD_ROWS, N_MOD * D), F32),
            jax.ShapeDtypeStruct((MOD_ROWS, LANES), F32),
            jax.ShapeDtypeStruct((D, N_SEG * D), BF),
            jax.ShapeDtypeStruct((D, D), BF),
            jax.ShapeDtypeStruct((D, D), BF),
            jax.ShapeDtypeStruct(pairs, BF),
        ],
        scratch_shapes=[pltpu.VMEM((MOD_ROWS, D), F32)],
        compiler_params=_params(1),
        name="prep",
    )(c_ctx, c, w_ada, b_ada, lq1, lk1, lq2, lk2, w_in, w_out, w_spatial)


def _head_norm(t, g, gsum, gbcast):
    ss = _dot((t * t).astype(BF), gsum)
    inv = lax.rsqrt(ss * (1.0 / HEAD_DIM) + EPS)
    hi, lo = _split_bf16(inv)
    bc = _dot(jnp.concatenate([hi, lo], axis=-1), gbcast)
    return t * bc * g


def _lane_tile(tab):
    return jnp.concatenate([tab] * (D // LANES), axis=1)


def _rope(t, cos, sin_signed):
    q = HEAD_DIM // 4
    lane = lax.broadcasted_iota(jnp.int32, (1, D), 1)
    first_half = (lane % (HEAD_DIM // 2)) < q
    fwd = pltpu.roll(t, D - q, axis=1)
    bwd = pltpu.roll(t, q, axis=1)
    return t * _lane_tile(cos) + jnp.where(first_half, fwd, bwd) * _lane_tile(sin_signed)


def _rope_t(t3, cos_t, sin_signed_t):
    q = HEAD_DIM // 4
    rot = jnp.concatenate([t3[:, q:2 * q], t3[:, 0:q], t3[:, 3 * q:4 * q], t3[:, 2 * q:3 * q]], axis=1)
    return t3 * cos_t[None] + rot * sin_signed_t[None]


def _inproj_kernel(rope, row_of_step, x_ref, m_ref, vecs_ref, w_ref, wkt_ref, kgt_ref, gsum_ref,
                   gbcast_ref, wsp_ref, bsp_ref, *rest):
    if rope:
        tab_ref, tab_t_ref, *rest = rest
    qst_out, kt_out, v_out = rest[-3:]
    tm = x_ref.shape[0]
    m = m_ref[pl.ds(row_of_step(pl.program_id(0)), 1), :]
    shift1, scale1 = m[:, 0:D], m[:, D:2 * D]
    n1g, qg, lng, lnb = (vecs_ref[r:r + 1, :] for r in range(4))
    h = (_rms(x_ref[...], n1g) * (1.0 + scale1) + shift1).astype(BF)

    def seg(j):
        return _dot(h, w_ref[:, j * D:(j + 1) * D])

    q = _head_norm(seg(0), qg, gsum_ref[...], gbcast_ref[...])
    if rope:
        q = _rope(q, tab_ref[:, 0:LANES], tab_ref[:, LANES:2 * LANES])
    qst_out[:, 0:D] = q.astype(qst_out.dtype)

    k3 = _dot_nt(wkt_ref[...], h).reshape(D // HEAD_DIM, HEAD_DIM, tm)
    inv = lax.rsqrt(jnp.mean(k3 * k3, axis=1, keepdims=True) + EPS)
    k3 = k3 * inv * kgt_ref[...][None]
    if rope:
        k3 = _rope_t(k3, tab_t_ref[0:HEAD_DIM, :], tab_t_ref[HEAD_DIM:2 * HEAD_DIM, :])
    kt = k3.reshape(D, tm).astype(kt_out.dtype)
    width = kt_out.shape[2]
    for r in range(kt_out.shape[0]):
        kt_out[r] = kt[:, r * width:(r + 1) * width]

    v_out[...] = seg(2).astype(v_out.dtype)
    qst_out[:, D:2 * D] = _sigmoid(seg(5)).astype(qst_out.dtype)

    zv = jax.nn.gelu(seg(4))
    mu = jnp.mean(zv, axis=-1, keepdims=True)
    zc = zv - mu
    var = jnp.mean(zc * zc, axis=-1, keepdims=True)
    zvn = (zc * lax.rsqrt(var + EPS) * lng + lnb).astype(BF)
    pre = jax.nn.gelu(seg(3)) * _sigmoid(seg(6))
    blank = jnp.zeros((CHUNK, LANES), BF)
    for c in range(tm // CHUNK):
        rows = slice(c * CHUNK, (c + 1) * CHUNK)
        for p in range(N_GROUPS // 2):
            cols = slice(2 * p * LANES, (2 * p + 2) * LANES)
            z = zvn[rows, cols]
            z_diag = jnp.concatenate([jnp.concatenate([z[:, :LANES], blank], axis=1),
                                      jnp.concatenate([blank, z[:, LANES:]], axis=1)], axis=0)
            sp = _dot(wsp_ref[p], z_diag) + bsp_ref[:, cols]
            out_cols = slice(2 * D + 2 * p * LANES, 2 * D + (2 * p + 2) * LANES)
            qst_out[rows, out_cols] = (pre[rows, cols] * sp).astype(qst_out.dtype)


INPROJ_ROWS = 512


def _inproj_call(x, m, row_of_req, rope_tabs, kv_dtype, consts, n_req, n_tok, after=None):
    t = x.shape[0]
    tm = INPROJ_ROWS
    per_req = max(1, n_tok // tm)
    per_tile = max(1, tm // n_tok)
    rope = rope_tabs is not None

    def row_of_step(i):
        return row_of_req(i * per_tile // per_req)

    tok = pl.BlockSpec((tm, D), lambda i: (i, 0))
    in_specs = [
        tok,
        _const_spec((MOD_ROWS, 6 * D)),
        _const_spec((MOD_ROWS, D)),
        pl.BlockSpec((D, N_SEG * D), lambda i: (0, 0), pipeline_mode=pl.Buffered(1)),
        _const_spec((D, D)),
        _const_spec((HEAD_DIM, tm)),
        _const_spec((D, LANES)), _const_spec((2 * LANES, D)),
        _const_spec((N_GROUPS // 2, CHUNK, 2 * CHUNK)), _const_spec((CHUNK, D)),
    ]
    args = [x, m, consts["vecs"], consts["w_in"], consts["wkt"], consts["kgt"], consts["gsum"],
            consts["gbcast"], consts["wsp"], consts["bsp"]]
    if rope:
        in_specs += [pl.BlockSpec((tm, 2 * LANES), lambda i: (i % per_req, 0)),
                     pl.BlockSpec((2 * HEAD_DIM, tm), lambda i: (0, i % per_req))]
        args += list(rope_tabs)
    if after is not None:
        in_specs += [pl.BlockSpec(memory_space=pl.ANY)]
        args += [after]
    kt_spec = pl.BlockSpec((per_tile, D, tm // per_tile), lambda i: (i, 0, 0))
    return pl.pallas_call(
        functools.partial(_inproj_kernel, rope, row_of_step),
        grid=(t // tm,),
        in_specs=in_specs,
        out_specs=[pl.BlockSpec((tm, 3 * D), lambda i: (i, 0)), kt_spec, tok],
        out_shape=[
            jax.ShapeDtypeStruct((t, 3 * D), BF),
            jax.ShapeDtypeStruct((n_req * per_req, D, tm // per_tile), kv_dtype),
            jax.ShapeDtypeStruct((t, D), kv_dtype),
        ],
        compiler_params=_params(1),
        name="inproj_rope" if rope else "inproj",
    )(*args)


SCORE_GROUP_ELEMS = 1 << 22
ATTN_ROWS = 256


def _attn_kernel(has_ctx, heads_per_group, row_of_req, qst_ref, kt_ref, v_ref, *rest):
    if has_ctx:
        kct_ref, vc_ref, *rest = rest
    lam_ref, vecs_ref, x_ref, m_ref, wo_ref, wr_ref, x1_out, h2_out, lg_out, merged_s = rest
    tq = qst_ref.shape[0]
    key_blocks, block_keys = kt_ref.shape[0], kt_ref.shape[2]
    n2g = vecs_ref[4:5, :]
    sub_g = vecs_ref[5:6, 0:V_HEAD_DIM]
    lam = lam_ref[0:1, 0:1]
    lane = lax.broadcasted_iota(jnp.int32, (1, LANES), 1)
    first = lane < HEAD_DIM
    zero = jnp.zeros((), BF)
    ones_col = jnp.where(lane == 0, 1.0, 0.0).astype(BF)

    def values(ref, rows, h):
        v = ref[rows, h * V_HEAD_DIM:(h + 1) * V_HEAD_DIM].astype(BF)
        return jnp.concatenate([v, jnp.broadcast_to(ones_col, (v.shape[0], LANES))], axis=1)

    def head_cols(h):
        return slice(h * LANES, (h + 1) * LANES)

    def head_scores(h):
        q = qst_ref[:, head_cols(h)]
        qz = jnp.concatenate([jnp.where(first, q, zero), jnp.where(first, zero, q)], axis=0)
        parts = [_dot(qz, kt_ref[j, head_cols(h), :].astype(BF)) for j in range(key_blocks)]
        if has_ctx:
            parts.append(_dot(qz, kct_ref[0, head_cols(h), :].astype(BF)))
        return parts

    def row_max(scores):
        return functools.reduce(jnp.maximum, [jnp.max(s, axis=-1, keepdims=True) for s in scores])

    def head_pv(h, scores, mx):
        vals = [values(v_ref, slice(j * block_keys, (j + 1) * block_keys), h) for j in range(key_blocks)]
        if has_ctx:
            vals.append(values(vc_ref, slice(None), h))
        return functools.reduce(jnp.add, [_dot(jnp.exp2(s - mx).astype(BF), v) for s, v in zip(scores, vals)])

    def head_finish(h, ob):
        den = ob[:, V_HEAD_DIM:V_HEAD_DIM + 1]
        o = ob[:tq, :V_HEAD_DIM] * (1.0 / den[:tq]) - ob[tq:, :V_HEAD_DIM] * (lam / den[tq:])
        o = _rms(o, sub_g) * (1.0 - LAMBDA_INIT)
        sga = qst_ref[:, D + h * LANES:D + (h + 1) * LANES].astype(F32)
        tb = qst_ref[:, 2 * D + h * LANES:2 * D + (h + 1) * LANES].astype(F32)
        merged_s[:, head_cols(h)] = (sga * o + tb).astype(BF)

    for g0 in range(0, N_HEADS, heads_per_group):
        group = range(g0, g0 + heads_per_group)
        scores = [head_scores(h) for h in group]
        maxes = [row_max(s) for s in scores]
        outs = [head_pv(h, s, mx) for h, s, mx in zip(group, scores, maxes)]
        for h, ob in zip(group, outs):
            head_finish(h, ob)

    m = m_ref[pl.ds(row_of_req(pl.program_id(0)), 1), :]
    gate1, shift2, scale2 = m[:, 2 * D:3 * D], m[:, 3 * D:4 * D], m[:, 4 * D:5 * D]
    x1 = x_ref[...] + gate1 * _dot(merged_s[...], wo_ref[...])
    x1_out[...] = x1
    h2 = _rms(x1, n2g) * (1.0 + scale2) + shift2
    h2_out[...] = h2.astype(BF)
    hi, lo = _split_bf16(h2)
    wr_hi, wr_lo = wr_ref[:, 0:LANES], wr_ref[:, LANES:2 * LANES]
    lg_out[...] = _dot(hi, wr_hi) + _dot(lo, wr_hi) + _dot(hi, wr_lo)


def _attn_call(qst, kt, v, ctx, lamv, x, m, row_of_req, consts, n_req, n_tok):
    tq = min(n_tok, ATTN_ROWS)
    nqb = n_tok // tq
    key_blocks = kt.shape[0] // n_req
    has_ctx = ctx is not None
    tok = pl.BlockSpec((tq, D), lambda b, i: (b * nqb + i, 0))
    in_specs = [pl.BlockSpec((tq, 3 * D), lambda b, i: (b * nqb + i, 0)),
                pl.BlockSpec((key_blocks, D, kt.shape[2]), lambda b, i: (b, 0, 0)),
                pl.BlockSpec((n_tok, D), lambda b, i: (b, 0))]
    args = [qst, kt, v]
    if has_ctx:
        n_ctx = ctx[0].shape[2]
        in_specs += [pl.BlockSpec((1, D, n_ctx), lambda b, i: (b, 0, 0)),
                     pl.BlockSpec((n_ctx, D), lambda b, i: (b, 0))]
        args += list(ctx)
    in_specs += [
        _const_spec((MOD_ROWS, LANES)), _const_spec((MOD_ROWS, D)),
        tok,
        _const_spec((MOD_ROWS, 6 * D)),
        _const_spec((D, D)),
        _const_spec((D, 2 * LANES)),
    ]
    args += [lamv, consts["vecs"], x, m, consts["w_out"], consts["wr"]]
    t = n_req * n_tok
    n_keys = n_tok + (ctx[0].shape[2] if has_ctx else 0)
    heads_per_group = max(1, min(N_HEADS, SCORE_GROUP_ELEMS // (2 * tq * n_keys)))
    while N_HEADS % heads_per_group:
        heads_per_group -= 1
    return pl.pallas_call(
        functools.partial(_attn_kernel, has_ctx, heads_per_group, row_of_req),
        grid=(n_req, nqb),
        in_specs=in_specs,
        out_specs=[tok, tok, pl.BlockSpec((tq, LANES), lambda b, i: (b * nqb + i, 0))],
        out_shape=[
            jax.ShapeDtypeStruct((t, D), F32),
            jax.ShapeDtypeStruct((t, D), BF),
            jax.ShapeDtypeStruct((t, LANES), F32),
        ],
        scratch_shapes=[pltpu.VMEM((tq, D), BF)],
        compiler_params=_params(2),
        name="attn_ctx" if has_ctx else "attn",
    )(*args)


GATHER_ROWS = 512
KEY_BITS = 31
ROUTE_UNROLL_ELEMS = 1 << 22
ROUTE_STEP_TOKENS = 2048
PREFIX_BLOCK = 128


def _route_kernel(cap, n_tok, unroll, lg_ref, h2_ref, before_ref, xe_ref, gate_ref, slot_ref,
                  aff_t_s, slot_t_s, p_s):
    reqs = range(lg_ref.shape[0] // n_tok)
    lane = lax.broadcasted_iota(jnp.int32, (1, LANES), 1)
    valid = lane < N_EXPERTS

    def rows(r):
        return slice(r * n_tok, (r + 1) * n_tok)

    def affinity(r):
        lg = jnp.where(valid, lg_ref[rows(r), :], -1e30)
        ex = jnp.where(valid, jnp.exp(lg - jnp.max(lg, axis=-1, keepdims=True)), 0.0)
        return ex / jnp.sum(ex, axis=-1, keepdims=True)

    def count(mask):
        return jnp.sum(jnp.where(mask, 1.0, 0.0), axis=0, keepdims=True)

    affs = [affinity(r) for r in reqs]
    kth_bits = [jnp.zeros((1, LANES), jnp.int32) for _ in reqs]
    for bit in range(KEY_BITS - 1, -1, -1):
        for r in reqs:
            cand = kth_bits[r] | (1 << bit)
            enough = count(affs[r] >= lax.bitcast_convert_type(cand, F32)) >= cap
            kth_bits[r] = jnp.where(enough, cand, kth_bits[r])

    before = before_ref[...]

    def earlier_count(flags):
        blocks, running = [], jnp.zeros((1, LANES), F32)
        for b in range(n_tok // PREFIX_BLOCK):
            blk = flags[b * PREFIX_BLOCK:(b + 1) * PREFIX_BLOCK]
            blocks.append(_dot(before, blk.astype(BF)) + running)
            running = running + jnp.sum(blk, axis=0, keepdims=True)
        return jnp.concatenate(blocks, axis=0)

    for r in reqs:
        aff = affs[r]
        kth = lax.bitcast_convert_type(kth_bits[r], F32)
        above = aff > kth
        tied = aff == kth
        need = cap - count(above)
        tied_before = earlier_count(jnp.where(tied, 1.0, 0.0))
        chosen = jnp.where(above, 1.0, jnp.where(tied, jnp.where(tied_before < need, 1.0, 0.0), 0.0))
        slot = earlier_count(chosen)
        slot = jnp.where(valid, jnp.where(chosen > 0.0, slot, float(cap)), float(cap))
        slot_ref[rows(r), :] = slot
        slot_t_s[r] = slot.T
        aff_t_s[r] = aff.T

    slot_ids = lax.broadcasted_iota(jnp.int32, (cap, 1), 0).astype(F32)

    def per_expert(e, carry):
        for r in reqs:
            hit = slot_t_s[r, pl.ds(e, 1), :] == slot_ids
            p_s[r, pl.ds(pl.multiple_of(e * cap, cap), cap), :] = jnp.where(hit, 1.0, 0.0).astype(BF)
            gate = jnp.sum(jnp.where(hit, aff_t_s[r, pl.ds(e, 1), :], 0.0), axis=-1, keepdims=True)
            gate_ref[e, r * cap:(r + 1) * cap, :] = gate
        return carry

    lax.fori_loop(0, N_EXPERTS, per_expert, 0, unroll=unroll)
    n_rows = min(GATHER_ROWS, N_EXPERTS * cap)
    e_per = n_rows // cap
    for r in reqs:
        h2 = h2_ref[rows(r), :]
        for t in range(N_EXPERTS * cap // n_rows):
            xe = _dot(p_s[r, t * n_rows:(t + 1) * n_rows, :], h2).astype(BF)
            xe_ref[t * e_per:(t + 1) * e_per, r * cap:(r + 1) * cap, :] = xe.reshape(e_per, cap, D)


def _capacity(n_tok):
    return max(1, CAPACITY_FACTOR * n_tok // N_EXPERTS)


def _route_call(lg, h2, n_req, n_tok):
    cap = _capacity(n_tok)
    per_step = max(1, min(n_req, ROUTE_STEP_TOKENS // n_tok))
    before = jnp.asarray((np.arange(PREFIX_BLOCK)[None, :] < np.arange(PREFIX_BLOCK)[:, None]).astype(BF))
    unroll = max(1, min(N_EXPERTS, ROUTE_UNROLL_ELEMS // (per_step * cap * n_tok)))
    return pl.pallas_call(
        functools.partial(_route_kernel, cap, n_tok, unroll),
        grid=(n_req // per_step,),
        in_specs=[
            pl.BlockSpec((per_step * n_tok, LANES), lambda b: (b, 0)),
            pl.BlockSpec((per_step * n_tok, D), lambda b: (b, 0)),
            _const_spec((PREFIX_BLOCK, PREFIX_BLOCK)),
        ],
        out_specs=[
            pl.BlockSpec((N_EXPERTS, per_step * cap, D), lambda b: (0, b, 0)),
            pl.BlockSpec((N_EXPERTS, per_step * cap, 1), lambda b: (0, b, 0)),
            pl.BlockSpec((per_step * n_tok, LANES), lambda b: (b, 0)),
        ],
        out_shape=[
            jax.ShapeDtypeStruct((N_EXPERTS, n_req * cap, D), BF),
            jax.ShapeDtypeStruct((N_EXPERTS, n_req * cap, 1), F32),
            jax.ShapeDtypeStruct((n_req * n_tok, LANES), F32),
        ],
        scratch_shapes=[
            pltpu.VMEM((per_step, LANES, n_tok), F32),
            pltpu.VMEM((per_step, LANES, n_tok), F32),
            pltpu.VMEM((per_step, N_EXPERTS * cap, n_tok), BF),
        ],
        compiler_params=_params(1),
        name="route",
    )(lg, h2, before)


EXPERT_BLOCK = 1024
EXPERT_SUB = 256


def _experts_kernel(xa_ref, xb_ref, ga_ref, gb_ref, wg_ref, wu_ref, wd_ref, ya_ref, yb_ref, acc_s):
    f = pl.program_id(1)
    ra = xa_ref.shape[1]
    x = jnp.concatenate([xa_ref[0], xb_ref[0]], axis=0)
    for c in range(wg_ref.shape[2] // EXPERT_SUB):
        cs = slice(c * EXPERT_SUB, (c + 1) * EXPERT_SUB)
        gate = _dot(x, wg_ref[0, :, cs].astype(BF))
        up = _dot(x, wu_ref[0, :, cs].astype(BF))
        hid = (gate * _sigmoid(gate) * up).astype(BF)
        down = _dot(hid, wd_ref[0, cs, :].astype(BF))
        if c == 0:
            acc_s[...] = jnp.where(f == 0, 0.0, acc_s[...]) + down
        else:
            acc_s[...] += down

    @pl.when(f == pl.num_programs(1) - 1)
    def _():
        ya_ref[0] = (acc_s[0:ra, :] * ga_ref[0]).astype(ya_ref.dtype)
        yb_ref[0] = (acc_s[ra:, :] * gb_ref[0]).astype(yb_ref.dtype)


def _experts_call(xa, xb, ga, gb, wg, wu, wd):
    ra, rb = xa.shape[1], xb.shape[1]
    tf = EXPERT_BLOCK
    xa_spec = pl.BlockSpec((1, ra, D), lambda e, f: (e, 0, 0))
    xb_spec = pl.BlockSpec((1, rb, D), lambda e, f: (e, 0, 0))
    return pl.pallas_call(
        _experts_kernel,
        grid=(N_EXPERTS, D_EXPERT // tf),
        in_specs=[
            xa_spec, xb_spec,
            pl.BlockSpec((1, ra, 1), lambda e, f: (e, 0, 0)),
            pl.BlockSpec((1, rb, 1), lambda e, f: (e, 0, 0)),
            pl.BlockSpec((1, D, tf), lambda e, f: (e, 0, f)),
            pl.BlockSpec((1, D, tf), lambda e, f: (e, 0, f)),
            pl.BlockSpec((1, tf, D), lambda e, f: (e, f, 0)),
        ],
        out_specs=[xa_spec, xb_spec],
        out_shape=[
            jax.ShapeDtypeStruct((N_EXPERTS, ra, D), BF),
            jax.ShapeDtypeStruct((N_EXPERTS, rb, D), BF),
        ],
        scratch_shapes=[pltpu.VMEM((ra + rb, D), F32)],
        compiler_params=_params(2),
        name="experts",
    )(xa, xb, ga, gb, wg, wu, wd)


SCATTER_STEP_TOKENS = 1024


def _scatter_kernel(cap, n_tok, row_of_req, slot_ref, y_ref, x1_ref, m_ref, expand_ref, out_ref):
    per_step = slot_ref.shape[0] // n_tok
    slot_ids = (lax.broadcasted_iota(jnp.int32, (1, N_EXPERTS * cap), 1) % cap).astype(F32)
    for r in range(per_step):
        rows = slice(r * n_tok, (r + 1) * n_tok)
        m = m_ref[pl.ds(row_of_req(pl.program_id(0) * per_step + r), 1), :]
        gate2 = m[:, 5 * D:6 * D]
        slot_wide = _dot(slot_ref[rows, :].astype(BF), expand_ref[...])
        onehot = jnp.where(slot_wide == slot_ids, 1.0, 0.0).astype(BF)
        y = y_ref[:, r * cap:(r + 1) * cap, :].reshape(N_EXPERTS * cap, D)
        out_ref[rows, :] = x1_ref[rows, :] + gate2 * _dot(onehot, y)


def _scatter_call(slots, y, x1, m, row_of_req, n_req, n_tok):
    cap = _capacity(n_tok)
    per_step = max(1, min(n_req, SCATTER_STEP_TOKENS // n_tok))
    expand = jnp.asarray((np.arange(LANES)[:, None] == np.arange(N_EXPERTS * cap)[None, :] // cap).astype(BF))
    tok = pl.BlockSpec((per_step * n_tok, D), lambda b: (b, 0))
    return pl.pallas_call(
        functools.partial(_scatter_kernel, cap, n_tok, row_of_req),
        grid=(n_req // per_step,),
        in_specs=[
            pl.BlockSpec((per_step * n_tok, LANES), lambda b: (b, 0)),
            pl.BlockSpec((N_EXPERTS, per_step * cap, D), lambda b: (0, b, 0)),
            tok,
            _const_spec((MOD_ROWS, 6 * D)),
            _const_spec((LANES, N_EXPERTS * cap)),
        ],
        out_specs=tok,
        out_shape=jax.ShapeDtypeStruct((n_req * n_tok, D), F32),
        compiler_params=_params(1),
        name="scatter",
    )(slots, y, x1, m, expand)


def _rope_tables(n_tokens):
    rows = n_tokens // GRID_W
    row = np.broadcast_to(np.arange(rows, dtype=np.float32)[:, None], (rows, GRID_W)).reshape(-1)
    col = np.broadcast_to(np.arange(GRID_W, dtype=np.float32)[None, :], (rows, GRID_W)).reshape(-1)
    half = HEAD_DIM // 4
    inv_freq = (np.float32(ROPE_BASE) ** (-np.arange(half, dtype=np.float32) / np.float32(half))).astype(np.float32)
    ar = row[:, None] * inv_freq
    ac = col[:, None] * inv_freq
    ang = np.concatenate([ar, ar, ac, ac], axis=-1).astype(np.float64)
    cos, sin = np.cos(ang).astype(np.float32), np.sin(ang).astype(np.float32)
    first_half = (np.arange(HEAD_DIM) % (HEAD_DIM // 2)) < (HEAD_DIM // 4)
    sin_signed = np.where(first_half[None, :], -sin, sin)
    reps = LANES // HEAD_DIM
    token_major = np.concatenate([np.tile(cos, (1, reps)), np.tile(sin_signed, (1, reps))], axis=1)
    transposed = np.concatenate([cos.T, sin_signed.T], axis=0)
    return jnp.asarray(token_major), jnp.asarray(transposed)


def kernel(x_prompt, x_sample, cache_k, cache_v, c, c_ctx, w_ada, b_ada, norm1_g, norm2_g, w_in, q_norm_g, k_norm_g, lambda_q1, lambda_k1, lambda_q2, lambda_k2, subln_g, gmlp_ln_g, gmlp_ln_b, w_spatial, b_spatial, w_out, w_router, w_gate_e, w_up_e, w_down_e):
    n_p, t_p = x_prompt.shape[0], x_prompt.shape[1]
    n_s, t_s = x_sample.shape[0], x_sample.shape[1]
    n_ctx = cache_k.shape[2]
    l = 0

    m, lamv, w_in_bf, wkt, w_out_bf, wsp_pairs = _prep_call(
        c_ctx[None, :], c, w_ada[l], b_ada[l][None, :], lambda_q1[l][None, :], lambda_k1[l][None, :],
        lambda_q2[l][None, :], lambda_k2[l][None, :], w_in[l], w_out[l], w_spatial[l])

    gsum_np = (np.arange(D)[:, None] // HEAD_DIM == np.arange(LANES)[None, :]).astype(BF)
    gsum = jnp.asarray(gsum_np)
    wr = jnp.pad(w_router[l], ((0, 0), (0, LANES - N_EXPERTS)))
    wr_hi = wr.astype(BF)
    qg = jnp.tile(q_norm_g[l] * (HEAD_DIM ** -0.5 * math.log2(math.e)), D // HEAD_DIM)

    def rows_of(*vectors):
        rows = [jnp.tile(vec, D // vec.shape[0])[None, :] for vec in vectors]
        return jnp.concatenate(rows + [jnp.zeros((MOD_ROWS - len(rows), D), F32)], axis=0)

    consts = {
        "vecs": rows_of(norm1_g[l], qg, gmlp_ln_g[l], gmlp_ln_b[l], norm2_g[l], subln_g[l]),
        "w_in": w_in_bf,
        "wkt": wkt,
        "kgt": jnp.broadcast_to(k_norm_g[l][:, None], (HEAD_DIM, INPROJ_ROWS)),
        "gsum": gsum,
        "gbcast": jnp.asarray(np.concatenate([gsum_np.T, gsum_np.T], axis=0)),
        "wsp": wsp_pairs,
        "bsp": jnp.repeat(b_spatial[l].T, D // N_GROUPS, axis=1),
        "w_out": w_out_bf,
        "wr": jnp.concatenate([wr_hi, (wr - wr_hi.astype(F32)).astype(BF)], axis=1),
    }

    passes = (
        (x_prompt, n_p, t_p, lambda b: 0, False),
        (x_sample, n_s, t_s, lambda b: 1 + b, True),
    )
    outs = []
    kv_out = None
    slots = None
    for x, n_req, n_tok, row_of_req, is_sample in passes:
        xf = x.reshape(n_req * n_tok, D)
        tabs = _rope_tables(n_tok) if is_sample else None
        qst, kt, v = _inproj_call(xf, m, row_of_req, tabs, BF if is_sample else F32, consts, n_req, n_tok,
                                  after=slots)
        if is_sample:
            ctx = (jnp.transpose(cache_k[:, l], (0, 2, 3, 4, 1)).reshape(n_req, D, n_ctx),
                   cache_v[:, l].reshape(n_req * n_ctx, D))
        else:
            ctx = None
            kv_out = (kt, v)
        x1, h2, lg = _attn_call(qst, kt, v, ctx, lamv, xf, m, row_of_req, consts, n_req, n_tok)
        xe, gates, slots = _route_call(lg, h2, n_req, n_tok)
        outs.append((x1, xe, gates, slots, row_of_req))

    kv_out = (kv_out[0], lax.optimization_barrier((kv_out[1], qst))[0])
    (x1p, xep, gp, slots_p, row_p), (x1s, xes, gs, slots_s, row_s) = outs
    yp, ys = _experts_call(xep, xes, gp, gs, w_gate_e[l], w_up_e[l], w_down_e[l])
    y_prompt = _scatter_call(slots_p, yp, x1p, m, row_p, n_p, t_p).reshape(x_prompt.shape)
    y_sample = _scatter_call(slots_s, ys, x1s, m, row_s, n_s, t_s).reshape(x_sample.shape)
    new_k = jnp.transpose(kv_out[0].reshape(n_p, N_HEADS, 2, HEAD_DIM, t_p), (0, 4, 1, 2, 3))
    new_k = new_k.reshape(n_p, 1, t_p, N_HEADS, 2, HEAD_DIM)
    new_v = kv_out[1].reshape(n_p, 1, t_p, N_HEADS, V_HEAD_DIM)
    return (y_prompt, y_sample, new_k, new_v)
```

```python
import functools
import math

import jax
import jax.numpy as jnp
import numpy as np
from jax import lax
from jax.experimental import pallas as pl
from jax.experimental.pallas import tpu as pltpu

D = 1024
N_HEADS = 8
HEAD_DIM = 64
V_HEAD_DIM = 128
GRID_W = 64
ROPE_BASE = 10000.0
CHUNK = 128
N_GROUPS = 8
N_EXPERTS = 16
CAPACITY_FACTOR = 2
D_EXPERT = 2048
N_SEG = 7
K_SEG = 1
EPS = 1e-6
LAMBDA_INIT = 0.8 - 0.6 * math.exp(-0.3 * 0)

LANES = 128
MOD_ROWS = 8
VMEM_LIMIT = 56 * 1024 * 1024

BF = jnp.bfloat16
F32 = jnp.float32


def _dot(a, b):
    return jnp.dot(a, b, preferred_element_type=F32)


def _dot_nt(a, b):
    return lax.dot_general(a, b, (((1,), (1,)), ((), ())), preferred_element_type=F32)


def _split_bf16(x):
    hi = x.astype(BF)
    lo = (x - hi.astype(F32)).astype(BF)
    return hi, lo


def _sigmoid(x):
    return 0.5 * jnp.tanh(0.5 * x) + 0.5


def _rms(x, g):
    return x * lax.rsqrt(jnp.mean(x * x, axis=-1, keepdims=True) + EPS) * g


def _params(n_grid_dims):
    return pltpu.CompilerParams(
        dimension_semantics=("arbitrary",) * n_grid_dims, vmem_limit_bytes=VMEM_LIMIT)


def _const_spec(shape):
    nd = len(shape)
    return pl.BlockSpec(shape, lambda *_: (0,) * nd)


N_MOD = 6


def _prep_kernel(cctx_ref, c_ref, wa_ref, ba_ref, lq1_ref, lk1_ref, lq2_ref, lk2_ref, wi_ref, wo_ref, wsp_ref,
                 m_ref, lam_ref, wib_ref, wkt_ref, wob_ref, wsp2_ref, cond_s):
    j = pl.program_id(0)
    n_lat = c_ref.shape[0]

    @pl.when(j == 0)
    def _():
        cond_s[...] = jnp.zeros(cond_s.shape, F32)
        cond_s[0:1, :] = cctx_ref[...]
        cond_s[1:1 + n_lat, :] = c_ref[...]
        s1 = jnp.sum(lq1_ref[...] * lk1_ref[...], axis=-1, keepdims=True)
        s2 = jnp.sum(lq2_ref[...] * lk2_ref[...], axis=-1, keepdims=True)
        lam_ref[...] = jnp.broadcast_to(jnp.exp(s1) - jnp.exp(s2) + LAMBDA_INIT, lam_ref.shape)
        wob_ref[...] = wo_ref[...].astype(BF)
        for p in range(N_GROUPS // 2):
            wsp2_ref[p] = jnp.concatenate([wsp_ref[2 * p], wsp_ref[2 * p + 1]], axis=1).astype(BF)

    @pl.when(j < N_MOD)
    def _():
        c = cond_s[...]
        a_hi, a_lo = _split_bf16(c * jax.nn.sigmoid(c))
        w_hi, w_lo = _split_bf16(wa_ref[...])
        m_ref[...] = _dot(a_hi, w_hi) + _dot(a_lo, w_hi) + _dot(a_hi, w_lo) + ba_ref[...]

    w = wi_ref[...]
    wib_ref[...] = w.astype(BF)

    @pl.when(j == K_SEG)
    def _():
        wkt_ref[...] = w.T.astype(BF)


def _prep_call(c_ctx, c, w_ada, b_ada, lq1, lk1, lq2, lk2, w_in, w_out, w_spatial):
    vec = _const_spec((1, HEAD_DIM))
    mod_block = lambda j: (0, jnp.minimum(j, N_MOD - 1))
    pairs = (N_GROUPS // 2, CHUNK, 2 * CHUNK)
    return pl.pallas_call(
        _prep_kernel,
        grid=(N_SEG,),
        in_specs=[
            _const_spec((1, D)), _const_spec(c.shape),
            pl.BlockSpec((D, D), mod_block),
            pl.BlockSpec((1, D), mod_block),
            vec, vec, vec, vec,
            pl.BlockSpec((D, D), lambda j: (0, j)),
            _const_spec((D, D)),
            _const_spec((N_GROUPS, CHUNK, CHUNK)),
        ],
        out_specs=[
            pl.BlockSpec((MOD_ROWS, D), mod_block),
            _const_spec((MOD_ROWS, LANES)),
            pl.BlockSpec((D, D), lambda j: (0, j)),
            _const_spec((D, D)),
            _const_spec((D, D)),
            _const_spec(pairs),
        ],
        out_shape=[
            jax.ShapeDtypeStruct((MOD_ROWS, N_MOD * D), F32),
            jax.ShapeDtypeStruct((MOD_ROWS, LANES), F32),
            jax.ShapeDtypeStruct((D, N_SEG * D), BF),
            jax.ShapeDtypeStruct((D, D), BF),
            jax.ShapeDtypeStruct((D, D), BF),
            jax.ShapeDtypeStruct(pairs, BF),
        ],
        scratch_shapes=[pltpu.VMEM((MOD_ROWS, D), F32)],
        compiler_params=_params(1),
        name="prep",
    )(c_ctx, c, w_ada, b_ada, lq1, lk1, lq2, lk2, w_in, w_out, w_spatial)


def _head_norm(t, g, gsum, gbcast):
    ss = _dot((t * t).astype(BF), gsum)
    inv = lax.rsqrt(ss * (1.0 / HEAD_DIM) + EPS)
    hi, lo = _split_bf16(inv)
    bc = _dot(jnp.concatenate([hi, lo], axis=-1), gbcast)
    return t * bc * g


def _lane_tile(tab):
    return jnp.concatenate([tab] * (D // LANES), axis=1)


def _rope(t, cos, sin_signed):
    q = HEAD_DIM // 4
    lane = lax.broadcasted_iota(jnp.int32, (1, D), 1)
    first_half = (lane % (HEAD_DIM // 2)) < q
    fwd = pltpu.roll(t, D - q, axis=1)
    bwd = pltpu.roll(t, q, axis=1)
    return t * _lane_tile(cos) + jnp.where(first_half, fwd, bwd) * _lane_tile(sin_signed)


def _rope_t(t3, cos_t, sin_signed_t):
    q = HEAD_DIM // 4
    rot = jnp.concatenate([t3[:, q:2 * q], t3[:, 0:q], t3[:, 3 * q:4 * q], t3[:, 2 * q:3 * q]], axis=1)
    return t3 * cos_t[None] + rot * sin_signed_t[None]


def _inproj_kernel(rope, row_of_step, x_ref, m_ref, vecs_ref, w_ref, wkt_ref, kgt_ref, gsum_ref,
                   gbcast_ref, wsp_ref, bsp_ref, *rest):
    if rope:
        tab_ref, tab_t_ref, *rest = rest
    qst_out, kt_out, v_out = rest[-3:]
    tm = x_ref.shape[0]
    m = m_ref[pl.ds(row_of_step(pl.program_id(0)), 1), :]
    shift1, scale1 = m[:, 0:D], m[:, D:2 * D]
    n1g, qg, lng, lnb = (vecs_ref[r:r + 1, :] for r in range(4))
    h = (_rms(x_ref[...], n1g) * (1.0 + scale1) + shift1).astype(BF)

    def seg(j):
        return _dot(h, w_ref[:, j * D:(j + 1) * D])

    q = _head_norm(seg(0), qg, gsum_ref[...], gbcast_ref[...])
    if rope:
        q = _rope(q, tab_ref[:, 0:LANES], tab_ref[:, LANES:2 * LANES])
    qst_out[:, 0:D] = q.astype(qst_out.dtype)

    k3 = _dot_nt(wkt_ref[...], h).reshape(D // HEAD_DIM, HEAD_DIM, tm)
    inv = lax.rsqrt(jnp.mean(k3 * k3, axis=1, keepdims=True) + EPS)
    k3 = k3 * inv * kgt_ref[...][None]
    if rope:
        k3 = _rope_t(k3, tab_t_ref[0:HEAD_DIM, :], tab_t_ref[HEAD_DIM:2 * HEAD_DIM, :])
    kt = k3.reshape(D, tm).astype(kt_out.dtype)
    width = kt_out.shape[2]
    for r in range(kt_out.shape[0]):
        kt_out[r] = kt[:, r * width:(r + 1) * width]

    v_out[...] = seg(2).astype(v_out.dtype)
    qst_out[:, D:2 * D] = _sigmoid(seg(5)).astype(qst_out.dtype)

    zv = jax.nn.gelu(seg(4))
    mu = jnp.mean(zv, axis=-1, keepdims=True)
    zc = zv - mu
    var = jnp.mean(zc * zc, axis=-1, keepdims=True)
    zvn = (zc * lax.rsqrt(var + EPS) * lng + lnb).astype(BF)
    pre = jax.nn.gelu(seg(3)) * _sigmoid(seg(6))
    blank = jnp.zeros((CHUNK, LANES), BF)
    for c in range(tm // CHUNK):
        rows = slice(c * CHUNK, (c + 1) * CHUNK)
        for p in range(N_GROUPS // 2):
            cols = slice(2 * p * LANES, (2 * p + 2) * LANES)
            z = zvn[rows, cols]
            z_diag = jnp.concatenate([jnp.concatenate([z[:, :LANES], blank], axis=1),
                                      jnp.concatenate([blank, z[:, LANES:]], axis=1)], axis=0)
            sp = _dot(wsp_ref[p], z_diag) + bsp_ref[:, cols]
            out_cols = slice(2 * D + 2 * p * LANES, 2 * D + (2 * p + 2) * LANES)
            qst_out[rows, out_cols] = (pre[rows, cols] * sp).astype(qst_out.dtype)


INPROJ_ROWS = 512


def _inproj_call(x, m, row_of_req, rope_tabs, kv_dtype, consts, n_req, n_tok, after=None):
    t = x.shape[0]
    tm = INPROJ_ROWS
    per_req = max(1, n_tok // tm)
    per_tile = max(1, tm // n_tok)
    rope = rope_tabs is not None

    def row_of_step(i):
        return row_of_req(i * per_tile // per_req)

    tok = pl.BlockSpec((tm, D), lambda i: (i, 0))
    in_specs = [
        tok,
        _const_spec((MOD_ROWS, 6 * D)),
        _const_spec((MOD_ROWS, D)),
        pl.BlockSpec((D, N_SEG * D), lambda i: (0, 0), pipeline_mode=pl.Buffered(1)),
        _const_spec((D, D)),
        _const_spec((HEAD_DIM, tm)),
        _const_spec((D, LANES)), _const_spec((2 * LANES, D)),
        _const_spec((N_GROUPS // 2, CHUNK, 2 * CHUNK)), _const_spec((CHUNK, D)),
    ]
    args = [x, m, consts["vecs"], consts["w_in"], consts["wkt"], consts["kgt"], consts["gsum"],
            consts["gbcast"], consts["wsp"], consts["bsp"]]
    if rope:
        in_specs += [pl.BlockSpec((tm, 2 * LANES), lambda i: (i % per_req, 0)),
                     pl.BlockSpec((2 * HEAD_DIM, tm), lambda i: (0, i % per_req))]
        args += list(rope_tabs)
    if after is not None:
        in_specs += [pl.BlockSpec(memory_space=pl.ANY)]
        args += [after]
    kt_spec = pl.BlockSpec((per_tile, D, tm // per_tile), lambda i: (i, 0, 0))
    return pl.pallas_call(
        functools.partial(_inproj_kernel, rope, row_of_step),
        grid=(t // tm,),
        in_specs=in_specs,
        out_specs=[pl.BlockSpec((tm, 3 * D), lambda i: (i, 0)), kt_spec, tok],
        out_shape=[
            jax.ShapeDtypeStruct((t, 3 * D), BF),
            jax.ShapeDtypeStruct((n_req * per_req, D, tm // per_tile), kv_dtype),
            jax.ShapeDtypeStruct((t, D), kv_dtype),
        ],
        compiler_params=_params(1),
        name="inproj_rope" if rope else "inproj",
    )(*args)


SCORE_GROUP_ELEMS = 1 << 22
ATTN_ROWS = 256


def _attn_kernel(has_ctx, heads_per_group, row_of_req, qst_ref, kt_ref, v_ref, *rest):
    if has_ctx:
        kct_ref, vc_ref, *rest = rest
    lam_ref, vecs_ref, x_ref, m_ref, wo_ref, wr_ref, x1_out, h2_out, lg_out, merged_s = rest
    tq = qst_ref.shape[0]
    key_blocks, block_keys = kt_ref.shape[0], kt_ref.shape[2]
    n2g = vecs_ref[4:5, :]
    sub_g = vecs_ref[5:6, 0:V_HEAD_DIM]
    lam = lam_ref[0:1, 0:1]
    lane = lax.broadcasted_iota(jnp.int32, (1, LANES), 1)
    first = lane < HEAD_DIM
    zero = jnp.zeros((), BF)
    ones_col = jnp.where(lane == 0, 1.0, 0.0).astype(BF)

    def values(ref, rows, h):
        v = ref[rows, h * V_HEAD_DIM:(h + 1) * V_HEAD_DIM].astype(BF)
        return jnp.concatenate([v, jnp.broadcast_to(ones_col, (v.shape[0], LANES))], axis=1)

    def head_cols(h):
        return slice(h * LANES, (h + 1) * LANES)

    def head_scores(h):
        q = qst_ref[:, head_cols(h)]
        qz = jnp.concatenate([jnp.where(first, q, zero), jnp.where(first, zero, q)], axis=0)
        parts = [_dot(qz, kt_ref[j, head_cols(h), :].astype(BF)) for j in range(key_blocks)]
        if has_ctx:
            parts.append(_dot(qz, kct_ref[0, head_cols(h), :].astype(BF)))
        return parts

    def row_max(scores):
        return functools.reduce(jnp.maximum, [jnp.max(s, axis=-1, keepdims=True) for s in scores])

    def head_pv(h, scores, mx):
        vals = [values(v_ref, slice(j * block_keys, (j + 1) * block_keys), h) for j in range(key_blocks)]
        if has_ctx:
            vals.append(values(vc_ref, slice(None), h))
        return functools.reduce(jnp.add, [_dot(jnp.exp2(s - mx).astype(BF), v) for s, v in zip(scores, vals)])

    def head_finish(h, ob):
        den = ob[:, V_HEAD_DIM:V_HEAD_DIM + 1]
        o = ob[:tq, :V_HEAD_DIM] * (1.0 / den[:tq]) - ob[tq:, :V_HEAD_DIM] * (lam / den[tq:])
        o = _rms(o, sub_g) * (1.0 - LAMBDA_INIT)
        sga = qst_ref[:, D + h * LANES:D + (h + 1) * LANES].astype(F32)
        tb = qst_ref[:, 2 * D + h * LANES:2 * D + (h + 1) * LANES].astype(F32)
        merged_s[:, head_cols(h)] = (sga * o + tb).astype(BF)

    for g0 in range(0, N_HEADS, heads_per_group):
        group = range(g0, g0 + heads_per_group)
        scores = [head_scores(h) for h in group]
        maxes = [row_max(s) for s in scores]
        outs = [head_pv(h, s, mx) for h, s, mx in zip(group, scores, maxes)]
        for h, ob in zip(group, outs):
            head_finish(h, ob)

    m = m_ref[pl.ds(row_of_req(pl.program_id(0)), 1), :]
    gate1, shift2, scale2 = m[:, 2 * D:3 * D], m[:, 3 * D:4 * D], m[:, 4 * D:5 * D]
    x1 = x_ref[...] + gate1 * _dot(merged_s[...], wo_ref[...])
    x1_out[...] = x1
    h2 = _rms(x1, n2g) * (1.0 + scale2) + shift2
    h2_out[...] = h2.astype(BF)
    hi, lo = _split_bf16(h2)
    wr_hi, wr_lo = wr_ref[:, 0:LANES], wr_ref[:, LANES:2 * LANES]
    lg_out[...] = _dot(hi, wr_hi) + _dot(lo, wr_hi) + _dot(hi, wr_lo)


def _attn_call(qst, kt, v, ctx, lamv, x, m, row_of_req, consts, n_req, n_tok):
    tq = min(n_tok, ATTN_ROWS)
    nqb = n_tok // tq
    key_blocks = kt.shape[0] // n_req
    has_ctx = ctx is not None
    tok = pl.BlockSpec((tq, D), lambda b, i: (b * nqb + i, 0))
    in_specs = [pl.BlockSpec((tq, 3 * D), lambda b, i: (b * nqb + i, 0)),
                pl.BlockSpec((key_blocks, D, kt.shape[2]), lambda b, i: (b, 0, 0)),
                pl.BlockSpec((n_tok, D), lambda b, i: (b, 0))]
    args = [qst, kt, v]
    if has_ctx:
        n_ctx = ctx[0].shape[2]
        in_specs += [pl.BlockSpec((1, D, n_ctx), lambda b, i: (b, 0, 0)),
                     pl.BlockSpec((n_ctx, D), lambda b, i: (b, 0))]
        args += list(ctx)
    in_specs += [
        _const_spec((MOD_ROWS, LANES)), _const_spec((MOD_ROWS, D)),
        tok,
        _const_spec((MOD_ROWS, 6 * D)),
        _const_spec((D, D)),
        _const_spec((D, 2 * LANES)),
    ]
    args += [lamv, consts["vecs"], x, m, consts["w_out"], consts["wr"]]
    t = n_req * n_tok
    n_keys = n_tok + (ctx[0].shape[2] if has_ctx else 0)
    heads_per_group = max(1, min(N_HEADS, SCORE_GROUP_ELEMS // (2 * tq * n_keys)))
    while N_HEADS % heads_per_group:
        heads_per_group -= 1
    return pl.pallas_call(
        functools.partial(_attn_kernel, has_ctx, heads_per_group, row_of_req),
        grid=(n_req, nqb),
        in_specs=in_specs,
        out_specs=[tok, tok, pl.BlockSpec((tq, LANES), lambda b, i: (b * nqb + i, 0))],
        out_shape=[
            jax.ShapeDtypeStruct((t, D), F32),
            jax.ShapeDtypeStruct((t, D), BF),
            jax.ShapeDtypeStruct((t, LANES), F32),
        ],
        scratch_shapes=[pltpu.VMEM((tq, D), BF)],
        compiler_params=_params(2),
        name="attn_ctx" if has_ctx else "attn",
    )(*args)


GATHER_ROWS = 512
KEY_BITS = 31
ROUTE_UNROLL_ELEMS = 1 << 22
ROUTE_STEP_TOKENS = 2048
PREFIX_BLOCK = 128


def _route_kernel(cap, n_tok, unroll, lg_ref, h2_ref, before_ref, xe_ref, gate_ref, slot_ref,
                  aff_t_s, slot_t_s, p_s):
    reqs = range(lg_ref.shape[0] // n_tok)
    lane = lax.broadcasted_iota(jnp.int32, (1, LANES), 1)
    valid = lane < N_EXPERTS

    def rows(r):
        return slice(r * n_tok, (r + 1) * n_tok)

    def affinity(r):
        lg = jnp.where(valid, lg_ref[rows(r), :], -1e30)
        ex = jnp.where(valid, jnp.exp(lg - jnp.max(lg, axis=-1, keepdims=True)), 0.0)
        return ex / jnp.sum(ex, axis=-1, keepdims=True)

    def count(mask):
        return jnp.sum(jnp.where(mask, 1.0, 0.0), axis=0, keepdims=True)

    affs = [affinity(r) for r in reqs]
    kth_bits = [jnp.zeros((1, LANES), jnp.int32) for _ in reqs]
    for bit in range(KEY_BITS - 1, -1, -1):
        for r in reqs:
            cand = kth_bits[r] | (1 << bit)
            enough = count(affs[r] >= lax.bitcast_convert_type(cand, F32)) >= cap
            kth_bits[r] = jnp.where(enough, cand, kth_bits[r])

    before = before_ref[...]

    def earlier_count(flags):
        blocks, running = [], jnp.zeros((1, LANES), F32)
        for b in range(n_tok // PREFIX_BLOCK):
            blk = flags[b * PREFIX_BLOCK:(b + 1) * PREFIX_BLOCK]
            blocks.append(_dot(before, blk.astype(BF)) + running)
            running = running + jnp.sum(blk, axis=0, keepdims=True)
        return jnp.concatenate(blocks, axis=0)

    for r in reqs:
        aff = affs[r]
        kth = lax.bitcast_convert_type(kth_bits[r], F32)
        above = aff > kth
        tied = aff == kth
        need = cap - count(above)
        tied_before = earlier_count(jnp.where(tied, 1.0, 0.0))
        chosen = jnp.where(above, 1.0, jnp.where(tied, jnp.where(tied_before < need, 1.0, 0.0), 0.0))
        slot = earlier_count(chosen)
        slot = jnp.where(valid, jnp.where(chosen > 0.0, slot, float(cap)), float(cap))
        slot_ref[rows(r), :] = slot
        slot_t_s[r] = slot.T
        aff_t_s[r] = aff.T

    slot_ids = lax.broadcasted_iota(jnp.int32, (cap, 1), 0).astype(F32)

    def per_expert(e, carry):
        for r in reqs:
            hit = slot_t_s[r, pl.ds(e, 1), :] == slot_ids
            p_s[r, pl.ds(pl.multiple_of(e * cap, cap), cap), :] = jnp.where(hit, 1.0, 0.0).astype(BF)
            gate = jnp.sum(jnp.where(hit, aff_t_s[r, pl.ds(e, 1), :], 0.0), axis=-1, keepdims=True)
            gate_ref[e, r * cap:(r + 1) * cap, :] = gate
        return carry

    lax.fori_loop(0, N_EXPERTS, per_expert, 0, unroll=unroll)
    n_rows = min(GATHER_ROWS, N_EXPERTS * cap)
    e_per = n_rows // cap
    for r in reqs:
        h2 = h2_ref[rows(r), :]
        for t in range(N_EXPERTS * cap // n_rows):
            xe = _dot(p_s[r, t * n_rows:(t + 1) * n_rows, :], h2).astype(BF)
            xe_ref[t * e_per:(t + 1) * e_per, r * cap:(r + 1) * cap, :] = xe.reshape(e_per, cap, D)


def _capacity(n_tok):
    return max(1, CAPACITY_FACTOR * n_tok // N_EXPERTS)


def _route_call(lg, h2, n_req, n_tok):
    cap = _capacity(n_tok)
    per_step = max(1, min(n_req, ROUTE_STEP_TOKENS // n_tok))
    before = jnp.asarray((np.arange(PREFIX_BLOCK)[None, :] < np.arange(PREFIX_BLOCK)[:, None]).astype(BF))
    unroll = max(1, min(N_EXPERTS, ROUTE_UNROLL_ELEMS // (per_step * cap * n_tok)))
    return pl.pallas_call(
        functools.partial(_route_kernel, cap, n_tok, unroll),
        grid=(n_req // per_step,),
        in_specs=[
            pl.BlockSpec((per_step * n_tok, LANES), lambda b: (b, 0)),
            pl.BlockSpec((per_step * n_tok, D), lambda b: (b, 0)),
            _const_spec((PREFIX_BLOCK, PREFIX_BLOCK)),
        ],
        out_specs=[
            pl.BlockSpec((N_EXPERTS, per_step * cap, D), lambda b: (0, b, 0)),
            pl.BlockSpec((N_EXPERTS, per_step * cap, 1), lambda b: (0, b, 0)),
            pl.BlockSpec((per_step * n_tok, LANES), lambda b: (b, 0)),
        ],
        out_shape=[
            jax.ShapeDtypeStruct((N_EXPERTS, n_req * cap, D), BF),
            jax.ShapeDtypeStruct((N_EXPERTS, n_req * cap, 1), F32),
            jax.ShapeDtypeStruct((n_req * n_tok, LANES), F32),
        ],
        scratch_shapes=[
            pltpu.VMEM((per_step, LANES, n_tok), F32),
            pltpu.VMEM((per_step, LANES, n_tok), F32),
            pltpu.VMEM((per_step, N_EXPERTS * cap, n_tok), BF),
        ],
        compiler_params=_params(1),
        name="route",
    )(lg, h2, before)


EXPERT_BLOCK = 1024
EXPERT_SUB = 256


def _experts_kernel(xa_ref, xb_ref, ga_ref, gb_ref, wg_ref, wu_ref, wd_ref, ya_ref, yb_ref, acc_s):
    f = pl.program_id(1)
    ra = xa_ref.shape[1]
    x = jnp.concatenate([xa_ref[0], xb_ref[0]], axis=0)
    for c in range(wg_ref.shape[2] // EXPERT_SUB):
        cs = slice(c * EXPERT_SUB, (c + 1) * EXPERT_SUB)
        gate = _dot(x, wg_ref[0, :, cs].astype(BF))
        up = _dot(x, wu_ref[0, :, cs].astype(BF))
        hid = (gate * _sigmoid(gate) * up).astype(BF)
        down = _dot(hid, wd_ref[0, cs, :].astype(BF))
        if c == 0:
            acc_s[...] = jnp.where(f == 0, 0.0, acc_s[...]) + down
        else:
            acc_s[...] += down

    @pl.when(f == pl.num_programs(1) - 1)
    def _():
        ya_ref[0] = (acc_s[0:ra, :] * ga_ref[0]).astype(ya_ref.dtype)
        yb_ref[0] = (acc_s[ra:, :] * gb_ref[0]).astype(yb_ref.dtype)


def _experts_call(xa, xb, ga, gb, wg, wu, wd):
    ra, rb = xa.shape[1], xb.shape[1]
    tf = EXPERT_BLOCK
    xa_spec = pl.BlockSpec((1, ra, D), lambda e, f: (e, 0, 0))
    xb_spec = pl.BlockSpec((1, rb, D), lambda e, f: (e, 0, 0))
    return pl.pallas_call(
        _experts_kernel,
        grid=(N_EXPERTS, D_EXPERT // tf),
        in_specs=[
            xa_spec, xb_spec,
            pl.BlockSpec((1, ra, 1), lambda e, f: (e, 0, 0)),
            pl.BlockSpec((1, rb, 1), lambda e, f: (e, 0, 0)),
            pl.BlockSpec((1, D, tf), lambda e, f: (e, 0, f)),
            pl.BlockSpec((1, D, tf), lambda e, f: (e, 0, f)),
            pl.BlockSpec((1, tf, D), lambda e, f: (e, f, 0)),
        ],
        out_specs=[xa_spec, xb_spec],
        out_shape=[
            jax.ShapeDtypeStruct((N_EXPERTS, ra, D), BF),
            jax.ShapeDtypeStruct((N_EXPERTS, rb, D), BF),
        ],
        scratch_shapes=[pltpu.VMEM((ra + rb, D), F32)],
        compiler_params=_params(2),
        name="experts",
    )(xa, xb, ga, gb, wg, wu, wd)


SCATTER_STEP_TOKENS = 1024


def _scatter_kernel(cap, n_tok, row_of_req, slot_ref, y_ref, x1_ref, m_ref, expand_ref, out_ref):
    per_step = slot_ref.shape[0] // n_tok
    slot_ids = (lax.broadcasted_iota(jnp.int32, (1, N_EXPERTS * cap), 1) % cap).astype(F32)
    for r in range(per_step):
        rows = slice(r * n_tok, (r + 1) * n_tok)
        m = m_ref[pl.ds(row_of_req(pl.program_id(0) * per_step + r), 1), :]
        gate2 = m[:, 5 * D:6 * D]
        slot_wide = _dot(slot_ref[rows, :].astype(BF), expand_ref[...])
        onehot = jnp.where(slot_wide == slot_ids, 1.0, 0.0).astype(BF)
        y = y_ref[:, r * cap:(r + 1) * cap, :].reshape(N_EXPERTS * cap, D)
        out_ref[rows, :] = x1_ref[rows, :] + gate2 * _dot(onehot, y)


def _scatter_call(slots, y, x1, m, row_of_req, n_req, n_tok):
    cap = _capacity(n_tok)
    per_step = max(1, min(n_req, SCATTER_STEP_TOKENS // n_tok))
    expand = jnp.asarray((np.arange(LANES)[:, None] == np.arange(N_EXPERTS * cap)[None, :] // cap).astype(BF))
    tok = pl.BlockSpec((per_step * n_tok, D), lambda b: (b, 0))
    return pl.pallas_call(
        functools.partial(_scatter_kernel, cap, n_tok, row_of_req),
        grid=(n_req // per_step,),
        in_specs=[
            pl.BlockSpec((per_step * n_tok, LANES), lambda b: (b, 0)),
            pl.BlockSpec((N_EXPERTS, per_step * cap, D), lambda b: (0, b, 0)),
            tok,
            _const_spec((MOD_ROWS, 6 * D)),
            _const_spec((LANES, N_EXPERTS * cap)),
        ],
        out_specs=tok,
        out_shape=jax.ShapeDtypeStruct((n_req * n_tok, D), F32),
        compiler_params=_params(1),
        name="scatter",
    )(slots, y, x1, m, expand)


def _rope_tables(n_tokens):
    rows = n_tokens // GRID_W
    row = np.broadcast_to(np.arange(rows, dtype=np.float32)[:, None], (rows, GRID_W)).reshape(-1)
    col = np.broadcast_to(np.arange(GRID_W, dtype=np.float32)[None, :], (rows, GRID_W)).reshape(-1)
    half = HEAD_DIM // 4
    inv_freq = (np.float32(ROPE_BASE) ** (-np.arange(half, dtype=np.float32) / np.float32(half))).astype(np.float32)
    ar = row[:, None] * inv_freq
    ac = col[:, None] * inv_freq
    ang = np.concatenate([ar, ar, ac, ac], axis=-1).astype(np.float64)
    cos, sin = np.cos(ang).astype(np.float32), np.sin(ang).astype(np.float32)
    first_half = (np.arange(HEAD_DIM) % (HEAD_DIM // 2)) < (HEAD_DIM // 4)
    sin_signed = np.where(first_half[None, :], -sin, sin)
    reps = LANES // HEAD_DIM
    token_major = np.concatenate([np.tile(cos, (1, reps)), np.tile(sin_signed, (1, reps))], axis=1)
    transposed = np.concatenate([cos.T, sin_signed.T], axis=0)
    return jnp.asarray(token_major), jnp.asarray(transposed)


def kernel(x_prompt, x_sample, cache_k, cache_v, c, c_ctx, w_ada, b_ada, norm1_g, norm2_g, w_in, q_norm_g, k_norm_g, lambda_q1, lambda_k1, lambda_q2, lambda_k2, subln_g, gmlp_ln_g, gmlp_ln_b, w_spatial, b_spatial, w_out, w_router, w_gate_e, w_up_e, w_down_e):
    n_p, t_p = x_prompt.shape[0], x_prompt.shape[1]
    n_s, t_s = x_sample.shape[0], x_sample.shape[1]
    n_ctx = cache_k.shape[2]
    l = 0

    m, lamv, w_in_bf, wkt, w_out_bf, wsp_pairs = _prep_call(
        c_ctx[None, :], c, w_ada[l], b_ada[l][None, :], lambda_q1[l][None, :], lambda_k1[l][None, :],
        lambda_q2[l][None, :], lambda_k2[l][None, :], w_in[l], w_out[l], w_spatial[l])

    gsum_np = (np.arange(D)[:, None] // HEAD_DIM == np.arange(LANES)[None, :]).astype(BF)
    gsum = jnp.asarray(gsum_np)
    wr = jnp.pad(w_router[l], ((0, 0), (0, LANES - N_EXPERTS)))
    wr_hi = wr.astype(BF)
    qg = jnp.tile(q_norm_g[l] * (HEAD_DIM ** -0.5 * math.log2(math.e)), D // HEAD_DIM)

    def rows_of(*vectors):
        rows = [jnp.tile(vec, D // vec.shape[0])[None, :] for vec in vectors]
        return jnp.concatenate(rows + [jnp.zeros((MOD_ROWS - len(rows), D), F32)], axis=0)

    consts = {
        "vecs": rows_of(norm1_g[l], qg, gmlp_ln_g[l], gmlp_ln_b[l], norm2_g[l], subln_g[l]),
        "w_in": w_in_bf,
        "wkt": wkt,
        "kgt": jnp.broadcast_to(k_norm_g[l][:, None], (HEAD_DIM, INPROJ_ROWS)),
        "gsum": gsum,
        "gbcast": jnp.asarray(np.concatenate([gsum_np.T, gsum_np.T], axis=0)),
        "wsp": wsp_pairs,
        "bsp": jnp.repeat(b_spatial[l].T, D // N_GROUPS, axis=1),
        "w_out": w_out_bf,
        "wr": jnp.concatenate([wr_hi, (wr - wr_hi.astype(F32)).astype(BF)], axis=1),
    }

    passes = (
        (x_prompt, n_p, t_p, lambda b: 0, False),
        (x_sample, n_s, t_s, lambda b: 1 + b, True),
    )
    outs = []
    kv_out = None
    lg = None
    for x, n_req, n_tok, row_of_req, is_sample in passes:
        xf = x.reshape(n_req * n_tok, D)
        tabs = _rope_tables(n_tok) if is_sample else None
        qst, kt, v = _inproj_call(xf, m, row_of_req, tabs, BF if is_sample else F32, consts, n_req, n_tok,
                                  after=lg)
        if is_sample:
            ctx = (jnp.transpose(cache_k[:, l], (0, 2, 3, 4, 1)).reshape(n_req, D, n_ctx),
                   cache_v[:, l].reshape(n_req * n_ctx, D))
        else:
            ctx = None
            kv_out = (kt, v)
        x1, h2, lg = _attn_call(qst, kt, v, ctx, lamv, xf, m, row_of_req, consts, n_req, n_tok)
        xe, gates, slots = _route_call(lg, h2, n_req, n_tok)
        outs.append((x1, xe, gates, slots, row_of_req))

    kv_out = (kv_out[0], lax.optimization_barrier((kv_out[1], lg))[0])
    (x1p, xep, gp, slots_p, row_p), (x1s, xes, gs, slots_s, row_s) = outs
    yp, ys = _experts_call(xep, xes, gp, gs, w_gate_e[l], w_up_e[l], w_down_e[l])
    y_prompt = _scatter_call(slots_p, yp, x1p, m, row_p, n_p, t_p).reshape(x_prompt.shape)
    y_sample = _scatter_call(slots_s, ys, x1s, m, row_s, n_s, t_s).reshape(x_sample.shape)
    new_k = jnp.transpose(kv_out[0].reshape(n_p, N_HEADS, 2, HEAD_DIM, t_p), (0, 4, 1, 2, 3))
    new_k = new_k.reshape(n_p, 1, t_p, N_HEADS, 2, HEAD_DIM)
    new_v = kv_out[1].reshape(n_p, 1, t_p, N_HEADS, V_HEAD_DIM)
    return (y_prompt, y_sample, new_k, new_v)
```

```python
import functools
import math

import jax
import jax.numpy as jnp
import numpy as np
from jax import lax
from jax.experimental import pallas as pl
from jax.experimental.pallas import tpu as pltpu

D = 1024
N_HEADS = 8
HEAD_DIM = 64
V_HEAD_DIM = 128
GRID_W = 64
ROPE_BASE = 10000.0
CHUNK = 128
N_GROUPS = 8
N_EXPERTS = 16
CAPACITY_FACTOR = 2
D_EXPERT = 2048
N_SEG = 7
K_SEG = 1
EPS = 1e-6
LAMBDA_INIT = 0.8 - 0.6 * math.exp(-0.3 * 0)

LANES = 128
MOD_ROWS = 8
VMEM_LIMIT = 56 * 1024 * 1024

BF = jnp.bfloat16
F32 = jnp.float32


def _dot(a, b):
    return jnp.dot(a, b, preferred_element_type=F32)


def _dot_nt(a, b):
    return lax.dot_general(a, b, (((1,), (1,)), ((), ())), preferred_element_type=F32)


def _split_bf16(x):
    hi = x.astype(BF)
    lo = (x - hi.astype(F32)).astype(BF)
    return hi, lo


def _sigmoid(x):
    return 0.5 * jnp.tanh(0.5 * x) + 0.5


def _rms(x, g):
    return x * lax.rsqrt(jnp.mean(x * x, axis=-1, keepdims=True) + EPS) * g


def _params(n_grid_dims):
    return pltpu.CompilerParams(
        dimension_semantics=("arbitrary",) * n_grid_dims, vmem_limit_bytes=VMEM_LIMIT)


def _const_spec(shape):
    nd = len(shape)
    return pl.BlockSpec(shape, lambda *_: (0,) * nd)


N_MOD = 6


def _prep_kernel(cctx_ref, c_ref, wa_ref, ba_ref, lq1_ref, lk1_ref, lq2_ref, lk2_ref, wi_ref, wo_ref, wsp_ref,
                 m_ref, lam_ref, wib_ref, wkt_ref, wob_ref, wsp2_ref, cond_s):
    j = pl.program_id(0)
    n_lat = c_ref.shape[0]

    @pl.when(j == 0)
    def _():
        cond_s[...] = jnp.zeros(cond_s.shape, F32)
        cond_s[0:1, :] = cctx_ref[...]
        cond_s[1:1 + n_lat, :] = c_ref[...]
        s1 = jnp.sum(lq1_ref[...] * lk1_ref[...], axis=-1, keepdims=True)
        s2 = jnp.sum(lq2_ref[...] * lk2_ref[...], axis=-1, keepdims=True)
        lam_ref[...] = jnp.broadcast_to(jnp.exp(s1) - jnp.exp(s2) + LAMBDA_INIT, lam_ref.shape)
        wob_ref[...] = wo_ref[...].astype(BF)
        for p in range(N_GROUPS // 2):
            wsp2_ref[p] = jnp.concatenate([wsp_ref[2 * p], wsp_ref[2 * p + 1]], axis=1).astype(BF)

    @pl.when(j < N_MOD)
    def _():
        c = cond_s[...]
        a_hi, a_lo = _split_bf16(c * jax.nn.sigmoid(c))
        w_hi, w_lo = _split_bf16(wa_ref[...])
        m_ref[...] = _dot(a_hi, w_hi) + _dot(a_lo, w_hi) + _dot(a_hi, w_lo) + ba_ref[...]

    w = wi_ref[...]
    wib_ref[...] = w.astype(BF)

    @pl.when(j == K_SEG)
    def _():
        wkt_ref[...] = w.T.astype(BF)


def _prep_call(c_ctx, c, w_ada, b_ada, lq1, lk1, lq2, lk2, w_in, w_out, w_spatial):
    vec = _const_spec((1, HEAD_DIM))
    mod_block = lambda j: (0, jnp.minimum(j, N_MOD - 1))
    pairs = (N_GROUPS // 2, CHUNK, 2 * CHUNK)
    return pl.pallas_call(
        _prep_kernel,
        grid=(N_SEG,),
        in_specs=[
            _const_spec((1, D)), _const_spec(c.shape),
            pl.BlockSpec((D, D), mod_block),
            pl.BlockSpec((1, D), mod_block),
            vec, vec, vec, vec,
            pl.BlockSpec((D, D), lambda j: (0, j)),
            _const_spec((D, D)),
            _const_spec((N_GROUPS, CHUNK, CHUNK)),
        ],
        out_specs=[
            pl.BlockSpec((MOD_ROWS, D), mod_block),
            _const_spec((MOD_ROWS, LANES)),
            pl.BlockSpec((D, D), lambda j: (0, j)),
            _const_spec((D, D)),
            _const_spec((D, D)),
            _const_spec(pairs),
        ],
        out_shape=[
            jax.ShapeDtypeStruct((MOD_ROWS, N_MOD * D), F32),
            jax.ShapeDtypeStruct((MOD_ROWS, LANES), F32),
            jax.ShapeDtypeStruct((D, N_SEG * D), BF),
            jax.ShapeDtypeStruct((D, D), BF),
            jax.ShapeDtypeStruct((D, D), BF),
            jax.ShapeDtypeStruct(pairs, BF),
        ],
        scratch_shapes=[pltpu.VMEM((MOD_ROWS, D), F32)],
        compiler_params=_params(1),
        name="prep",
    )(c_ctx, c, w_ada, b_ada, lq1, lk1, lq2, lk2, w_in, w_out, w_spatial)


def _head_norm(t, g, gsum, gbcast):
    ss = _dot((t * t).astype(BF), gsum)
    inv = lax.rsqrt(ss * (1.0 / HEAD_DIM) + EPS)
    hi, lo = _split_bf16(inv)
    bc = _dot(jnp.concatenate([hi, lo], axis=-1), gbcast)
    return t * bc * g


def _lane_tile(tab):
    return jnp.concatenate([tab] * (D // LANES), axis=1)


def _rope(t, cos, sin_signed):
    q = HEAD_DIM // 4
    lane = lax.broadcasted_iota(jnp.int32, (1, D), 1)
    first_half = (lane % (HEAD_DIM // 2)) < q
    fwd = pltpu.roll(t, D - q, axis=1)
    bwd = pltpu.roll(t, q, axis=1)
    return t * _lane_tile(cos) + jnp.where(first_half, fwd, bwd) * _lane_tile(sin_signed)


def _rope_t(t3, cos_t, sin_signed_t):
    q = HEAD_DIM // 4
    rot = jnp.concatenate([t3[:, q:2 * q], t3[:, 0:q], t3[:, 3 * q:4 * q], t3[:, 2 * q:3 * q]], axis=1)
    return t3 * cos_t[None] + rot * sin_signed_t[None]


def _inproj_kernel(rope, row_of_step, x_ref, m_ref, vecs_ref, w_ref, wkt_ref, kgt_ref, gsum_ref,
                   gbcast_ref, wsp_ref, bsp_ref, *rest):
    if rope:
        tab_ref, tab_t_ref, *rest = rest
    qst_out, kt_out, v_out = rest[-3:]
    tm = x_ref.shape[0]
    m = m_ref[pl.ds(row_of_step(pl.program_id(0)), 1), :]
    shift1, scale1 = m[:, 0:D], m[:, D:2 * D]
    n1g, qg, lng, lnb = (vecs_ref[r:r + 1, :] for r in range(4))
    h = (_rms(x_ref[...], n1g) * (1.0 + scale1) + shift1).astype(BF)

    def seg(j):
        return _dot(h, w_ref[:, j * D:(j + 1) * D])

    q = _head_norm(seg(0), qg, gsum_ref[...], gbcast_ref[...])
    if rope:
        q = _rope(q, tab_ref[:, 0:LANES], tab_ref[:, LANES:2 * LANES])
    qst_out[:, 0:D] = q.astype(qst_out.dtype)

    k3 = _dot_nt(wkt_ref[...], h).reshape(D // HEAD_DIM, HEAD_DIM, tm)
    inv = lax.rsqrt(jnp.mean(k3 * k3, axis=1, keepdims=True) + EPS)
    k3 = k3 * inv * kgt_ref[...][None]
    if rope:
        k3 = _rope_t(k3, tab_t_ref[0:HEAD_DIM, :], tab_t_ref[HEAD_DIM:2 * HEAD_DIM, :])
    kt = k3.reshape(D, tm).astype(kt_out.dtype)
    width = kt_out.shape[2]
    for r in range(kt_out.shape[0]):
        kt_out[r] = kt[:, r * width:(r + 1) * width]

    v_out[...] = seg(2).astype(v_out.dtype)
    qst_out[:, D:2 * D] = _sigmoid(seg(5)).astype(qst_out.dtype)

    zv = jax.nn.gelu(seg(4))
    mu = jnp.mean(zv, axis=-1, keepdims=True)
    zc = zv - mu
    var = jnp.mean(zc * zc, axis=-1, keepdims=True)
    zvn = (zc * lax.rsqrt(var + EPS) * lng + lnb).astype(BF)
    pre = jax.nn.gelu(seg(3)) * _sigmoid(seg(6))
    blank = jnp.zeros((CHUNK, LANES), BF)
    for c in range(tm // CHUNK):
        rows = slice(c * CHUNK, (c + 1) * CHUNK)
        for p in range(N_GROUPS // 2):
            cols = slice(2 * p * LANES, (2 * p + 2) * LANES)
            z = zvn[rows, cols]
            z_diag = jnp.concatenate([jnp.concatenate([z[:, :LANES], blank], axis=1),
                                      jnp.concatenate([blank, z[:, LANES:]], axis=1)], axis=0)
            sp = _dot(wsp_ref[p], z_diag) + bsp_ref[:, cols]
            out_cols = slice(2 * D + 2 * p * LANES, 2 * D + (2 * p + 2) * LANES)
            qst_out[rows, out_cols] = (pre[rows, cols] * sp).astype(qst_out.dtype)


INPROJ_ROWS = 512


def _inproj_call(x, m, row_of_req, rope_tabs, kv_dtype, consts, n_req, n_tok, after=None):
    t = x.shape[0]
    tm = INPROJ_ROWS
    per_req = max(1, n_tok // tm)
    per_tile = max(1, tm // n_tok)
    rope = rope_tabs is not None

    def row_of_step(i):
        return row_of_req(i * per_tile // per_req)

    tok = pl.BlockSpec((tm, D), lambda i: (i, 0))
    in_specs = [
        tok,
        _const_spec((MOD_ROWS, 6 * D)),
        _const_spec((MOD_ROWS, D)),
        pl.BlockSpec((D, N_SEG * D), lambda i: (0, 0), pipeline_mode=pl.Buffered(1)),
        _const_spec((D, D)),
        _const_spec((HEAD_DIM, tm)),
        _const_spec((D, LANES)), _const_spec((2 * LANES, D)),
        _const_spec((N_GROUPS // 2, CHUNK, 2 * CHUNK)), _const_spec((CHUNK, D)),
    ]
    args = [x, m, consts["vecs"], consts["w_in"], consts["wkt"], consts["kgt"], consts["gsum"],
            consts["gbcast"], consts["wsp"], consts["bsp"]]
    if rope:
        in_specs += [pl.BlockSpec((tm, 2 * LANES), lambda i: (i % per_req, 0)),
                     pl.BlockSpec((2 * HEAD_DIM, tm), lambda i: (0, i % per_req))]
        args += list(rope_tabs)
    if after is not None:
        in_specs += [pl.BlockSpec(memory_space=pl.ANY)]
        args += [after]
    kt_spec = pl.BlockSpec((per_tile, D, tm // per_tile), lambda i: (i, 0, 0))
    return pl.pallas_call(
        functools.partial(_inproj_kernel, rope, row_of_step),
        grid=(t // tm,),
        in_specs=in_specs,
        out_specs=[pl.BlockSpec((tm, 3 * D), lambda i: (i, 0)), kt_spec, tok],
        out_shape=[
            jax.ShapeDtypeStruct((t, 3 * D), BF),
            jax.ShapeDtypeStruct((n_req * per_req, D, tm // per_tile), kv_dtype),
            jax.ShapeDtypeStruct((t, D), kv_dtype),
        ],
        compiler_params=_params(1),
        name="inproj_rope" if rope else "inproj",
    )(*args)


SCORE_GROUP_ELEMS = 1 << 22
ATTN_ROWS = 256


def _attn_kernel(has_ctx, heads_per_group, row_of_req, qst_ref, kt_ref, v_ref, *rest):
    if has_ctx:
        kct_ref, vc_ref, *rest = rest
    lam_ref, vecs_ref, x_ref, m_ref, wo_ref, wr_ref, x1_out, h2_out, lg_out, merged_s = rest
    tq = qst_ref.shape[0]
    key_blocks, block_keys = kt_ref.shape[0], kt_ref.shape[2]
    n2g = vecs_ref[4:5, :]
    sub_g = vecs_ref[5:6, 0:V_HEAD_DIM]
    lam = lam_ref[0:1, 0:1]
    lane = lax.broadcasted_iota(jnp.int32, (1, LANES), 1)
    first = lane < HEAD_DIM
    zero = jnp.zeros((), BF)
    ones_col = jnp.where(lane == 0, 1.0, 0.0).astype(BF)

    def values(ref, rows, h):
        v = ref[rows, h * V_HEAD_DIM:(h + 1) * V_HEAD_DIM].astype(BF)
        return jnp.concatenate([v, jnp.broadcast_to(ones_col, (v.shape[0], LANES))], axis=1)

    def head_cols(h):
        return slice(h * LANES, (h + 1) * LANES)

    def head_scores(h):
        q = qst_ref[:, head_cols(h)]
        qz = jnp.concatenate([jnp.where(first, q, zero), jnp.where(first, zero, q)], axis=0)
        parts = [_dot(qz, kt_ref[j, head_cols(h), :].astype(BF)) for j in range(key_blocks)]
        if has_ctx:
            parts.append(_dot(qz, kct_ref[0, head_cols(h), :].astype(BF)))
        return parts

    def row_max(scores):
        return functools.reduce(jnp.maximum, [jnp.max(s, axis=-1, keepdims=True) for s in scores])

    def head_pv(h, scores, mx):
        vals = [values(v_ref, slice(j * block_keys, (j + 1) * block_keys), h) for j in range(key_blocks)]
        if has_ctx:
            vals.append(values(vc_ref, slice(None), h))
        return functools.reduce(jnp.add, [_dot(jnp.exp2(s - mx).astype(BF), v) for s, v in zip(scores, vals)])

    def head_finish(h, ob):
        den = ob[:, V_HEAD_DIM:V_HEAD_DIM + 1]
        o = ob[:tq, :V_HEAD_DIM] * (1.0 / den[:tq]) - ob[tq:, :V_HEAD_DIM] * (lam / den[tq:])
        o = _rms(o, sub_g) * (1.0 - LAMBDA_INIT)
        sga = qst_ref[:, D + h * LANES:D + (h + 1) * LANES].astype(F32)
        tb = qst_ref[:, 2 * D + h * LANES:2 * D + (h + 1) * LANES].astype(F32)
        merged_s[:, head_cols(h)] = (sga * o + tb).astype(BF)

    for g0 in range(0, N_HEADS, heads_per_group):
        group = range(g0, g0 + heads_per_group)
        scores = [head_scores(h) for h in group]
        maxes = [row_max(s) for s in scores]
        outs = [head_pv(h, s, mx) for h, s, mx in zip(group, scores, maxes)]
        for h, ob in zip(group, outs):
            head_finish(h, ob)

    m = m_ref[pl.ds(row_of_req(pl.program_id(0)), 1), :]
    gate1, shift2, scale2 = m[:, 2 * D:3 * D], m[:, 3 * D:4 * D], m[:, 4 * D:5 * D]
    x1 = x_ref[...] + gate1 * _dot(merged_s[...], wo_ref[...])
    x1_out[...] = x1
    h2 = _rms(x1, n2g) * (1.0 + scale2) + shift2
    h2_out[...] = h2.astype(BF)
    hi, lo = _split_bf16(h2)
    wr_hi, wr_lo = wr_ref[:, 0:LANES], wr_ref[:, LANES:2 * LANES]
    lg_out[...] = _dot(hi, wr_hi) + _dot(lo, wr_hi) + _dot(hi, wr_lo)


def _attn_call(qst, kt, v, ctx, lamv, x, m, row_of_req, consts, n_req, n_tok):
    tq = min(n_tok, ATTN_ROWS)
    nqb = n_tok // tq
    key_blocks = kt.shape[0] // n_req
    has_ctx = ctx is not None
    tok = pl.BlockSpec((tq, D), lambda b, i: (b * nqb + i, 0))
    in_specs = [pl.BlockSpec((tq, 3 * D), lambda b, i: (b * nqb + i, 0)),
                pl.BlockSpec((key_blocks, D, kt.shape[2]), lambda b, i: (b, 0, 0)),
                pl.BlockSpec((n_tok, D), lambda b, i: (b, 0))]
    args = [qst, kt, v]
    if has_ctx:
        n_ctx = ctx[0].shape[2]
        in_specs += [pl.BlockSpec((1, D, n_ctx), lambda b, i: (b, 0, 0)),
                     pl.BlockSpec((n_ctx, D), lambda b, i: (b, 0))]
        args += list(ctx)
    in_specs += [
        _const_spec((MOD_ROWS, LANES)), _const_spec((MOD_ROWS, D)),
        tok,
        _const_spec((MOD_ROWS, 6 * D)),
        _const_spec((D, D)),
        _const_spec((D, 2 * LANES)),
    ]
    args += [lamv, consts["vecs"], x, m, consts["w_out"], consts["wr"]]
    t = n_req * n_tok
    n_keys = n_tok + (ctx[0].shape[2] if has_ctx else 0)
    heads_per_group = max(1, min(N_HEADS, SCORE_GROUP_ELEMS // (2 * tq * n_keys)))
    while N_HEADS % heads_per_group:
        heads_per_group -= 1
    return pl.pallas_call(
        functools.partial(_attn_kernel, has_ctx, heads_per_group, row_of_req),
        grid=(n_req, nqb),
        in_specs=in_specs,
        out_specs=[tok, tok, pl.BlockSpec((tq, LANES), lambda b, i: (b * nqb + i, 0))],
        out_shape=[
            jax.ShapeDtypeStruct((t, D), F32),
            jax.ShapeDtypeStruct((t, D), BF),
            jax.ShapeDtypeStruct((t, LANES), F32),
        ],
        scratch_shapes=[pltpu.VMEM((tq, D), BF)],
        compiler_params=_params(2),
        name="attn_ctx" if has_ctx else "attn",
    )(*args)


GATHER_ROWS = 512
KEY_BITS = 31
ROUTE_UNROLL_ELEMS = 1 << 22
ROUTE_STEP_TOKENS = 2048
PREFIX_BLOCK = 128


def _route_kernel(cap, n_tok, unroll, lg_ref, h2_ref, before_ref, xe_ref, gate_ref, slot_ref,
                  aff_t_s, slot_t_s, p_s):
    reqs = range(lg_ref.shape[0] // n_tok)
    lane = lax.broadcasted_iota(jnp.int32, (1, LANES), 1)
    valid = lane < N_EXPERTS

    def rows(r):
        return slice(r * n_tok, (r + 1) * n_tok)

    def affinity(r):
        lg = jnp.where(valid, lg_ref[rows(r), :], -1e30)
        ex = jnp.where(valid, jnp.exp(lg - jnp.max(lg, axis=-1, keepdims=True)), 0.0)
        return ex / jnp.sum(ex, axis=-1, keepdims=True)

    def count(mask):
        return jnp.sum(jnp.where(mask, 1.0, 0.0), axis=0, keepdims=True)

    affs = [affinity(r) for r in reqs]
    kth_bits = [jnp.zeros((1, LANES), jnp.int32) for _ in reqs]
    for bit in range(KEY_BITS - 1, -1, -1):
        for r in reqs:
            cand = kth_bits[r] | (1 << bit)
            enough = count(affs[r] >= lax.bitcast_convert_type(cand, F32)) >= cap
            kth_bits[r] = jnp.where(enough, cand, kth_bits[r])

    before = before_ref[...]

    def earlier_count(flags):
        blocks, running = [], jnp.zeros((1, LANES), F32)
        for b in range(n_tok // PREFIX_BLOCK):
            blk = flags[b * PREFIX_BLOCK:(b + 1) * PREFIX_BLOCK]
            blocks.append(_dot(before, blk.astype(BF)) + running)
            running = running + jnp.sum(blk, axis=0, keepdims=True)
        return jnp.concatenate(blocks, axis=0)

    for r in reqs:
        aff = affs[r]
        kth = lax.bitcast_convert_type(kth_bits[r], F32)
        above = aff > kth
        tied = aff == kth
        need = cap - count(above)
        tied_before = earlier_count(jnp.where(tied, 1.0, 0.0))
        chosen = jnp.where(above, 1.0, jnp.where(tied, jnp.where(tied_before < need, 1.0, 0.0), 0.0))
        slot = earlier_count(chosen)
        slot = jnp.where(valid, jnp.where(chosen > 0.0, slot, float(cap)), float(cap))
        slot_ref[rows(r), :] = slot
        slot_t_s[r] = slot.T
        aff_t_s[r] = aff.T

    slot_ids = lax.broadcasted_iota(jnp.int32, (cap, 1), 0).astype(F32)

    def per_expert(e, carry):
        for r in reqs:
            hit = slot_t_s[r, pl.ds(e, 1), :] == slot_ids
            p_s[r, pl.ds(pl.multiple_of(e * cap, cap), cap), :] = jnp.where(hit, 1.0, 0.0).astype(BF)
            gate = jnp.sum(jnp.where(hit, aff_t_s[r, pl.ds(e, 1), :], 0.0), axis=-1, keepdims=True)
            gate_ref[e, r * cap:(r + 1) * cap, :] = gate
        return carry

    lax.fori_loop(0, N_EXPERTS, per_expert, 0, unroll=unroll)
    n_rows = min(GATHER_ROWS, N_EXPERTS * cap)
    e_per = n_rows // cap
    for r in reqs:
        h2 = h2_ref[rows(r), :]
        for t in range(N_EXPERTS * cap // n_rows):
            xe = _dot(p_s[r, t * n_rows:(t + 1) * n_rows, :], h2).astype(BF)
            xe_ref[t * e_per:(t + 1) * e_per, r * cap:(r + 1) * cap, :] = xe.reshape(e_per, cap, D)


def _capacity(n_tok):
    return max(1, CAPACITY_FACTOR * n_tok // N_EXPERTS)


def _route_call(lg, h2, n_req, n_tok):
    cap = _capacity(n_tok)
    per_step = max(1, min(n_req, ROUTE_STEP_TOKENS // n_tok))
    before = jnp.asarray((np.arange(PREFIX_BLOCK)[None, :] < np.arange(PREFIX_BLOCK)[:, None]).astype(BF))
    unroll = max(1, min(N_EXPERTS, ROUTE_UNROLL_ELEMS // (per_step * cap * n_tok)))
    return pl.pallas_call(
        functools.partial(_route_kernel, cap, n_tok, unroll),
        grid=(n_req // per_step,),
        in_specs=[
            pl.BlockSpec((per_step * n_tok, LANES), lambda b: (b, 0)),
            pl.BlockSpec((per_step * n_tok, D), lambda b: (b, 0)),
            _const_spec((PREFIX_BLOCK, PREFIX_BLOCK)),
        ],
        out_specs=[
            pl.BlockSpec((N_EXPERTS, per_step * cap, D), lambda b: (0, b, 0)),
            pl.BlockSpec((N_EXPERTS, per_step * cap, 1), lambda b: (0, b, 0)),
            pl.BlockSpec((per_step * n_tok, LANES), lambda b: (b, 0)),
        ],
        out_shape=[
            jax.ShapeDtypeStruct((N_EXPERTS, n_req * cap, D), BF),
            jax.ShapeDtypeStruct((N_EXPERTS, n_req * cap, 1), F32),
            jax.ShapeDtypeStruct((n_req * n_tok, LANES), F32),
        ],
        scratch_shapes=[
            pltpu.VMEM((per_step, LANES, n_tok), F32),
            pltpu.VMEM((per_step, LANES, n_tok), F32),
            pltpu.VMEM((per_step, N_EXPERTS * cap, n_tok), BF),
        ],
        compiler_params=_params(1),
        name="route",
    )(lg, h2, before)


EXPERT_BLOCK = 1024
EXPERT_SUB = 256


def _experts_kernel(xa_ref, xb_ref, ga_ref, gb_ref, wg_ref, wu_ref, wd_ref, ya_ref, yb_ref, acc_s):
    f = pl.program_id(1)
    ra = xa_ref.shape[1]
    x = jnp.concatenate([xa_ref[0], xb_ref[0]], axis=0)
    for c in range(wg_ref.shape[2] // EXPERT_SUB):
        cs = slice(c * EXPERT_SUB, (c + 1) * EXPERT_SUB)
        gate = _dot(x, wg_ref[0, :, cs].astype(BF))
        up = _dot(x, wu_ref[0, :, cs].astype(BF))
        hid = (gate * _sigmoid(gate) * up).astype(BF)
        down = _dot(hid, wd_ref[0, cs, :].astype(BF))
        if c == 0:
            acc_s[...] = jnp.where(f == 0, 0.0, acc_s[...]) + down
        else:
            acc_s[...] += down

    @pl.when(f == pl.num_programs(1) - 1)
    def _():
        ya_ref[0] = (acc_s[0:ra, :] * ga_ref[0]).astype(ya_ref.dtype)
        yb_ref[0] = (acc_s[ra:, :] * gb_ref[0]).astype(yb_ref.dtype)


def _experts_call(xa, xb, ga, gb, wg, wu, wd):
    ra, rb = xa.shape[1], xb.shape[1]
    tf = EXPERT_BLOCK
    xa_spec = pl.BlockSpec((1, ra, D), lambda e, f: (e, 0, 0))
    xb_spec = pl.BlockSpec((1, rb, D), lambda e, f: (e, 0, 0))
    return pl.pallas_call(
        _experts_kernel,
        grid=(N_EXPERTS, D_EXPERT // tf),
        in_specs=[
            xa_spec, xb_spec,
            pl.BlockSpec((1, ra, 1), lambda e, f: (e, 0, 0)),
            pl.BlockSpec((1, rb, 1), lambda e, f: (e, 0, 0)),
            pl.BlockSpec((1, D, tf), lambda e, f: (e, 0, f)),
            pl.BlockSpec((1, D, tf), lambda e, f: (e, 0, f)),
            pl.BlockSpec((1, tf, D), lambda e, f: (e, f, 0)),
        ],
        out_specs=[xa_spec, xb_spec],
        out_shape=[
            jax.ShapeDtypeStruct((N_EXPERTS, ra, D), BF),
            jax.ShapeDtypeStruct((N_EXPERTS, rb, D), BF),
        ],
        scratch_shapes=[pltpu.VMEM((ra + rb, D), F32)],
        compiler_params=_params(2),
        name="experts",
    )(xa, xb, ga, gb, wg, wu, wd)


SCATTER_STEP_TOKENS = 1024


def _scatter_kernel(cap, n_tok, row_of_req, slot_ref, y_ref, x1_ref, m_ref, expand_ref, out_ref):
    per_step = slot_ref.shape[0] // n_tok
    slot_ids = (lax.broadcasted_iota(jnp.int32, (1, N_EXPERTS * cap), 1) % cap).astype(F32)
    for r in range(per_step):
        rows = slice(r * n_tok, (r + 1) * n_tok)
        m = m_ref[pl.ds(row_of_req(pl.program_id(0) * per_step + r), 1), :]
        gate2 = m[:, 5 * D:6 * D]
        slot_wide = _dot(slot_ref[rows, :].astype(BF), expand_ref[...])
        onehot = jnp.where(slot_wide == slot_ids, 1.0, 0.0).astype(BF)
        y = y_ref[:, r * cap:(r + 1) * cap, :].reshape(N_EXPERTS * cap, D)
        out_ref[rows, :] = x1_ref[rows, :] + gate2 * _dot(onehot, y)


def _scatter_call(slots, y, x1, m, row_of_req, n_req, n_tok):
    cap = _capacity(n_tok)
    per_step = max(1, min(n_req, SCATTER_STEP_TOKENS // n_tok))
    expand = jnp.asarray((np.arange(LANES)[:, None] == np.arange(N_EXPERTS * cap)[None, :] // cap).astype(BF))
    tok = pl.BlockSpec((per_step * n_tok, D), lambda b: (b, 0))
    return pl.pallas_call(
        functools.partial(_scatter_kernel, cap, n_tok, row_of_req),
        grid=(n_req // per_step,),
        in_specs=[
            pl.BlockSpec((per_step * n_tok, LANES), lambda b: (b, 0)),
            pl.BlockSpec((N_EXPERTS, per_step * cap, D), lambda b: (0, b, 0)),
            tok,
            _const_spec((MOD_ROWS, 6 * D)),
            _const_spec((LANES, N_EXPERTS * cap)),
        ],
        out_specs=tok,
        out_shape=jax.ShapeDtypeStruct((n_req * n_tok, D), F32),
        compiler_params=_params(1),
        name="scatter",
    )(slots, y, x1, m, expand)


def _rope_tables(n_tokens):
    rows = n_tokens // GRID_W
    row = np.broadcast_to(np.arange(rows, dtype=np.float32)[:, None], (rows, GRID_W)).reshape(-1)
    col = np.broadcast_to(np.arange(GRID_W, dtype=np.float32)[None, :], (rows, GRID_W)).reshape(-1)
    half = HEAD_DIM // 4
    inv_freq = (np.float32(ROPE_BASE) ** (-np.arange(half, dtype=np.float32) / np.float32(half))).astype(np.float32)
    ar = row[:, None] * inv_freq
    ac = col[:, None] * inv_freq
    ang = np.concatenate([ar, ar, ac, ac], axis=-1).astype(np.float64)
    cos, sin = np.cos(ang).astype(np.float32), np.sin(ang).astype(np.float32)
    first_half = (np.arange(HEAD_DIM) % (HEAD_DIM // 2)) < (HEAD_DIM // 4)
    sin_signed = np.where(first_half[None, :], -sin, sin)
    reps = LANES // HEAD_DIM
    token_major = np.concatenate([np.tile(cos, (1, reps)), np.tile(sin_signed, (1, reps))], axis=1)
    transposed = np.concatenate([cos.T, sin_signed.T], axis=0)
    return jnp.asarray(token_major), jnp.asarray(transposed)


def kernel(x_prompt, x_sample, cache_k, cache_v, c, c_ctx, w_ada, b_ada, norm1_g, norm2_g, w_in, q_norm_g, k_norm_g, lambda_q1, lambda_k1, lambda_q2, lambda_k2, subln_g, gmlp_ln_g, gmlp_ln_b, w_spatial, b_spatial, w_out, w_router, w_gate_e, w_up_e, w_down_e):
    n_p, t_p = x_prompt.shape[0], x_prompt.shape[1]
    n_s, t_s = x_sample.shape[0], x_sample.shape[1]
    n_ctx = cache_k.shape[2]
    l = 0

    m, lamv, w_in_bf, wkt, w_out_bf, wsp_pairs = _prep_call(
        c_ctx[None, :], c, w_ada[l], b_ada[l][None, :], lambda_q1[l][None, :], lambda_k1[l][None, :],
        lambda_q2[l][None, :], lambda_k2[l][None, :], w_in[l], w_out[l], w_spatial[l])

    gsum_np = (np.arange(D)[:, None] // HEAD_DIM == np.arange(LANES)[None, :]).astype(BF)
    gsum = jnp.asarray(gsum_np)
    wr = jnp.pad(w_router[l], ((0, 0), (0, LANES - N_EXPERTS)))
    wr_hi = wr.astype(BF)
    qg = jnp.tile(q_norm_g[l] * (HEAD_DIM ** -0.5 * math.log2(math.e)), D // HEAD_DIM)

    def rows_of(*vectors):
        rows = [jnp.tile(vec, D // vec.shape[0])[None, :] for vec in vectors]
        return jnp.concatenate(rows + [jnp.zeros((MOD_ROWS - len(rows), D), F32)], axis=0)

    consts = {
        "vecs": rows_of(norm1_g[l], qg, gmlp_ln_g[l], gmlp_ln_b[l], norm2_g[l], subln_g[l]),
        "w_in": w_in_bf,
        "wkt": wkt,
        "kgt": jnp.broadcast_to(k_norm_g[l][:, None], (HEAD_DIM, INPROJ_ROWS)),
        "gsum": gsum,
        "gbcast": jnp.asarray(np.concatenate([gsum_np.T, gsum_np.T], axis=0)),
        "wsp": wsp_pairs,
        "bsp": jnp.repeat(b_spatial[l].T, D // N_GROUPS, axis=1),
        "w_out": w_out_bf,
        "wr": jnp.concatenate([wr_hi, (wr - wr_hi.astype(F32)).astype(BF)], axis=1),
    }

    passes = (
        (x_prompt, n_p, t_p, lambda b: 0, False),
        (x_sample, n_s, t_s, lambda b: 1 + b, True),
    )
    outs = []
    kv_out = None
    lg = None
    for x, n_req, n_tok, row_of_req, is_sample in passes:
        xf = x.reshape(n_req * n_tok, D)
        tabs = _rope_tables(n_tok) if is_sample else None
        qst, kt, v = _inproj_call(xf, m, row_of_req, tabs, BF if is_sample else F32, consts, n_req, n_tok,
                                  after=lg)
        if is_sample:
            ctx = (jnp.transpose(cache_k[:, l], (0, 2, 3, 4, 1)).reshape(n_req, D, n_ctx),
                   cache_v[:, l].reshape(n_req * n_ctx, D))
        else:
            ctx = None
            kv_out = (kt, v)
        x1, h2, lg = _attn_call(qst, kt, v, ctx, lamv, xf, m, row_of_req, consts, n_req, n_tok)
        xe, gates, slots = _route_call(lg, h2, n_req, n_tok)
        outs.append((x1, xe, gates, slots, row_of_req))

    kv_out = (kv_out[0], lax.optimization_barrier((kv_out[1], qst))[0])
    (x1p, xep, gp, slots_p, row_p), (x1s, xes, gs, slots_s, row_s) = outs
    yp, ys = _experts_call(xep, xes, gp, gs, w_gate_e[l], w_up_e[l], w_down_e[l])
    y_prompt = _scatter_call(slots_p, yp, x1p, m, row_p, n_p, t_p).reshape(x_prompt.shape)
    y_sample = _scatter_call(slots_s, ys, x1s, m, row_s, n_s, t_s).reshape(x_sample.shape)
    new_k = jnp.transpose(kv_out[0].reshape(n_p, N_HEADS, 2, HEAD_DIM, t_p), (0, 4, 1, 2, 3))
    new_k = new_k.reshape(n_p, 1, t_p, N_HEADS, 2, HEAD_DIM)
    new_v = kv_out[1].reshape(n_p, 1, t_p, N_HEADS, V_HEAD_DIM)
    return (y_prompt, y_sample, new_k, new_v)
```
